```python
import math
import jax, jax.numpy as jnp
from jax import lax
import numpy as np

D_MODEL = 1024
BATCH = 8
SEQ = 4096
DEPTH = 1

HEAD_DIM = 64
SB_HEADS = 8
SB_WIDTH = SB_HEADS * HEAD_DIM
DIL_CONFIGS = ((128, 1), (512, 4), (2048, 16))
DIL_HEADS_PER_GROUP = 4
DIL_HEADS = DIL_HEADS_PER_GROUP * len(DIL_CONFIGS)
DIL_WIDTH = DIL_HEADS * HEAD_DIM
DIL_OUT_WIDTH = DIL_HEADS_PER_GROUP * HEAD_DIM
IN_SPLIT_SIZES = (SB_WIDTH, SB_WIDTH, SB_WIDTH, DIL_WIDTH, DIL_WIDTH, DIL_WIDTH, D_MODEL, D_MODEL)
IN_PROJ_WIDTH = sum(IN_SPLIT_SIZES)
IN_SPLIT_POINTS = tuple(int(v) for v in np.cumsum(IN_SPLIT_SIZES)[:-1])
Q_BLOCK = 128
N_BUCKETS = 32
MAX_DISTANCE = 2048
N_EXPERTS = 256
TOP_K = 8
N_GROUPS = 8
TOP_K_GROUPS = 4
EXPERT_DIM = 256
ROUTED_SCALE = 2.5
DISPATCH_BLOCK = 128
N_MOD = 6
EPS = 1e-6

kernel_name = 'hybrid_stickbreak_dilated_moe_block'


def rms_norm(x, g):
    xf = x.astype(jnp.float32)
    y = xf * lax.rsqrt(jnp.mean(xf * xf, axis=-1, keepdims=True) + EPS)
    return (y * g.astype(jnp.float32)).astype(x.dtype)


def swiglu(x, w_gate, w_up, w_down):
    return (jax.nn.silu(x @ w_gate) * (x @ w_up)) @ w_down


def t5_causal_bucket(dist):
    max_exact = N_BUCKETS // 2
    d = jnp.maximum(dist, 1).astype(jnp.float32)
    large = max_exact + (jnp.log(d / max_exact) / math.log(MAX_DISTANCE / max_exact)
                         * (N_BUCKETS - max_exact)).astype(jnp.int32)
    large = jnp.minimum(large, N_BUCKETS - 1)
    return jnp.where(dist < max_exact, dist, large)


def stick_breaking_attention(q, k, v):
    b, s, h, dh = q.shape
    n_blocks = s // Q_BLOCK
    scale = dh ** -0.5
    k_t = k.transpose(0, 2, 1, 3).astype(jnp.float32)
    v_t = v.transpose(0, 2, 1, 3)
    q_blocks = q.reshape(b, n_blocks, Q_BLOCK, h, dh).transpose(1, 0, 3, 2, 4)
    key_pos = jnp.arange(s)

    def one_block(args):
        qb, blk = args
        z = jnp.einsum('bhqd,bhkd->bhqk', qb.astype(jnp.float32), k_t) * scale
        q_pos = blk * Q_BLOCK + jnp.arange(Q_BLOCK)
        past = key_pos[None, :] < q_pos[:, None]
        log_keep = jnp.where(past, jax.nn.log_sigmoid(-z), 0.0)
        log_between = lax.cumsum(log_keep, axis=3, reverse=True) - log_keep
        weights = jnp.where(past, jnp.exp(jax.nn.log_sigmoid(z) + log_between), 0.0)
        return jnp.einsum('bhqk,bhkd->bhqd', weights.astype(v_t.dtype), v_t)

    out = lax.map(one_block, (q_blocks, jnp.arange(n_blocks)))
    return out.transpose(1, 0, 3, 2, 4).reshape(b, s, h * dh)


def dilated_group_attention(q, k, v, bias_table, window, dilation):
    b, s, hg, dh = q.shape
    n_sub = s // dilation
    n_blk = -(-n_sub // Q_BLOCK)
    l_pad = n_blk * Q_BLOCK
    w_sub = window // dilation
    scale = dh ** -0.5

    def to_sub(t):
        return t.reshape(b, n_sub, dilation, hg, dh).transpose(0, 2, 3, 1, 4)

    pad_q = ((0, 0), (0, 0), (0, 0), (0, l_pad - n_sub), (0, 0))
    pad_kv = ((0, 0), (0, 0), (0, 0), (Q_BLOCK, l_pad - n_sub), (0, 0))
    qb = jnp.pad(to_sub(q), pad_q).reshape(b, dilation, hg, n_blk, Q_BLOCK, dh)

    def band(t):
        tp = jnp.pad(to_sub(t), pad_kv)
        prev = tp[:, :, :, :l_pad].reshape(b, dilation, hg, n_blk, Q_BLOCK, dh)
        cur = tp[:, :, :, Q_BLOCK:].reshape(b, dilation, hg, n_blk, Q_BLOCK, dh)
        return jnp.concatenate([prev, cur], axis=4)

    kb, vb = band(k), band(v)
    qi = jnp.arange(Q_BLOCK)[:, None]
    kj = jnp.arange(2 * Q_BLOCK)[None, :]
    dist_sub = qi + Q_BLOCK - kj
    bucket = t5_causal_bucket(jnp.maximum(dist_sub, 0) * dilation)
    bias = bias_table[bucket].astype(jnp.float32).transpose(2, 0, 1)
    key_n = jnp.arange(n_blk)[:, None, None] * Q_BLOCK - Q_BLOCK + kj[None]
    valid = (dist_sub >= 0) & (dist_sub <= w_sub) & (key_n >= 0)
    logits = jnp.einsum('brhnqd,brhnkd->brhnqk', qb.astype(jnp.float32), kb.astype(jnp.float32)) * scale
    logits = jnp.where(valid, logits + bias[None, None, :, None], -jnp.inf)
    lse = jax.nn.logsumexp(logits, axis=-1)
    p = jnp.exp(logits - lse[..., None])
    o = jnp.einsum('brhnqk,brhnkd->brhnqd', p.astype(vb.dtype), vb)
    o = o.reshape(b, dilation, hg, l_pad, dh)[:, :, :, :n_sub]
    o = o.transpose(0, 3, 1, 2, 4).reshape(b, s, hg, dh)
    lse = lse.reshape(b, dilation, hg, l_pad)[:, :, :, :n_sub]
    lse = lse.transpose(0, 3, 1, 2).reshape(b, s, hg)
    return o, lse


def hybrid_mixer(h, w_in, w_branch_a, w_branch_b, w_out, rel_bias):
    b, s, _ = h.shape
    proj = h @ w_in
    sb_q, sb_k, sb_v, dq, dk, dv, gate_a, gate_b = jnp.split(proj, IN_SPLIT_POINTS, axis=-1)
    heads_a = lambda t: t.reshape(b, s, SB_HEADS, HEAD_DIM)
    out_a = stick_breaking_attention(heads_a(sb_q), heads_a(sb_k), heads_a(sb_v)) @ w_branch_a
    heads_b = lambda t: t.reshape(b, s, DIL_HEADS, HEAD_DIM)
    dq, dk, dv = heads_b(dq), heads_b(dk), heads_b(dv)
    outs, lses = [], []
    for g, (window, dilation) in enumerate(DIL_CONFIGS):
        sl = slice(g * DIL_HEADS_PER_GROUP, (g + 1) * DIL_HEADS_PER_GROUP)
        o, l = dilated_group_attention(dq[:, :, sl], dk[:, :, sl], dv[:, :, sl], rel_bias[:, sl], window, dilation)
        outs.append(o)
        lses.append(l)
    mix_w = jax.nn.softmax(jnp.stack(lses, axis=0), axis=0)
    out_b = jnp.sum(mix_w[..., None].astype(outs[0].dtype) * jnp.stack(outs, axis=0), axis=0)
    out_b = out_b.reshape(b, s, DIL_OUT_WIDTH) @ w_branch_b
    merged = jax.nn.sigmoid(gate_a) * out_a + jax.nn.sigmoid(gate_b) * out_b
    return merged @ w_out


def moe_ffn(h, w_router, router_bias, w_exp_gate, w_exp_up, w_exp_down, w_sh_gate, w_sh_up, w_sh_down):
    b, s, d = h.shape
    n_tok = b * s
    hf = h.reshape(n_tok, d)
    shared = swiglu(hf, w_sh_gate, w_sh_up, w_sh_down)
    scores = jax.nn.sigmoid((hf @ w_router).astype(jnp.float32))
    sel = scores + router_bias.astype(jnp.float32)
    per_group = N_EXPERTS // N_GROUPS
    group_score = lax.top_k(sel.reshape(n_tok, N_GROUPS, per_group), 2)[0].sum(-1)
    _, top_groups = lax.top_k(group_score, TOP_K_GROUPS)
    group_mask = jax.nn.one_hot(top_groups, N_GROUPS, dtype=jnp.float32).sum(1)
    sel = jnp.where(jnp.repeat(group_mask, per_group, axis=1) > 0, sel, -jnp.inf)
    _, expert_idx = lax.top_k(sel, TOP_K)
    gate_w = jnp.take_along_axis(scores, expert_idx, axis=1)
    gate_w = gate_w / jnp.sum(gate_w, axis=-1, keepdims=True) * ROUTED_SCALE
    n_slot = n_tok * TOP_K
    e_flat = expert_idx.reshape(-1)
    tok_flat = jnp.repeat(jnp.arange(n_tok, dtype=jnp.int32), TOP_K)
    w_flat = gate_w.reshape(-1)
    order = jnp.argsort(e_flat)
    e_sorted = e_flat[order]
    counts = jnp.zeros((N_EXPERTS,), jnp.int32).at[e_flat].add(1)
    starts = jnp.cumsum(counts) - counts
    padded = (counts + DISPATCH_BLOCK - 1) // DISPATCH_BLOCK * DISPATCH_BLOCK
    p_ends = jnp.cumsum(padded)
    p_starts = p_ends - padded
    dest = p_starts[e_sorted] + (jnp.arange(n_slot, dtype=jnp.int32) - starts[e_sorted])
    n_blk = -(-n_slot // DISPATCH_BLOCK) + N_EXPERTS
    n_pad = n_blk * DISPATCH_BLOCK
    slot_tok = jnp.full((n_pad,), n_tok, jnp.int32).at[dest].set(tok_flat[order])
    slot_w = jnp.zeros((n_pad,), jnp.float32).at[dest].set(w_flat[order])
    blk_expert = jnp.minimum(jnp.searchsorted(p_ends, jnp.arange(n_blk, dtype=jnp.int32) * DISPATCH_BLOCK,
                                              side='right'), N_EXPERTS - 1)
    h_pad = jnp.concatenate([hf, jnp.zeros((1, d), hf.dtype)], axis=0)

    def expert_block(args):
        tok, wt, e = args
        yb = swiglu(h_pad[tok], w_exp_gate[e], w_exp_up[e], w_exp_down[e])
        return yb * wt[:, None].astype(yb.dtype)

    y = lax.map(expert_block, (slot_tok.reshape(n_blk, DISPATCH_BLOCK),
                               slot_w.reshape(n_blk, DISPATCH_BLOCK), blk_expert))
    routed = jax.ops.segment_sum(y.reshape(n_pad, d), slot_tok, num_segments=n_tok + 1)[:n_tok]
    return (shared + routed).reshape(b, s, d)


def setup_inputs(seed: int = 0) -> dict:
    key = jax.random.key(seed)
    ks = jax.random.split(key, 21)

    def normal(k, shape, scale):
        return jax.random.normal(k, shape, jnp.float32) * scale

    L = DEPTH
    return {
        'x': normal(ks[0], (BATCH, SEQ, D_MODEL), 1.0),
        'c': normal(ks[1], (BATCH, D_MODEL), 1.0),
        'rel_bias': normal(ks[2], (N_BUCKETS, DIL_HEADS), 0.5),
        'w_ada': normal(ks[3], (L, D_MODEL, N_MOD * D_MODEL), 0.5 * D_MODEL ** -0.5),
        'b_ada': normal(ks[4], (L, N_MOD * D_MODEL), 0.02),
        'g_pre_mix': 1.0 + normal(ks[5], (L, D_MODEL), 0.05),
        'w_in': normal(ks[6], (L, D_MODEL, IN_PROJ_WIDTH), D_MODEL ** -0.5),
        'w_branch_a': normal(ks[7], (L, SB_WIDTH, D_MODEL), SB_WIDTH ** -0.5),
        'w_branch_b': normal(ks[8], (L, DIL_OUT_WIDTH, D_MODEL), DIL_OUT_WIDTH ** -0.5),
        'w_out': normal(ks[9], (L, D_MODEL, D_MODEL), D_MODEL ** -0.5),
        'g_post_mix': 1.0 + normal(ks[10], (L, D_MODEL), 0.05),
        'g_pre_ffn': 1.0 + normal(ks[11], (L, D_MODEL), 0.05),
        'w_router': normal(ks[12], (L, D_MODEL, N_EXPERTS), D_MODEL ** -0.5),
        'router_bias': normal(ks[13], (L, N_EXPERTS), 0.01),
        'w_exp_gate': normal(ks[14], (L, N_EXPERTS, D_MODEL, EXPERT_DIM), D_MODEL ** -0.5),
        'w_exp_up': normal(ks[15], (L, N_EXPERTS, D_MODEL, EXPERT_DIM), D_MODEL ** -0.5),
        'w_exp_down': normal(ks[16], (L, N_EXPERTS, EXPERT_DIM, D_MODEL), EXPERT_DIM ** -0.5),
        'w_sh_gate': normal(ks[17], (L, D_MODEL, EXPERT_DIM), D_MODEL ** -0.5),
        'w_sh_up': normal(ks[18], (L, D_MODEL, EXPERT_DIM), D_MODEL ** -0.5),
        'w_sh_down': normal(ks[19], (L, EXPERT_DIM, D_MODEL), EXPERT_DIM ** -0.5),
        'g_post_ffn': 1.0 + normal(ks[20], (L, D_MODEL), 0.05),
    }


def reference(x, c, rel_bias, w_ada, b_ada, g_pre_mix, w_in, w_branch_a, w_branch_b, w_out, g_post_mix,
              g_pre_ffn, w_router, router_bias, w_exp_gate, w_exp_up, w_exp_down, w_sh_gate, w_sh_up,
              w_sh_down, g_post_ffn):
    for layer in range(DEPTH):
        mod = jax.nn.silu(c) @ w_ada[layer] + b_ada[layer]
        sh1, sc1, gt1, sh2, sc2, gt2 = jnp.split(mod[:, None, :], N_MOD, axis=-1)
        h = rms_norm(x, g_pre_mix[layer]) * (1.0 + sc1) + sh1
        y = hybrid_mixer(h, w_in[layer], w_branch_a[layer], w_branch_b[layer], w_out[layer], rel_bias)
        x = x + gt1 * rms_norm(y, g_post_mix[layer])
        h = rms_norm(x, g_pre_ffn[layer]) * (1.0 + sc2) + sh2
        y = moe_ffn(h, w_router[layer], router_bias[layer], w_exp_gate[layer], w_exp_up[layer],
                    w_exp_down[layer], w_sh_gate[layer], w_sh_up[layer], w_sh_down[layer])
        x = x + gt2 * rms_norm(y, g_post_ffn[layer])
    return x
```

```python
import functools
import math

import jax
import jax.numpy as jnp
import numpy as np
from jax import lax
from jax.experimental import pallas as pl
from jax.experimental.pallas import tpu as pltpu

D_MODEL = 1024
HEAD_DIM = 64
SB_HEADS = 8
SB_WIDTH = SB_HEADS * HEAD_DIM
DIL_CONFIGS = ((128, 1), (512, 4), (2048, 16))
DIL_HEADS_PER_GROUP = 4
DIL_GROUP_WIDTH = DIL_HEADS_PER_GROUP * HEAD_DIM
DIL_WIDTH = DIL_GROUP_WIDTH * len(DIL_CONFIGS)
Q_BLOCK = 128
N_BUCKETS = 32
MAX_DISTANCE = 2048
N_EXPERTS = 256
TOP_K = 8
N_GROUPS = 8
TOP_K_GROUPS = 4
EXPERT_DIM = 256
ROUTED_SCALE = 2.5
N_MOD = 6
EPS = 1e-6

LANES = 128
VMEM_LIMIT_BYTES = 56 * 1024 * 1024
NEG_INF = float("-inf")


def _cparams(*semantics):
    return pltpu.CompilerParams(dimension_semantics=semantics, vmem_limit_bytes=VMEM_LIMIT_BYTES)


def _rms_norm(v, g):
    return v * lax.rsqrt(jnp.mean(v * v, axis=-1, keepdims=True) + EPS) * g


def _sigmoid(v):
    return 1.0 / (1.0 + jnp.exp(-v))


def _dot(a, b):
    return jnp.dot(a, b, preferred_element_type=jnp.float32)


def _dot_nt(a, b):
    return lax.dot_general(a, b, (((1,), (1,)), ((), ())), preferred_element_type=jnp.float32)


def _split_bf16(v):
    hi = v.astype(jnp.bfloat16)
    lo = (v - hi.astype(jnp.float32)).astype(jnp.bfloat16)
    return hi, lo


def _ada_kernel(c_ref, w_ref, b_ref, o_ref):
    c = c_ref[...]
    a_hi, a_lo = _split_bf16(c * _sigmoid(c))
    w_hi, w_lo = _split_bf16(w_ref[...])
    o_ref[...] = _dot(a_hi, w_hi) + _dot(a_lo, w_hi) + _dot(a_hi, w_lo) + b_ref[...]


def ada_modulation(c, w_ada, b_ada, *, tn=1536):
    b, d = c.shape
    n = w_ada.shape[1]
    return pl.pallas_call(
        _ada_kernel,
        grid=(n // tn,),
        in_specs=[
            pl.BlockSpec((b, d), lambda j: (0, 0)),
            pl.BlockSpec((d, tn), lambda j: (0, j)),
            pl.BlockSpec((1, tn), lambda j: (0, j)),
        ],
        out_specs=pl.BlockSpec((b, tn), lambda j: (0, j)),
        out_shape=jax.ShapeDtypeStruct((b, n), jnp.float32),
        compiler_params=_cparams("parallel"),
        name="ada_modulation",
    )(c, w_ada, b_ada.reshape(1, n))


_ROW_SPLITS = (SB_WIDTH,) + (DIL_GROUP_WIDTH,) * 9 + (D_MODEL, D_MODEL)


def _in_proj_kernel(x_ref, mod_ref, g_ref, w_rows_ref, w_cols_ref, qT_ref, k_ref, vT_ref, *rest, tm):
    dil_refs, ga_ref, gb_ref = rest[:9], rest[9], rest[10]
    h = _rms_norm(x_ref[0], g_ref[...]) * (1.0 + mod_ref[0, 1:2, :]) + mod_ref[0, 0:1, :]
    hb = h.astype(jnp.bfloat16)
    off = 0
    k_ref[0] = _dot(hb, w_rows_ref[:, off:off + SB_WIDTH]).astype(k_ref.dtype)
    off += SB_WIDTH
    for r in dil_refs:
        r[0] = _dot(hb, w_rows_ref[:, off:off + DIL_GROUP_WIDTH]).astype(r.dtype)
        off += DIL_GROUP_WIDTH
    for r in (ga_ref, gb_ref):
        r[0] = _sigmoid(_dot(hb, w_rows_ref[:, off:off + D_MODEL])).astype(r.dtype)
        off += D_MODEL
    qT_ref[0] = _dot_nt(w_cols_ref[0:SB_WIDTH, :], hb).astype(qT_ref.dtype)
    vT = _dot_nt(w_cols_ref[SB_WIDTH:2 * SB_WIDTH, :], hb).astype(vT_ref.dtype)
    for p in range(SB_WIDTH // LANES):
        for jb in range(tm // LANES):
            vT_ref[0, p, jb] = vT[p * LANES:(p + 1) * LANES, jb * LANES:(jb + 1) * LANES]


def in_projection(x, mod, g_pre, w_rows, w_cols, *, tm=256):
    b, s, d = x.shape
    nt = s // tm
    bf = jnp.bfloat16
    tok = lambda width: pl.BlockSpec((1, tm, width), lambda bi, i: (bi, i, 0))
    out_shapes = [
        jax.ShapeDtypeStruct((b, SB_WIDTH, s), bf),
        jax.ShapeDtypeStruct((b, s, SB_WIDTH), bf),
        jax.ShapeDtypeStruct((b, SB_WIDTH // LANES, s // LANES, LANES, LANES), bf),
    ] + [jax.ShapeDtypeStruct((b, s, DIL_GROUP_WIDTH), bf)] * 9 + [jax.ShapeDtypeStruct((b, s, d), bf)] * 2
    out_specs = [
        pl.BlockSpec((1, SB_WIDTH, tm), lambda bi, i: (bi, 0, i)),
        tok(SB_WIDTH),
        pl.BlockSpec((1, SB_WIDTH // LANES, tm // LANES, LANES, LANES), lambda bi, i: (bi, 0, i, 0, 0)),
    ] + [tok(DIL_GROUP_WIDTH)] * 9 + [tok(d)] * 2
    return pl.pallas_call(
        functools.partial(_in_proj_kernel, tm=tm),
        grid=(b, nt),
        in_specs=[
            tok(d),
            pl.BlockSpec((1, N_MOD, d), lambda bi, i: (bi, 0, 0)),
            pl.BlockSpec((1, d), lambda bi, i: (0, 0)),
            pl.BlockSpec(w_rows.shape, lambda bi, i: (0, 0)),
            pl.BlockSpec(w_cols.shape, lambda bi, i: (0, 0)),
        ],
        out_specs=out_specs,
        out_shape=out_shapes,
        compiler_params=_cparams("parallel", "parallel"),
        name="in_projection",
    )(x, mod, g_pre.reshape(1, d), w_rows, w_cols)


def _sb_attn_kernel(qT_ref, k_ref, vT_ref, tri_ref, o_ref, *, tq):
    i = pl.program_id(2)
    q2 = qT_ref[0]
    row = lax.broadcasted_iota(jnp.int32, (LANES, tq), 0)
    col = lax.broadcasted_iota(jnp.int32, (LANES, tq), 1)
    zero = jnp.zeros_like(q2)
    q_heads = (jnp.where(row < HEAD_DIM, q2, zero), jnp.where(row >= HEAD_DIM, q2, zero))
    tri = tri_ref[...]
    q_pos = i * tq + col
    n_kblocks = (i + 1) * (tq // LANES)

    def body(it, carry):
        j = n_kblocks - 1 - it
        kj = k_ref[0, pl.ds(pl.multiple_of(j * LANES, LANES), LANES), :]
        vj = vT_ref[0, 0, j]
        past = (j * LANES + row) < q_pos
        new = []
        for h in range(2):
            acc, csum = carry[h]
            z = _dot(kj, q_heads[h])
            softplus = jnp.maximum(z, 0.0) + jnp.log(1.0 + jnp.exp(-jnp.abs(z)))
            log_keep = jnp.where(past, -softplus, 0.0)
            hi, lo = _split_bf16(log_keep)
            later = _dot(tri, hi) + _dot(tri, lo)
            w = jnp.where(past, jnp.exp(z + log_keep + later + csum), 0.0)
            vh = vj[h * HEAD_DIM:(h + 1) * HEAD_DIM, :]
            acc = acc + _dot(vh, w.astype(jnp.bfloat16))
            csum = csum + jnp.sum(log_keep, axis=0, keepdims=True)
            new.append((acc, csum))
        return tuple(new)

    init = tuple((jnp.zeros((HEAD_DIM, tq), jnp.float32), jnp.zeros((1, tq), jnp.float32)) for _ in range(2))
    res = lax.fori_loop(0, n_kblocks, body, init)
    out_t = jnp.concatenate([res[0][0], res[1][0]], axis=0)
    o_ref[0] = out_t.T.astype(o_ref.dtype)


def sb_attention(qT, k, vT3, *, tq=256):
    b, width, s = qT.shape
    n_pairs = width // LANES
    tri = jnp.asarray(np.triu(np.ones((LANES, LANES), np.float32), k=1), jnp.bfloat16)
    return pl.pallas_call(
        functools.partial(_sb_attn_kernel, tq=tq),
        grid=(b, n_pairs, s // tq),
        in_specs=[
            pl.BlockSpec((1, LANES, tq), lambda bi, p, i: (bi, p, i)),
            pl.BlockSpec((1, s, LANES), lambda bi, p, i: (bi, 0, p)),
            pl.BlockSpec((1, 1, s // LANES, LANES, LANES), lambda bi, p, i: (bi, p, 0, 0, 0)),
            pl.BlockSpec((LANES, LANES), lambda bi, p, i: (0, 0)),
        ],
        out_specs=pl.BlockSpec((1, tq, LANES), lambda bi, p, i: (bi, i, p)),
        out_shape=jax.ShapeDtypeStruct((b, s, width), jnp.bfloat16),
        compiler_params=_cparams("parallel", "parallel", "arbitrary"),
        name="sb_attention",
    )(qT, k, vT3, tri)


def _dil_attn_kernel(q_ref, kp_ref, kc_ref, vp_ref, vc_ref, bias_ref, o_ref, lse_ref):
    m = pl.program_id(2)
    q2 = q_ref[0]
    kk = jnp.concatenate([kp_ref[0], kc_ref[0]], axis=0)
    vv = jnp.concatenate([vp_ref[0], vc_ref[0]], axis=0)
    lane = lax.broadcasted_iota(jnp.int32, (Q_BLOCK, LANES), 1)
    key = lax.broadcasted_iota(jnp.int32, (Q_BLOCK, 2 * Q_BLOCK), 1)
    has_prev = jnp.logical_or(key >= Q_BLOCK, m > 0)
    zero = jnp.zeros_like(q2)
    outs, lses = [], []
    for h in range(2):
        in_head = (lane >= HEAD_DIM) if h else (lane < HEAD_DIM)
        logits = _dot_nt(jnp.where(in_head, q2, zero), kk) + bias_ref[h]
        logits = jnp.where(has_prev, logits, NEG_INF)
        mx = jnp.max(logits, axis=1, keepdims=True)
        p = jnp.exp(logits - mx)
        den = jnp.sum(p, axis=1, keepdims=True)
        outs.append(_dot(p.astype(jnp.bfloat16), vv) / den)
        lses.append(mx + jnp.log(den))
    first = lane < HEAD_DIM
    o_ref[0] = jnp.where(first, outs[0], outs[1]).astype(o_ref.dtype)
    lse_ref[0] = jnp.where(first, lses[0], lses[1])


def dilated_attention(q, k, v, bias, dilation):
    b, s, gw = q.shape
    length, width = s // dilation, dilation * gw
    q, k, v = (t.reshape(b, length, width) for t in (q, k, v))
    cur = pl.BlockSpec((1, Q_BLOCK, LANES), lambda bi, p, m: (bi, m, p))
    prev = pl.BlockSpec((1, Q_BLOCK, LANES), lambda bi, p, m: (bi, jnp.maximum(m - 1, 0), p))
    pairs_per_group = gw // LANES
    o, lse = pl.pallas_call(
        _dil_attn_kernel,
        grid=(b, width // LANES, length // Q_BLOCK),
        in_specs=[cur, prev, cur, prev, cur,
                  pl.BlockSpec((2, Q_BLOCK, 2 * Q_BLOCK), lambda bi, p, m: (p % pairs_per_group, 0, 0))],
        out_specs=[cur, cur],
        out_shape=[jax.ShapeDtypeStruct((b, length, width), jnp.bfloat16),
                   jax.ShapeDtypeStruct((b, length, width), jnp.float32)],
        compiler_params=_cparams("parallel", "parallel", "arbitrary"),
        name=f"dilated_attention_r{dilation}",
    )(q, k, k, v, v, bias)
    return o.reshape(b, s, gw), lse.reshape(b, s, gw)


def _t5_causal_bucket(dist):
    max_exact = N_BUCKETS // 2
    d = jnp.maximum(dist, 1).astype(jnp.float32)
    large = max_exact + (jnp.log(d / max_exact) / math.log(MAX_DISTANCE / max_exact)
                         * (N_BUCKETS - max_exact)).astype(jnp.int32)
    large = jnp.minimum(large, N_BUCKETS - 1)
    return jnp.where(dist < max_exact, dist, large)


def dilated_bias(rel_bias_group, window, dilation):
    qi = jnp.arange(Q_BLOCK)[:, None]
    kj = jnp.arange(2 * Q_BLOCK)[None, :]
    dist_sub = qi + Q_BLOCK - kj
    bucket = _t5_causal_bucket(jnp.maximum(dist_sub, 0) * dilation)
    bias = rel_bias_group[bucket].astype(jnp.float32).transpose(2, 0, 1)
    valid = (dist_sub >= 0) & (dist_sub <= window // dilation)
    return jnp.where(valid[None], bias, NEG_INF)


def _mix_out_kernel(oa_ref, o0_ref, o1_ref, o2_ref, l0_ref, l1_ref, l2_ref, ga_ref, gb_ref, x_ref, mod_ref,
                    wa_ref, wb_ref, wo_ref, gpost_ref, gpre_ref, wr_ref, wsgu_ref, wsd_ref,
                    x1_ref, hp_ref, sh_ref, lt_ref):
    f32 = jnp.float32
    l0, l1, l2 = l0_ref[0], l1_ref[0], l2_ref[0]
    mx = jnp.maximum(jnp.maximum(l0, l1), l2)
    e0, e1, e2 = jnp.exp(l0 - mx), jnp.exp(l1 - mx), jnp.exp(l2 - mx)
    den = e0 + e1 + e2
    out_b = (e0 / den) * o0_ref[0].astype(f32) + (e1 / den) * o1_ref[0].astype(f32) + (e2 / den) * o2_ref[0].astype(f32)
    ya = _dot(oa_ref[0], wa_ref[...])
    yb = _dot(out_b.astype(jnp.bfloat16), wb_ref[...])
    merged = ga_ref[0].astype(f32) * ya + gb_ref[0].astype(f32) * yb
    y = _dot(merged.astype(jnp.bfloat16), wo_ref[...])
    x1 = x_ref[0] + mod_ref[0, 2:3, :] * _rms_norm(y, gpost_ref[...])
    x1_ref[0] = x1
    h = _rms_norm(x1, gpre_ref[...]) * (1.0 + mod_ref[0, 4:5, :]) + mod_ref[0, 3:4, :]
    hb = h.astype(jnp.bfloat16)
    half = hb.shape[1] // 2
    bits = pltpu.bitcast(hb.astype(f32), jnp.uint32)
    hp_ref[0] = bits[:, :half] | (bits[:, half:] >> 16)
    lt_ref[...] = _dot_nt(wr_ref[...], hb)
    gu = _dot(hb, wsgu_ref[...])
    g, u = gu[:, :EXPERT_DIM], gu[:, EXPERT_DIM:]
    sh_ref[0] = _dot((g * _sigmoid(g) * u).astype(jnp.bfloat16), wsd_ref[...])


def mix_out(out_a, o_g, lse_g, gate_a, gate_b, x, mod, w_a, w_b, w_o, g_post, g_pre_ffn, w_rT, w_sgu, w_sd, *, tm=256):
    b, s, d = x.shape
    tok = lambda width: pl.BlockSpec((1, tm, width), lambda bi, i: (bi, i, 0))
    full = lambda a: pl.BlockSpec(a.shape, lambda bi, i: (0,) * a.ndim)
    nt = s // tm
    g_post, g_pre_ffn = g_post.reshape(1, d), g_pre_ffn.reshape(1, d)
    weights = (w_a, w_b, w_o, g_post, g_pre_ffn, w_rT, w_sgu, w_sd)
    return pl.pallas_call(
        _mix_out_kernel,
        grid=(b, nt),
        in_specs=[tok(SB_WIDTH)] + [tok(DIL_GROUP_WIDTH)] * 6 + [tok(d)] * 3
        + [pl.BlockSpec((1, N_MOD, d), lambda bi, i: (bi, 0, 0))] + [full(w) for w in weights],
        out_specs=[tok(d), tok(d // 2), tok(d), pl.BlockSpec((N_EXPERTS, tm), lambda bi, i: (0, bi * nt + i))],
        out_shape=[jax.ShapeDtypeStruct((b, s, d), jnp.float32),
                   jax.ShapeDtypeStruct((b, s, d // 2), jnp.uint32),
                   jax.ShapeDtypeStruct((b, s, d), jnp.float32),
                   jax.ShapeDtypeStruct((N_EXPERTS, b * s), jnp.float32)],
        compiler_params=_cparams("parallel", "parallel"),
        name="mix_out",
    )(out_a, *o_g, *lse_g, gate_a, gate_b, x, mod, *weights)


def _route_kernel(lt_ref, bias_ref, tri_ref, idx_ref, gate_ref, rank_ref, cnt_ref, base_ref, *, tn):
    f32 = jnp.float32

    @pl.when(pl.program_id(0) == 0)
    def _():
        base_ref[...] = jnp.zeros_like(base_ref)

    per_group = N_EXPERTS // N_GROUPS
    scores = _sigmoid(lt_ref[...])
    sel = scores + bias_ref[...]
    sel_groups = [sel[g * per_group:(g + 1) * per_group] for g in range(N_GROUPS)]
    ri = lax.broadcasted_iota(jnp.int32, (per_group, tn), 0)
    group_scores = []
    for sg in sel_groups:
        m1 = jnp.max(sg, axis=0, keepdims=True)
        first = jnp.min(jnp.where(sg == m1, ri, per_group), axis=0, keepdims=True)
        group_scores.append(m1 + jnp.max(jnp.where(ri == first, NEG_INF, sg), axis=0, keepdims=True))
    cur = jnp.concatenate(group_scores, axis=0)
    gi = lax.broadcasted_iota(jnp.int32, cur.shape, 0)
    keep = jnp.zeros(cur.shape, jnp.float32)
    for _ in range(TOP_K_GROUPS):
        mg = jnp.max(cur, axis=0, keepdims=True)
        pick = gi == jnp.min(jnp.where(cur == mg, gi, N_GROUPS), axis=0, keepdims=True)
        keep = jnp.where(pick, 1.0, keep)
        cur = jnp.where(pick, NEG_INF, cur)
    cur = jnp.concatenate([jnp.where(keep[g:g + 1, :] > 0.0, sg, NEG_INF) for g, sg in enumerate(sel_groups)], axis=0)
    ei = lax.broadcasted_iota(jnp.int32, cur.shape, 0)
    picks, idxs, gates = [], [], []
    for _ in range(TOP_K):
        mk = jnp.max(cur, axis=0, keepdims=True)
        fk = jnp.min(jnp.where(cur == mk, ei, N_EXPERTS), axis=0, keepdims=True)
        pick = ei == fk
        picks.append(pick)
        idxs.append(fk)
        gates.append(jnp.sum(jnp.where(pick, scores, 0.0), axis=0, keepdims=True))
        cur = jnp.where(pick, NEG_INF, cur)
    gate = jnp.concatenate(gates, axis=0)
    gate_ref[...] = gate / jnp.sum(gate, axis=0, keepdims=True) * ROUTED_SCALE
    idx_ref[...] = jnp.concatenate(idxs, axis=0)
    chosen = functools.reduce(jnp.logical_or, picks)
    onehot = jnp.where(chosen, 1.0, 0.0)
    earlier = _dot(onehot.astype(jnp.bfloat16), tri_ref[...]) + base_ref[...]
    ranks = [jnp.sum(jnp.where(p, earlier, 0.0), axis=0, keepdims=True) for p in picks]
    rank_ref[...] = jnp.concatenate(ranks, axis=0).astype(jnp.int32)
    base = base_ref[...] + jnp.sum(onehot, axis=1, keepdims=True)
    base_ref[...] = base
    cnt_ref[...] = jnp.broadcast_to(base, cnt_ref.shape).astype(jnp.int32)


def route(logits_t, router_bias, *, tn=512):
    ne, n = logits_t.shape
    tri = jnp.asarray(np.triu(np.ones((tn, tn), np.float32), k=1), jnp.bfloat16)
    slot = pl.BlockSpec((TOP_K, tn), lambda i: (0, i))
    return pl.pallas_call(
        functools.partial(_route_kernel, tn=tn),
        grid=(n // tn,),
        in_specs=[pl.BlockSpec((ne, tn), lambda i: (0, i)),
                  pl.BlockSpec((ne, 1), lambda i: (0, 0)),
                  pl.BlockSpec((tn, tn), lambda i: (0, 0))],
        out_specs=[slot, slot, slot, pl.BlockSpec((ne, LANES), lambda i: (0, 0))],
        out_shape=[jax.ShapeDtypeStruct((TOP_K, n), jnp.int32),
                   jax.ShapeDtypeStruct((TOP_K, n), jnp.float32),
                   jax.ShapeDtypeStruct((TOP_K, n), jnp.int32),
                   jax.ShapeDtypeStruct((ne, LANES), jnp.int32)],
        scratch_shapes=[pltpu.VMEM((ne, 1), jnp.float32)],
        compiler_params=_cparams("arbitrary"),
        name="route",
    )(logits_t, router_bias.reshape(ne, 1), tri)


def _dest_kernel(idx_ref, rank_ref, start_ref, dest_ref):
    idx = idx_ref[...]
    ei = lax.broadcasted_iota(jnp.int32, (N_EXPERTS, idx.shape[1]), 0)
    start = start_ref[...]
    rows = [jnp.sum(jnp.where(ei == idx[k:k + 1, :], start, 0.0), axis=0, keepdims=True) for k in range(TOP_K)]
    dest_ref[...] = jnp.concatenate(rows, axis=0).astype(jnp.int32) + rank_ref[...]


def slot_destinations(idx, rank, starts, *, tn=512):
    kk, n = idx.shape
    slot = pl.BlockSpec((kk, tn), lambda i: (0, i))
    return pl.pallas_call(
        _dest_kernel,
        grid=(n // tn,),
        in_specs=[slot, slot, pl.BlockSpec((N_EXPERTS, 1), lambda i: (0, 0))],
        out_specs=slot,
        out_shape=jax.ShapeDtypeStruct((kk, n), jnp.int32),
        compiler_params=_cparams("parallel"),
        name="slot_destinations",
    )(idx, rank, starts.astype(jnp.float32).reshape(N_EXPERTS, 1))


def _dispatch_kernel(dest_ref, h_ref, xs_ref, sem, *, tm):
    def body(t, _):
        for k in range(TOP_K):
            pltpu.make_async_copy(h_ref.at[pl.ds(t, 1), :], xs_ref.at[pl.ds(dest_ref[k, t], 1), :], sem).start()
        return 0

    lax.fori_loop(0, tm, body, 0)
    for _ in range(TOP_K):
        pltpu.make_async_copy(h_ref, xs_ref.at[pl.ds(0, tm), :], sem).wait()


def dispatch(dest, hp, *, tm=512):
    n, w = hp.shape
    kk = dest.shape[0]
    return pl.pallas_call(
        functools.partial(_dispatch_kernel, tm=tm),
        grid=(n // tm,),
        in_specs=[pl.BlockSpec((kk, tm), lambda i: (0, i), memory_space=pltpu.SMEM),
                  pl.BlockSpec((tm, w), lambda i: (i, 0))],
        out_specs=pl.BlockSpec(memory_space=pl.ANY),
        out_shape=jax.ShapeDtypeStruct((n * kk, w), hp.dtype),
        scratch_shapes=[pltpu.SemaphoreType.DMA],
        compiler_params=_cparams("arbitrary"),
        name="dispatch",
    )(dest, hp)


def _expert_kernel(tile_ref, exp_ref, lo_ref, hi_ref, xs_ref, wg_ref, wu_ref, wd_ref, y_ref, *, tm):
    it = pl.program_id(0)
    prev_tile = tile_ref[jnp.maximum(it - 1, 0)]

    @pl.when(jnp.logical_or(it == 0, tile_ref[it] != prev_tile))
    def _():
        y_ref[...] = jnp.zeros_like(y_ref)

    lo, hi = lo_ref[it], hi_ref[it]

    @pl.when(hi > lo)
    def _():
        bf = jnp.bfloat16
        bits = xs_ref[...]
        xa = pltpu.bitcast(bits & jnp.uint32(0xFFFF0000), jnp.float32).astype(bf)
        xb = pltpu.bitcast(bits << 16, jnp.float32).astype(bf)
        half = bits.shape[1]
        wg, wu = wg_ref[0].astype(bf), wu_ref[0].astype(bf)
        g = _dot(xa, wg[:half]) + _dot(xb, wg[half:])
        u = _dot(xa, wu[:half]) + _dot(xb, wu[half:])
        y = _dot((g * _sigmoid(g) * u).astype(bf), wd_ref[0].astype(bf))
        row = lax.broadcasted_iota(jnp.int32, y.shape, 0)
        y_ref[...] += jnp.where(jnp.logical_and(row >= lo, row < hi), y, 0.0)


def expert_ffn(item_tile, item_expert, item_lo, item_hi, xs, w_gate, w_up, w_down, *, tm):
    rows, half = xs.shape
    d = 2 * half
    n_items = item_tile.shape[0]
    grid_spec = pltpu.PrefetchScalarGridSpec(
        num_scalar_prefetch=4,
        grid=(n_items,),
        in_specs=[pl.BlockSpec((tm, half), lambda i, t, e, lo, hi: (t[i], 0)),
                  pl.BlockSpec((1, d, EXPERT_DIM), lambda i, t, e, lo, hi: (e[i], 0, 0)),
                  pl.BlockSpec((1, d, EXPERT_DIM), lambda i, t, e, lo, hi: (e[i], 0, 0)),
                  pl.BlockSpec((1, EXPERT_DIM, d), lambda i, t, e, lo, hi: (e[i], 0, 0))],
        out_specs=pl.BlockSpec((tm, d), lambda i, t, e, lo, hi: (t[i], 0)),
    )
    return pl.pallas_call(
        functools.partial(_expert_kernel, tm=tm),
        grid_spec=grid_spec,
        out_shape=jax.ShapeDtypeStruct((rows, d), jnp.float32),
        compiler_params=_cparams("arbitrary"),
        name="expert_ffn",
    )(item_tile, item_expert, item_lo, item_hi, xs, w_gate, w_up, w_down)


def expert_work_items(counts, n_rows, tm):
    ends = jnp.cumsum(counts)
    starts = ends - counts
    tile_starts = jnp.arange(n_rows // tm, dtype=jnp.int32) * tm
    bounds = jnp.sort(jnp.concatenate([tile_starts, starts[1:].astype(jnp.int32)]))
    nxt = jnp.concatenate([bounds[1:], jnp.array([n_rows], jnp.int32)])
    tile = jnp.minimum(bounds // tm, n_rows // tm - 1)
    expert = jnp.minimum(jnp.searchsorted(ends, bounds, side="right"), N_EXPERTS - 1).astype(jnp.int32)
    lo = bounds - tile * tm
    hi = jnp.minimum(nxt, (tile + 1) * tm) - tile * tm
    return starts, tile, expert, lo, hi


def _combine_kernel(dest_ref, y_ref, gate_ref, sh_ref, x1_ref, mod_ref, g_ref, o_ref, buf, sem, *, tm):
    def body(t, _):
        for k in range(TOP_K):
            pltpu.make_async_copy(y_ref.at[pl.ds(dest_ref[k, t], 1), :], buf.at[k, pl.ds(t, 1), :], sem).start()
        return 0

    lax.fori_loop(0, tm, body, 0)
    for k in range(TOP_K):
        pltpu.make_async_copy(y_ref.at[pl.ds(0, tm), :], buf.at[k], sem).wait()
    gate = gate_ref[...]
    y = sh_ref[...]
    for k in range(TOP_K):
        y = y + buf[k] * gate[:, k:k + 1]
    o_ref[...] = x1_ref[...] + mod_ref[0, 5:6, :] * _rms_norm(y, g_ref[...])


def combine(dest, y_sorted, gate_nk, shared, x1, mod, g_post_ffn, *, seq, tm=128):
    n, d = x1.shape
    kk = dest.shape[0]
    per_batch = seq // tm
    tok = pl.BlockSpec((tm, d), lambda i: (i, 0))
    return pl.pallas_call(
        functools.partial(_combine_kernel, tm=tm),
        grid=(n // tm,),
        in_specs=[pl.BlockSpec((kk, tm), lambda i: (0, i), memory_space=pltpu.SMEM),
                  pl.BlockSpec(memory_space=pl.ANY),
                  pl.BlockSpec((tm, kk), lambda i: (i, 0)),
                  tok, tok,
                  pl.BlockSpec((1, N_MOD, d), lambda i: (i // per_batch, 0, 0)),
                  pl.BlockSpec((1, d), lambda i: (0, 0))],
        out_specs=tok,
        out_shape=jax.ShapeDtypeStruct((n, d), jnp.float32),
        scratch_shapes=[pltpu.VMEM((kk, tm, d), jnp.float32), pltpu.SemaphoreType.DMA],
        compiler_params=_cparams("arbitrary"),
        name="combine",
    )(dest, y_sorted, gate_nk, shared, x1, mod, g_post_ffn.reshape(1, d))


EXPERT_TILE = 256


def _layer(x, mod, rel_bias, g_pre_mix, w_in, w_branch_a, w_branch_b, w_out, g_post_mix, g_pre_ffn, w_router,
           router_bias, w_exp_gate, w_exp_up, w_exp_down, w_sh_gate, w_sh_up, w_sh_down, g_post_ffn):
    b, s, d = x.shape
    bf = jnp.bfloat16
    scale = HEAD_DIM ** -0.5
    sb_q, sb_k, sb_v, dq, dk, dv, w_ga, w_gb = jnp.split(
        w_in, np.cumsum((SB_WIDTH,) * 3 + (DIL_WIDTH,) * 3 + (d,)).tolist(), axis=1)
    groups = lambda w: [w[:, g * DIL_GROUP_WIDTH:(g + 1) * DIL_GROUP_WIDTH] for g in range(len(DIL_CONFIGS))]
    w_rows = jnp.concatenate([sb_k] + [w * scale for w in groups(dq)] + groups(dk) + groups(dv) + [w_ga, w_gb],
                             axis=1).astype(bf)
    w_cols = jnp.concatenate([sb_q * scale, sb_v], axis=1).T.astype(bf)
    qT, k, vT3, *rest = in_projection(x, mod, g_pre_mix, w_rows, w_cols)
    dqs, dks, dvs, gate_a, gate_b = rest[0:3], rest[3:6], rest[6:9], rest[9], rest[10]
    out_a = sb_attention(qT, k, vT3)
    o_g, lse_g = [], []
    for g, (window, dilation) in enumerate(DIL_CONFIGS):
        bias = dilated_bias(rel_bias[:, g * DIL_HEADS_PER_GROUP:(g + 1) * DIL_HEADS_PER_GROUP], window, dilation)
        o, lse = dilated_attention(dqs[g], dks[g], dvs[g], bias, dilation)
        o_g.append(o)
        lse_g.append(lse)
    x1, hp, shared, logits_t = mix_out(
        out_a, o_g, lse_g, gate_a, gate_b, x, mod, w_branch_a.astype(bf), w_branch_b.astype(bf), w_out.astype(bf),
        g_post_mix, g_pre_ffn, w_router.T.astype(bf), jnp.concatenate([w_sh_gate, w_sh_up], axis=1).astype(bf),
        w_sh_down.astype(bf))
    n = b * s
    idx, gate, rank, counts = route(logits_t, router_bias)
    starts, item_tile, item_expert, item_lo, item_hi = expert_work_items(counts[:, 0], n * TOP_K, EXPERT_TILE)
    dest = slot_destinations(idx, rank, starts)
    xs = dispatch(dest, hp.reshape(n, d // 2))
    y_sorted = expert_ffn(item_tile, item_expert, item_lo, item_hi, xs, w_exp_gate, w_exp_up, w_exp_down,
                          tm=EXPERT_TILE)
    out = combine(dest, y_sorted, gate.T, shared.reshape(n, d), x1.reshape(n, d), mod, g_post_ffn, seq=s)
    return out.reshape(b, s, d)


def kernel(x, c, rel_bias, w_ada, b_ada, g_pre_mix, w_in, w_branch_a, w_branch_b, w_out, g_post_mix, g_pre_ffn,
           w_router, router_bias, w_exp_gate, w_exp_up, w_exp_down, w_sh_gate, w_sh_up, w_sh_down, g_post_ffn):
    depth = w_in.shape[0]
    for layer in range(depth):
        mod = ada_modulation(c, w_ada[layer], b_ada[layer]).reshape(x.shape[0], N_MOD, x.shape[2])
        x = _layer(x, mod, rel_bias, g_pre_mix[layer], w_in[layer], w_branch_a[layer], w_branch_b[layer], w_out[layer],
                   g_post_mix[layer], g_pre_ffn[layer], w_router[layer], router_bias[layer], w_exp_gate[layer],
                   w_exp_up[layer], w_exp_down[layer], w_sh_gate[layer], w_sh_up[layer], w_sh_down[layer],
                   g_post_ffn[layer])
    return x
```

```python
import functools
import math

import jax
import jax.numpy as jnp
import numpy as np
from jax import lax
from jax.experimental import pallas as pl
from jax.experimental.pallas import tpu as pltpu

D_MODEL = 1024
HEAD_DIM = 64
SB_HEADS = 8
SB_WIDTH = SB_HEADS * HEAD_DIM
DIL_CONFIGS = ((128, 1), (512, 4), (2048, 16))
DIL_HEADS_PER_GROUP = 4
DIL_GROUP_WIDTH = DIL_HEADS_PER_GROUP * HEAD_DIM
DIL_WIDTH = DIL_GROUP_WIDTH * len(DIL_CONFIGS)
Q_BLOCK = 128
N_BUCKETS = 32
MAX_DISTANCE = 2048
N_EXPERTS = 256
TOP_K = 8
N_GROUPS = 8
TOP_K_GROUPS = 4
EXPERT_DIM = 256
ROUTED_SCALE = 2.5
N_MOD = 6
EPS = 1e-6

LANES = 128
VMEM_LIMIT_BYTES = 56 * 1024 * 1024
NEG_INF = float("-inf")


def _cparams(*semantics):
    return pltpu.CompilerParams(dimension_semantics=semantics, vmem_limit_bytes=VMEM_LIMIT_BYTES)


def _rms_norm(v, g):
    return v * lax.rsqrt(jnp.mean(v * v, axis=-1, keepdims=True) + EPS) * g


def _sigmoid(v):
    return 1.0 / (1.0 + jnp.exp(-v))


def _dot(a, b):
    return jnp.dot(a, b, preferred_element_type=jnp.float32)


def _dot_nt(a, b):
    return lax.dot_general(a, b, (((1,), (1,)), ((), ())), preferred_element_type=jnp.float32)


def _split_bf16(v):
    hi = v.astype(jnp.bfloat16)
    lo = (v - hi.astype(jnp.float32)).astype(jnp.bfloat16)
    return hi, lo


def _ada_kernel(c_ref, w_ref, b_ref, o_ref):
    c = c_ref[...]
    a_hi, a_lo = _split_bf16(c * _sigmoid(c))
    w_hi, w_lo = _split_bf16(w_ref[...])
    o_ref[...] = _dot(a_hi, w_hi) + _dot(a_lo, w_hi) + _dot(a_hi, w_lo) + b_ref[...]


def ada_modulation(c, w_ada, b_ada, *, tn=1536):
    b, d = c.shape
    n = w_ada.shape[1]
    return pl.pallas_call(
        _ada_kernel,
        grid=(n // tn,),
        in_specs=[
            pl.BlockSpec((b, d), lambda j: (0, 0)),
            pl.BlockSpec((d, tn), lambda j: (0, j)),
            pl.BlockSpec((1, tn), lambda j: (0, j)),
        ],
        out_specs=pl.BlockSpec((b, tn), lambda j: (0, j)),
        out_shape=jax.ShapeDtypeStruct((b, n), jnp.float32),
        compiler_params=_cparams("parallel"),
        name="ada_modulation",
    )(c, w_ada, b_ada.reshape(1, n))


def _in_proj_kernel(x_ref, mod_ref, g_ref, w_rows_ref, w_cols_ref, qT_ref, k_ref, vT_ref, *rest, tm):
    dil_refs, ga_ref, gb_ref, stage_ref = rest[:9], rest[9], rest[10], rest[11]
    h = _rms_norm(x_ref[0], g_ref[...]) * (1.0 + mod_ref[0, 1:2, :]) + mod_ref[0, 0:1, :]
    hb = h.astype(jnp.bfloat16)
    off = 0
    k_ref[0] = _dot(hb, w_rows_ref[:, off:off + SB_WIDTH]).astype(k_ref.dtype)
    off += SB_WIDTH
    for n, r in enumerate(dil_refs):
        y = _dot(hb, w_rows_ref[:, off:off + DIL_GROUP_WIDTH])
        off += DIL_GROUP_WIDTH
        dilation = DIL_CONFIGS[n % len(DIL_CONFIGS)][1]
        if dilation == 1:
            r[0] = y.astype(r.dtype)
            continue
        for j in range(DIL_GROUP_WIDTH // LANES):
            stage_ref[j] = y[:, j * LANES:(j + 1) * LANES]
        for c in range(dilation):
            for j in range(DIL_GROUP_WIDTH // LANES):
                col = c * DIL_GROUP_WIDTH + j * LANES
                r[0, :, col:col + LANES] = stage_ref[j, pl.ds(c, tm // dilation, stride=dilation), :].astype(r.dtype)
    for r in (ga_ref, gb_ref):
        r[0] = _sigmoid(_dot(hb, w_rows_ref[:, off:off + D_MODEL])).astype(r.dtype)
        off += D_MODEL
    qT_ref[0] = _dot_nt(w_cols_ref[0:SB_WIDTH, :], hb).astype(qT_ref.dtype)
    vT = _dot_nt(w_cols_ref[SB_WIDTH:2 * SB_WIDTH, :], hb).astype(vT_ref.dtype)
    for p in range(SB_WIDTH // LANES):
        for jb in range(tm // LANES):
            vT_ref[0, p, jb] = vT[p * LANES:(p + 1) * LANES, jb * LANES:(jb + 1) * LANES]


def _dilated_shape(b, s, dilation, dtype):
    return jax.ShapeDtypeStruct((b, s // dilation, dilation * DIL_GROUP_WIDTH), dtype)


def _dilated_spec(tm, dilation):
    return pl.BlockSpec((1, tm // dilation, dilation * DIL_GROUP_WIDTH), lambda bi, i: (bi, i, 0))


def in_projection(x, mod, g_pre, w_rows, w_cols, *, tm=256):
    b, s, d = x.shape
    nt = s // tm
    bf = jnp.bfloat16
    tok = lambda width: pl.BlockSpec((1, tm, width), lambda bi, i: (bi, i, 0))
    out_shapes = [
        jax.ShapeDtypeStruct((b, SB_WIDTH, s), bf),
        jax.ShapeDtypeStruct((b, s, SB_WIDTH), bf),
        jax.ShapeDtypeStruct((b, SB_WIDTH // LANES, s // LANES, LANES, LANES), bf),
    ] + [_dilated_shape(b, s, r, bf) for _, r in DIL_CONFIGS] * 3 + [jax.ShapeDtypeStruct((b, s, d), bf)] * 2
    out_specs = [
        pl.BlockSpec((1, SB_WIDTH, tm), lambda bi, i: (bi, 0, i)),
        tok(SB_WIDTH),
        pl.BlockSpec((1, SB_WIDTH // LANES, tm // LANES, LANES, LANES), lambda bi, i: (bi, 0, i, 0, 0)),
    ] + [_dilated_spec(tm, r) for _, r in DIL_CONFIGS] * 3 + [tok(d)] * 2
    return pl.pallas_call(
        functools.partial(_in_proj_kernel, tm=tm),
        grid=(b, nt),
        in_specs=[
            tok(d),
            pl.BlockSpec((1, N_MOD, d), lambda bi, i: (bi, 0, 0)),
            pl.BlockSpec((1, d), lambda bi, i: (0, 0)),
            pl.BlockSpec(w_rows.shape, lambda bi, i: (0, 0)),
            pl.BlockSpec(w_cols.shape, lambda bi, i: (0, 0)),
        ],
        out_specs=out_specs,
        out_shape=out_shapes,
        scratch_shapes=[pltpu.VMEM((DIL_GROUP_WIDTH // LANES, tm, LANES), jnp.float32)],
        compiler_params=_cparams("parallel", "parallel"),
        name="in_projection",
    )(x, mod, g_pre.reshape(1, d), w_rows, w_cols)


SB_TILE = 512
SB_SUB = 256


def _sb_chunk(kc, vc, q_h, tri, csum, past):
    z = _dot(kc, q_h)
    log_keep = -(jnp.maximum(z, 0.0) + jnp.log(1.0 + jnp.exp(-jnp.abs(z))))
    if past is not None:
        log_keep = jnp.where(past, log_keep, 0.0)
    weights = [None] * (SB_TILE // SB_SUB)
    for sb in reversed(range(SB_TILE // SB_SUB)):
        rows = slice(sb * SB_SUB, (sb + 1) * SB_SUB)
        lk = log_keep[rows]
        hi, lo = _split_bf16(lk)
        later = _dot(tri, hi) + _dot(tri, lo)
        w = jnp.exp(z[rows] + lk + later + csum)
        if past is not None:
            w = jnp.where(past[rows], w, 0.0)
        weights[sb] = w.astype(jnp.bfloat16)
        csum = csum + jnp.sum(lk, axis=0, keepdims=True)
    return _dot(vc, jnp.concatenate(weights, axis=0)), csum


def _sb_attn_kernel(qT_ref, k_ref, vT_ref, tri_ref, o_ref, acc_ref, csum_ref):
    i = pl.program_id(2)
    t = SB_TILE
    q2 = qT_ref[0]
    row = lax.broadcasted_iota(jnp.int32, q2.shape, 0)
    zero = jnp.zeros_like(q2)
    q_heads = (jnp.where(row < HEAD_DIM, q2, zero), jnp.where(row >= HEAD_DIM, q2, zero))
    tri = tri_ref[...]
    blocks = t // LANES

    def chunk(c, past, first):
        kc = k_ref[0, pl.ds(pl.multiple_of(c * t, t), t), :]
        vc = jnp.concatenate([vT_ref[0, 0, c * blocks + jb] for jb in range(blocks)], axis=1)
        for h in range(2):
            rows = slice(h * HEAD_DIM, (h + 1) * HEAD_DIM)
            csum = jnp.zeros((1, t), jnp.float32) if first else csum_ref[h:h + 1, :]
            part, csum = _sb_chunk(kc, vc[rows], q_heads[h], tri, csum, past)
            acc_ref[rows, :] = part if first else acc_ref[rows, :] + part
            csum_ref[h:h + 1, :] = csum

    key = lax.broadcasted_iota(jnp.int32, (t, t), 0)
    query = lax.broadcasted_iota(jnp.int32, (t, t), 1)
    chunk(i, key < query, True)

    def body(it, carry):
        chunk(i - 1 - it, None, False)
        return carry

    lax.fori_loop(0, i, body, 0)
    o_ref[0] = acc_ref[...].T.astype(o_ref.dtype)


def sb_attention(qT, k, vT3):
    b, width, s = qT.shape
    n_pairs = width // LANES
    t = SB_TILE
    tri = jnp.asarray(np.triu(np.ones((SB_SUB, SB_SUB), np.float32), k=1), jnp.bfloat16)
    return pl.pallas_call(
        _sb_attn_kernel,
        grid=(b, n_pairs, s // t),
        in_specs=[
            pl.BlockSpec((1, LANES, t), lambda bi, p, i: (bi, p, i)),
            pl.BlockSpec((1, s, LANES), lambda bi, p, i: (bi, 0, p)),
            pl.BlockSpec((1, 1, s // LANES, LANES, LANES), lambda bi, p, i: (bi, p, 0, 0, 0)),
            pl.BlockSpec((SB_SUB, SB_SUB), lambda bi, p, i: (0, 0)),
        ],
        out_specs=pl.BlockSpec((1, t, LANES), lambda bi, p, i: (bi, i, p)),
        out_shape=jax.ShapeDtypeStruct((b, s, width), jnp.bfloat16),
        scratch_shapes=[pltpu.VMEM((LANES, t), jnp.float32), pltpu.VMEM((2, t), jnp.float32)],
        compiler_params=_cparams("parallel", "parallel", "arbitrary"),
        name="sb_attention",
    )(qT, k, vT3, tri)


def _dil_attn_kernel(q_ref, kp_ref, kc_ref, vp_ref, vc_ref, bias_ref, o_ref, lse_ref):
    m = pl.program_id(2)
    q4 = q_ref[0]
    kk = jnp.concatenate([kp_ref[0], kc_ref[0]], axis=0)
    vv = jnp.concatenate([vp_ref[0], vc_ref[0]], axis=0)
    lane = lax.broadcasted_iota(jnp.int32, q4.shape, 1)
    zero = jnp.zeros_like(q4)
    in_head = [jnp.logical_and(lane >= h * HEAD_DIM, lane < (h + 1) * HEAD_DIM) for h in range(DIL_HEADS_PER_GROUP)]
    q_stack = jnp.concatenate([jnp.where(mask, q4, zero) for mask in in_head], axis=0)
    logits = _dot_nt(q_stack, kk) + bias_ref[...]
    key = lax.broadcasted_iota(jnp.int32, logits.shape, 1)
    logits = jnp.where(jnp.logical_or(key >= Q_BLOCK, m > 0), logits, NEG_INF)
    mx = jnp.max(logits, axis=1, keepdims=True)
    p = jnp.exp(logits - mx)
    den = jnp.sum(p, axis=1, keepdims=True)
    pv = _dot(p.astype(jnp.bfloat16), vv) / den
    lse_rows = mx + jnp.log(den)
    out = jnp.zeros(q4.shape, jnp.float32)
    lse = jnp.zeros(q4.shape, jnp.float32)
    for h, mask in enumerate(in_head):
        rows = slice(h * Q_BLOCK, (h + 1) * Q_BLOCK)
        out = jnp.where(mask, pv[rows], out)
        lse = jnp.where(mask, lse_rows[rows], lse)
    o_ref[0] = out.astype(o_ref.dtype)
    lse_ref[0] = lse


def dilated_attention(q, k, v, bias, dilation):
    b, length, width = q.shape
    gw = DIL_GROUP_WIDTH
    cur = pl.BlockSpec((1, Q_BLOCK, gw), lambda bi, c, m: (bi, m, c))
    prev = pl.BlockSpec((1, Q_BLOCK, gw), lambda bi, c, m: (bi, jnp.maximum(m - 1, 0), c))
    return pl.pallas_call(
        _dil_attn_kernel,
        grid=(b, width // gw, length // Q_BLOCK),
        in_specs=[cur, prev, cur, prev, cur,
                  pl.BlockSpec((DIL_HEADS_PER_GROUP * Q_BLOCK, 2 * Q_BLOCK), lambda bi, c, m: (0, 0))],
        out_specs=[cur, cur],
        out_shape=[jax.ShapeDtypeStruct((b, length, width), jnp.bfloat16),
                   jax.ShapeDtypeStruct((b, length, width), jnp.float32)],
        compiler_params=_cparams("parallel", "parallel", "arbitrary"),
        name=f"dilated_attention_r{dilation}",
    )(q, k, k, v, v, bias.reshape(DIL_HEADS_PER_GROUP * Q_BLOCK, 2 * Q_BLOCK))


def _t5_causal_bucket(dist):
    max_exact = N_BUCKETS // 2
    d = jnp.maximum(dist, 1).astype(jnp.float32)
    large = max_exact + (jnp.log(d / max_exact) / math.log(MAX_DISTANCE / max_exact)
                         * (N_BUCKETS - max_exact)).astype(jnp.int32)
    large = jnp.minimum(large, N_BUCKETS - 1)
    return jnp.where(dist < max_exact, dist, large)


def dilated_bias(rel_bias_group, window, dilation):
    qi = jnp.arange(Q_BLOCK)[:, None]
    kj = jnp.arange(2 * Q_BLOCK)[None, :]
    dist_sub = qi + Q_BLOCK - kj
    bucket = _t5_causal_bucket(jnp.maximum(dist_sub, 0) * dilation)
    onehot = (bucket[..., None] == jnp.arange(N_BUCKETS)).astype(jnp.float32)
    bias = jnp.einsum("qkn,nh->hqk", onehot, rel_bias_group.astype(jnp.float32), precision=lax.Precision.HIGHEST)
    valid = (dist_sub >= 0) & (dist_sub <= window // dilation)
    return jnp.where(valid[None], bias, NEG_INF)


def _mix_out_kernel(oa_ref, o0_ref, o1_ref, o2_ref, l0_ref, l1_ref, l2_ref, ga_ref, gb_ref, x_ref, mod_ref,
                    wa_ref, wb_ref, wo_ref, gpost_ref, gpre_ref, wr_ref, wsgu_ref, wsd_ref,
                    x1_ref, hp_ref, sh_ref, lt_ref, stage_ref):
    f32 = jnp.float32
    tm = x_ref.shape[1]

    def token_major(ref, dilation):
        if dilation == 1:
            return ref[0].astype(f32)
        slabs = DIL_GROUP_WIDTH // LANES
        for c in range(dilation):
            for j in range(slabs):
                col = c * DIL_GROUP_WIDTH + j * LANES
                stage_ref[j, pl.ds(c, tm // dilation, stride=dilation), :] = ref[0, :, col:col + LANES].astype(f32)
        return jnp.concatenate([stage_ref[j] for j in range(slabs)], axis=1)

    dils = [r for _, r in DIL_CONFIGS]
    l0, l1, l2 = (token_major(ref, r) for ref, r in zip((l0_ref, l1_ref, l2_ref), dils))
    mx = jnp.maximum(jnp.maximum(l0, l1), l2)
    e0, e1, e2 = jnp.exp(l0 - mx), jnp.exp(l1 - mx), jnp.exp(l2 - mx)
    den = e0 + e1 + e2
    o0, o1, o2 = (token_major(ref, r) for ref, r in zip((o0_ref, o1_ref, o2_ref), dils))
    out_b = (e0 / den) * o0 + (e1 / den) * o1 + (e2 / den) * o2
    ya = _dot(oa_ref[0], wa_ref[...])
    yb = _dot(out_b.astype(jnp.bfloat16), wb_ref[...])
    merged = ga_ref[0].astype(f32) * ya + gb_ref[0].astype(f32) * yb
    y = _dot(merged.astype(jnp.bfloat16), wo_ref[...])
    x1 = x_ref[0] + mod_ref[0, 2:3, :] * _rms_norm(y, gpost_ref[...])
    x1_ref[0] = x1
    h = _rms_norm(x1, gpre_ref[...]) * (1.0 + mod_ref[0, 4:5, :]) + mod_ref[0, 3:4, :]
    hb = h.astype(jnp.bfloat16)
    half = hb.shape[1] // 2
    bits = pltpu.bitcast(hb.astype(f32), jnp.uint32)
    hp_ref[0] = bits[:, :half] | (bits[:, half:] >> 16)
    lt_ref[...] = _dot_nt(wr_ref[...], hb)
    gu = _dot(hb, wsgu_ref[...])
    g, u = gu[:, :EXPERT_DIM], gu[:, EXPERT_DIM:]
    sh_ref[0] = _dot((g * _sigmoid(g) * u).astype(jnp.bfloat16), wsd_ref[...])


def mix_out(out_a, o_g, lse_g, gate_a, gate_b, x, mod, w_a, w_b, w_o, g_post, g_pre_ffn, w_rT, w_sgu, w_sd, *, tm=256):
    b, s, d = x.shape
    tok = lambda width: pl.BlockSpec((1, tm, width), lambda bi, i: (bi, i, 0))
    full = lambda a: pl.BlockSpec(a.shape, lambda bi, i: (0,) * a.ndim)
    nt = s // tm
    g_post, g_pre_ffn = g_post.reshape(1, d), g_pre_ffn.reshape(1, d)
    weights = (w_a, w_b, w_o, g_post, g_pre_ffn, w_rT, w_sgu, w_sd)
    return pl.pallas_call(
        _mix_out_kernel,
        grid=(b, nt),
        in_specs=[tok(SB_WIDTH)] + [_dilated_spec(tm, r) for _, r in DIL_CONFIGS] * 2 + [tok(d)] * 3
        + [pl.BlockSpec((1, N_MOD, d), lambda bi, i: (bi, 0, 0))] + [full(w) for w in weights],
        out_specs=[tok(d), tok(d // 2), tok(d), pl.BlockSpec((N_EXPERTS, tm), lambda bi, i: (0, bi * nt + i))],
        out_shape=[jax.ShapeDtypeStruct((b, s, d), jnp.float32),
                   jax.ShapeDtypeStruct((b, s, d // 2), jnp.uint32),
                   jax.ShapeDtypeStruct((b, s, d), jnp.float32),
                   jax.ShapeDtypeStruct((N_EXPERTS, b * s), jnp.float32)],
        scratch_shapes=[pltpu.VMEM((DIL_GROUP_WIDTH // LANES, tm, LANES), jnp.float32)],
        compiler_params=_cparams("parallel", "parallel"),
        name="mix_out",
    )(out_a, *o_g, *lse_g, gate_a, gate_b, x, mod, *weights)


def _route_kernel(lt_ref, bias_ref, tri_ref, idx_ref, gate_ref, rank_ref, cnt_ref, base_ref, *, tn):
    f32 = jnp.float32

    @pl.when(pl.program_id(0) == 0)
    def _():
        base_ref[...] = jnp.zeros_like(base_ref)

    per_group = N_EXPERTS // N_GROUPS
    scores = _sigmoid(lt_ref[...])
    sel = scores + bias_ref[...]
    sel_groups = [sel[g * per_group:(g + 1) * per_group] for g in range(N_GROUPS)]
    ri = lax.broadcasted_iota(jnp.int32, (per_group, tn), 0)
    group_scores = []
    for sg in sel_groups:
        m1 = jnp.max(sg, axis=0, keepdims=True)
        first = jnp.min(jnp.where(sg == m1, ri, per_group), axis=0, keepdims=True)
        group_scores.append(m1 + jnp.max(jnp.where(ri == first, NEG_INF, sg), axis=0, keepdims=True))
    cur = jnp.concatenate(group_scores, axis=0)
    gi = lax.broadcasted_iota(jnp.int32, cur.shape, 0)
    keep = jnp.zeros(cur.shape, jnp.float32)
    for _ in range(TOP_K_GROUPS):
        mg = jnp.max(cur, axis=0, keepdims=True)
        pick = gi == jnp.min(jnp.where(cur == mg, gi, N_GROUPS), axis=0, keepdims=True)
        keep = jnp.where(pick, 1.0, keep)
        cur = jnp.where(pick, NEG_INF, cur)
    cur = jnp.concatenate([jnp.where(keep[g:g + 1, :] > 0.0, sg, NEG_INF) for g, sg in enumerate(sel_groups)], axis=0)
    ei = lax.broadcasted_iota(jnp.int32, cur.shape, 0)
    picks, idxs, gates = [], [], []
    for _ in range(TOP_K):
        mk = jnp.max(cur, axis=0, keepdims=True)
        fk = jnp.min(jnp.where(cur == mk, ei, N_EXPERTS), axis=0, keepdims=True)
        pick = ei == fk
        picks.append(pick)
        idxs.append(fk)
        gates.append(jnp.sum(jnp.where(pick, scores, 0.0), axis=0, keepdims=True))
        cur = jnp.where(pick, NEG_INF, cur)
    gate = jnp.concatenate(gates, axis=0)
    gate_ref[...] = gate / jnp.sum(gate, axis=0, keepdims=True) * ROUTED_SCALE
    idx_ref[...] = jnp.concatenate(idxs, axis=0)
    chosen = functools.reduce(jnp.logical_or, picks)
    onehot = jnp.where(chosen, 1.0, 0.0)
    earlier = _dot(onehot.astype(jnp.bfloat16), tri_ref[...]) + base_ref[...]
    ranks = [jnp.sum(jnp.where(p, earlier, 0.0), axis=0, keepdims=True) for p in picks]
    rank_ref[...] = jnp.concatenate(ranks, axis=0).astype(jnp.int32)
    base = base_ref[...] + jnp.sum(onehot, axis=1, keepdims=True)
    base_ref[...] = base
    cnt_ref[...] = jnp.broadcast_to(base, cnt_ref.shape).astype(jnp.int32)


def route(logits_t, router_bias, *, tn=512):
    ne, n = logits_t.shape
    tri = jnp.asarray(np.triu(np.ones((tn, tn), np.float32), k=1), jnp.bfloat16)
    slot = pl.BlockSpec((TOP_K, tn), lambda i: (0, i))
    return pl.pallas_call(
        functools.partial(_route_kernel, tn=tn),
        grid=(n // tn,),
        in_specs=[pl.BlockSpec((ne, tn), lambda i: (0, i)),
                  pl.BlockSpec((ne, 1), lambda i: (0, 0)),
                  pl.BlockSpec((tn, tn), lambda i: (0, 0))],
        out_specs=[slot, slot, slot, pl.BlockSpec((ne, LANES), lambda i: (0, 0))],
        out_shape=[jax.ShapeDtypeStruct((TOP_K, n), jnp.int32),
                   jax.ShapeDtypeStruct((TOP_K, n), jnp.float32),
                   jax.ShapeDtypeStruct((TOP_K, n), jnp.int32),
                   jax.ShapeDtypeStruct((ne, LANES), jnp.int32)],
        scratch_shapes=[pltpu.VMEM((ne, 1), jnp.float32)],
        compiler_params=_cparams("arbitrary"),
        name="route",
    )(logits_t, router_bias.reshape(ne, 1), tri)


def _dest_kernel(idx_ref, rank_ref, start_ref, dest_ref):
    idx = idx_ref[...]
    ei = lax.broadcasted_iota(jnp.int32, (N_EXPERTS, idx.shape[1]), 0)
    start = start_ref[...]
    rows = [jnp.sum(jnp.where(ei == idx[k:k + 1, :], start, 0.0), axis=0, keepdims=True) for k in range(TOP_K)]
    dest_ref[...] = jnp.concatenate(rows, axis=0).astype(jnp.int32) + rank_ref[...]


def slot_destinations(idx, rank, starts, *, tn=512):
    kk, n = idx.shape
    slot = pl.BlockSpec((kk, tn), lambda i: (0, i))
    return pl.pallas_call(
        _dest_kernel,
        grid=(n // tn,),
        in_specs=[slot, slot, pl.BlockSpec((N_EXPERTS, 1), lambda i: (0, 0))],
        out_specs=slot,
        out_shape=jax.ShapeDtypeStruct((kk, n), jnp.int32),
        compiler_params=_cparams("parallel"),
        name="slot_destinations",
    )(idx, rank, starts.astype(jnp.float32).reshape(N_EXPERTS, 1))


def _dispatch_kernel(dest_ref, h_ref, xs_ref, sem, *, tm):
    def body(t, _):
        for k in range(TOP_K):
            pltpu.make_async_copy(h_ref.at[pl.ds(t, 1), :], xs_ref.at[pl.ds(dest_ref[k, t], 1), :], sem).start()
        return 0

    lax.fori_loop(0, tm, body, 0)
    for _ in range(TOP_K):
        pltpu.make_async_copy(h_ref, xs_ref.at[pl.ds(0, tm), :], sem).wait()


def dispatch(dest, hp, *, tm=512):
    n, w = hp.shape
    kk = dest.shape[0]
    return pl.pallas_call(
        functools.partial(_dispatch_kernel, tm=tm),
        grid=(n // tm,),
        in_specs=[pl.BlockSpec((kk, tm), lambda i: (0, i), memory_space=pltpu.SMEM),
                  pl.BlockSpec((tm, w), lambda i: (i, 0))],
        out_specs=pl.BlockSpec(memory_space=pl.ANY),
        out_shape=jax.ShapeDtypeStruct((n * kk, w), hp.dtype),
        scratch_shapes=[pltpu.SemaphoreType.DMA],
        compiler_params=_cparams("arbitrary"),
        name="dispatch",
    )(dest, hp)


def _expert_kernel(tile_ref, exp_ref, lo_ref, hi_ref, xs_ref, wg_ref, wu_ref, wd_ref, y_ref, wgu_bf, wd_bf, *, tm):
    it = pl.program_id(0)
    prev = jnp.maximum(it - 1, 0)

    @pl.when(jnp.logical_or(it == 0, tile_ref[it] != tile_ref[prev]))
    def _():
        y_ref[...] = jnp.zeros_like(y_ref)

    @pl.when(jnp.logical_or(it == 0, exp_ref[it] != exp_ref[prev]))
    def _():
        wgu_bf[:, :EXPERT_DIM] = wg_ref[0].astype(wgu_bf.dtype)
        wgu_bf[:, EXPERT_DIM:] = wu_ref[0].astype(wgu_bf.dtype)
        wd_bf[...] = wd_ref[0].astype(wd_bf.dtype)

    lo, hi = lo_ref[it], hi_ref[it]

    @pl.when(hi > lo)
    def _():
        bf = jnp.bfloat16
        bits = xs_ref[...]
        xa = pltpu.bitcast(bits & jnp.uint32(0xFFFF0000), jnp.float32).astype(bf)
        xb = pltpu.bitcast(bits << 16, jnp.float32).astype(bf)
        half = bits.shape[1]
        gu = _dot(xa, wgu_bf[:half, :]) + _dot(xb, wgu_bf[half:, :])
        g, u = gu[:, :EXPERT_DIM], gu[:, EXPERT_DIM:]
        y = _dot((g * _sigmoid(g) * u).astype(bf), wd_bf[...])
        row = lax.broadcasted_iota(jnp.int32, y.shape, 0)
        y_ref[...] += jnp.where(jnp.logical_and(row >= lo, row < hi), y, 0.0)


def expert_ffn(item_tile, item_expert, item_lo, item_hi, xs, w_gate, w_up, w_down, *, tm):
    rows, half = xs.shape
    d = 2 * half
    n_items = item_tile.shape[0]
    grid_spec = pltpu.PrefetchScalarGridSpec(
        num_scalar_prefetch=4,
        grid=(n_items,),
        in_specs=[pl.BlockSpec((tm, half), lambda i, t, e, lo, hi: (t[i], 0)),
                  pl.BlockSpec((1, d, EXPERT_DIM), lambda i, t, e, lo, hi: (e[i], 0, 0)),
                  pl.BlockSpec((1, d, EXPERT_DIM), lambda i, t, e, lo, hi: (e[i], 0, 0)),
                  pl.BlockSpec((1, EXPERT_DIM, d), lambda i, t, e, lo, hi: (e[i], 0, 0))],
        out_specs=pl.BlockSpec((tm, d), lambda i, t, e, lo, hi: (t[i], 0)),
        scratch_shapes=[pltpu.VMEM((d, 2 * EXPERT_DIM), jnp.bfloat16), pltpu.VMEM((EXPERT_DIM, d), jnp.bfloat16)],
    )
    return pl.pallas_call(
        functools.partial(_expert_kernel, tm=tm),
        grid_spec=grid_spec,
        out_shape=jax.ShapeDtypeStruct((rows, d), jnp.float32),
        compiler_params=_cparams("arbitrary"),
        name="expert_ffn",
    )(item_tile, item_expert, item_lo, item_hi, xs, w_gate, w_up, w_down)


def expert_work_items(counts, n_rows, tm):
    ends = jnp.cumsum(counts)
    starts = ends - counts
    tile_starts = jnp.arange(n_rows // tm, dtype=jnp.int32) * tm
    bounds = jnp.sort(jnp.concatenate([tile_starts, starts[1:].astype(jnp.int32)]))
    nxt = jnp.concatenate([bounds[1:], jnp.array([n_rows], jnp.int32)])
    tile = jnp.minimum(bounds // tm, n_rows // tm - 1)
    expert = jnp.minimum(jnp.searchsorted(ends, bounds, side="right"), N_EXPERTS - 1).astype(jnp.int32)
    lo = bounds - tile * tm
    hi = jnp.minimum(nxt, (tile + 1) * tm) - tile * tm
    return starts, tile, expert, lo, hi


def _combine_kernel(dest_ref, y_ref, gate_ref, sh_ref, x1_ref, mod_ref, g_ref, o_ref, buf, sem, *, tm):
    def body(t, _):
        for k in range(TOP_K):
            pltpu.make_async_copy(y_ref.at[pl.ds(dest_ref[k, t], 1), :], buf.at[k, pl.ds(t, 1), :], sem).start()
        return 0

    lax.fori_loop(0, tm, body, 0)
    for k in range(TOP_K):
        pltpu.make_async_copy(y_ref.at[pl.ds(0, tm), :], buf.at[k], sem).wait()
    gate = gate_ref[...]
    y = sh_ref[...]
    for k in range(TOP_K):
        y = y + buf[k] * gate[:, k:k + 1]
    o_ref[...] = x1_ref[...] + mod_ref[0, 5:6, :] * _rms_norm(y, g_ref[...])


def combine(dest, y_sorted, gate_nk, shared, x1, mod, g_post_ffn, *, seq, tm=128):
    n, d = x1.shape
    kk = dest.shape[0]
    per_batch = seq // tm
    tok = pl.BlockSpec((tm, d), lambda i: (i, 0))
    return pl.pallas_call(
        functools.partial(_combine_kernel, tm=tm),
        grid=(n // tm,),
        in_specs=[pl.BlockSpec((kk, tm), lambda i: (0, i), memory_space=pltpu.SMEM),
                  pl.BlockSpec(memory_space=pl.ANY),
                  pl.BlockSpec((tm, kk), lambda i: (i, 0)),
                  tok, tok,
                  pl.BlockSpec((1, N_MOD, d), lambda i: (i // per_batch, 0, 0)),
                  pl.BlockSpec((1, d), lambda i: (0, 0))],
        out_specs=tok,
        out_shape=jax.ShapeDtypeStruct((n, d), jnp.float32),
        scratch_shapes=[pltpu.VMEM((kk, tm, d), jnp.float32), pltpu.SemaphoreType.DMA],
        compiler_params=_cparams("arbitrary"),
        name="combine",
    )(dest, y_sorted, gate_nk, shared, x1, mod, g_post_ffn.reshape(1, d))


EXPERT_TILE = 256


def _layer(x, mod, rel_bias, g_pre_mix, w_in, w_branch_a, w_branch_b, w_out, g_post_mix, g_pre_ffn, w_router,
           router_bias, w_exp_gate, w_exp_up, w_exp_down, w_sh_gate, w_sh_up, w_sh_down, g_post_ffn):
    b, s, d = x.shape
    bf = jnp.bfloat16
    scale = HEAD_DIM ** -0.5
    sb_q, sb_k, sb_v, dq, dk, dv, w_ga, w_gb = jnp.split(
        w_in, np.cumsum((SB_WIDTH,) * 3 + (DIL_WIDTH,) * 3 + (d,)).tolist(), axis=1)
    groups = lambda w: [w[:, g * DIL_GROUP_WIDTH:(g + 1) * DIL_GROUP_WIDTH] for g in range(len(DIL_CONFIGS))]
    w_rows = jnp.concatenate([sb_k] + [w * scale for w in groups(dq)] + groups(dk) + groups(dv) + [w_ga, w_gb],
                             axis=1).astype(bf)
    w_cols = jnp.concatenate([sb_q * scale, sb_v], axis=1).T.astype(bf)
    qT, k, vT3, *rest = in_projection(x, mod, g_pre_mix, w_rows, w_cols)
    dqs, dks, dvs, gate_a, gate_b = rest[0:3], rest[3:6], rest[6:9], rest[9], rest[10]
    out_a = sb_attention(qT, k, vT3)
    o_g, lse_g = [], []
    for g, (window, dilation) in enumerate(DIL_CONFIGS):
        bias = dilated_bias(rel_bias[:, g * DIL_HEADS_PER_GROUP:(g + 1) * DIL_HEADS_PER_GROUP], window, dilation)
        o, lse = dilated_attention(dqs[g], dks[g], dvs[g], bias, dilation)
        o_g.append(o)
        lse_g.append(lse)
    x1, hp, shared, logits_t = mix_out(
        out_a, o_g, lse_g, gate_a, gate_b, x, mod, w_branch_a.astype(bf), w_branch_b.astype(bf), w_out.astype(bf),
        g_post_mix, g_pre_ffn, w_router.T.astype(bf), jnp.concatenate([w_sh_gate, w_sh_up], axis=1).astype(bf),
        w_sh_down.astype(bf))
    n = b * s
    idx, gate, rank, counts = route(logits_t, router_bias)
    starts, item_tile, item_expert, item_lo, item_hi = expert_work_items(counts[:, 0], n * TOP_K, EXPERT_TILE)
    dest = slot_destinations(idx, rank, starts)
    xs = dispatch(dest, hp.reshape(n, d // 2))
    y_sorted = expert_ffn(item_tile, item_expert, item_lo, item_hi, xs, w_exp_gate, w_exp_up, w_exp_down,
                          tm=EXPERT_TILE)
    out = combine(dest, y_sorted, gate.T, shared.reshape(n, d), x1.reshape(n, d), mod, g_post_ffn, seq=s)
    return out.reshape(b, s, d)


def kernel(x, c, rel_bias, w_ada, b_ada, g_pre_mix, w_in, w_branch_a, w_branch_b, w_out, g_post_mix, g_pre_ffn,
           w_router, router_bias, w_exp_gate, w_exp_up, w_exp_down, w_sh_gate, w_sh_up, w_sh_down, g_post_ffn):
    depth = w_in.shape[0]
    for layer in range(depth):
        mod = ada_modulation(c, w_ada[layer], b_ada[layer]).reshape(x.shape[0], N_MOD, x.shape[2])
        x = _layer(x, mod, rel_bias, g_pre_mix[layer], w_in[layer], w_branch_a[layer], w_branch_b[layer], w_out[layer],
                   g_post_mix[layer], g_pre_ffn[layer], w_router[layer], router_bias[layer], w_exp_gate[layer],
                   w_exp_up[layer], w_exp_down[layer], w_sh_gate[layer], w_sh_up[layer], w_sh_down[layer],
                   g_post_ffn[layer])
    return x
```

```python
import functools
import math

import jax
import jax.numpy as jnp
import numpy as np
from jax import lax
from jax.experimental import pallas as pl
from jax.experimental.pallas import tpu as pltpu

D_MODEL = 1024
HEAD_DIM = 64
SB_HEADS = 8
SB_WIDTH = SB_HEADS * HEAD_DIM
DIL_CONFIGS = ((128, 1), (512, 4), (2048, 16))
DIL_HEADS_PER_GROUP = 4
DIL_GROUP_WIDTH = DIL_HEADS_PER_GROUP * HEAD_DIM
DIL_WIDTH = DIL_GROUP_WIDTH * len(DIL_CONFIGS)
Q_BLOCK = 128
N_BUCKETS = 32
MAX_DISTANCE = 2048
N_EXPERTS = 256
TOP_K = 8
N_GROUPS = 8
TOP_K_GROUPS = 4
EXPERT_DIM = 256
ROUTED_SCALE = 2.5
N_MOD = 6
EPS = 1e-6

LANES = 128
VMEM_LIMIT_BYTES = 56 * 1024 * 1024
NEG_INF = float("-inf")


def _cparams(*semantics):
    return pltpu.CompilerParams(dimension_semantics=semantics, vmem_limit_bytes=VMEM_LIMIT_BYTES)


def _rms_norm(v, g):
    return v * lax.rsqrt(jnp.mean(v * v, axis=-1, keepdims=True) + EPS) * g


def _sigmoid(v):
    return 1.0 / (1.0 + jnp.exp(-v))


def _dot(a, b):
    return jnp.dot(a, b, preferred_element_type=jnp.float32)


def _dot_nt(a, b):
    return lax.dot_general(a, b, (((1,), (1,)), ((), ())), preferred_element_type=jnp.float32)


def _split_bf16(v):
    hi = v.astype(jnp.bfloat16)
    lo = (v - hi.astype(jnp.float32)).astype(jnp.bfloat16)
    return hi, lo


ROW_WORDS = D_MODEL // 2
ROW_SUB = ROW_WORDS // LANES


def _pack_rows(ref, v):
    m = v.shape[0]
    bits = pltpu.bitcast(v.astype(jnp.bfloat16).astype(jnp.float32), jnp.uint32)
    words = bits[:, :ROW_WORDS] | (bits[:, ROW_WORDS:] >> 16)
    for j in range(ROW_SUB):
        ref[pl.ds(j, m, stride=ROW_SUB), :] = words[:, j * LANES:(j + 1) * LANES]


def _unpack_rows(ref, m):
    words = jnp.concatenate([ref[pl.ds(j, m, stride=ROW_SUB), :] for j in range(ROW_SUB)], axis=1)
    first = pltpu.bitcast(words & jnp.uint32(0xFFFF0000), jnp.float32)
    second = pltpu.bitcast(words << 16, jnp.float32)
    return first, second


def _ada_kernel(c_ref, w_ref, b_ref, o_ref):
    c = c_ref[...]
    a_hi, a_lo = _split_bf16(c * _sigmoid(c))
    w_hi, w_lo = _split_bf16(w_ref[...])
    o_ref[...] = _dot(a_hi, w_hi) + _dot(a_lo, w_hi) + _dot(a_hi, w_lo) + b_ref[...]


def ada_modulation(c, w_ada, b_ada, *, tn=1536):
    b, d = c.shape
    n = w_ada.shape[1]
    return pl.pallas_call(
        _ada_kernel,
        grid=(n // tn,),
        in_specs=[
            pl.BlockSpec((b, d), lambda j: (0, 0)),
            pl.BlockSpec((d, tn), lambda j: (0, j)),
            pl.BlockSpec((1, tn), lambda j: (0, j)),
        ],
        out_specs=pl.BlockSpec((b, tn), lambda j: (0, j)),
        out_shape=jax.ShapeDtypeStruct((b, n), jnp.float32),
        compiler_params=_cparams("parallel"),
        name="ada_modulation",
    )(c, w_ada, b_ada.reshape(1, n))


def _in_proj_kernel(x_ref, mod_ref, g_ref, w_rows_ref, w_cols_ref, qT_ref, k_ref, vT_ref, *rest, tm):
    dil_refs, ga_ref, gb_ref, stage_ref = rest[:9], rest[9], rest[10], rest[11]
    h = _rms_norm(x_ref[0], g_ref[...]) * (1.0 + mod_ref[0, 1:2, :]) + mod_ref[0, 0:1, :]
    hb = h.astype(jnp.bfloat16)
    off = 0
    k_ref[0] = _dot(hb, w_rows_ref[:, off:off + SB_WIDTH]).astype(k_ref.dtype)
    off += SB_WIDTH
    for n, r in enumerate(dil_refs):
        y = _dot(hb, w_rows_ref[:, off:off + DIL_GROUP_WIDTH])
        off += DIL_GROUP_WIDTH
        dilation = DIL_CONFIGS[n % len(DIL_CONFIGS)][1]
        if dilation == 1:
            r[0] = y.astype(r.dtype)
            continue
        for j in range(DIL_GROUP_WIDTH // LANES):
            stage_ref[j] = y[:, j * LANES:(j + 1) * LANES]
        for c in range(dilation):
            for j in range(DIL_GROUP_WIDTH // LANES):
                col = c * DIL_GROUP_WIDTH + j * LANES
                r[0, :, col:col + LANES] = stage_ref[j, pl.ds(c, tm // dilation, stride=dilation), :].astype(r.dtype)
    for r in (ga_ref, gb_ref):
        r[0] = _sigmoid(_dot(hb, w_rows_ref[:, off:off + D_MODEL])).astype(r.dtype)
        off += D_MODEL
    qT_ref[0] = _dot_nt(w_cols_ref[0:SB_WIDTH, :], hb).astype(qT_ref.dtype)
    vT = _dot_nt(w_cols_ref[SB_WIDTH:2 * SB_WIDTH, :], hb).astype(vT_ref.dtype)
    for p in range(SB_WIDTH // LANES):
        for jb in range(tm // LANES):
            vT_ref[0, p, jb] = vT[p * LANES:(p + 1) * LANES, jb * LANES:(jb + 1) * LANES]


def _dilated_shape(b, s, dilation, dtype):
    return jax.ShapeDtypeStruct((b, s // dilation, dilation * DIL_GROUP_WIDTH), dtype)


def _dilated_spec(tm, dilation):
    return pl.BlockSpec((1, tm // dilation, dilation * DIL_GROUP_WIDTH), lambda bi, i: (bi, i, 0))


def in_projection(x, mod, g_pre, w_rows, w_cols, *, tm=256):
    b, s, d = x.shape
    nt = s // tm
    bf = jnp.bfloat16
    tok = lambda width: pl.BlockSpec((1, tm, width), lambda bi, i: (bi, i, 0))
    out_shapes = [
        jax.ShapeDtypeStruct((b, SB_WIDTH, s), bf),
        jax.ShapeDtypeStruct((b, s, SB_WIDTH), bf),
        jax.ShapeDtypeStruct((b, SB_WIDTH // LANES, s // LANES, LANES, LANES), bf),
    ] + [_dilated_shape(b, s, r, bf) for _, r in DIL_CONFIGS] * 3 + [jax.ShapeDtypeStruct((b, s, d), bf)] * 2
    out_specs = [
        pl.BlockSpec((1, SB_WIDTH, tm), lambda bi, i: (bi, 0, i)),
        tok(SB_WIDTH),
        pl.BlockSpec((1, SB_WIDTH // LANES, tm // LANES, LANES, LANES), lambda bi, i: (bi, 0, i, 0, 0)),
    ] + [_dilated_spec(tm, r) for _, r in DIL_CONFIGS] * 3 + [tok(d)] * 2
    return pl.pallas_call(
        functools.partial(_in_proj_kernel, tm=tm),
        grid=(b, nt),
        in_specs=[
            tok(d),
            pl.BlockSpec((1, N_MOD, d), lambda bi, i: (bi, 0, 0)),
            pl.BlockSpec((1, d), lambda bi, i: (0, 0)),
            pl.BlockSpec(w_rows.shape, lambda bi, i: (0, 0)),
            pl.BlockSpec(w_cols.shape, lambda bi, i: (0, 0)),
        ],
        out_specs=out_specs,
        out_shape=out_shapes,
        scratch_shapes=[pltpu.VMEM((DIL_GROUP_WIDTH // LANES, tm, LANES), jnp.float32)],
        compiler_params=_cparams("parallel", "parallel"),
        name="in_projection",
    )(x, mod, g_pre.reshape(1, d), w_rows, w_cols)


SB_TILE = 512
SB_SUB = 256


def _sb_chunk(kc, vc, q_h, tri, csum, past):
    z = _dot(kc, q_h)
    log_keep = -(jnp.maximum(z, 0.0) + jnp.log(1.0 + jnp.exp(-jnp.abs(z))))
    if past is not None:
        log_keep = jnp.where(past, log_keep, 0.0)
    weights = [None] * (SB_TILE // SB_SUB)
    for sb in reversed(range(SB_TILE // SB_SUB)):
        rows = slice(sb * SB_SUB, (sb + 1) * SB_SUB)
        lk = log_keep[rows]
        hi, lo = _split_bf16(lk)
        later = _dot(tri, hi) + _dot(tri, lo)
        w = jnp.exp(z[rows] + lk + later + csum)
        if past is not None:
            w = jnp.where(past[rows], w, 0.0)
        weights[sb] = w.astype(jnp.bfloat16)
        csum = csum + jnp.sum(lk, axis=0, keepdims=True)
    return _dot(vc, jnp.concatenate(weights, axis=0)), csum


def _sb_attn_kernel(qT_ref, k_ref, vT_ref, tri_ref, o_ref, acc_ref, csum_ref):
    i = pl.program_id(2)
    t = SB_TILE
    q2 = qT_ref[0]
    row = lax.broadcasted_iota(jnp.int32, q2.shape, 0)
    zero = jnp.zeros_like(q2)
    q_heads = (jnp.where(row < HEAD_DIM, q2, zero), jnp.where(row >= HEAD_DIM, q2, zero))
    tri = tri_ref[...]
    blocks = t // LANES

    def chunk(c, past, first):
        kc = k_ref[0, pl.ds(pl.multiple_of(c * t, t), t), :]
        vc = jnp.concatenate([vT_ref[0, 0, c * blocks + jb] for jb in range(blocks)], axis=1)
        for h in range(2):
            rows = slice(h * HEAD_DIM, (h + 1) * HEAD_DIM)
            csum = jnp.zeros((1, t), jnp.float32) if first else csum_ref[h:h + 1, :]
            part, csum = _sb_chunk(kc, vc[rows], q_heads[h], tri, csum, past)
            acc_ref[rows, :] = part if first else acc_ref[rows, :] + part
            csum_ref[h:h + 1, :] = csum

    key = lax.broadcasted_iota(jnp.int32, (t, t), 0)
    query = lax.broadcasted_iota(jnp.int32, (t, t), 1)
    chunk(i, key < query, True)

    def body(it, carry):
        chunk(i - 1 - it, None, False)
        return carry

    lax.fori_loop(0, i, body, 0)
    o_ref[0] = acc_ref[...].T.astype(o_ref.dtype)


def sb_attention(qT, k, vT3):
    b, width, s = qT.shape
    n_pairs = width // LANES
    t = SB_TILE
    tri = jnp.asarray(np.triu(np.ones((SB_SUB, SB_SUB), np.float32), k=1), jnp.bfloat16)
    return pl.pallas_call(
        _sb_attn_kernel,
        grid=(b, n_pairs, s // t),
        in_specs=[
            pl.BlockSpec((1, LANES, t), lambda bi, p, i: (bi, p, i)),
            pl.BlockSpec((1, s, LANES), lambda bi, p, i: (bi, 0, p)),
            pl.BlockSpec((1, 1, s // LANES, LANES, LANES), lambda bi, p, i: (bi, p, 0, 0, 0)),
            pl.BlockSpec((SB_SUB, SB_SUB), lambda bi, p, i: (0, 0)),
        ],
        out_specs=pl.BlockSpec((1, t, LANES), lambda bi, p, i: (bi, i, p)),
        out_shape=jax.ShapeDtypeStruct((b, s, width), jnp.bfloat16),
        scratch_shapes=[pltpu.VMEM((LANES, t), jnp.float32), pltpu.VMEM((2, t), jnp.float32)],
        compiler_params=_cparams("parallel", "parallel", "arbitrary"),
        name="sb_attention",
    )(qT, k, vT3, tri)


def _dil_attn_kernel(q_ref, kp_ref, kc_ref, vp_ref, vc_ref, bias_ref, o_ref, lse_ref):
    m = pl.program_id(2)
    q4 = q_ref[0]
    kk = jnp.concatenate([kp_ref[0], kc_ref[0]], axis=0)
    vv = jnp.concatenate([vp_ref[0], vc_ref[0]], axis=0)
    lane = lax.broadcasted_iota(jnp.int32, q4.shape, 1)
    zero = jnp.zeros_like(q4)
    in_head = [jnp.logical_and(lane >= h * HEAD_DIM, lane < (h + 1) * HEAD_DIM) for h in range(DIL_HEADS_PER_GROUP)]
    q_stack = jnp.concatenate([jnp.where(mask, q4, zero) for mask in in_head], axis=0)
    logits = _dot_nt(q_stack, kk) + bias_ref[...]
    key = lax.broadcasted_iota(jnp.int32, logits.shape, 1)
    logits = jnp.where(jnp.logical_or(key >= Q_BLOCK, m > 0), logits, NEG_INF)
    mx = jnp.max(logits, axis=1, keepdims=True)
    p = jnp.exp(logits - mx)
    den = jnp.sum(p, axis=1, keepdims=True)
    pv = _dot(p.astype(jnp.bfloat16), vv) / den
    lse_rows = mx + jnp.log(den)
    out = jnp.zeros(q4.shape, jnp.float32)
    lse = jnp.zeros(q4.shape, jnp.float32)
    for h, mask in enumerate(in_head):
        rows = slice(h * Q_BLOCK, (h + 1) * Q_BLOCK)
        out = jnp.where(mask, pv[rows], out)
        lse = jnp.where(mask, lse_rows[rows], lse)
    o_ref[0] = out.astype(o_ref.dtype)
    lse_ref[0] = lse


def dilated_attention(q, k, v, bias, dilation):
    b, length, width = q.shape
    gw = DIL_GROUP_WIDTH
    cur = pl.BlockSpec((1, Q_BLOCK, gw), lambda bi, c, m: (bi, m, c))
    prev = pl.BlockSpec((1, Q_BLOCK, gw), lambda bi, c, m: (bi, jnp.maximum(m - 1, 0), c))
    return pl.pallas_call(
        _dil_attn_kernel,
        grid=(b, width // gw, length // Q_BLOCK),
        in_specs=[cur, prev, cur, prev, cur,
                  pl.BlockSpec((DIL_HEADS_PER_GROUP * Q_BLOCK, 2 * Q_BLOCK), lambda bi, c, m: (0, 0))],
        out_specs=[cur, cur],
        out_shape=[jax.ShapeDtypeStruct((b, length, width), jnp.bfloat16),
                   jax.ShapeDtypeStruct((b, length, width), jnp.float32)],
        compiler_params=_cparams("parallel", "parallel", "arbitrary"),
        name=f"dilated_attention_r{dilation}",
    )(q, k, k, v, v, bias.reshape(DIL_HEADS_PER_GROUP * Q_BLOCK, 2 * Q_BLOCK))


def _t5_causal_bucket(dist):
    max_exact = N_BUCKETS // 2
    d = jnp.maximum(dist, 1).astype(jnp.float32)
    large = max_exact + (jnp.log(d / max_exact) / math.log(MAX_DISTANCE / max_exact)
                         * (N_BUCKETS - max_exact)).astype(jnp.int32)
    large = jnp.minimum(large, N_BUCKETS - 1)
    return jnp.where(dist < max_exact, dist, large)


def dilated_bias(rel_bias_group, window, dilation):
    qi = jnp.arange(Q_BLOCK)[:, None]
    kj = jnp.arange(2 * Q_BLOCK)[None, :]
    dist_sub = qi + Q_BLOCK - kj
    bucket = _t5_causal_bucket(jnp.maximum(dist_sub, 0) * dilation)
    onehot = (bucket[..., None] == jnp.arange(N_BUCKETS)).astype(jnp.float32)
    bias = jnp.einsum("qkn,nh->hqk", onehot, rel_bias_group.astype(jnp.float32), precision=lax.Precision.HIGHEST)
    valid = (dist_sub >= 0) & (dist_sub <= window // dilation)
    return jnp.where(valid[None], bias, NEG_INF)


def _mix_out_kernel(oa_ref, o0_ref, o1_ref, o2_ref, l0_ref, l1_ref, l2_ref, ga_ref, gb_ref, x_ref, mod_ref,
                    wa_ref, wb_ref, wo_ref, gpost_ref, gpre_ref, wr_ref, wsgu_ref, wsd_ref,
                    x1_ref, hp_ref, sh_ref, lt_ref, stage_ref):
    f32 = jnp.float32
    tm = x_ref.shape[1]

    def token_major(ref, dilation):
        if dilation == 1:
            return ref[0].astype(f32)
        slabs = DIL_GROUP_WIDTH // LANES
        for c in range(dilation):
            for j in range(slabs):
                col = c * DIL_GROUP_WIDTH + j * LANES
                stage_ref[j, pl.ds(c, tm // dilation, stride=dilation), :] = ref[0, :, col:col + LANES].astype(f32)
        return jnp.concatenate([stage_ref[j] for j in range(slabs)], axis=1)

    dils = [r for _, r in DIL_CONFIGS]
    l0, l1, l2 = (token_major(ref, r) for ref, r in zip((l0_ref, l1_ref, l2_ref), dils))
    mx = jnp.maximum(jnp.maximum(l0, l1), l2)
    e0, e1, e2 = jnp.exp(l0 - mx), jnp.exp(l1 - mx), jnp.exp(l2 - mx)
    den = e0 + e1 + e2
    o0, o1, o2 = (token_major(ref, r) for ref, r in zip((o0_ref, o1_ref, o2_ref), dils))
    out_b = (e0 / den) * o0 + (e1 / den) * o1 + (e2 / den) * o2
    ya = _dot(oa_ref[0], wa_ref[...])
    yb = _dot(out_b.astype(jnp.bfloat16), wb_ref[...])
    merged = ga_ref[0].astype(f32) * ya + gb_ref[0].astype(f32) * yb
    y = _dot(merged.astype(jnp.bfloat16), wo_ref[...])
    x1 = x_ref[0] + mod_ref[0, 2:3, :] * _rms_norm(y, gpost_ref[...])
    x1_ref[0] = x1
    h = _rms_norm(x1, gpre_ref[...]) * (1.0 + mod_ref[0, 4:5, :]) + mod_ref[0, 3:4, :]
    hb = h.astype(jnp.bfloat16)
    _pack_rows(hp_ref, h)
    lt_ref[...] = _dot_nt(wr_ref[...], hb)
    gu = _dot(hb, wsgu_ref[...])
    g, u = gu[:, :EXPERT_DIM], gu[:, EXPERT_DIM:]
    sh_ref[0] = _dot((g * _sigmoid(g) * u).astype(jnp.bfloat16), wsd_ref[...])


def mix_out(out_a, o_g, lse_g, gate_a, gate_b, x, mod, w_a, w_b, w_o, g_post, g_pre_ffn, w_rT, w_sgu, w_sd, *, tm=256):
    b, s, d = x.shape
    tok = lambda width: pl.BlockSpec((1, tm, width), lambda bi, i: (bi, i, 0))
    full = lambda a: pl.BlockSpec(a.shape, lambda bi, i: (0,) * a.ndim)
    nt = s // tm
    g_post, g_pre_ffn = g_post.reshape(1, d), g_pre_ffn.reshape(1, d)
    weights = (w_a, w_b, w_o, g_post, g_pre_ffn, w_rT, w_sgu, w_sd)
    return pl.pallas_call(
        _mix_out_kernel,
        grid=(b, nt),
        in_specs=[tok(SB_WIDTH)] + [_dilated_spec(tm, r) for _, r in DIL_CONFIGS] * 2 + [tok(d)] * 3
        + [pl.BlockSpec((1, N_MOD, d), lambda bi, i: (bi, 0, 0))] + [full(w) for w in weights],
        out_specs=[tok(d), pl.BlockSpec((tm * ROW_SUB, LANES), lambda bi, i: (bi * nt + i, 0)), tok(d),
                   pl.BlockSpec((N_EXPERTS, tm), lambda bi, i: (0, bi * nt + i))],
        out_shape=[jax.ShapeDtypeStruct((b, s, d), jnp.float32),
                   jax.ShapeDtypeStruct((b * s * ROW_SUB, LANES), jnp.uint32),
                   jax.ShapeDtypeStruct((b, s, d), jnp.float32),
                   jax.ShapeDtypeStruct((N_EXPERTS, b * s), jnp.float32)],
        scratch_shapes=[pltpu.VMEM((DIL_GROUP_WIDTH // LANES, tm, LANES), jnp.float32)],
        compiler_params=_cparams("parallel", "parallel"),
        name="mix_out",
    )(out_a, *o_g, *lse_g, gate_a, gate_b, x, mod, *weights)


def _route_kernel(lt_ref, bias_ref, tri_ref, idx_ref, gate_ref, rank_ref, cnt_ref, base_ref, *, tn):
    f32 = jnp.float32

    @pl.when(pl.program_id(0) == 0)
    def _():
        base_ref[...] = jnp.zeros_like(base_ref)

    per_group = N_EXPERTS // N_GROUPS
    scores = _sigmoid(lt_ref[...])
    sel = scores + bias_ref[...]
    sel_groups = [sel[g * per_group:(g + 1) * per_group] for g in range(N_GROUPS)]
    ri = lax.broadcasted_iota(jnp.int32, (per_group, tn), 0)
    group_scores = []
    for sg in sel_groups:
        m1 = jnp.max(sg, axis=0, keepdims=True)
        first = jnp.min(jnp.where(sg == m1, ri, per_group), axis=0, keepdims=True)
        group_scores.append(m1 + jnp.max(jnp.where(ri == first, NEG_INF, sg), axis=0, keepdims=True))
    cur = jnp.concatenate(group_scores, axis=0)
    gi = lax.broadcasted_iota(jnp.int32, cur.shape, 0)
    keep = jnp.zeros(cur.shape, jnp.float32)
    for _ in range(TOP_K_GROUPS):
        mg = jnp.max(cur, axis=0, keepdims=True)
        pick = gi == jnp.min(jnp.where(cur == mg, gi, N_GROUPS), axis=0, keepdims=True)
        keep = jnp.where(pick, 1.0, keep)
        cur = jnp.where(pick, NEG_INF, cur)
    cur = jnp.concatenate([jnp.where(keep[g:g + 1, :] > 0.0, sg, NEG_INF) for g, sg in enumerate(sel_groups)], axis=0)
    ei = lax.broadcasted_iota(jnp.int32, cur.shape, 0)
    picks, idxs, gates = [], [], []
    for _ in range(TOP_K):
        mk = jnp.max(cur, axis=0, keepdims=True)
        fk = jnp.min(jnp.where(cur == mk, ei, N_EXPERTS), axis=0, keepdims=True)
        pick = ei == fk
        picks.append(pick)
        idxs.append(fk)
        gates.append(jnp.sum(jnp.where(pick, scores, 0.0), axis=0, keepdims=True))
        cur = jnp.where(pick, NEG_INF, cur)
    gate = jnp.concatenate(gates, axis=0)
    gate_ref[...] = gate / jnp.sum(gate, axis=0, keepdims=True) * ROUTED_SCALE
    idx_ref[...] = jnp.concatenate(idxs, axis=0)
    chosen = functools.reduce(jnp.logical_or, picks)
    onehot = jnp.where(chosen, 1.0, 0.0)
    earlier = _dot(onehot.astype(jnp.bfloat16), tri_ref[...]) + base_ref[...]
    ranks = [jnp.sum(jnp.where(p, earlier, 0.0), axis=0, keepdims=True) for p in picks]
    rank_ref[...] = jnp.concatenate(ranks, axis=0).astype(jnp.int32)
    base = base_ref[...] + jnp.sum(onehot, axis=1, keepdims=True)
    base_ref[...] = base
    cnt_ref[...] = jnp.broadcast_to(base, cnt_ref.shape).astype(jnp.int32)


def route(logits_t, router_bias, *, tn=512):
    ne, n = logits_t.shape
    tri = jnp.asarray(np.triu(np.ones((tn, tn), np.float32), k=1), jnp.bfloat16)
    slot = pl.BlockSpec((TOP_K, tn), lambda i: (0, i))
    return pl.pallas_call(
        functools.partial(_route_kernel, tn=tn),
        grid=(n // tn,),
        in_specs=[pl.BlockSpec((ne, tn), lambda i: (0, i)),
                  pl.BlockSpec((ne, 1), lambda i: (0, 0)),
                  pl.BlockSpec((tn, tn), lambda i: (0, 0))],
        out_specs=[slot, slot, slot, pl.BlockSpec((ne, LANES), lambda i: (0, 0))],
        out_shape=[jax.ShapeDtypeStruct((TOP_K, n), jnp.int32),
                   jax.ShapeDtypeStruct((TOP_K, n), jnp.float32),
                   jax.ShapeDtypeStruct((TOP_K, n), jnp.int32),
                   jax.ShapeDtypeStruct((ne, LANES), jnp.int32)],
        scratch_shapes=[pltpu.VMEM((ne, 1), jnp.float32)],
        compiler_params=_cparams("arbitrary"),
        name="route",
    )(logits_t, router_bias.reshape(ne, 1), tri)


def _dest_kernel(idx_ref, rank_ref, start_ref, dest_ref):
    idx = idx_ref[...]
    ei = lax.broadcasted_iota(jnp.int32, (N_EXPERTS, idx.shape[1]), 0)
    start = start_ref[...]
    rows = [jnp.sum(jnp.where(ei == idx[k:k + 1, :], start, 0.0), axis=0, keepdims=True) for k in range(TOP_K)]
    dest_ref[...] = jnp.concatenate(rows, axis=0).astype(jnp.int32) + rank_ref[...]


def slot_destinations(idx, rank, starts, *, tn=512):
    kk, n = idx.shape
    slot = pl.BlockSpec((kk, tn), lambda i: (0, i))
    return pl.pallas_call(
        _dest_kernel,
        grid=(n // tn,),
        in_specs=[slot, slot, pl.BlockSpec((N_EXPERTS, 1), lambda i: (0, 0))],
        out_specs=slot,
        out_shape=jax.ShapeDtypeStruct((kk, n), jnp.int32),
        compiler_params=_cparams("parallel"),
        name="slot_destinations",
    )(idx, rank, starts.astype(jnp.float32).reshape(N_EXPERTS, 1))


def _packed_row(ref, row):
    return ref.at[pl.ds(pl.multiple_of(row * ROW_SUB, ROW_SUB), ROW_SUB), :]


def _dispatch_kernel(dest_ref, h_ref, xs_ref, zero_ref, sem, *, tm, n_rows):
    pad = zero_ref.shape[0]

    @pl.when(pl.program_id(0) == 0)
    def _():
        zero_ref[...] = jnp.zeros_like(zero_ref)
        tail = pltpu.make_async_copy(zero_ref, xs_ref.at[pl.ds(n_rows * ROW_SUB, pad), :], sem)
        tail.start()
        tail.wait()

    def body(t, _):
        for k in range(TOP_K):
            pltpu.make_async_copy(_packed_row(h_ref, t), _packed_row(xs_ref, dest_ref[k, t]), sem).start(priority=k % 2)
        return 0

    lax.fori_loop(0, tm, body, 0)
    for _ in range(TOP_K):
        pltpu.make_async_copy(h_ref, xs_ref.at[pl.ds(0, tm * ROW_SUB), :], sem).wait()


def dispatch(dest, hp, *, pad_rows, tm=512):
    kk, n = dest.shape
    n_rows = n * kk
    return pl.pallas_call(
        functools.partial(_dispatch_kernel, tm=tm, n_rows=n_rows),
        grid=(n // tm,),
        in_specs=[pl.BlockSpec((kk, tm), lambda i: (0, i), memory_space=pltpu.SMEM),
                  pl.BlockSpec((tm * ROW_SUB, LANES), lambda i: (i, 0))],
        out_specs=pl.BlockSpec(memory_space=pl.ANY),
        out_shape=jax.ShapeDtypeStruct(((n_rows + pad_rows) * ROW_SUB, LANES), hp.dtype),
        scratch_shapes=[pltpu.VMEM((pad_rows * ROW_SUB, LANES), hp.dtype), pltpu.SemaphoreType.DMA],
        compiler_params=_cparams("arbitrary"),
        name="dispatch",
    )(dest, hp)


EXPERT_CHUNK = 256


def _expert_kernel(start_ref, count_ref, xs_ref, wg_ref, wu_ref, wd_ref, ys_ref, xbuf, ybuf, wgu_bf, wd_bf,
                   sem_in, sem_out):
    e = pl.program_id(0)
    start, count = start_ref[e], count_ref[e]
    chunk = EXPERT_CHUNK
    n_chunks = lax.shift_right_logical(count + (chunk - 1), chunk.bit_length() - 1)

    def rows(c):
        return pl.ds(pl.multiple_of((start + c * chunk) * ROW_SUB, ROW_SUB), chunk * ROW_SUB)

    def in_copy(c, slot):
        return pltpu.make_async_copy(xs_ref.at[rows(c), :], xbuf.at[slot], sem_in.at[slot])

    def out_copy(c, slot):
        return pltpu.make_async_copy(ybuf.at[slot], ys_ref.at[rows(c), :], sem_out.at[slot])

    @pl.when(n_chunks > 0)
    def _():
        in_copy(0, 0).start()
        wgu_bf[:, :EXPERT_DIM] = wg_ref[0].astype(wgu_bf.dtype)
        wgu_bf[:, EXPERT_DIM:] = wu_ref[0].astype(wgu_bf.dtype)
        wd_bf[...] = wd_ref[0].astype(wd_bf.dtype)

    def body(c, carry):
        slot = lax.rem(c, 2)
        in_copy(c, slot).wait()

        @pl.when(c + 1 < n_chunks)
        def _():
            in_copy(c + 1, 1 - slot).start()

        bf = jnp.bfloat16
        xa, xb = _unpack_rows(xbuf.at[slot], chunk)
        gu = _dot(xa.astype(bf), wgu_bf[:ROW_WORDS, :]) + _dot(xb.astype(bf), wgu_bf[ROW_WORDS:, :])
        g, u = gu[:, :EXPERT_DIM], gu[:, EXPERT_DIM:]
        y = _dot((g * _sigmoid(g) * u).astype(bf), wd_bf[...])

        @pl.when(c >= 2)
        def _():
            out_copy(c - 2, slot).wait()

        _pack_rows(ybuf.at[slot], y)
        out_copy(c, slot).start()
        return carry

    lax.fori_loop(0, n_chunks, body, 0)

    @pl.when(n_chunks > 1)
    def _():
        out_copy(n_chunks - 2, lax.rem(n_chunks, 2)).wait()

    @pl.when(n_chunks > 0)
    def _():
        out_copy(n_chunks - 1, lax.rem(n_chunks - 1, 2)).wait()

    @pl.when(e == pl.num_programs(0) - 1)
    def _():
        ybuf[0] = jnp.zeros(ybuf.shape[1:], ybuf.dtype)
        pad = pl.ds(ys_ref.shape[0] - chunk * ROW_SUB, chunk * ROW_SUB)
        tail = pltpu.make_async_copy(ybuf.at[0], ys_ref.at[pad, :], sem_out.at[0])
        tail.start()
        tail.wait()


def expert_ffn(starts, counts, xs, w_gate, w_up, w_down):
    d = D_MODEL
    chunk_shape = (2, EXPERT_CHUNK * ROW_SUB, LANES)
    grid_spec = pltpu.PrefetchScalarGridSpec(
        num_scalar_prefetch=2,
        grid=(N_EXPERTS,),
        in_specs=[pl.BlockSpec(memory_space=pl.ANY),
                  pl.BlockSpec((1, d, EXPERT_DIM), lambda e, s, c: (e, 0, 0)),
                  pl.BlockSpec((1, d, EXPERT_DIM), lambda e, s, c: (e, 0, 0)),
                  pl.BlockSpec((1, EXPERT_DIM, d), lambda e, s, c: (e, 0, 0))],
        out_specs=pl.BlockSpec(memory_space=pl.ANY),
        scratch_shapes=[pltpu.VMEM(chunk_shape, xs.dtype), pltpu.VMEM(chunk_shape, xs.dtype),
                        pltpu.VMEM((d, 2 * EXPERT_DIM), jnp.bfloat16), pltpu.VMEM((EXPERT_DIM, d), jnp.bfloat16),
                        pltpu.SemaphoreType.DMA((2,)), pltpu.SemaphoreType.DMA((2,))],
    )
    return pl.pallas_call(
        _expert_kernel,
        grid_spec=grid_spec,
        out_shape=jax.ShapeDtypeStruct(xs.shape, xs.dtype),
        compiler_params=_cparams("arbitrary"),
        name="expert_ffn",
    )(starts, counts, xs, w_gate, w_up, w_down)


def _combine_kernel(dest_ref, ys_ref, gate_ref, sh_ref, x1_ref, mod_ref, g_ref, o_ref, buf, sem, *, tm):
    def body(t, _):
        for k in range(TOP_K):
            pltpu.make_async_copy(_packed_row(ys_ref, dest_ref[k, t]), _packed_row(buf.at[k], t), sem).start(
                priority=k % 2)
        return 0

    lax.fori_loop(0, tm, body, 0)
    for k in range(TOP_K):
        pltpu.make_async_copy(ys_ref.at[pl.ds(0, tm * ROW_SUB), :], buf.at[k], sem).wait()
    gate = gate_ref[...]
    first = jnp.zeros((tm, ROW_WORDS), jnp.float32)
    second = jnp.zeros((tm, ROW_WORDS), jnp.float32)
    for k in range(TOP_K):
        ya, yb = _unpack_rows(buf.at[k], tm)
        first = first + ya * gate[:, k:k + 1]
        second = second + yb * gate[:, k:k + 1]
    y = sh_ref[...] + jnp.concatenate([first, second], axis=1)
    o_ref[...] = x1_ref[...] + mod_ref[0, 5:6, :] * _rms_norm(y, g_ref[...])


def combine(dest, ys, gate_nk, shared, x1, mod, g_post_ffn, *, seq, tm=128):
    n, d = x1.shape
    kk = dest.shape[0]
    per_batch = seq // tm
    tok = pl.BlockSpec((tm, d), lambda i: (i, 0))
    return pl.pallas_call(
        functools.partial(_combine_kernel, tm=tm),
        grid=(n // tm,),
        in_specs=[pl.BlockSpec((kk, tm), lambda i: (0, i), memory_space=pltpu.SMEM),
                  pl.BlockSpec(memory_space=pl.ANY),
                  pl.BlockSpec((tm, kk), lambda i: (i, 0)),
                  tok, tok,
                  pl.BlockSpec((1, N_MOD, d), lambda i: (i // per_batch, 0, 0)),
                  pl.BlockSpec((1, d), lambda i: (0, 0))],
        out_specs=tok,
        out_shape=jax.ShapeDtypeStruct((n, d), jnp.float32),
        scratch_shapes=[pltpu.VMEM((kk, tm * ROW_SUB, LANES), ys.dtype), pltpu.SemaphoreType.DMA],
        compiler_params=_cparams("arbitrary"),
        name="combine",
    )(dest, ys, gate_nk, shared, x1, mod, g_post_ffn.reshape(1, d))


def _layer(x, mod, rel_bias, g_pre_mix, w_in, w_branch_a, w_branch_b, w_out, g_post_mix, g_pre_ffn, w_router,
           router_bias, w_exp_gate, w_exp_up, w_exp_down, w_sh_gate, w_sh_up, w_sh_down, g_post_ffn):
    b, s, d = x.shape
    bf = jnp.bfloat16
    scale = HEAD_DIM ** -0.5
    sb_q, sb_k, sb_v, dq, dk, dv, w_ga, w_gb = jnp.split(
        w_in, np.cumsum((SB_WIDTH,) * 3 + (DIL_WIDTH,) * 3 + (d,)).tolist(), axis=1)
    groups = lambda w: [w[:, g * DIL_GROUP_WIDTH:(g + 1) * DIL_GROUP_WIDTH] for g in range(len(DIL_CONFIGS))]
    w_rows = jnp.concatenate([sb_k] + [w * scale for w in groups(dq)] + groups(dk) + groups(dv) + [w_ga, w_gb],
                             axis=1).astype(bf)
    w_cols = jnp.concatenate([sb_q * scale, sb_v], axis=1).T.astype(bf)
    qT, k, vT3, *rest = in_projection(x, mod, g_pre_mix, w_rows, w_cols)
    dqs, dks, dvs, gate_a, gate_b = rest[0:3], rest[3:6], rest[6:9], rest[9], rest[10]
    out_a = sb_attention(qT, k, vT3)
    o_g, lse_g = [], []
    for g, (window, dilation) in enumerate(DIL_CONFIGS):
        bias = dilated_bias(rel_bias[:, g * DIL_HEADS_PER_GROUP:(g + 1) * DIL_HEADS_PER_GROUP], window, dilation)
        o, lse = dilated_attention(dqs[g], dks[g], dvs[g], bias, dilation)
        o_g.append(o)
        lse_g.append(lse)
    x1, hp, shared, logits_t = mix_out(
        out_a, o_g, lse_g, gate_a, gate_b, x, mod, w_branch_a.astype(bf), w_branch_b.astype(bf), w_out.astype(bf),
        g_post_mix, g_pre_ffn, w_router.T.astype(bf), jnp.concatenate([w_sh_gate, w_sh_up], axis=1).astype(bf),
        w_sh_down.astype(bf))
    n = b * s
    idx, gate, rank, counts = route(logits_t, router_bias)
    counts = counts[:, 0]
    starts = jnp.cumsum(counts) - counts
    dest = slot_destinations(idx, rank, starts)
    xs = dispatch(dest, hp, pad_rows=EXPERT_CHUNK)
    ys = expert_ffn(starts, counts, xs, w_exp_gate, w_exp_up, w_exp_down)
    out = combine(dest, ys, gate.T, shared.reshape(n, d), x1.reshape(n, d), mod, g_post_ffn, seq=s)
    return out.reshape(b, s, d)


def kernel(x, c, rel_bias, w_ada, b_ada, g_pre_mix, w_in, w_branch_a, w_branch_b, w_out, g_post_mix, g_pre_ffn,
           w_router, router_bias, w_exp_gate, w_exp_up, w_exp_down, w_sh_gate, w_sh_up, w_sh_down, g_post_ffn):
    depth = w_in.shape[0]
    for layer in range(depth):
        mod = ada_modulation(c, w_ada[layer], b_ada[layer]).reshape(x.shape[0], N_MOD, x.shape[2])
        x = _layer(x, mod, rel_bias, g_pre_mix[layer], w_in[layer], w_branch_a[layer], w_branch_b[layer], w_out[layer],
                   g_post_mix[layer], g_pre_ffn[layer], w_router[layer], router_bias[layer], w_exp_gate[layer],
                   w_exp_up[layer], w_exp_down[layer], w_sh_gate[layer], w_sh_up[layer], w_sh_down[layer],
                   g_post_ffn[layer])
    return x
```

```python
import functools
import math

import jax
import jax.numpy as jnp
import numpy as np
from jax import lax
from jax.experimental import pallas as pl
from jax.experimental.pallas import tpu as pltpu

D_MODEL = 1024
HEAD_DIM = 64
SB_HEADS = 8
SB_WIDTH = SB_HEADS * HEAD_DIM
DIL_CONFIGS = ((128, 1), (512, 4), (2048, 16))
DIL_HEADS_PER_GROUP = 4
DIL_GROUP_WIDTH = DIL_HEADS_PER_GROUP * HEAD_DIM
DIL_WIDTH = DIL_GROUP_WIDTH * len(DIL_CONFIGS)
Q_BLOCK = 128
N_BUCKETS = 32
MAX_DISTANCE = 2048
N_EXPERTS = 256
TOP_K = 8
N_GROUPS = 8
TOP_K_GROUPS = 4
EXPERT_DIM = 256
ROUTED_SCALE = 2.5
N_MOD = 6
EPS = 1e-6

LANES = 128
VMEM_LIMIT_BYTES = 56 * 1024 * 1024
NEG_INF = float("-inf")


def _cparams(*semantics):
    return pltpu.CompilerParams(dimension_semantics=semantics, vmem_limit_bytes=VMEM_LIMIT_BYTES)


def _rms_norm(v, g):
    return v * lax.rsqrt(jnp.mean(v * v, axis=-1, keepdims=True) + EPS) * g


def _sigmoid(v):
    return 1.0 / (1.0 + jnp.exp(-v))


def _dot(a, b):
    return jnp.dot(a, b, preferred_element_type=jnp.float32)


def _dot_nt(a, b):
    return lax.dot_general(a, b, (((1,), (1,)), ((), ())), preferred_element_type=jnp.float32)


def _split_bf16(v):
    hi = v.astype(jnp.bfloat16)
    lo = (v - hi.astype(jnp.float32)).astype(jnp.bfloat16)
    return hi, lo


ROW_WORDS = D_MODEL // 2
ROW_SUB = ROW_WORDS // LANES


def _pack_rows(ref, v):
    m = v.shape[0]
    bits = pltpu.bitcast(v.astype(jnp.bfloat16).astype(jnp.float32), jnp.uint32)
    words = bits[:, :ROW_WORDS] | (bits[:, ROW_WORDS:] >> 16)
    for j in range(ROW_SUB):
        ref[pl.ds(j, m, stride=ROW_SUB), :] = words[:, j * LANES:(j + 1) * LANES]


def _unpack_rows(ref, m):
    words = jnp.concatenate([ref[pl.ds(j, m, stride=ROW_SUB), :] for j in range(ROW_SUB)], axis=1)
    first = pltpu.bitcast(words & jnp.uint32(0xFFFF0000), jnp.float32)
    second = pltpu.bitcast(words << 16, jnp.float32)
    return first, second


def _ada_kernel(c_ref, w_ref, b_ref, o_ref):
    c = c_ref[...]
    a_hi, a_lo = _split_bf16(c * _sigmoid(c))
    w_hi, w_lo = _split_bf16(w_ref[...])
    o_ref[...] = _dot(a_hi, w_hi) + _dot(a_lo, w_hi) + _dot(a_hi, w_lo) + b_ref[...]


def ada_modulation(c, w_ada, b_ada, *, tn=1536):
    b, d = c.shape
    n = w_ada.shape[1]
    return pl.pallas_call(
        _ada_kernel,
        grid=(n // tn,),
        in_specs=[
            pl.BlockSpec((b, d), lambda j: (0, 0)),
            pl.BlockSpec((d, tn), lambda j: (0, j)),
            pl.BlockSpec((1, tn), lambda j: (0, j)),
        ],
        out_specs=pl.BlockSpec((b, tn), lambda j: (0, j)),
        out_shape=jax.ShapeDtypeStruct((b, n), jnp.float32),
        compiler_params=_cparams("parallel"),
        name="ada_modulation",
    )(c, w_ada, b_ada.reshape(1, n))


def _in_proj_kernel(x_ref, mod_ref, g_ref, w_rows_ref, w_cols_ref, qT_ref, k_ref, vT_ref, *rest, tm):
    dil_refs, ga_ref, gb_ref, stage_ref = rest[:9], rest[9], rest[10], rest[11]
    h = _rms_norm(x_ref[0], g_ref[...]) * (1.0 + mod_ref[0, 1:2, :]) + mod_ref[0, 0:1, :]
    hb = h.astype(jnp.bfloat16)
    off = 0
    k_ref[0] = _dot(hb, w_rows_ref[:, off:off + SB_WIDTH]).astype(k_ref.dtype)
    off += SB_WIDTH
    for n, r in enumerate(dil_refs):
        y = _dot(hb, w_rows_ref[:, off:off + DIL_GROUP_WIDTH])
        off += DIL_GROUP_WIDTH
        dilation = DIL_CONFIGS[n % len(DIL_CONFIGS)][1]
        if dilation == 1:
            r[0] = y.astype(r.dtype)
            continue
        for j in range(DIL_GROUP_WIDTH // LANES):
            stage_ref[j] = y[:, j * LANES:(j + 1) * LANES]
        for c in range(dilation):
            for j in range(DIL_GROUP_WIDTH // LANES):
                col = c * DIL_GROUP_WIDTH + j * LANES
                r[0, :, col:col + LANES] = stage_ref[j, pl.ds(c, tm // dilation, stride=dilation), :].astype(r.dtype)
    for r in (ga_ref, gb_ref):
        r[0] = _sigmoid(_dot(hb, w_rows_ref[:, off:off + D_MODEL])).astype(r.dtype)
        off += D_MODEL
    qT_ref[0] = _dot_nt(w_cols_ref[0:SB_WIDTH, :], hb).astype(qT_ref.dtype)
    vT = _dot_nt(w_cols_ref[SB_WIDTH:2 * SB_WIDTH, :], hb).astype(vT_ref.dtype)
    for p in range(SB_WIDTH // LANES):
        for jb in range(tm // LANES):
            vT_ref[0, p, jb] = vT[p * LANES:(p + 1) * LANES, jb * LANES:(jb + 1) * LANES]


def _dilated_shape(b, s, dilation, dtype):
    return jax.ShapeDtypeStruct((b, s // dilation, dilation * DIL_GROUP_WIDTH), dtype)


def _dilated_spec(tm, dilation):
    return pl.BlockSpec((1, tm // dilation, dilation * DIL_GROUP_WIDTH), lambda bi, i: (bi, i, 0))


def in_projection(x, mod, g_pre, w_rows, w_cols, *, tm=256):
    b, s, d = x.shape
    nt = s // tm
    bf = jnp.bfloat16
    tok = lambda width: pl.BlockSpec((1, tm, width), lambda bi, i: (bi, i, 0))
    out_shapes = [
        jax.ShapeDtypeStruct((b, SB_WIDTH, s), bf),
        jax.ShapeDtypeStruct((b, s, SB_WIDTH), bf),
        jax.ShapeDtypeStruct((b, SB_WIDTH // LANES, s // LANES, LANES, LANES), bf),
    ] + [_dilated_shape(b, s, r, bf) for _, r in DIL_CONFIGS] * 3 + [jax.ShapeDtypeStruct((b, s, d), bf)] * 2
    out_specs = [
        pl.BlockSpec((1, SB_WIDTH, tm), lambda bi, i: (bi, 0, i)),
        tok(SB_WIDTH),
        pl.BlockSpec((1, SB_WIDTH // LANES, tm // LANES, LANES, LANES), lambda bi, i: (bi, 0, i, 0, 0)),
    ] + [_dilated_spec(tm, r) for _, r in DIL_CONFIGS] * 3 + [tok(d)] * 2
    return pl.pallas_call(
        functools.partial(_in_proj_kernel, tm=tm),
        grid=(b, nt),
        in_specs=[
            tok(d),
            pl.BlockSpec((1, N_MOD, d), lambda bi, i: (bi, 0, 0)),
            pl.BlockSpec((1, d), lambda bi, i: (0, 0)),
            pl.BlockSpec(w_rows.shape, lambda bi, i: (0, 0)),
            pl.BlockSpec(w_cols.shape, lambda bi, i: (0, 0)),
        ],
        out_specs=out_specs,
        out_shape=out_shapes,
        scratch_shapes=[pltpu.VMEM((DIL_GROUP_WIDTH // LANES, tm, LANES), jnp.float32)],
        compiler_params=_cparams("parallel", "parallel"),
        name="in_projection",
    )(x, mod, g_pre.reshape(1, d), w_rows, w_cols)


SB_TILE = 512
SB_SUB = 256


def _sb_chunk(kc, vc, q_h, tri, csum, past):
    z = _dot(kc, q_h)
    log_keep = -(jnp.maximum(z, 0.0) + jnp.log(1.0 + jnp.exp(-jnp.abs(z))))
    if past is not None:
        log_keep = jnp.where(past, log_keep, 0.0)
    weights = [None] * (SB_TILE // SB_SUB)
    for sb in reversed(range(SB_TILE // SB_SUB)):
        rows = slice(sb * SB_SUB, (sb + 1) * SB_SUB)
        lk = log_keep[rows]
        hi, lo = _split_bf16(lk)
        later = _dot(tri, hi) + _dot(tri, lo)
        w = jnp.exp(z[rows] + lk + later + csum)
        if past is not None:
            w = jnp.where(past[rows], w, 0.0)
        weights[sb] = w.astype(jnp.bfloat16)
        csum = csum + jnp.sum(lk, axis=0, keepdims=True)
    return _dot(vc, jnp.concatenate(weights, axis=0)), csum


def _sb_attn_kernel(qT_ref, k_ref, vT_ref, tri_ref, o_ref, acc_ref, csum_ref):
    i = pl.program_id(2)
    t = SB_TILE
    q2 = qT_ref[0]
    row = lax.broadcasted_iota(jnp.int32, q2.shape, 0)
    zero = jnp.zeros_like(q2)
    q_heads = (jnp.where(row < HEAD_DIM, q2, zero), jnp.where(row >= HEAD_DIM, q2, zero))
    tri = tri_ref[...]
    blocks = t // LANES

    def chunk(c, past, first):
        kc = k_ref[0, pl.ds(pl.multiple_of(c * t, t), t), :]
        vc = jnp.concatenate([vT_ref[0, 0, c * blocks + jb] for jb in range(blocks)], axis=1)
        for h in range(2):
            rows = slice(h * HEAD_DIM, (h + 1) * HEAD_DIM)
            csum = jnp.zeros((1, t), jnp.float32) if first else csum_ref[h:h + 1, :]
            part, csum = _sb_chunk(kc, vc[rows], q_heads[h], tri, csum, past)
            acc_ref[rows, :] = part if first else acc_ref[rows, :] + part
            csum_ref[h:h + 1, :] = csum

    key = lax.broadcasted_iota(jnp.int32, (t, t), 0)
    query = lax.broadcasted_iota(jnp.int32, (t, t), 1)
    chunk(i, key < query, True)

    def body(it, carry):
        chunk(i - 1 - it, None, False)
        return carry

    lax.fori_loop(0, i, body, 0)
    o_ref[0] = acc_ref[...].T.astype(o_ref.dtype)


def sb_attention(qT, k, vT3):
    b, width, s = qT.shape
    n_pairs = width // LANES
    t = SB_TILE
    tri = jnp.asarray(np.triu(np.ones((SB_SUB, SB_SUB), np.float32), k=1), jnp.bfloat16)
    return pl.pallas_call(
        _sb_attn_kernel,
        grid=(b, n_pairs, s // t),
        in_specs=[
            pl.BlockSpec((1, LANES, t), lambda bi, p, i: (bi, p, i)),
            pl.BlockSpec((1, s, LANES), lambda bi, p, i: (bi, 0, p)),
            pl.BlockSpec((1, 1, s // LANES, LANES, LANES), lambda bi, p, i: (bi, p, 0, 0, 0)),
            pl.BlockSpec((SB_SUB, SB_SUB), lambda bi, p, i: (0, 0)),
        ],
        out_specs=pl.BlockSpec((1, t, LANES), lambda bi, p, i: (bi, i, p)),
        out_shape=jax.ShapeDtypeStruct((b, s, width), jnp.bfloat16),
        scratch_shapes=[pltpu.VMEM((LANES, t), jnp.float32), pltpu.VMEM((2, t), jnp.float32)],
        compiler_params=_cparams("parallel", "parallel", "arbitrary"),
        name="sb_attention",
    )(qT, k, vT3, tri)


def _dil_attn_kernel(q_ref, kp_ref, kc_ref, vp_ref, vc_ref, bias_ref, o_ref, lse_ref):
    m = pl.program_id(2)
    q4 = q_ref[0]
    kk = jnp.concatenate([kp_ref[0], kc_ref[0]], axis=0)
    vv = jnp.concatenate([vp_ref[0], vc_ref[0]], axis=0)
    lane = lax.broadcasted_iota(jnp.int32, q4.shape, 1)
    zero = jnp.zeros_like(q4)
    in_head = [jnp.logical_and(lane >= h * HEAD_DIM, lane < (h + 1) * HEAD_DIM) for h in range(DIL_HEADS_PER_GROUP)]
    q_stack = jnp.concatenate([jnp.where(mask, q4, zero) for mask in in_head], axis=0)
    logits = _dot_nt(q_stack, kk) + bias_ref[...]
    key = lax.broadcasted_iota(jnp.int32, logits.shape, 1)
    logits = jnp.where(jnp.logical_or(key >= Q_BLOCK, m > 0), logits, NEG_INF)
    mx = jnp.max(logits, axis=1, keepdims=True)
    p = jnp.exp(logits - mx)
    den = jnp.sum(p, axis=1, keepdims=True)
    pv = _dot(p.astype(jnp.bfloat16), vv) / den
    lse_rows = mx + jnp.log(den)
    out = jnp.zeros(q4.shape, jnp.float32)
    lse = jnp.zeros(q4.shape, jnp.float32)
    for h, mask in enumerate(in_head):
        rows = slice(h * Q_BLOCK, (h + 1) * Q_BLOCK)
        out = jnp.where(mask, pv[rows], out)
        lse = jnp.where(mask, lse_rows[rows], lse)
    o_ref[0] = out.astype(o_ref.dtype)
    lse_ref[0] = lse


def dilated_attention(q, k, v, bias, dilation):
    b, length, width = q.shape
    gw = DIL_GROUP_WIDTH
    cur = pl.BlockSpec((1, Q_BLOCK, gw), lambda bi, c, m: (bi, m, c))
    prev = pl.BlockSpec((1, Q_BLOCK, gw), lambda bi, c, m: (bi, jnp.maximum(m - 1, 0), c))
    return pl.pallas_call(
        _dil_attn_kernel,
        grid=(b, width // gw, length // Q_BLOCK),
        in_specs=[cur, prev, cur, prev, cur,
                  pl.BlockSpec((DIL_HEADS_PER_GROUP * Q_BLOCK, 2 * Q_BLOCK), lambda bi, c, m: (0, 0))],
        out_specs=[cur, cur],
        out_shape=[jax.ShapeDtypeStruct((b, length, width), jnp.bfloat16),
                   jax.ShapeDtypeStruct((b, length, width), jnp.float32)],
        compiler_params=_cparams("parallel", "parallel", "arbitrary"),
        name=f"dilated_attention_r{dilation}",
    )(q, k, k, v, v, bias.reshape(DIL_HEADS_PER_GROUP * Q_BLOCK, 2 * Q_BLOCK))


def _t5_causal_bucket(dist):
    max_exact = N_BUCKETS // 2
    d = jnp.maximum(dist, 1).astype(jnp.float32)
    large = max_exact + (jnp.log(d / max_exact) / math.log(MAX_DISTANCE / max_exact)
                         * (N_BUCKETS - max_exact)).astype(jnp.int32)
    large = jnp.minimum(large, N_BUCKETS - 1)
    return jnp.where(dist < max_exact, dist, large)


def dilated_bias(rel_bias_group, window, dilation):
    qi = jnp.arange(Q_BLOCK)[:, None]
    kj = jnp.arange(2 * Q_BLOCK)[None, :]
    dist_sub = qi + Q_BLOCK - kj
    bucket = _t5_causal_bucket(jnp.maximum(dist_sub, 0) * dilation)
    onehot = (bucket[..., None] == jnp.arange(N_BUCKETS)).astype(jnp.float32)
    bias = jnp.einsum("qkn,nh->hqk", onehot, rel_bias_group.astype(jnp.float32), precision=lax.Precision.HIGHEST)
    valid = (dist_sub >= 0) & (dist_sub <= window // dilation)
    return jnp.where(valid[None], bias, NEG_INF)


def _mix_out_kernel(oa_ref, o0_ref, o1_ref, o2_ref, l0_ref, l1_ref, l2_ref, ga_ref, gb_ref, x_ref, mod_ref,
                    wa_ref, wb_ref, wo_ref, gpost_ref, gpre_ref, wr_ref, wsgu_ref, wsd_ref,
                    x1_ref, hp_ref, sh_ref, lt_ref, stage_ref):
    f32 = jnp.float32
    tm = x_ref.shape[1]

    def token_major(ref, dilation):
        if dilation == 1:
            return ref[0].astype(f32)
        slabs = DIL_GROUP_WIDTH // LANES
        for c in range(dilation):
            for j in range(slabs):
                col = c * DIL_GROUP_WIDTH + j * LANES
                stage_ref[j, pl.ds(c, tm // dilation, stride=dilation), :] = ref[0, :, col:col + LANES].astype(f32)
        return jnp.concatenate([stage_ref[j] for j in range(slabs)], axis=1)

    dils = [r for _, r in DIL_CONFIGS]
    l0, l1, l2 = (token_major(ref, r) for ref, r in zip((l0_ref, l1_ref, l2_ref), dils))
    mx = jnp.maximum(jnp.maximum(l0, l1), l2)
    e0, e1, e2 = jnp.exp(l0 - mx), jnp.exp(l1 - mx), jnp.exp(l2 - mx)
    den = e0 + e1 + e2
    o0, o1, o2 = (token_major(ref, r) for ref, r in zip((o0_ref, o1_ref, o2_ref), dils))
    out_b = (e0 / den) * o0 + (e1 / den) * o1 + (e2 / den) * o2
    ya = _dot(oa_ref[0], wa_ref[...])
    yb = _dot(out_b.astype(jnp.bfloat16), wb_ref[...])
    merged = ga_ref[0].astype(f32) * ya + gb_ref[0].astype(f32) * yb
    y = _dot(merged.astype(jnp.bfloat16), wo_ref[...])
    x1 = x_ref[0] + mod_ref[0, 2:3, :] * _rms_norm(y, gpost_ref[...])
    x1_ref[0] = x1
    h = _rms_norm(x1, gpre_ref[...]) * (1.0 + mod_ref[0, 4:5, :]) + mod_ref[0, 3:4, :]
    hb = h.astype(jnp.bfloat16)
    _pack_rows(hp_ref, h)
    lt_ref[...] = _dot_nt(wr_ref[...], hb)
    gu = _dot(hb, wsgu_ref[...])
    g, u = gu[:, :EXPERT_DIM], gu[:, EXPERT_DIM:]
    sh_ref[0] = _dot((g * _sigmoid(g) * u).astype(jnp.bfloat16), wsd_ref[...])


def mix_out(out_a, o_g, lse_g, gate_a, gate_b, x, mod, w_a, w_b, w_o, g_post, g_pre_ffn, w_rT, w_sgu, w_sd, *, tm=256):
    b, s, d = x.shape
    tok = lambda width: pl.BlockSpec((1, tm, width), lambda bi, i: (bi, i, 0))
    full = lambda a: pl.BlockSpec(a.shape, lambda bi, i: (0,) * a.ndim)
    nt = s // tm
    g_post, g_pre_ffn = g_post.reshape(1, d), g_pre_ffn.reshape(1, d)
    weights = (w_a, w_b, w_o, g_post, g_pre_ffn, w_rT, w_sgu, w_sd)
    return pl.pallas_call(
        _mix_out_kernel,
        grid=(b, nt),
        in_specs=[tok(SB_WIDTH)] + [_dilated_spec(tm, r) for _, r in DIL_CONFIGS] * 2 + [tok(d)] * 3
        + [pl.BlockSpec((1, N_MOD, d), lambda bi, i: (bi, 0, 0))] + [full(w) for w in weights],
        out_specs=[tok(d), pl.BlockSpec((tm * ROW_SUB, LANES), lambda bi, i: (bi * nt + i, 0)), tok(d),
                   pl.BlockSpec((N_EXPERTS, tm), lambda bi, i: (0, bi * nt + i))],
        out_shape=[jax.ShapeDtypeStruct((b, s, d), jnp.float32),
                   jax.ShapeDtypeStruct((b * s * ROW_SUB, LANES), jnp.uint32),
                   jax.ShapeDtypeStruct((b, s, d), jnp.float32),
                   jax.ShapeDtypeStruct((N_EXPERTS, b * s), jnp.float32)],
        scratch_shapes=[pltpu.VMEM((DIL_GROUP_WIDTH // LANES, tm, LANES), jnp.float32)],
        compiler_params=_cparams("parallel", "parallel"),
        name="mix_out",
    )(out_a, *o_g, *lse_g, gate_a, gate_b, x, mod, *weights)


def _route_kernel(lt_ref, bias_ref, tri_ref, idx_ref, gate_ref, rank_ref, cnt_ref, base_ref, *, tn):
    f32 = jnp.float32

    @pl.when(pl.program_id(0) == 0)
    def _():
        base_ref[...] = jnp.zeros_like(base_ref)

    per_group = N_EXPERTS // N_GROUPS
    scores = _sigmoid(lt_ref[...])
    sel = scores + bias_ref[...]
    sel_groups = [sel[g * per_group:(g + 1) * per_group] for g in range(N_GROUPS)]
    ri = lax.broadcasted_iota(jnp.int32, (per_group, tn), 0)
    group_scores = []
    for sg in sel_groups:
        m1 = jnp.max(sg, axis=0, keepdims=True)
        first = jnp.min(jnp.where(sg == m1, ri, per_group), axis=0, keepdims=True)
        group_scores.append(m1 + jnp.max(jnp.where(ri == first, NEG_INF, sg), axis=0, keepdims=True))
    cur = jnp.concatenate(group_scores, axis=0)
    gi = lax.broadcasted_iota(jnp.int32, cur.shape, 0)
    keep = jnp.zeros(cur.shape, jnp.float32)
    for _ in range(TOP_K_GROUPS):
        mg = jnp.max(cur, axis=0, keepdims=True)
        pick = gi == jnp.min(jnp.where(cur == mg, gi, N_GROUPS), axis=0, keepdims=True)
        keep = jnp.where(pick, 1.0, keep)
        cur = jnp.where(pick, NEG_INF, cur)
    cur = jnp.concatenate([jnp.where(keep[g:g + 1, :] > 0.0, sg, NEG_INF) for g, sg in enumerate(sel_groups)], axis=0)
    ei = lax.broadcasted_iota(jnp.int32, cur.shape, 0)
    picks, idxs, gates = [], [], []
    for _ in range(TOP_K):
        mk = jnp.max(cur, axis=0, keepdims=True)
        fk = jnp.min(jnp.where(cur == mk, ei, N_EXPERTS), axis=0, keepdims=True)
        pick = ei == fk
        picks.append(pick)
        idxs.append(fk)
        gates.append(jnp.sum(jnp.where(pick, scores, 0.0), axis=0, keepdims=True))
        cur = jnp.where(pick, NEG_INF, cur)
    gate = jnp.concatenate(gates, axis=0)
    gate_ref[...] = gate / jnp.sum(gate, axis=0, keepdims=True) * ROUTED_SCALE
    idx_ref[...] = jnp.concatenate(idxs, axis=0)
    chosen = functools.reduce(jnp.logical_or, picks)
    onehot = jnp.where(chosen, 1.0, 0.0)
    earlier = _dot(onehot.astype(jnp.bfloat16), tri_ref[...]) + base_ref[...]
    ranks = [jnp.sum(jnp.where(p, earlier, 0.0), axis=0, keepdims=True) for p in picks]
    rank_ref[...] = jnp.concatenate(ranks, axis=0).astype(jnp.int32)
    base = base_ref[...] + jnp.sum(onehot, axis=1, keepdims=True)
    base_ref[...] = base
    cnt_ref[...] = jnp.broadcast_to(base, cnt_ref.shape).astype(jnp.int32)


def route(logits_t, router_bias, *, tn=512):
    ne, n = logits_t.shape
    tri = jnp.asarray(np.triu(np.ones((tn, tn), np.float32), k=1), jnp.bfloat16)
    slot = pl.BlockSpec((TOP_K, tn), lambda i: (0, i))
    return pl.pallas_call(
        functools.partial(_route_kernel, tn=tn),
        grid=(n // tn,),
        in_specs=[pl.BlockSpec((ne, tn), lambda i: (0, i)),
                  pl.BlockSpec((ne, 1), lambda i: (0, 0)),
                  pl.BlockSpec((tn, tn), lambda i: (0, 0))],
        out_specs=[slot, slot, slot, pl.BlockSpec((ne, LANES), lambda i: (0, 0))],
        out_shape=[jax.ShapeDtypeStruct((TOP_K, n), jnp.int32),
                   jax.ShapeDtypeStruct((TOP_K, n), jnp.float32),
                   jax.ShapeDtypeStruct((TOP_K, n), jnp.int32),
                   jax.ShapeDtypeStruct((ne, LANES), jnp.int32)],
        scratch_shapes=[pltpu.VMEM((ne, 1), jnp.float32)],
        compiler_params=_cparams("arbitrary"),
        name="route",
    )(logits_t, router_bias.reshape(ne, 1), tri)


def _dest_kernel(idx_ref, rank_ref, start_ref, dest_ref):
    idx = idx_ref[...]
    ei = lax.broadcasted_iota(jnp.int32, (N_EXPERTS, idx.shape[1]), 0)
    start = start_ref[...]
    rows = [jnp.sum(jnp.where(ei == idx[k:k + 1, :], start, 0.0), axis=0, keepdims=True) for k in range(TOP_K)]
    dest_ref[...] = jnp.concatenate(rows, axis=0).astype(jnp.int32) + rank_ref[...]


def slot_destinations(idx, rank, starts, *, tn=512):
    kk, n = idx.shape
    slot = pl.BlockSpec((kk, tn), lambda i: (0, i))
    return pl.pallas_call(
        _dest_kernel,
        grid=(n // tn,),
        in_specs=[slot, slot, pl.BlockSpec((N_EXPERTS, 1), lambda i: (0, 0))],
        out_specs=slot,
        out_shape=jax.ShapeDtypeStruct((kk, n), jnp.int32),
        compiler_params=_cparams("parallel"),
        name="slot_destinations",
    )(idx, rank, starts.astype(jnp.float32).reshape(N_EXPERTS, 1))


def _packed_row(ref, row):
    return ref.at[pl.ds(pl.multiple_of(row * ROW_SUB, ROW_SUB), ROW_SUB), :]


def _dispatch_kernel(dest_ref, h_ref, xs_ref, zero_ref, sem, *, tm, n_rows):
    pad = zero_ref.shape[0]

    @pl.when(pl.program_id(0) == 0)
    def _():
        zero_ref[...] = jnp.zeros_like(zero_ref)
        tail = pltpu.make_async_copy(zero_ref, xs_ref.at[pl.ds(n_rows * ROW_SUB, pad), :], sem)
        tail.start()
        tail.wait()

    def body(t, _):
        for k in range(TOP_K):
            pltpu.make_async_copy(_packed_row(h_ref, t), _packed_row(xs_ref, dest_ref[k, t]), sem).start(priority=k % 2)
        return 0

    lax.fori_loop(0, tm, body, 0)
    for _ in range(TOP_K):
        pltpu.make_async_copy(h_ref, xs_ref.at[pl.ds(0, tm * ROW_SUB), :], sem).wait()


def dispatch(dest, hp, *, pad_rows, tm=512):
    kk, n = dest.shape
    n_rows = n * kk
    return pl.pallas_call(
        functools.partial(_dispatch_kernel, tm=tm, n_rows=n_rows),
        grid=(n // tm,),
        in_specs=[pl.BlockSpec((kk, tm), lambda i: (0, i), memory_space=pltpu.SMEM),
                  pl.BlockSpec((tm * ROW_SUB, LANES), lambda i: (i, 0))],
        out_specs=pl.BlockSpec(memory_space=pl.ANY),
        out_shape=jax.ShapeDtypeStruct(((n_rows + pad_rows) * ROW_SUB, LANES), hp.dtype),
        scratch_shapes=[pltpu.VMEM((pad_rows * ROW_SUB, LANES), hp.dtype), pltpu.SemaphoreType.DMA],
        compiler_params=_cparams("arbitrary"),
        name="dispatch",
    )(dest, hp)


EXPERT_CHUNK = 256
EXPERT_LOOKAHEAD = 3
EXPERT_IN_BUFS = EXPERT_LOOKAHEAD + 1
EXPERT_OUT_BUFS = 3


def _expert_kernel(first_ref, nch_ref, crow_ref, total_ref, xs_ref, wg_ref, wu_ref, wd_ref, ys_ref,
                   xbuf, ybuf, wgu_bf, wd_bf, sem_in, sem_out, waited_ref):
    e = pl.program_id(0)
    chunk = EXPERT_CHUNK
    first, n_chunks, total = first_ref[e], nch_ref[e], total_ref[0]

    def rows(g):
        return pl.ds(pl.multiple_of(crow_ref[g] * ROW_SUB, ROW_SUB), chunk * ROW_SUB)

    def in_copy(g):
        slot = lax.rem(g, EXPERT_IN_BUFS)
        return pltpu.make_async_copy(xs_ref.at[rows(g), :], xbuf.at[slot], sem_in.at[slot])

    def out_copy(g):
        slot = lax.rem(g, EXPERT_OUT_BUFS)
        return pltpu.make_async_copy(ybuf.at[slot], ys_ref.at[rows(g), :], sem_out.at[slot])

    def wait_writes_through(g):
        def wait_one(j, carry):
            out_copy(j).wait()
            return carry

        lax.fori_loop(waited_ref[0] + 1, g + 1, wait_one, 0)
        waited_ref[0] = jnp.maximum(waited_ref[0], g)

    @pl.when(e == 0)
    def _():
        waited_ref[0] = -1
        for g in range(EXPERT_LOOKAHEAD):
            @pl.when(g < total)
            def _():
                in_copy(g).start()

    @pl.when(n_chunks > 0)
    def _():
        wgu_bf[:, :EXPERT_DIM] = wg_ref[0].astype(wgu_bf.dtype)
        wgu_bf[:, EXPERT_DIM:] = wu_ref[0].astype(wgu_bf.dtype)
        wd_bf[...] = wd_ref[0].astype(wd_bf.dtype)

    def body(c, carry):
        g = first + c
        in_copy(g).wait()

        @pl.when(g + EXPERT_LOOKAHEAD < total)
        def _():
            in_copy(g + EXPERT_LOOKAHEAD).start()

        bf = jnp.bfloat16
        xa, xb = _unpack_rows(xbuf.at[lax.rem(g, EXPERT_IN_BUFS)], chunk)
        gu = _dot(xa.astype(bf), wgu_bf[:ROW_WORDS, :]) + _dot(xb.astype(bf), wgu_bf[ROW_WORDS:, :])
        gate, up = gu[:, :EXPERT_DIM], gu[:, EXPERT_DIM:]
        y = _dot((gate * _sigmoid(gate) * up).astype(bf), wd_bf[...])
        wait_writes_through(jnp.where(c == 0, g - 1, g - EXPERT_OUT_BUFS))
        _pack_rows(ybuf.at[lax.rem(g, EXPERT_OUT_BUFS)], y)
        out_copy(g).start()
        return carry

    lax.fori_loop(0, n_chunks, body, 0)

    @pl.when(e == pl.num_programs(0) - 1)
    def _():
        wait_writes_through(total - 1)
        ybuf[0] = jnp.zeros(ybuf.shape[1:], ybuf.dtype)
        pad = pl.ds(ys_ref.shape[0] - chunk * ROW_SUB, chunk * ROW_SUB)
        tail = pltpu.make_async_copy(ybuf.at[0], ys_ref.at[pad, :], sem_out.at[0])
        tail.start()
        tail.wait()


def expert_chunks(starts, counts, n_rows):
    chunk = EXPERT_CHUNK
    max_chunks = n_rows // chunk + N_EXPERTS
    nch = (counts + (chunk - 1)) // chunk
    ends = jnp.cumsum(nch)
    first = ends - nch
    g = jnp.arange(max_chunks, dtype=jnp.int32)
    owner = jnp.minimum(jnp.sum(g[:, None] >= ends[None, :], axis=1), N_EXPERTS - 1)
    onehot = (owner[:, None] == jnp.arange(N_EXPERTS)[None, :]).astype(jnp.int32)
    crow = (onehot * (starts - first * chunk)[None, :]).sum(axis=1) + g * chunk
    crow = jnp.where(g < ends[-1], crow, 0).astype(jnp.int32)
    return first.astype(jnp.int32), nch.astype(jnp.int32), crow, ends[-1:].astype(jnp.int32)


def expert_ffn(starts, counts, xs, w_gate, w_up, w_down):
    d = D_MODEL
    n_rows = xs.shape[0] // ROW_SUB - EXPERT_CHUNK
    tables = expert_chunks(starts, counts, n_rows)
    chunk_shape = (EXPERT_CHUNK * ROW_SUB, LANES)
    weights = lambda shape: pl.BlockSpec(shape, lambda e, *_: (e, 0, 0))
    grid_spec = pltpu.PrefetchScalarGridSpec(
        num_scalar_prefetch=len(tables),
        grid=(N_EXPERTS,),
        in_specs=[pl.BlockSpec(memory_space=pl.ANY),
                  weights((1, d, EXPERT_DIM)), weights((1, d, EXPERT_DIM)), weights((1, EXPERT_DIM, d))],
        out_specs=pl.BlockSpec(memory_space=pl.ANY),
        scratch_shapes=[pltpu.VMEM((EXPERT_IN_BUFS,) + chunk_shape, xs.dtype),
                        pltpu.VMEM((EXPERT_OUT_BUFS,) + chunk_shape, xs.dtype),
                        pltpu.VMEM((d, 2 * EXPERT_DIM), jnp.bfloat16), pltpu.VMEM((EXPERT_DIM, d), jnp.bfloat16),
                        pltpu.SemaphoreType.DMA((EXPERT_IN_BUFS,)), pltpu.SemaphoreType.DMA((EXPERT_OUT_BUFS,)),
                        pltpu.SMEM((1,), jnp.int32)],
    )
    return pl.pallas_call(
        _expert_kernel,
        grid_spec=grid_spec,
        out_shape=jax.ShapeDtypeStruct(xs.shape, xs.dtype),
        compiler_params=_cparams("arbitrary"),
        name="expert_ffn",
    )(*tables, xs, w_gate, w_up, w_down)


def _combine_kernel(dest_ref, ys_ref, gate_ref, sh_ref, x1_ref, mod_ref, g_ref, o_ref, buf, sem, *, tm):
    def body(t, _):
        for k in range(TOP_K):
            pltpu.make_async_copy(_packed_row(ys_ref, dest_ref[k, t]), _packed_row(buf.at[k], t), sem).start(
                priority=k % 2)
        return 0

    lax.fori_loop(0, tm, body, 0)
    for k in range(TOP_K):
        pltpu.make_async_copy(ys_ref.at[pl.ds(0, tm * ROW_SUB), :], buf.at[k], sem).wait()
    gate = gate_ref[...]
    first = jnp.zeros((tm, ROW_WORDS), jnp.float32)
    second = jnp.zeros((tm, ROW_WORDS), jnp.float32)
    for k in range(TOP_K):
        ya, yb = _unpack_rows(buf.at[k], tm)
        first = first + ya * gate[:, k:k + 1]
        second = second + yb * gate[:, k:k + 1]
    y = sh_ref[...] + jnp.concatenate([first, second], axis=1)
    o_ref[...] = x1_ref[...] + mod_ref[0, 5:6, :] * _rms_norm(y, g_ref[...])


def combine(dest, ys, gate_nk, shared, x1, mod, g_post_ffn, *, seq, tm=128):
    n, d = x1.shape
    kk = dest.shape[0]
    per_batch = seq // tm
    tok = pl.BlockSpec((tm, d), lambda i: (i, 0))
    return pl.pallas_call(
        functools.partial(_combine_kernel, tm=tm),
        grid=(n // tm,),
        in_specs=[pl.BlockSpec((kk, tm), lambda i: (0, i), memory_space=pltpu.SMEM),
                  pl.BlockSpec(memory_space=pl.ANY),
                  pl.BlockSpec((tm, kk), lambda i: (i, 0)),
                  tok, tok,
                  pl.BlockSpec((1, N_MOD, d), lambda i: (i // per_batch, 0, 0)),
                  pl.BlockSpec((1, d), lambda i: (0, 0))],
        out_specs=tok,
        out_shape=jax.ShapeDtypeStruct((n, d), jnp.float32),
        scratch_shapes=[pltpu.VMEM((kk, tm * ROW_SUB, LANES), ys.dtype), pltpu.SemaphoreType.DMA],
        compiler_params=_cparams("arbitrary"),
        name="combine",
    )(dest, ys, gate_nk, shared, x1, mod, g_post_ffn.reshape(1, d))


def _layer(x, mod, rel_bias, g_pre_mix, w_in, w_branch_a, w_branch_b, w_out, g_post_mix, g_pre_ffn, w_router,
           router_bias, w_exp_gate, w_exp_up, w_exp_down, w_sh_gate, w_sh_up, w_sh_down, g_post_ffn):
    b, s, d = x.shape
    bf = jnp.bfloat16
    scale = HEAD_DIM ** -0.5
    sb_q, sb_k, sb_v, dq, dk, dv, w_ga, w_gb = jnp.split(
        w_in, np.cumsum((SB_WIDTH,) * 3 + (DIL_WIDTH,) * 3 + (d,)).tolist(), axis=1)
    groups = lambda w: [w[:, g * DIL_GROUP_WIDTH:(g + 1) * DIL_GROUP_WIDTH] for g in range(len(DIL_CONFIGS))]
    w_rows = jnp.concatenate([sb_k] + [w * scale for w in groups(dq)] + groups(dk) + groups(dv) + [w_ga, w_gb],
                             axis=1).astype(bf)
    w_cols = jnp.concatenate([sb_q * scale, sb_v], axis=1).T.astype(bf)
    qT, k, vT3, *rest = in_projection(x, mod, g_pre_mix, w_rows, w_cols)
    dqs, dks, dvs, gate_a, gate_b = rest[0:3], rest[3:6], rest[6:9], rest[9], rest[10]
    out_a = sb_attention(qT, k, vT3)
    o_g, lse_g = [], []
    for g, (window, dilation) in enumerate(DIL_CONFIGS):
        bias = dilated_bias(rel_bias[:, g * DIL_HEADS_PER_GROUP:(g + 1) * DIL_HEADS_PER_GROUP], window, dilation)
        o, lse = dilated_attention(dqs[g], dks[g], dvs[g], bias, dilation)
        o_g.append(o)
        lse_g.append(lse)
    x1, hp, shared, logits_t = mix_out(
        out_a, o_g, lse_g, gate_a, gate_b, x, mod, w_branch_a.astype(bf), w_branch_b.astype(bf), w_out.astype(bf),
        g_post_mix, g_pre_ffn, w_router.T.astype(bf), jnp.concatenate([w_sh_gate, w_sh_up], axis=1).astype(bf),
        w_sh_down.astype(bf))
    n = b * s
    idx, gate, rank, counts = route(logits_t, router_bias)
    counts = counts[:, 0]
    starts = jnp.cumsum(counts) - counts
    dest = slot_destinations(idx, rank, starts)
    xs = dispatch(dest, hp, pad_rows=EXPERT_CHUNK)
    ys = expert_ffn(starts, counts, xs, w_exp_gate, w_exp_up, w_exp_down)
    out = combine(dest, ys, gate.T, shared.reshape(n, d), x1.reshape(n, d), mod, g_post_ffn, seq=s)
    return out.reshape(b, s, d)


def kernel(x, c, rel_bias, w_ada, b_ada, g_pre_mix, w_in, w_branch_a, w_branch_b, w_out, g_post_mix, g_pre_ffn,
           w_router, router_bias, w_exp_gate, w_exp_up, w_exp_down, w_sh_gate, w_sh_up, w_sh_down, g_post_ffn):
    depth = w_in.shape[0]
    for layer in range(depth):
        mod = ada_modulation(c, w_ada[layer], b_ada[layer]).reshape(x.shape[0], N_MOD, x.shape[2])
        x = _layer(x, mod, rel_bias, g_pre_mix[layer], w_in[layer], w_branch_a[layer], w_branch_b[layer], w_out[layer],
                   g_post_mix[layer], g_pre_ffn[layer], w_router[layer], router_bias[layer], w_exp_gate[layer],
                   w_exp_up[layer], w_exp_down[layer], w_sh_gate[layer], w_sh_up[layer], w_sh_down[layer],
                   g_post_ffn[layer])
    return x
```

```python
import functools
import math

import jax
import jax.numpy as jnp
import numpy as np
from jax import lax
from jax.experimental import pallas as pl
from jax.experimental.pallas import tpu as pltpu

D_MODEL = 1024
HEAD_DIM = 64
SB_HEADS = 8
SB_WIDTH = SB_HEADS * HEAD_DIM
DIL_CONFIGS = ((128, 1), (512, 4), (2048, 16))
DIL_HEADS_PER_GROUP = 4
DIL_GROUP_WIDTH = DIL_HEADS_PER_GROUP * HEAD_DIM
DIL_WIDTH = DIL_GROUP_WIDTH * len(DIL_CONFIGS)
Q_BLOCK = 128
N_BUCKETS = 32
MAX_DISTANCE = 2048
N_EXPERTS = 256
TOP_K = 8
N_GROUPS = 8
TOP_K_GROUPS = 4
EXPERT_DIM = 256
ROUTED_SCALE = 2.5
N_MOD = 6
EPS = 1e-6

LANES = 128
VMEM_LIMIT_BYTES = 56 * 1024 * 1024
NEG_INF = float("-inf")
LOG2E = math.log2(math.e)


def _cparams(*semantics):
    return pltpu.CompilerParams(dimension_semantics=semantics, vmem_limit_bytes=VMEM_LIMIT_BYTES)


def _rms_norm(v, g):
    return v * lax.rsqrt(jnp.mean(v * v, axis=-1, keepdims=True) + EPS) * g


def _sigmoid(v):
    return 1.0 / (1.0 + jnp.exp(-v))


def _dot(a, b):
    return jnp.dot(a, b, preferred_element_type=jnp.float32)


def _dot_nt(a, b):
    return lax.dot_general(a, b, (((1,), (1,)), ((), ())), preferred_element_type=jnp.float32)


def _split_bf16(v):
    hi = v.astype(jnp.bfloat16)
    lo = (v - hi.astype(jnp.float32)).astype(jnp.bfloat16)
    return hi, lo


ROW_WORDS = D_MODEL // 2
ROW_SUB = ROW_WORDS // LANES


def _pack_rows(ref, v):
    m = v.shape[0]
    bits = pltpu.bitcast(v.astype(jnp.bfloat16).astype(jnp.float32), jnp.uint32)
    words = bits[:, :ROW_WORDS] | (bits[:, ROW_WORDS:] >> 16)
    for j in range(ROW_SUB):
        ref[pl.ds(j, m, stride=ROW_SUB), :] = words[:, j * LANES:(j + 1) * LANES]


def _unpack_rows(ref, m):
    words = jnp.concatenate([ref[pl.ds(j, m, stride=ROW_SUB), :] for j in range(ROW_SUB)], axis=1)
    first = pltpu.bitcast(words & jnp.uint32(0xFFFF0000), jnp.float32)
    second = pltpu.bitcast(words << 16, jnp.float32)
    return first, second


def _ada_kernel(c_ref, w_ref, b_ref, o_ref):
    c = c_ref[...]
    a_hi, a_lo = _split_bf16(c * _sigmoid(c))
    w_hi, w_lo = _split_bf16(w_ref[...])
    o_ref[...] = _dot(a_hi, w_hi) + _dot(a_lo, w_hi) + _dot(a_hi, w_lo) + b_ref[...]


def ada_modulation(c, w_ada, b_ada, *, tn=1536):
    b, d = c.shape
    n = w_ada.shape[1]
    return pl.pallas_call(
        _ada_kernel,
        grid=(n // tn,),
        in_specs=[
            pl.BlockSpec((b, d), lambda j: (0, 0)),
            pl.BlockSpec((d, tn), lambda j: (0, j)),
            pl.BlockSpec((1, tn), lambda j: (0, j)),
        ],
        out_specs=pl.BlockSpec((b, tn), lambda j: (0, j)),
        out_shape=jax.ShapeDtypeStruct((b, n), jnp.float32),
        compiler_params=_cparams("parallel"),
        name="ada_modulation",
    )(c, w_ada, b_ada.reshape(1, n))


def _in_proj_kernel(x_ref, mod_ref, g_ref, w_rows_ref, w_cols_ref, qT_ref, k_ref, vT_ref, *rest, tm):
    dil_refs, ga_ref, gb_ref, stage_ref = rest[:9], rest[9], rest[10], rest[11]
    h = _rms_norm(x_ref[0], g_ref[...]) * (1.0 + mod_ref[0, 1:2, :]) + mod_ref[0, 0:1, :]
    hb = h.astype(jnp.bfloat16)
    off = 0
    k_ref[0] = _dot(hb, w_rows_ref[:, off:off + SB_WIDTH]).astype(k_ref.dtype)
    off += SB_WIDTH
    for n, r in enumerate(dil_refs):
        y = _dot(hb, w_rows_ref[:, off:off + DIL_GROUP_WIDTH])
        off += DIL_GROUP_WIDTH
        dilation = DIL_CONFIGS[n % len(DIL_CONFIGS)][1]
        if dilation == 1:
            r[0] = y.astype(r.dtype)
            continue
        for j in range(DIL_GROUP_WIDTH // LANES):
            stage_ref[j] = y[:, j * LANES:(j + 1) * LANES]
        for c in range(dilation):
            for j in range(DIL_GROUP_WIDTH // LANES):
                col = c * DIL_GROUP_WIDTH + j * LANES
                r[0, :, col:col + LANES] = stage_ref[j, pl.ds(c, tm // dilation, stride=dilation), :].astype(r.dtype)
    for r in (ga_ref, gb_ref):
        r[0] = _sigmoid(_dot(hb, w_rows_ref[:, off:off + D_MODEL])).astype(r.dtype)
        off += D_MODEL
    qT_ref[0] = _dot_nt(w_cols_ref[0:SB_WIDTH, :], hb).astype(qT_ref.dtype)
    vT = _dot_nt(w_cols_ref[SB_WIDTH:2 * SB_WIDTH, :], hb).astype(vT_ref.dtype)
    for p in range(SB_WIDTH // LANES):
        for jb in range(tm // LANES):
            vT_ref[0, p, jb] = vT[p * LANES:(p + 1) * LANES, jb * LANES:(jb + 1) * LANES]


def _dilated_shape(b, s, dilation, dtype):
    return jax.ShapeDtypeStruct((b, s // dilation, dilation * DIL_GROUP_WIDTH), dtype)


def _dilated_spec(tm, dilation):
    return pl.BlockSpec((1, tm // dilation, dilation * DIL_GROUP_WIDTH), lambda bi, i: (bi, i, 0))


def in_projection(x, mod, g_pre, w_rows, w_cols, *, tm=256):
    b, s, d = x.shape
    nt = s // tm
    bf = jnp.bfloat16
    tok = lambda width: pl.BlockSpec((1, tm, width), lambda bi, i: (bi, i, 0))
    out_shapes = [
        jax.ShapeDtypeStruct((b, SB_WIDTH, s), bf),
        jax.ShapeDtypeStruct((b, s, SB_WIDTH), bf),
        jax.ShapeDtypeStruct((b, SB_WIDTH // LANES, s // LANES, LANES, LANES), bf),
    ] + [_dilated_shape(b, s, r, bf) for _, r in DIL_CONFIGS] * 3 + [jax.ShapeDtypeStruct((b, s, d), bf)] * 2
    out_specs = [
        pl.BlockSpec((1, SB_WIDTH, tm), lambda bi, i: (bi, 0, i)),
        tok(SB_WIDTH),
        pl.BlockSpec((1, SB_WIDTH // LANES, tm // LANES, LANES, LANES), lambda bi, i: (bi, 0, i, 0, 0)),
    ] + [_dilated_spec(tm, r) for _, r in DIL_CONFIGS] * 3 + [tok(d)] * 2
    return pl.pallas_call(
        functools.partial(_in_proj_kernel, tm=tm),
        grid=(b, nt),
        in_specs=[
            tok(d),
            pl.BlockSpec((1, N_MOD, d), lambda bi, i: (bi, 0, 0)),
            pl.BlockSpec((1, d), lambda bi, i: (0, 0)),
            pl.BlockSpec(w_rows.shape, lambda bi, i: (0, 0)),
            pl.BlockSpec(w_cols.shape, lambda bi, i: (0, 0)),
        ],
        out_specs=out_specs,
        out_shape=out_shapes,
        scratch_shapes=[pltpu.VMEM((DIL_GROUP_WIDTH // LANES, tm, LANES), jnp.float32)],
        compiler_params=_cparams("parallel", "parallel"),
        name="in_projection",
    )(x, mod, g_pre.reshape(1, d), w_rows, w_cols)


SB_TILE = 512
SB_SUB = 256


def _sb_chunk(kc, vc, q_h, tri, csum, past):
    z = _dot(kc, q_h)
    neg_abs = pltpu.bitcast(pltpu.bitcast(z, jnp.uint32) | jnp.uint32(0x80000000), jnp.float32)
    drop = jnp.maximum(z, 0.0) + jnp.log(1.0 + jnp.exp2(neg_abs)) * LOG2E
    if past is not None:
        drop = jnp.where(past, drop, 0.0)
    weights = [None] * (SB_TILE // SB_SUB)
    for sb in reversed(range(SB_TILE // SB_SUB)):
        rows = slice(sb * SB_SUB, (sb + 1) * SB_SUB)
        d = drop[rows]
        later = _dot(tri, d.astype(jnp.bfloat16))
        w = jnp.exp2(z[rows] - d - later - csum)
        if past is not None:
            w = jnp.where(past[rows], w, 0.0)
        weights[sb] = w.astype(jnp.bfloat16)
        csum = csum + jnp.sum(d, axis=0, keepdims=True)
    return _dot(vc, jnp.concatenate(weights, axis=0)), csum


def _sb_attn_kernel(qT_ref, k_ref, vT_ref, tri_ref, o_ref, acc_ref, csum_ref):
    i = pl.program_id(2)
    t = SB_TILE
    q2 = qT_ref[0]
    row = lax.broadcasted_iota(jnp.int32, q2.shape, 0)
    zero = jnp.zeros_like(q2)
    q_heads = (jnp.where(row < HEAD_DIM, q2, zero), jnp.where(row >= HEAD_DIM, q2, zero))
    tri = tri_ref[...]
    blocks = t // LANES

    def chunk(c, past, first):
        kc = k_ref[0, pl.ds(pl.multiple_of(c * t, t), t), :]
        vc = jnp.concatenate([vT_ref[0, 0, c * blocks + jb] for jb in range(blocks)], axis=1)
        for h in range(2):
            rows = slice(h * HEAD_DIM, (h + 1) * HEAD_DIM)
            csum = jnp.zeros((1, t), jnp.float32) if first else csum_ref[h:h + 1, :]
            part, csum = _sb_chunk(kc, vc[rows], q_heads[h], tri, csum, past)
            acc_ref[rows, :] = part if first else acc_ref[rows, :] + part
            csum_ref[h:h + 1, :] = csum

    key = lax.broadcasted_iota(jnp.int32, (t, t), 0)
    query = lax.broadcasted_iota(jnp.int32, (t, t), 1)
    chunk(i, key < query, True)

    def body(it, carry):
        chunk(i - 1 - it, None, False)
        return carry

    lax.fori_loop(0, i, body, 0)
    o_ref[0] = acc_ref[...].T.astype(o_ref.dtype)


def sb_attention(qT, k, vT3):
    b, width, s = qT.shape
    n_pairs = width // LANES
    t = SB_TILE
    tri = jnp.asarray(np.triu(np.ones((SB_SUB, SB_SUB), np.float32), k=1), jnp.bfloat16)
    return pl.pallas_call(
        _sb_attn_kernel,
        grid=(b, n_pairs, s // t),
        in_specs=[
            pl.BlockSpec((1, LANES, t), lambda bi, p, i: (bi, p, i)),
            pl.BlockSpec((1, s, LANES), lambda bi, p, i: (bi, 0, p)),
            pl.BlockSpec((1, 1, s // LANES, LANES, LANES), lambda bi, p, i: (bi, p, 0, 0, 0)),
            pl.BlockSpec((SB_SUB, SB_SUB), lambda bi, p, i: (0, 0)),
        ],
        out_specs=pl.BlockSpec((1, t, LANES), lambda bi, p, i: (bi, i, p)),
        out_shape=jax.ShapeDtypeStruct((b, s, width), jnp.bfloat16),
        scratch_shapes=[pltpu.VMEM((LANES, t), jnp.float32), pltpu.VMEM((2, t), jnp.float32)],
        compiler_params=_cparams("parallel", "parallel", "arbitrary"),
        name="sb_attention",
    )(qT, k, vT3, tri)


def _dil_attn_kernel(q_ref, kp_ref, kc_ref, vp_ref, vc_ref, bias_ref, o_ref, lse_ref):
    m = pl.program_id(2)
    q4 = q_ref[0]
    kk = jnp.concatenate([kp_ref[0], kc_ref[0]], axis=0)
    vv = jnp.concatenate([vp_ref[0], vc_ref[0]], axis=0)
    lane = lax.broadcasted_iota(jnp.int32, q4.shape, 1)
    zero = jnp.zeros_like(q4)
    in_head = [jnp.logical_and(lane >= h * HEAD_DIM, lane < (h + 1) * HEAD_DIM) for h in range(DIL_HEADS_PER_GROUP)]
    q_stack = jnp.concatenate([jnp.where(mask, q4, zero) for mask in in_head], axis=0)
    logits = _dot_nt(q_stack, kk) + bias_ref[...]
    key = lax.broadcasted_iota(jnp.int32, logits.shape, 1)
    logits = jnp.where(jnp.logical_or(key >= Q_BLOCK, m > 0), logits, NEG_INF)
    mx = jnp.max(logits, axis=1, keepdims=True)
    p = jnp.exp(logits - mx)
    den = jnp.sum(p, axis=1, keepdims=True)
    pv = _dot(p.astype(jnp.bfloat16), vv) / den
    lse_rows = mx + jnp.log(den)
    out = jnp.zeros(q4.shape, jnp.float32)
    lse = jnp.zeros(q4.shape, jnp.float32)
    for h, mask in enumerate(in_head):
        rows = slice(h * Q_BLOCK, (h + 1) * Q_BLOCK)
        out = jnp.where(mask, pv[rows], out)
        lse = jnp.where(mask, lse_rows[rows], lse)
    o_ref[0] = out.astype(o_ref.dtype)
    lse_ref[0] = lse


def dilated_attention(q, k, v, bias, dilation):
    b, length, width = q.shape
    gw = DIL_GROUP_WIDTH
    cur = pl.BlockSpec((1, Q_BLOCK, gw), lambda bi, c, m: (bi, m, c))
    prev = pl.BlockSpec((1, Q_BLOCK, gw), lambda bi, c, m: (bi, jnp.maximum(m - 1, 0), c))
    return pl.pallas_call(
        _dil_attn_kernel,
        grid=(b, width // gw, length // Q_BLOCK),
        in_specs=[cur, prev, cur, prev, cur,
                  pl.BlockSpec((DIL_HEADS_PER_GROUP * Q_BLOCK, 2 * Q_BLOCK), lambda bi, c, m: (0, 0))],
        out_specs=[cur, cur],
        out_shape=[jax.ShapeDtypeStruct((b, length, width), jnp.bfloat16),
                   jax.ShapeDtypeStruct((b, length, width), jnp.float32)],
        compiler_params=_cparams("parallel", "parallel", "arbitrary"),
        name=f"dilated_attention_r{dilation}",
    )(q, k, k, v, v, bias.reshape(DIL_HEADS_PER_GROUP * Q_BLOCK, 2 * Q_BLOCK))


def _t5_causal_bucket(dist):
    max_exact = N_BUCKETS // 2
    d = jnp.maximum(dist, 1).astype(jnp.float32)
    large = max_exact + (jnp.log(d / max_exact) / math.log(MAX_DISTANCE / max_exact)
                         * (N_BUCKETS - max_exact)).astype(jnp.int32)
    large = jnp.minimum(large, N_BUCKETS - 1)
    return jnp.where(dist < max_exact, dist, large)


def dilated_bias(rel_bias_group, window, dilation):
    qi = jnp.arange(Q_BLOCK)[:, None]
    kj = jnp.arange(2 * Q_BLOCK)[None, :]
    dist_sub = qi + Q_BLOCK - kj
    bucket = _t5_causal_bucket(jnp.maximum(dist_sub, 0) * dilation)
    onehot = (bucket[..., None] == jnp.arange(N_BUCKETS)).astype(jnp.float32)
    bias = jnp.einsum("qkn,nh->hqk", onehot, rel_bias_group.astype(jnp.float32), precision=lax.Precision.HIGHEST)
    valid = (dist_sub >= 0) & (dist_sub <= window // dilation)
    return jnp.where(valid[None], bias, NEG_INF)


def _mix_out_kernel(oa_ref, o0_ref, o1_ref, o2_ref, l0_ref, l1_ref, l2_ref, ga_ref, gb_ref, x_ref, mod_ref,
                    wa_ref, wb_ref, wo_ref, gpost_ref, gpre_ref, wr_ref, wsgu_ref, wsd_ref,
                    x1_ref, hp_ref, sh_ref, lt_ref, stage_ref):
    f32 = jnp.float32
    tm = x_ref.shape[1]

    def token_major(ref, dilation):
        if dilation == 1:
            return ref[0].astype(f32)
        slabs = DIL_GROUP_WIDTH // LANES
        for c in range(dilation):
            for j in range(slabs):
                col = c * DIL_GROUP_WIDTH + j * LANES
                stage_ref[j, pl.ds(c, tm // dilation, stride=dilation), :] = ref[0, :, col:col + LANES].astype(f32)
        return jnp.concatenate([stage_ref[j] for j in range(slabs)], axis=1)

    dils = [r for _, r in DIL_CONFIGS]
    l0, l1, l2 = (token_major(ref, r) for ref, r in zip((l0_ref, l1_ref, l2_ref), dils))
    mx = jnp.maximum(jnp.maximum(l0, l1), l2)
    e0, e1, e2 = jnp.exp(l0 - mx), jnp.exp(l1 - mx), jnp.exp(l2 - mx)
    den = e0 + e1 + e2
    o0, o1, o2 = (token_major(ref, r) for ref, r in zip((o0_ref, o1_ref, o2_ref), dils))
    out_b = (e0 / den) * o0 + (e1 / den) * o1 + (e2 / den) * o2
    ya = _dot(oa_ref[0], wa_ref[...])
    yb = _dot(out_b.astype(jnp.bfloat16), wb_ref[...])
    merged = ga_ref[0].astype(f32) * ya + gb_ref[0].astype(f32) * yb
    y = _dot(merged.astype(jnp.bfloat16), wo_ref[...])
    x1 = x_ref[0] + mod_ref[0, 2:3, :] * _rms_norm(y, gpost_ref[...])
    x1_ref[0] = x1
    h = _rms_norm(x1, gpre_ref[...]) * (1.0 + mod_ref[0, 4:5, :]) + mod_ref[0, 3:4, :]
    hb = h.astype(jnp.bfloat16)
    _pack_rows(hp_ref, h)
    lt_ref[...] = _dot_nt(wr_ref[...], hb)
    gu = _dot(hb, wsgu_ref[...])
    g, u = gu[:, :EXPERT_DIM], gu[:, EXPERT_DIM:]
    sh_ref[0] = _dot((g * _sigmoid(g) * u).astype(jnp.bfloat16), wsd_ref[...])


def mix_out(out_a, o_g, lse_g, gate_a, gate_b, x, mod, w_a, w_b, w_o, g_post, g_pre_ffn, w_rT, w_sgu, w_sd, *, tm=256):
    b, s, d = x.shape
    tok = lambda width: pl.BlockSpec((1, tm, width), lambda bi, i: (bi, i, 0))
    full = lambda a: pl.BlockSpec(a.shape, lambda bi, i: (0,) * a.ndim)
    nt = s // tm
    g_post, g_pre_ffn = g_post.reshape(1, d), g_pre_ffn.reshape(1, d)
    weights = (w_a, w_b, w_o, g_post, g_pre_ffn, w_rT, w_sgu, w_sd)
    return pl.pallas_call(
        _mix_out_kernel,
        grid=(b, nt),
        in_specs=[tok(SB_WIDTH)] + [_dilated_spec(tm, r) for _, r in DIL_CONFIGS] * 2 + [tok(d)] * 3
        + [pl.BlockSpec((1, N_MOD, d), lambda bi, i: (bi, 0, 0))] + [full(w) for w in weights],
        out_specs=[tok(d), pl.BlockSpec((tm * ROW_SUB, LANES), lambda bi, i: (bi * nt + i, 0)), tok(d),
                   pl.BlockSpec((N_EXPERTS, tm), lambda bi, i: (0, bi * nt + i))],
        out_shape=[jax.ShapeDtypeStruct((b, s, d), jnp.float32),
                   jax.ShapeDtypeStruct((b * s * ROW_SUB, LANES), jnp.uint32),
                   jax.ShapeDtypeStruct((b, s, d), jnp.float32),
                   jax.ShapeDtypeStruct((N_EXPERTS, b * s), jnp.float32)],
        scratch_shapes=[pltpu.VMEM((DIL_GROUP_WIDTH // LANES, tm, LANES), jnp.float32)],
        compiler_params=_cparams("parallel", "parallel"),
        name="mix_out",
    )(out_a, *o_g, *lse_g, gate_a, gate_b, x, mod, *weights)


def _route_kernel(lt_ref, bias_ref, tri_ref, idx_ref, gate_ref, rank_ref, cnt_ref, base_ref, *, tn):
    f32 = jnp.float32

    @pl.when(pl.program_id(0) == 0)
    def _():
        base_ref[...] = jnp.zeros_like(base_ref)

    per_group = N_EXPERTS // N_GROUPS
    scores = _sigmoid(lt_ref[...])
    sel = scores + bias_ref[...]
    sel_groups = [sel[g * per_group:(g + 1) * per_group] for g in range(N_GROUPS)]
    ri = lax.broadcasted_iota(jnp.int32, (per_group, tn), 0)
    group_scores = []
    for sg in sel_groups:
        m1 = jnp.max(sg, axis=0, keepdims=True)
        first = jnp.min(jnp.where(sg == m1, ri, per_group), axis=0, keepdims=True)
        group_scores.append(m1 + jnp.max(jnp.where(ri == first, NEG_INF, sg), axis=0, keepdims=True))
    cur = jnp.concatenate(group_scores, axis=0)
    gi = lax.broadcasted_iota(jnp.int32, cur.shape, 0)
    keep = jnp.zeros(cur.shape, jnp.float32)
    for _ in range(TOP_K_GROUPS):
        mg = jnp.max(cur, axis=0, keepdims=True)
        pick = gi == jnp.min(jnp.where(cur == mg, gi, N_GROUPS), axis=0, keepdims=True)
        keep = jnp.where(pick, 1.0, keep)
        cur = jnp.where(pick, NEG_INF, cur)
    cur = jnp.concatenate([jnp.where(keep[g:g + 1, :] > 0.0, sg, NEG_INF) for g, sg in enumerate(sel_groups)], axis=0)
    ei = lax.broadcasted_iota(jnp.int32, cur.shape, 0)
    picks, idxs, gates = [], [], []
    for _ in range(TOP_K):
        mk = jnp.max(cur, axis=0, keepdims=True)
        fk = jnp.min(jnp.where(cur == mk, ei, N_EXPERTS), axis=0, keepdims=True)
        pick = ei == fk
        picks.append(pick)
        idxs.append(fk)
        gates.append(jnp.sum(jnp.where(pick, scores, 0.0), axis=0, keepdims=True))
        cur = jnp.where(pick, NEG_INF, cur)
    gate = jnp.concatenate(gates, axis=0)
    gate_ref[...] = gate / jnp.sum(gate, axis=0, keepdims=True) * ROUTED_SCALE
    idx_ref[...] = jnp.concatenate(idxs, axis=0)
    chosen = functools.reduce(jnp.logical_or, picks)
    onehot = jnp.where(chosen, 1.0, 0.0)
    earlier = _dot(onehot.astype(jnp.bfloat16), tri_ref[...]) + base_ref[...]
    ranks = [jnp.sum(jnp.where(p, earlier, 0.0), axis=0, keepdims=True) for p in picks]
    rank_ref[...] = jnp.concatenate(ranks, axis=0).astype(jnp.int32)
    base = base_ref[...] + jnp.sum(onehot, axis=1, keepdims=True)
    base_ref[...] = base
    cnt_ref[...] = jnp.broadcast_to(base, cnt_ref.shape).astype(jnp.int32)


def route(logits_t, router_bias, *, tn=512):
    ne, n = logits_t.shape
    tri = jnp.asarray(np.triu(np.ones((tn, tn), np.float32), k=1), jnp.bfloat16)
    slot = pl.BlockSpec((TOP_K, tn), lambda i: (0, i))
    return pl.pallas_call(
        functools.partial(_route_kernel, tn=tn),
        grid=(n // tn,),
        in_specs=[pl.BlockSpec((ne, tn), lambda i: (0, i)),
                  pl.BlockSpec((ne, 1), lambda i: (0, 0)),
                  pl.BlockSpec((tn, tn), lambda i: (0, 0))],
        out_specs=[slot, slot, slot, pl.BlockSpec((ne, LANES), lambda i: (0, 0))],
        out_shape=[jax.ShapeDtypeStruct((TOP_K, n), jnp.int32),
                   jax.ShapeDtypeStruct((TOP_K, n), jnp.float32),
                   jax.ShapeDtypeStruct((TOP_K, n), jnp.int32),
                   jax.ShapeDtypeStruct((ne, LANES), jnp.int32)],
        scratch_shapes=[pltpu.VMEM((ne, 1), jnp.float32)],
        compiler_params=_cparams("arbitrary"),
        name="route",
    )(logits_t, router_bias.reshape(ne, 1), tri)


def _dest_kernel(idx_ref, rank_ref, start_ref, dest_ref):
    idx = idx_ref[...]
    ei = lax.broadcasted_iota(jnp.int32, (N_EXPERTS, idx.shape[1]), 0)
    start = start_ref[...]
    rows = [jnp.sum(jnp.where(ei == idx[k:k + 1, :], start, 0.0), axis=0, keepdims=True) for k in range(TOP_K)]
    dest_ref[...] = jnp.concatenate(rows, axis=0).astype(jnp.int32) + rank_ref[...]


def slot_destinations(idx, rank, starts, *, tn=512):
    kk, n = idx.shape
    slot = pl.BlockSpec((kk, tn), lambda i: (0, i))
    return pl.pallas_call(
        _dest_kernel,
        grid=(n // tn,),
        in_specs=[slot, slot, pl.BlockSpec((N_EXPERTS, 1), lambda i: (0, 0))],
        out_specs=slot,
        out_shape=jax.ShapeDtypeStruct((kk, n), jnp.int32),
        compiler_params=_cparams("parallel"),
        name="slot_destinations",
    )(idx, rank, starts.astype(jnp.float32).reshape(N_EXPERTS, 1))


def _packed_row(ref, row):
    return ref.at[pl.ds(pl.multiple_of(row * ROW_SUB, ROW_SUB), ROW_SUB), :]


def _dispatch_kernel(dest_ref, h_ref, xs_ref, zero_ref, sem, *, tm, n_rows):
    pad = zero_ref.shape[0]

    @pl.when(pl.program_id(0) == 0)
    def _():
        zero_ref[...] = jnp.zeros_like(zero_ref)
        tail = pltpu.make_async_copy(zero_ref, xs_ref.at[pl.ds(n_rows * ROW_SUB, pad), :], sem)
        tail.start()
        tail.wait()

    def body(t, _):
        for k in range(TOP_K):
            pltpu.make_async_copy(_packed_row(h_ref, t), _packed_row(xs_ref, dest_ref[k, t]), sem).start(priority=k % 2)
        return 0

    lax.fori_loop(0, tm, body, 0)
    for _ in range(TOP_K):
        pltpu.make_async_copy(h_ref, xs_ref.at[pl.ds(0, tm * ROW_SUB), :], sem).wait()


def dispatch(dest, hp, *, pad_rows, tm=512):
    kk, n = dest.shape
    n_rows = n * kk
    return pl.pallas_call(
        functools.partial(_dispatch_kernel, tm=tm, n_rows=n_rows),
        grid=(n // tm,),
        in_specs=[pl.BlockSpec((kk, tm), lambda i: (0, i), memory_space=pltpu.SMEM),
                  pl.BlockSpec((tm * ROW_SUB, LANES), lambda i: (i, 0))],
        out_specs=pl.BlockSpec(memory_space=pl.ANY),
        out_shape=jax.ShapeDtypeStruct(((n_rows + pad_rows) * ROW_SUB, LANES), hp.dtype),
        scratch_shapes=[pltpu.VMEM((pad_rows * ROW_SUB, LANES), hp.dtype), pltpu.SemaphoreType.DMA],
        compiler_params=_cparams("arbitrary"),
        name="dispatch",
    )(dest, hp)


EXPERT_CHUNK = 256
EXPERT_LOOKAHEAD = 3
EXPERT_IN_BUFS = EXPERT_LOOKAHEAD + 1
EXPERT_OUT_BUFS = 3


def _expert_kernel(first_ref, nch_ref, crow_ref, total_ref, xs_ref, wg_ref, wu_ref, wd_ref, ys_ref,
                   xbuf, ybuf, wgu_bf, wd_bf, sem_in, sem_out, waited_ref):
    e = pl.program_id(0)
    chunk = EXPERT_CHUNK
    first, n_chunks, total = first_ref[e], nch_ref[e], total_ref[0]

    def rows(g):
        return pl.ds(pl.multiple_of(crow_ref[g] * ROW_SUB, ROW_SUB), chunk * ROW_SUB)

    def in_copy(g):
        slot = lax.rem(g, EXPERT_IN_BUFS)
        return pltpu.make_async_copy(xs_ref.at[rows(g), :], xbuf.at[slot], sem_in.at[slot])

    def out_copy(g):
        slot = lax.rem(g, EXPERT_OUT_BUFS)
        return pltpu.make_async_copy(ybuf.at[slot], ys_ref.at[rows(g), :], sem_out.at[slot])

    def wait_writes_through(g):
        def wait_one(j, carry):
            out_copy(j).wait()
            return carry

        lax.fori_loop(waited_ref[0] + 1, g + 1, wait_one, 0)
        waited_ref[0] = jnp.maximum(waited_ref[0], g)

    @pl.when(e == 0)
    def _():
        waited_ref[0] = -1
        for g in range(EXPERT_LOOKAHEAD):
            @pl.when(g < total)
            def _():
                in_copy(g).start()

    @pl.when(n_chunks > 0)
    def _():
        wgu_bf[:, :EXPERT_DIM] = wg_ref[0].astype(wgu_bf.dtype)
        wgu_bf[:, EXPERT_DIM:] = wu_ref[0].astype(wgu_bf.dtype)
        wd_bf[...] = wd_ref[0].astype(wd_bf.dtype)

    def body(c, carry):
        g = first + c
        in_copy(g).wait()

        @pl.when(g + EXPERT_LOOKAHEAD < total)
        def _():
            in_copy(g + EXPERT_LOOKAHEAD).start()

        bf = jnp.bfloat16
        xa, xb = _unpack_rows(xbuf.at[lax.rem(g, EXPERT_IN_BUFS)], chunk)
        gu = _dot(xa.astype(bf), wgu_bf[:ROW_WORDS, :]) + _dot(xb.astype(bf), wgu_bf[ROW_WORDS:, :])
        gate, up = gu[:, :EXPERT_DIM], gu[:, EXPERT_DIM:]
        y = _dot((gate * _sigmoid(gate) * up).astype(bf), wd_bf[...])
        wait_writes_through(jnp.where(c == 0, g - 1, g - EXPERT_OUT_BUFS))
        _pack_rows(ybuf.at[lax.rem(g, EXPERT_OUT_BUFS)], y)
        out_copy(g).start()
        return carry

    lax.fori_loop(0, n_chunks, body, 0)

    @pl.when(e == pl.num_programs(0) - 1)
    def _():
        wait_writes_through(total - 1)
        ybuf[0] = jnp.zeros(ybuf.shape[1:], ybuf.dtype)
        pad = pl.ds(ys_ref.shape[0] - chunk * ROW_SUB, chunk * ROW_SUB)
        tail = pltpu.make_async_copy(ybuf.at[0], ys_ref.at[pad, :], sem_out.at[0])
        tail.start()
        tail.wait()


def expert_chunks(starts, counts, n_rows):
    chunk = EXPERT_CHUNK
    max_chunks = n_rows // chunk + N_EXPERTS
    nch = (counts + (chunk - 1)) // chunk
    ends = jnp.cumsum(nch)
    first = ends - nch
    g = jnp.arange(max_chunks, dtype=jnp.int32)
    owner = jnp.minimum(jnp.sum(g[:, None] >= ends[None, :], axis=1), N_EXPERTS - 1)
    onehot = (owner[:, None] == jnp.arange(N_EXPERTS)[None, :]).astype(jnp.int32)
    crow = (onehot * (starts - first * chunk)[None, :]).sum(axis=1) + g * chunk
    crow = jnp.where(g < ends[-1], crow, 0).astype(jnp.int32)
    return first.astype(jnp.int32), nch.astype(jnp.int32), crow, ends[-1:].astype(jnp.int32)


def expert_ffn(starts, counts, xs, w_gate, w_up, w_down):
    d = D_MODEL
    n_rows = xs.shape[0] // ROW_SUB - EXPERT_CHUNK
    tables = expert_chunks(starts, counts, n_rows)
    chunk_shape = (EXPERT_CHUNK * ROW_SUB, LANES)
    weights = lambda shape: pl.BlockSpec(shape, lambda e, *_: (e, 0, 0))
    grid_spec = pltpu.PrefetchScalarGridSpec(
        num_scalar_prefetch=len(tables),
        grid=(N_EXPERTS,),
        in_specs=[pl.BlockSpec(memory_space=pl.ANY),
                  weights((1, d, EXPERT_DIM)), weights((1, d, EXPERT_DIM)), weights((1, EXPERT_DIM, d))],
        out_specs=pl.BlockSpec(memory_space=pl.ANY),
        scratch_shapes=[pltpu.VMEM((EXPERT_IN_BUFS,) + chunk_shape, xs.dtype),
                        pltpu.VMEM((EXPERT_OUT_BUFS,) + chunk_shape, xs.dtype),
                        pltpu.VMEM((d, 2 * EXPERT_DIM), jnp.bfloat16), pltpu.VMEM((EXPERT_DIM, d), jnp.bfloat16),
                        pltpu.SemaphoreType.DMA((EXPERT_IN_BUFS,)), pltpu.SemaphoreType.DMA((EXPERT_OUT_BUFS,)),
                        pltpu.SMEM((1,), jnp.int32)],
    )
    return pl.pallas_call(
        _expert_kernel,
        grid_spec=grid_spec,
        out_shape=jax.ShapeDtypeStruct(xs.shape, xs.dtype),
        compiler_params=_cparams("arbitrary"),
        name="expert_ffn",
    )(*tables, xs, w_gate, w_up, w_down)


def _combine_kernel(dest_ref, ys_ref, gate_ref, sh_ref, x1_ref, mod_ref, g_ref, o_ref, buf, sem, *, tm):
    def body(t, _):
        for k in range(TOP_K):
            pltpu.make_async_copy(_packed_row(ys_ref, dest_ref[k, t]), _packed_row(buf.at[k], t), sem).start(
                priority=k % 2)
        return 0

    lax.fori_loop(0, tm, body, 0)
    for k in range(TOP_K):
        pltpu.make_async_copy(ys_ref.at[pl.ds(0, tm * ROW_SUB), :], buf.at[k], sem).wait()
    gate = gate_ref[...]
    first = jnp.zeros((tm, ROW_WORDS), jnp.float32)
    second = jnp.zeros((tm, ROW_WORDS), jnp.float32)
    for k in range(TOP_K):
        ya, yb = _unpack_rows(buf.at[k], tm)
        first = first + ya * gate[:, k:k + 1]
        second = second + yb * gate[:, k:k + 1]
    y = sh_ref[...] + jnp.concatenate([first, second], axis=1)
    o_ref[...] = x1_ref[...] + mod_ref[0, 5:6, :] * _rms_norm(y, g_ref[...])


def combine(dest, ys, gate_nk, shared, x1, mod, g_post_ffn, *, seq, tm=128):
    n, d = x1.shape
    kk = dest.shape[0]
    per_batch = seq // tm
    tok = pl.BlockSpec((tm, d), lambda i: (i, 0))
    return pl.pallas_call(
        functools.partial(_combine_kernel, tm=tm),
        grid=(n // tm,),
        in_specs=[pl.BlockSpec((kk, tm), lambda i: (0, i), memory_space=pltpu.SMEM),
                  pl.BlockSpec(memory_space=pl.ANY),
                  pl.BlockSpec((tm, kk), lambda i: (i, 0)),
                  tok, tok,
                  pl.BlockSpec((1, N_MOD, d), lambda i: (i // per_batch, 0, 0)),
                  pl.BlockSpec((1, d), lambda i: (0, 0))],
        out_specs=tok,
        out_shape=jax.ShapeDtypeStruct((n, d), jnp.float32),
        scratch_shapes=[pltpu.VMEM((kk, tm * ROW_SUB, LANES), ys.dtype), pltpu.SemaphoreType.DMA],
        compiler_params=_cparams("arbitrary"),
        name="combine",
    )(dest, ys, gate_nk, shared, x1, mod, g_post_ffn.reshape(1, d))


def _layer(x, mod, rel_bias, g_pre_mix, w_in, w_branch_a, w_branch_b, w_out, g_post_mix, g_pre_ffn, w_router,
           router_bias, w_exp_gate, w_exp_up, w_exp_down, w_sh_gate, w_sh_up, w_sh_down, g_post_ffn):
    b, s, d = x.shape
    bf = jnp.bfloat16
    scale = HEAD_DIM ** -0.5
    sb_q, sb_k, sb_v, dq, dk, dv, w_ga, w_gb = jnp.split(
        w_in, np.cumsum((SB_WIDTH,) * 3 + (DIL_WIDTH,) * 3 + (d,)).tolist(), axis=1)
    groups = lambda w: [w[:, g * DIL_GROUP_WIDTH:(g + 1) * DIL_GROUP_WIDTH] for g in range(len(DIL_CONFIGS))]
    w_rows = jnp.concatenate([sb_k] + [w * scale for w in groups(dq)] + groups(dk) + groups(dv) + [w_ga, w_gb],
                             axis=1).astype(bf)
    w_cols = jnp.concatenate([sb_q * (scale * LOG2E), sb_v], axis=1).T.astype(bf)
    qT, k, vT3, *rest = in_projection(x, mod, g_pre_mix, w_rows, w_cols)
    dqs, dks, dvs, gate_a, gate_b = rest[0:3], rest[3:6], rest[6:9], rest[9], rest[10]
    out_a = sb_attention(qT, k, vT3)
    o_g, lse_g = [], []
    for g, (window, dilation) in enumerate(DIL_CONFIGS):
        bias = dilated_bias(rel_bias[:, g * DIL_HEADS_PER_GROUP:(g + 1) * DIL_HEADS_PER_GROUP], window, dilation)
        o, lse = dilated_attention(dqs[g], dks[g], dvs[g], bias, dilation)
        o_g.append(o)
        lse_g.append(lse)
    x1, hp, shared, logits_t = mix_out(
        out_a, o_g, lse_g, gate_a, gate_b, x, mod, w_branch_a.astype(bf), w_branch_b.astype(bf), w_out.astype(bf),
        g_post_mix, g_pre_ffn, w_router.T.astype(bf), jnp.concatenate([w_sh_gate, w_sh_up], axis=1).astype(bf),
        w_sh_down.astype(bf))
    n = b * s
    idx, gate, rank, counts = route(logits_t, router_bias)
    counts = counts[:, 0]
    starts = jnp.cumsum(counts) - counts
    dest = slot_destinations(idx, rank, starts)
    xs = dispatch(dest, hp, pad_rows=EXPERT_CHUNK)
    ys = expert_ffn(starts, counts, xs, w_exp_gate, w_exp_up, w_exp_down)
    out = combine(dest, ys, gate.T, shared.reshape(n, d), x1.reshape(n, d), mod, g_post_ffn, seq=s)
    return out.reshape(b, s, d)


def kernel(x, c, rel_bias, w_ada, b_ada, g_pre_mix, w_in, w_branch_a, w_branch_b, w_out, g_post_mix, g_pre_ffn,
           w_router, router_bias, w_exp_gate, w_exp_up, w_exp_down, w_sh_gate, w_sh_up, w_sh_down, g_post_ffn):
    depth = w_in.shape[0]
    for layer in range(depth):
        mod = ada_modulation(c, w_ada[layer], b_ada[layer]).reshape(x.shape[0], N_MOD, x.shape[2])
        x = _layer(x, mod, rel_bias, g_pre_mix[layer], w_in[layer], w_branch_a[layer], w_branch_b[layer], w_out[layer],
                   g_post_mix[layer], g_pre_ffn[layer], w_router[layer], router_bias[layer], w_exp_gate[layer],
                   w_exp_up[layer], w_exp_down[layer], w_sh_gate[layer], w_sh_up[layer], w_sh_down[layer],
                   g_post_ffn[layer])
    return x
```

```python
import functools
import math

import jax
import jax.numpy as jnp
import numpy as np
from jax import lax
from jax.experimental import pallas as pl
from jax.experimental.pallas import tpu as pltpu

D_MODEL = 1024
HEAD_DIM = 64
SB_HEADS = 8
SB_WIDTH = SB_HEADS * HEAD_DIM
DIL_CONFIGS = ((128, 1), (512, 4), (2048, 16))
DIL_HEADS_PER_GROUP = 4
DIL_GROUP_WIDTH = DIL_HEADS_PER_GROUP * HEAD_DIM
DIL_WIDTH = DIL_GROUP_WIDTH * len(DIL_CONFIGS)
Q_BLOCK = 128
N_BUCKETS = 32
MAX_DISTANCE = 2048
N_EXPERTS = 256
TOP_K = 8
N_GROUPS = 8
TOP_K_GROUPS = 4
EXPERT_DIM = 256
ROUTED_SCALE = 2.5
N_MOD = 6
EPS = 1e-6

LANES = 128
VMEM_LIMIT_BYTES = 56 * 1024 * 1024
NEG_INF = float("-inf")
LOG2E = math.log2(math.e)


def _cparams(*semantics):
    return pltpu.CompilerParams(dimension_semantics=semantics, vmem_limit_bytes=VMEM_LIMIT_BYTES)


def _rms_norm(v, g):
    return v * lax.rsqrt(jnp.mean(v * v, axis=-1, keepdims=True) + EPS) * g


def _sigmoid(v):
    return 1.0 / (1.0 + jnp.exp(-v))


def _dot(a, b):
    return jnp.dot(a, b, preferred_element_type=jnp.float32)


def _dot_nt(a, b):
    return lax.dot_general(a, b, (((1,), (1,)), ((), ())), preferred_element_type=jnp.float32)


def _split_bf16(v):
    hi = v.astype(jnp.bfloat16)
    lo = (v - hi.astype(jnp.float32)).astype(jnp.bfloat16)
    return hi, lo


ROW_WORDS = D_MODEL // 2
ROW_SUB = ROW_WORDS // LANES


def _pack_rows(ref, v):
    m = v.shape[0]
    bits = pltpu.bitcast(v.astype(jnp.bfloat16).astype(jnp.float32), jnp.uint32)
    words = bits[:, :ROW_WORDS] | (bits[:, ROW_WORDS:] >> 16)
    for j in range(ROW_SUB):
        ref[pl.ds(j, m, stride=ROW_SUB), :] = words[:, j * LANES:(j + 1) * LANES]


def _unpack_rows(ref, m):
    words = jnp.concatenate([ref[pl.ds(j, m, stride=ROW_SUB), :] for j in range(ROW_SUB)], axis=1)
    first = pltpu.bitcast(words & jnp.uint32(0xFFFF0000), jnp.float32)
    second = pltpu.bitcast(words << 16, jnp.float32)
    return first, second


def _ada_kernel(c_ref, w_ref, b_ref, o_ref):
    c = c_ref[...]
    a_hi, a_lo = _split_bf16(c * _sigmoid(c))
    w_hi, w_lo = _split_bf16(w_ref[...])
    o_ref[...] = _dot(a_hi, w_hi) + _dot(a_lo, w_hi) + _dot(a_hi, w_lo) + b_ref[...]


def ada_modulation(c, w_ada, b_ada, *, tn=1536):
    b, d = c.shape
    n = w_ada.shape[1]
    return pl.pallas_call(
        _ada_kernel,
        grid=(n // tn,),
        in_specs=[
            pl.BlockSpec((b, d), lambda j: (0, 0)),
            pl.BlockSpec((d, tn), lambda j: (0, j)),
            pl.BlockSpec((1, tn), lambda j: (0, j)),
        ],
        out_specs=pl.BlockSpec((b, tn), lambda j: (0, j)),
        out_shape=jax.ShapeDtypeStruct((b, n), jnp.float32),
        compiler_params=_cparams("parallel"),
        name="ada_modulation",
    )(c, w_ada, b_ada.reshape(1, n))


def _in_proj_kernel(x_ref, mod_ref, g_ref, w_rows_ref, w_cols_ref, qT_ref, k_ref, vT_ref, *rest, tm):
    dil_refs, ga_ref, gb_ref, stage_ref = rest[:9], rest[9], rest[10], rest[11]
    h = _rms_norm(x_ref[0], g_ref[...]) * (1.0 + mod_ref[0, 1:2, :]) + mod_ref[0, 0:1, :]
    hb = h.astype(jnp.bfloat16)
    off = 0
    k_ref[0] = _dot(hb, w_rows_ref[:, off:off + SB_WIDTH]).astype(k_ref.dtype)
    off += SB_WIDTH
    for n, r in enumerate(dil_refs):
        y = _dot(hb, w_rows_ref[:, off:off + DIL_GROUP_WIDTH])
        off += DIL_GROUP_WIDTH
        dilation = DIL_CONFIGS[n % len(DIL_CONFIGS)][1]
        if dilation == 1:
            r[0] = y.astype(r.dtype)
            continue
        for j in range(DIL_GROUP_WIDTH // LANES):
            stage_ref[j] = y[:, j * LANES:(j + 1) * LANES]
        for c in range(dilation):
            for j in range(DIL_GROUP_WIDTH // LANES):
                col = c * DIL_GROUP_WIDTH + j * LANES
                r[0, :, col:col + LANES] = stage_ref[j, pl.ds(c, tm // dilation, stride=dilation), :].astype(r.dtype)
    for r in (ga_ref, gb_ref):
        r[0] = _sigmoid(_dot(hb, w_rows_ref[:, off:off + D_MODEL])).astype(r.dtype)
        off += D_MODEL
    qT_ref[0] = _dot_nt(w_cols_ref[0:SB_WIDTH, :], hb).astype(qT_ref.dtype)
    vT = _dot_nt(w_cols_ref[SB_WIDTH:2 * SB_WIDTH, :], hb).astype(vT_ref.dtype)
    for p in range(SB_WIDTH // LANES):
        for jb in range(tm // LANES):
            vT_ref[0, p, jb] = vT[p * LANES:(p + 1) * LANES, jb * LANES:(jb + 1) * LANES]


def _dilated_shape(b, s, dilation, dtype):
    return jax.ShapeDtypeStruct((b, s // dilation, dilation * DIL_GROUP_WIDTH), dtype)


def _dilated_spec(tm, dilation):
    return pl.BlockSpec((1, tm // dilation, dilation * DIL_GROUP_WIDTH), lambda bi, i: (bi, i, 0))


def in_projection(x, mod, g_pre, w_rows, w_cols, *, tm=256):
    b, s, d = x.shape
    nt = s // tm
    bf = jnp.bfloat16
    tok = lambda width: pl.BlockSpec((1, tm, width), lambda bi, i: (bi, i, 0))
    out_shapes = [
        jax.ShapeDtypeStruct((b, SB_WIDTH, s), bf),
        jax.ShapeDtypeStruct((b, s, SB_WIDTH), bf),
        jax.ShapeDtypeStruct((b, SB_WIDTH // LANES, s // LANES, LANES, LANES), bf),
    ] + [_dilated_shape(b, s, r, bf) for _, r in DIL_CONFIGS] * 3 + [jax.ShapeDtypeStruct((b, s, d), bf)] * 2
    out_specs = [
        pl.BlockSpec((1, SB_WIDTH, tm), lambda bi, i: (bi, 0, i)),
        tok(SB_WIDTH),
        pl.BlockSpec((1, SB_WIDTH // LANES, tm // LANES, LANES, LANES), lambda bi, i: (bi, 0, i, 0, 0)),
    ] + [_dilated_spec(tm, r) for _, r in DIL_CONFIGS] * 3 + [tok(d)] * 2
    return pl.pallas_call(
        functools.partial(_in_proj_kernel, tm=tm),
        grid=(b, nt),
        in_specs=[
            tok(d),
            pl.BlockSpec((1, N_MOD, d), lambda bi, i: (bi, 0, 0)),
            pl.BlockSpec((1, d), lambda bi, i: (0, 0)),
            pl.BlockSpec(w_rows.shape, lambda bi, i: (0, 0)),
            pl.BlockSpec(w_cols.shape, lambda bi, i: (0, 0)),
        ],
        out_specs=out_specs,
        out_shape=out_shapes,
        scratch_shapes=[pltpu.VMEM((DIL_GROUP_WIDTH // LANES, tm, LANES), jnp.float32)],
        compiler_params=_cparams("parallel", "parallel"),
        name="in_projection",
    )(x, mod, g_pre.reshape(1, d), w_rows, w_cols)


SB_TILE = 512
SB_SUB = 256


def _sb_chunk(kc, vc, q_h, tri, csum, past):
    z = _dot(kc, q_h)
    neg_abs = pltpu.bitcast(pltpu.bitcast(z, jnp.uint32) | jnp.uint32(0x80000000), jnp.float32)
    drop = jnp.maximum(z, 0.0) + jnp.log(1.0 + jnp.exp2(neg_abs)) * LOG2E
    if past is not None:
        drop = jnp.where(past, drop, 0.0)
    weights = [None] * (SB_TILE // SB_SUB)
    for sb in reversed(range(SB_TILE // SB_SUB)):
        rows = slice(sb * SB_SUB, (sb + 1) * SB_SUB)
        d = drop[rows]
        later = _dot(tri, d.astype(jnp.bfloat16))
        w = jnp.exp2(z[rows] - d - later - csum)
        if past is not None:
            w = jnp.where(past[rows], w, 0.0)
        weights[sb] = w.astype(jnp.bfloat16)
        csum = csum + jnp.sum(d, axis=0, keepdims=True)
    w = jnp.concatenate(weights, axis=0)
    t = SB_TILE
    part = jnp.concatenate([_dot(vc[h * HEAD_DIM:(h + 1) * HEAD_DIM], w[:, h * t:(h + 1) * t]) for h in range(2)],
                           axis=0)
    return part, csum


def _sb_attn_kernel(qT_ref, k_ref, vT_ref, tri_ref, o_ref, acc_ref, csum_ref):
    i = pl.program_id(2)
    t = SB_TILE
    q2 = qT_ref[0]
    row = lax.broadcasted_iota(jnp.int32, q2.shape, 0)
    zero = jnp.zeros_like(q2)
    q_heads = jnp.concatenate([jnp.where(row < HEAD_DIM, q2, zero), jnp.where(row >= HEAD_DIM, q2, zero)], axis=1)
    tri = tri_ref[...]
    blocks = t // LANES

    def chunk(c, past, first):
        kc = k_ref[0, pl.ds(pl.multiple_of(c * t, t), t), :]
        vc = jnp.concatenate([vT_ref[0, 0, c * blocks + jb] for jb in range(blocks)], axis=1)
        csum = jnp.zeros((1, 2 * t), jnp.float32) if first else csum_ref[...]
        part, csum = _sb_chunk(kc, vc, q_heads, tri, csum, past)
        acc_ref[...] = part if first else acc_ref[...] + part
        csum_ref[...] = csum

    key = lax.broadcasted_iota(jnp.int32, (t, 2 * t), 0)
    query = lax.broadcasted_iota(jnp.int32, (t, 2 * t), 1)
    chunk(i, key < jnp.where(query < t, query, query - t), True)

    def body(it, carry):
        chunk(i - 1 - it, None, False)
        return carry

    lax.fori_loop(0, i, body, 0)
    o_ref[0] = acc_ref[...].T.astype(o_ref.dtype)


def sb_attention(qT, k, vT3):
    b, width, s = qT.shape
    n_pairs = width // LANES
    t = SB_TILE
    tri = jnp.asarray(np.triu(np.ones((SB_SUB, SB_SUB), np.float32), k=1), jnp.bfloat16)
    return pl.pallas_call(
        _sb_attn_kernel,
        grid=(b, n_pairs, s // t),
        in_specs=[
            pl.BlockSpec((1, LANES, t), lambda bi, p, i: (bi, p, i)),
            pl.BlockSpec((1, s, LANES), lambda bi, p, i: (bi, 0, p)),
            pl.BlockSpec((1, 1, s // LANES, LANES, LANES), lambda bi, p, i: (bi, p, 0, 0, 0)),
            pl.BlockSpec((SB_SUB, SB_SUB), lambda bi, p, i: (0, 0)),
        ],
        out_specs=pl.BlockSpec((1, t, LANES), lambda bi, p, i: (bi, i, p)),
        out_shape=jax.ShapeDtypeStruct((b, s, width), jnp.bfloat16),
        scratch_shapes=[pltpu.VMEM((LANES, t), jnp.float32), pltpu.VMEM((1, 2 * t), jnp.float32)],
        compiler_params=_cparams("parallel", "parallel", "arbitrary"),
        name="sb_attention",
    )(qT, k, vT3, tri)


def _dil_attn_kernel(q_ref, kp_ref, kc_ref, vp_ref, vc_ref, bias_ref, o_ref, lse_ref):
    m = pl.program_id(2)
    q4 = q_ref[0]
    kk = jnp.concatenate([kp_ref[0], kc_ref[0]], axis=0)
    vv = jnp.concatenate([vp_ref[0], vc_ref[0]], axis=0)
    lane = lax.broadcasted_iota(jnp.int32, q4.shape, 1)
    zero = jnp.zeros_like(q4)
    in_head = [jnp.logical_and(lane >= h * HEAD_DIM, lane < (h + 1) * HEAD_DIM) for h in range(DIL_HEADS_PER_GROUP)]
    q_stack = jnp.concatenate([jnp.where(mask, q4, zero) for mask in in_head], axis=0)
    logits = _dot_nt(q_stack, kk) + bias_ref[...]
    key = lax.broadcasted_iota(jnp.int32, logits.shape, 1)
    logits = jnp.where(jnp.logical_or(key >= Q_BLOCK, m > 0), logits, NEG_INF)
    mx = jnp.max(logits, axis=1, keepdims=True)
    p = jnp.exp(logits - mx)
    den = jnp.sum(p, axis=1, keepdims=True)
    pv = _dot(p.astype(jnp.bfloat16), vv) / den
    lse_rows = mx + jnp.log(den)
    out = jnp.zeros(q4.shape, jnp.float32)
    lse = jnp.zeros(q4.shape, jnp.float32)
    for h, mask in enumerate(in_head):
        rows = slice(h * Q_BLOCK, (h + 1) * Q_BLOCK)
        out = jnp.where(mask, pv[rows], out)
        lse = jnp.where(mask, lse_rows[rows], lse)
    o_ref[0] = out.astype(o_ref.dtype)
    lse_ref[0] = lse


def dilated_attention(q, k, v, bias, dilation):
    b, length, width = q.shape
    gw = DIL_GROUP_WIDTH
    cur = pl.BlockSpec((1, Q_BLOCK, gw), lambda bi, c, m: (bi, m, c))
    prev = pl.BlockSpec((1, Q_BLOCK, gw), lambda bi, c, m: (bi, jnp.maximum(m - 1, 0), c))
    return pl.pallas_call(
        _dil_attn_kernel,
        grid=(b, width // gw, length // Q_BLOCK),
        in_specs=[cur, prev, cur, prev, cur,
                  pl.BlockSpec((DIL_HEADS_PER_GROUP * Q_BLOCK, 2 * Q_BLOCK), lambda bi, c, m: (0, 0))],
        out_specs=[cur, cur],
        out_shape=[jax.ShapeDtypeStruct((b, length, width), jnp.bfloat16),
                   jax.ShapeDtypeStruct((b, length, width), jnp.float32)],
        compiler_params=_cparams("parallel", "parallel", "arbitrary"),
        name=f"dilated_attention_r{dilation}",
    )(q, k, k, v, v, bias.reshape(DIL_HEADS_PER_GROUP * Q_BLOCK, 2 * Q_BLOCK))


def _t5_causal_bucket(dist):
    max_exact = N_BUCKETS // 2
    d = jnp.maximum(dist, 1).astype(jnp.float32)
    large = max_exact + (jnp.log(d / max_exact) / math.log(MAX_DISTANCE / max_exact)
                         * (N_BUCKETS - max_exact)).astype(jnp.int32)
    large = jnp.minimum(large, N_BUCKETS - 1)
    return jnp.where(dist < max_exact, dist, large)


def dilated_bias(rel_bias_group, window, dilation):
    qi = jnp.arange(Q_BLOCK)[:, None]
    kj = jnp.arange(2 * Q_BLOCK)[None, :]
    dist_sub = qi + Q_BLOCK - kj
    bucket = _t5_causal_bucket(jnp.maximum(dist_sub, 0) * dilation)
    onehot = (bucket[..., None] == jnp.arange(N_BUCKETS)).astype(jnp.float32)
    bias = jnp.einsum("qkn,nh->hqk", onehot, rel_bias_group.astype(jnp.float32), precision=lax.Precision.HIGHEST)
    valid = (dist_sub >= 0) & (dist_sub <= window // dilation)
    return jnp.where(valid[None], bias, NEG_INF)


def _mix_out_kernel(oa_ref, o0_ref, o1_ref, o2_ref, l0_ref, l1_ref, l2_ref, ga_ref, gb_ref, x_ref, mod_ref,
                    wa_ref, wb_ref, wo_ref, gpost_ref, gpre_ref, wr_ref, wsgu_ref, wsd_ref,
                    x1_ref, hp_ref, sh_ref, lt_ref, stage_ref):
    f32 = jnp.float32
    tm = x_ref.shape[1]

    def token_major(ref, dilation):
        if dilation == 1:
            return ref[0].astype(f32)
        slabs = DIL_GROUP_WIDTH // LANES
        for c in range(dilation):
            for j in range(slabs):
                col = c * DIL_GROUP_WIDTH + j * LANES
                stage_ref[j, pl.ds(c, tm // dilation, stride=dilation), :] = ref[0, :, col:col + LANES].astype(f32)
        return jnp.concatenate([stage_ref[j] for j in range(slabs)], axis=1)

    dils = [r for _, r in DIL_CONFIGS]
    l0, l1, l2 = (token_major(ref, r) for ref, r in zip((l0_ref, l1_ref, l2_ref), dils))
    mx = jnp.maximum(jnp.maximum(l0, l1), l2)
    e0, e1, e2 = jnp.exp(l0 - mx), jnp.exp(l1 - mx), jnp.exp(l2 - mx)
    den = e0 + e1 + e2
    o0, o1, o2 = (token_major(ref, r) for ref, r in zip((o0_ref, o1_ref, o2_ref), dils))
    out_b = (e0 / den) * o0 + (e1 / den) * o1 + (e2 / den) * o2
    ya = _dot(oa_ref[0], wa_ref[...])
    yb = _dot(out_b.astype(jnp.bfloat16), wb_ref[...])
    merged = ga_ref[0].astype(f32) * ya + gb_ref[0].astype(f32) * yb
    y = _dot(merged.astype(jnp.bfloat16), wo_ref[...])
    x1 = x_ref[0] + mod_ref[0, 2:3, :] * _rms_norm(y, gpost_ref[...])
    x1_ref[0] = x1
    h = _rms_norm(x1, gpre_ref[...]) * (1.0 + mod_ref[0, 4:5, :]) + mod_ref[0, 3:4, :]
    hb = h.astype(jnp.bfloat16)
    _pack_rows(hp_ref, h)
    lt_ref[...] = _dot_nt(wr_ref[...], hb)
    gu = _dot(hb, wsgu_ref[...])
    g, u = gu[:, :EXPERT_DIM], gu[:, EXPERT_DIM:]
    sh_ref[0] = _dot((g * _sigmoid(g) * u).astype(jnp.bfloat16), wsd_ref[...])


def mix_out(out_a, o_g, lse_g, gate_a, gate_b, x, mod, w_a, w_b, w_o, g_post, g_pre_ffn, w_rT, w_sgu, w_sd, *, tm=256):
    b, s, d = x.shape
    tok = lambda width: pl.BlockSpec((1, tm, width), lambda bi, i: (bi, i, 0))
    full = lambda a: pl.BlockSpec(a.shape, lambda bi, i: (0,) * a.ndim)
    nt = s // tm
    g_post, g_pre_ffn = g_post.reshape(1, d), g_pre_ffn.reshape(1, d)
    weights = (w_a, w_b, w_o, g_post, g_pre_ffn, w_rT, w_sgu, w_sd)
    return pl.pallas_call(
        _mix_out_kernel,
        grid=(b, nt),
        in_specs=[tok(SB_WIDTH)] + [_dilated_spec(tm, r) for _, r in DIL_CONFIGS] * 2 + [tok(d)] * 3
        + [pl.BlockSpec((1, N_MOD, d), lambda bi, i: (bi, 0, 0))] + [full(w) for w in weights],
        out_specs=[tok(d), pl.BlockSpec((tm * ROW_SUB, LANES), lambda bi, i: (bi * nt + i, 0)), tok(d),
                   pl.BlockSpec((N_EXPERTS, tm), lambda bi, i: (0, bi * nt + i))],
        out_shape=[jax.ShapeDtypeStruct((b, s, d), jnp.float32),
                   jax.ShapeDtypeStruct((b * s * ROW_SUB, LANES), jnp.uint32),
                   jax.ShapeDtypeStruct((b, s, d), jnp.float32),
                   jax.ShapeDtypeStruct((N_EXPERTS, b * s), jnp.float32)],
        scratch_shapes=[pltpu.VMEM((DIL_GROUP_WIDTH // LANES, tm, LANES), jnp.float32)],
        compiler_params=_cparams("parallel", "parallel"),
        name="mix_out",
    )(out_a, *o_g, *lse_g, gate_a, gate_b, x, mod, *weights)


def _route_kernel(lt_ref, bias_ref, tri_ref, idx_ref, gate_ref, rank_ref, cnt_ref, base_ref, *, tn):
    f32 = jnp.float32

    @pl.when(pl.program_id(0) == 0)
    def _():
        base_ref[...] = jnp.zeros_like(base_ref)

    per_group = N_EXPERTS // N_GROUPS
    scores = _sigmoid(lt_ref[...])
    sel = scores + bias_ref[...]
    sel_groups = [sel[g * per_group:(g + 1) * per_group] for g in range(N_GROUPS)]
    ri = lax.broadcasted_iota(jnp.int32, (per_group, tn), 0)
    group_scores = []
    for sg in sel_groups:
        m1 = jnp.max(sg, axis=0, keepdims=True)
        first = jnp.min(jnp.where(sg == m1, ri, per_group), axis=0, keepdims=True)
        group_scores.append(m1 + jnp.max(jnp.where(ri == first, NEG_INF, sg), axis=0, keepdims=True))
    cur = jnp.concatenate(group_scores, axis=0)
    gi = lax.broadcasted_iota(jnp.int32, cur.shape, 0)
    keep = jnp.zeros(cur.shape, jnp.float32)
    for _ in range(TOP_K_GROUPS):
        mg = jnp.max(cur, axis=0, keepdims=True)
        pick = gi == jnp.min(jnp.where(cur == mg, gi, N_GROUPS), axis=0, keepdims=True)
        keep = jnp.where(pick, 1.0, keep)
        cur = jnp.where(pick, NEG_INF, cur)
    cur = jnp.concatenate([jnp.where(keep[g:g + 1, :] > 0.0, sg, NEG_INF) for g, sg in enumerate(sel_groups)], axis=0)
    ei = lax.broadcasted_iota(jnp.int32, cur.shape, 0)
    picks, idxs, gates = [], [], []
    for _ in range(TOP_K):
        mk = jnp.max(cur, axis=0, keepdims=True)
        fk = jnp.min(jnp.where(cur == mk, ei, N_EXPERTS), axis=0, keepdims=True)
        pick = ei == fk
        picks.append(pick)
        idxs.append(fk)
        gates.append(jnp.sum(jnp.where(pick, scores, 0.0), axis=0, keepdims=True))
        cur = jnp.where(pick, NEG_INF, cur)
    gate = jnp.concatenate(gates, axis=0)
    gate_ref[...] = gate / jnp.sum(gate, axis=0, keepdims=True) * ROUTED_SCALE
    idx_ref[...] = jnp.concatenate(idxs, axis=0)
    chosen = functools.reduce(jnp.logical_or, picks)
    onehot = jnp.where(chosen, 1.0, 0.0)
    earlier = _dot(onehot.astype(jnp.bfloat16), tri_ref[...]) + base_ref[...]
    ranks = [jnp.sum(jnp.where(p, earlier, 0.0), axis=0, keepdims=True) for p in picks]
    rank_ref[...] = jnp.concatenate(ranks, axis=0).astype(jnp.int32)
    base = base_ref[...] + jnp.sum(onehot, axis=1, keepdims=True)
    base_ref[...] = base
    cnt_ref[...] = jnp.broadcast_to(base, cnt_ref.shape).astype(jnp.int32)


def route(logits_t, router_bias, *, tn=512):
    ne, n = logits_t.shape
    tri = jnp.asarray(np.triu(np.ones((tn, tn), np.float32), k=1), jnp.bfloat16)
    slot = pl.BlockSpec((TOP_K, tn), lambda i: (0, i))
    return pl.pallas_call(
        functools.partial(_route_kernel, tn=tn),
        grid=(n // tn,),
        in_specs=[pl.BlockSpec((ne, tn), lambda i: (0, i)),
                  pl.BlockSpec((ne, 1), lambda i: (0, 0)),
                  pl.BlockSpec((tn, tn), lambda i: (0, 0))],
        out_specs=[slot, slot, slot, pl.BlockSpec((ne, LANES), lambda i: (0, 0))],
        out_shape=[jax.ShapeDtypeStruct((TOP_K, n), jnp.int32),
                   jax.ShapeDtypeStruct((TOP_K, n), jnp.float32),
                   jax.ShapeDtypeStruct((TOP_K, n), jnp.int32),
                   jax.ShapeDtypeStruct((ne, LANES), jnp.int32)],
        scratch_shapes=[pltpu.VMEM((ne, 1), jnp.float32)],
        compiler_params=_cparams("arbitrary"),
        name="route",
    )(logits_t, router_bias.reshape(ne, 1), tri)


def _dest_kernel(idx_ref, rank_ref, start_ref, dest_ref):
    idx = idx_ref[...]
    ei = lax.broadcasted_iota(jnp.int32, (N_EXPERTS, idx.shape[1]), 0)
    start = start_ref[...]
    rows = [jnp.sum(jnp.where(ei == idx[k:k + 1, :], start, 0.0), axis=0, keepdims=True) for k in range(TOP_K)]
    dest_ref[...] = jnp.concatenate(rows, axis=0).astype(jnp.int32) + rank_ref[...]


def slot_destinations(idx, rank, starts, *, tn=512):
    kk, n = idx.shape
    slot = pl.BlockSpec((kk, tn), lambda i: (0, i))
    return pl.pallas_call(
        _dest_kernel,
        grid=(n // tn,),
        in_specs=[slot, slot, pl.BlockSpec((N_EXPERTS, 1), lambda i: (0, 0))],
        out_specs=slot,
        out_shape=jax.ShapeDtypeStruct((kk, n), jnp.int32),
        compiler_params=_cparams("parallel"),
        name="slot_destinations",
    )(idx, rank, starts.astype(jnp.float32).reshape(N_EXPERTS, 1))


def _packed_row(ref, row):
    return ref.at[pl.ds(pl.multiple_of(row * ROW_SUB, ROW_SUB), ROW_SUB), :]


def _dispatch_kernel(dest_ref, h_ref, xs_ref, zero_ref, sem, *, tm, n_rows):
    pad = zero_ref.shape[0]

    @pl.when(pl.program_id(0) == 0)
    def _():
        zero_ref[...] = jnp.zeros_like(zero_ref)
        tail = pltpu.make_async_copy(zero_ref, xs_ref.at[pl.ds(n_rows * ROW_SUB, pad), :], sem)
        tail.start()
        tail.wait()

    def body(t, _):
        for k in range(TOP_K):
            pltpu.make_async_copy(_packed_row(h_ref, t), _packed_row(xs_ref, dest_ref[k, t]), sem).start(priority=k % 2)
        return 0

    lax.fori_loop(0, tm, body, 0)
    for _ in range(TOP_K):
        pltpu.make_async_copy(h_ref, xs_ref.at[pl.ds(0, tm * ROW_SUB), :], sem).wait()


def dispatch(dest, hp, *, pad_rows, tm=512):
    kk, n = dest.shape
    n_rows = n * kk
    return pl.pallas_call(
        functools.partial(_dispatch_kernel, tm=tm, n_rows=n_rows),
        grid=(n // tm,),
        in_specs=[pl.BlockSpec((kk, tm), lambda i: (0, i), memory_space=pltpu.SMEM),
                  pl.BlockSpec((tm * ROW_SUB, LANES), lambda i: (i, 0))],
        out_specs=pl.BlockSpec(memory_space=pl.ANY),
        out_shape=jax.ShapeDtypeStruct(((n_rows + pad_rows) * ROW_SUB, LANES), hp.dtype),
        scratch_shapes=[pltpu.VMEM((pad_rows * ROW_SUB, LANES), hp.dtype), pltpu.SemaphoreType.DMA],
        compiler_params=_cparams("arbitrary"),
        name="dispatch",
    )(dest, hp)


EXPERT_CHUNK = 256
EXPERT_LOOKAHEAD = 3
EXPERT_IN_BUFS = EXPERT_LOOKAHEAD + 1
EXPERT_OUT_BUFS = 3


def _expert_kernel(first_ref, nch_ref, crow_ref, total_ref, xs_ref, wg_ref, wu_ref, wd_ref, ys_ref,
                   xbuf, ybuf, wgu_bf, wd_bf, sem_in, sem_out, waited_ref):
    e = pl.program_id(0)
    chunk = EXPERT_CHUNK
    first, n_chunks, total = first_ref[e], nch_ref[e], total_ref[0]

    def rows(g):
        return pl.ds(pl.multiple_of(crow_ref[g] * ROW_SUB, ROW_SUB), chunk * ROW_SUB)

    def in_copy(g):
        slot = lax.rem(g, EXPERT_IN_BUFS)
        return pltpu.make_async_copy(xs_ref.at[rows(g), :], xbuf.at[slot], sem_in.at[slot])

    def out_copy(g):
        slot = lax.rem(g, EXPERT_OUT_BUFS)
        return pltpu.make_async_copy(ybuf.at[slot], ys_ref.at[rows(g), :], sem_out.at[slot])

    def wait_writes_through(g):
        def wait_one(j, carry):
            out_copy(j).wait()
            return carry

        lax.fori_loop(waited_ref[0] + 1, g + 1, wait_one, 0)
        waited_ref[0] = jnp.maximum(waited_ref[0], g)

    @pl.when(e == 0)
    def _():
        waited_ref[0] = -1
        for g in range(EXPERT_LOOKAHEAD):
            @pl.when(g < total)
            def _():
                in_copy(g).start()

    @pl.when(n_chunks > 0)
    def _():
        wgu_bf[:, :EXPERT_DIM] = wg_ref[0].astype(wgu_bf.dtype)
        wgu_bf[:, EXPERT_DIM:] = wu_ref[0].astype(wgu_bf.dtype)
        wd_bf[...] = wd_ref[0].astype(wd_bf.dtype)

    def body(c, carry):
        g = first + c
        in_copy(g).wait()

        @pl.when(g + EXPERT_LOOKAHEAD < total)
        def _():
            in_copy(g + EXPERT_LOOKAHEAD).start()

        bf = jnp.bfloat16
        xa, xb = _unpack_rows(xbuf.at[lax.rem(g, EXPERT_IN_BUFS)], chunk)
        gu = _dot(xa.astype(bf), wgu_bf[:ROW_WORDS, :]) + _dot(xb.astype(bf), wgu_bf[ROW_WORDS:, :])
        gate, up = gu[:, :EXPERT_DIM], gu[:, EXPERT_DIM:]
        y = _dot((gate * _sigmoid(gate) * up).astype(bf), wd_bf[...])
        wait_writes_through(jnp.where(c == 0, g - 1, g - EXPERT_OUT_BUFS))
        _pack_rows(ybuf.at[lax.rem(g, EXPERT_OUT_BUFS)], y)
        out_copy(g).start()
        return carry

    lax.fori_loop(0, n_chunks, body, 0)

    @pl.when(e == pl.num_programs(0) - 1)
    def _():
        wait_writes_through(total - 1)
        ybuf[0] = jnp.zeros(ybuf.shape[1:], ybuf.dtype)
        pad = pl.ds(ys_ref.shape[0] - chunk * ROW_SUB, chunk * ROW_SUB)
        tail = pltpu.make_async_copy(ybuf.at[0], ys_ref.at[pad, :], sem_out.at[0])
        tail.start()
        tail.wait()


def expert_chunks(starts, counts, n_rows):
    chunk = EXPERT_CHUNK
    max_chunks = n_rows // chunk + N_EXPERTS
    nch = (counts + (chunk - 1)) // chunk
    ends = jnp.cumsum(nch)
    first = ends - nch
    g = jnp.arange(max_chunks, dtype=jnp.int32)
    owner = jnp.minimum(jnp.sum(g[:, None] >= ends[None, :], axis=1), N_EXPERTS - 1)
    onehot = (owner[:, None] == jnp.arange(N_EXPERTS)[None, :]).astype(jnp.int32)
    crow = (onehot * (starts - first * chunk)[None, :]).sum(axis=1) + g * chunk
    crow = jnp.where(g < ends[-1], crow, 0).astype(jnp.int32)
    return first.astype(jnp.int32), nch.astype(jnp.int32), crow, ends[-1:].astype(jnp.int32)


def expert_ffn(starts, counts, xs, w_gate, w_up, w_down):
    d = D_MODEL
    n_rows = xs.shape[0] // ROW_SUB - EXPERT_CHUNK
    tables = expert_chunks(starts, counts, n_rows)
    chunk_shape = (EXPERT_CHUNK * ROW_SUB, LANES)
    weights = lambda shape: pl.BlockSpec(shape, lambda e, *_: (e, 0, 0))
    grid_spec = pltpu.PrefetchScalarGridSpec(
        num_scalar_prefetch=len(tables),
        grid=(N_EXPERTS,),
        in_specs=[pl.BlockSpec(memory_space=pl.ANY),
                  weights((1, d, EXPERT_DIM)), weights((1, d, EXPERT_DIM)), weights((1, EXPERT_DIM, d))],
        out_specs=pl.BlockSpec(memory_space=pl.ANY),
        scratch_shapes=[pltpu.VMEM((EXPERT_IN_BUFS,) + chunk_shape, xs.dtype),
                        pltpu.VMEM((EXPERT_OUT_BUFS,) + chunk_shape, xs.dtype),
                        pltpu.VMEM((d, 2 * EXPERT_DIM), jnp.bfloat16), pltpu.VMEM((EXPERT_DIM, d), jnp.bfloat16),
                        pltpu.SemaphoreType.DMA((EXPERT_IN_BUFS,)), pltpu.SemaphoreType.DMA((EXPERT_OUT_BUFS,)),
                        pltpu.SMEM((1,), jnp.int32)],
    )
    return pl.pallas_call(
        _expert_kernel,
        grid_spec=grid_spec,
        out_shape=jax.ShapeDtypeStruct(xs.shape, xs.dtype),
        compiler_params=_cparams("arbitrary"),
        name="expert_ffn",
    )(*tables, xs, w_gate, w_up, w_down)


COMBINE_AHEAD = 2
COMBINE_SLOTS = COMBINE_AHEAD + 1
COMBINE_GROUP = 32


def _combine_kernel(*refs, tm):
    head_refs, (dest_ref, ys_ref, gate_ref, sh_ref, x1_ref, mod_ref, g_ref, o_ref, buf, sem) = (
        refs[:COMBINE_AHEAD], refs[COMBINE_AHEAD:])
    i, n = pl.program_id(0), pl.num_programs(0)

    def gather(src_ref, slot, t):
        for k in range(TOP_K):
            pltpu.make_async_copy(_packed_row(ys_ref, src_ref[k, t]), _packed_row(buf.at[slot, k], t),
                                  sem.at[slot]).start(priority=k % 2)

    def wait_tile(slot):
        for k in range(TOP_K):
            pltpu.make_async_copy(ys_ref.at[pl.ds(0, tm * ROW_SUB), :], buf.at[slot, k], sem.at[slot]).wait()

    @pl.when(i == 0)
    def _():
        for a, head in enumerate(head_refs):
            def body(t, carry, head=head, a=a):
                gather(head, a, t)
                return carry

            lax.fori_loop(0, tm, body, 0)

    cur = lax.rem(i, COMBINE_SLOTS)
    ahead = lax.rem(i + COMBINE_AHEAD, COMBINE_SLOTS)
    wait_tile(cur)
    g_post = g_ref[...]
    gt2 = mod_ref[0, 5:6, :]

    def group(gi, carry):
        t0 = pl.multiple_of(gi * COMBINE_GROUP, COMBINE_GROUP)
        for tt in range(COMBINE_GROUP):
            gather(dest_ref, ahead, t0 + tt)
        rows = pl.ds(t0, COMBINE_GROUP)
        gate = gate_ref[rows, :]
        first = jnp.zeros((COMBINE_GROUP, ROW_WORDS), jnp.float32)
        second = jnp.zeros((COMBINE_GROUP, ROW_WORDS), jnp.float32)
        for k in range(TOP_K):
            packed = buf.at[cur, k, pl.ds(pl.multiple_of(t0 * ROW_SUB, COMBINE_GROUP * ROW_SUB),
                                         COMBINE_GROUP * ROW_SUB), :]
            ya, yb = _unpack_rows(packed, COMBINE_GROUP)
            first = first + ya * gate[:, k:k + 1]
            second = second + yb * gate[:, k:k + 1]
        y = sh_ref[rows, :] + jnp.concatenate([first, second], axis=1)
        o_ref[rows, :] = x1_ref[rows, :] + gt2 * _rms_norm(y, g_post)
        return carry

    lax.fori_loop(0, tm // COMBINE_GROUP, group, 0)

    @pl.when(i == n - 1)
    def _():
        for a in range(1, COMBINE_SLOTS):
            wait_tile(lax.rem(i + a, COMBINE_SLOTS))


def combine(dest, ys, gate_nk, shared, x1, mod, g_post_ffn, *, seq, tm=128):
    n, d = x1.shape
    kk = dest.shape[0]
    per_batch = seq // tm
    n_tiles = n // tm
    tok = pl.BlockSpec((tm, d), lambda i: (i, 0))
    dest_tile = lambda index: pl.BlockSpec((kk, tm), index, memory_space=pltpu.SMEM)
    head_specs = [dest_tile(lambda i, a=a: (0, a)) for a in range(COMBINE_AHEAD)]
    return pl.pallas_call(
        functools.partial(_combine_kernel, tm=tm),
        grid=(n_tiles,),
        in_specs=head_specs + [dest_tile(lambda i: (0, jnp.minimum(i + COMBINE_AHEAD, n_tiles - 1))),
                               pl.BlockSpec(memory_space=pl.ANY),
                               pl.BlockSpec((tm, kk), lambda i: (i, 0)),
                               tok, tok,
                               pl.BlockSpec((1, N_MOD, d), lambda i: (i // per_batch, 0, 0)),
                               pl.BlockSpec((1, d), lambda i: (0, 0))],
        out_specs=tok,
        out_shape=jax.ShapeDtypeStruct((n, d), jnp.float32),
        scratch_shapes=[pltpu.VMEM((COMBINE_SLOTS, kk, tm * ROW_SUB, LANES), ys.dtype),
                        pltpu.SemaphoreType.DMA((COMBINE_SLOTS,))],
        compiler_params=_cparams("arbitrary"),
        name="combine",
    )(*([dest] * COMBINE_AHEAD), dest, ys, gate_nk, shared, x1, mod, g_post_ffn.reshape(1, d))


def _layer(x, mod, rel_bias, g_pre_mix, w_in, w_branch_a, w_branch_b, w_out, g_post_mix, g_pre_ffn, w_router,
           router_bias, w_exp_gate, w_exp_up, w_exp_down, w_sh_gate, w_sh_up, w_sh_down, g_post_ffn):
    b, s, d = x.shape
    bf = jnp.bfloat16
    scale = HEAD_DIM ** -0.5
    sb_q, sb_k, sb_v, dq, dk, dv, w_ga, w_gb = jnp.split(
        w_in, np.cumsum((SB_WIDTH,) * 3 + (DIL_WIDTH,) * 3 + (d,)).tolist(), axis=1)
    groups = lambda w: [w[:, g * DIL_GROUP_WIDTH:(g + 1) * DIL_GROUP_WIDTH] for g in range(len(DIL_CONFIGS))]
    w_rows = jnp.concatenate([sb_k] + [w * scale for w in groups(dq)] + groups(dk) + groups(dv) + [w_ga, w_gb],
                             axis=1).astype(bf)
    w_cols = jnp.concatenate([sb_q * (scale * LOG2E), sb_v], axis=1).T.astype(bf)
    qT, k, vT3, *rest = in_projection(x, mod, g_pre_mix, w_rows, w_cols)
    dqs, dks, dvs, gate_a, gate_b = rest[0:3], rest[3:6], rest[6:9], rest[9], rest[10]
    out_a = sb_attention(qT, k, vT3)
    o_g, lse_g = [], []
    for g, (window, dilation) in enumerate(DIL_CONFIGS):
        bias = dilated_bias(rel_bias[:, g * DIL_HEADS_PER_GROUP:(g + 1) * DIL_HEADS_PER_GROUP], window, dilation)
        o, lse = dilated_attention(dqs[g], dks[g], dvs[g], bias, dilation)
        o_g.append(o)
        lse_g.append(lse)
    x1, hp, shared, logits_t = mix_out(
        out_a, o_g, lse_g, gate_a, gate_b, x, mod, w_branch_a.astype(bf), w_branch_b.astype(bf), w_out.astype(bf),
        g_post_mix, g_pre_ffn, w_router.T.astype(bf), jnp.concatenate([w_sh_gate, w_sh_up], axis=1).astype(bf),
        w_sh_down.astype(bf))
    n = b * s
    idx, gate, rank, counts = route(logits_t, router_bias)
    counts = counts[:, 0]
    starts = jnp.cumsum(counts) - counts
    dest = slot_destinations(idx, rank, starts)
    xs = dispatch(dest, hp, pad_rows=EXPERT_CHUNK)
    ys = expert_ffn(starts, counts, xs, w_exp_gate, w_exp_up, w_exp_down)
    out = combine(dest, ys, gate.T, shared.reshape(n, d), x1.reshape(n, d), mod, g_post_ffn, seq=s)
    return out.reshape(b, s, d)


def kernel(x, c, rel_bias, w_ada, b_ada, g_pre_mix, w_in, w_branch_a, w_branch_b, w_out, g_post_mix, g_pre_ffn,
           w_router, router_bias, w_exp_gate, w_exp_up, w_exp_down, w_sh_gate, w_sh_up, w_sh_down, g_post_ffn):
    depth = w_in.shape[0]
    for layer in range(depth):
        mod = ada_modulation(c, w_ada[layer], b_ada[layer]).reshape(x.shape[0], N_MOD, x.shape[2])
        x = _layer(x, mod, rel_bias, g_pre_mix[layer], w_in[layer], w_branch_a[layer], w_branch_b[layer], w_out[layer],
                   g_post_mix[layer], g_pre_ffn[layer], w_router[layer], router_bias[layer], w_exp_gate[layer],
                   w_exp_up[layer], w_exp_down[layer], w_sh_gate[layer], w_sh_up[layer], w_sh_down[layer],
                   g_post_ffn[layer])
    return x
```

```python
import functools
import math

import jax
import jax.numpy as jnp
import numpy as np
from jax import lax
from jax.experimental import pallas as pl
from jax.experimental.pallas import tpu as pltpu

D_MODEL = 1024
HEAD_DIM = 64
SB_HEADS = 8
SB_WIDTH = SB_HEADS * HEAD_DIM
DIL_CONFIGS = ((128, 1), (512, 4), (2048, 16))
DIL_HEADS_PER_GROUP = 4
DIL_GROUP_WIDTH = DIL_HEADS_PER_GROUP * HEAD_DIM
DIL_WIDTH = DIL_GROUP_WIDTH * len(DIL_CONFIGS)
Q_BLOCK = 128
N_BUCKETS = 32
MAX_DISTANCE = 2048
N_EXPERTS = 256
TOP_K = 8
N_GROUPS = 8
TOP_K_GROUPS = 4
EXPERT_DIM = 256
ROUTED_SCALE = 2.5
N_MOD = 6
EPS = 1e-6

LANES = 128
VMEM_LIMIT_BYTES = 56 * 1024 * 1024
NEG_INF = float("-inf")
LOG2E = math.log2(math.e)


def _cparams(*semantics):
    return pltpu.CompilerParams(dimension_semantics=semantics, vmem_limit_bytes=VMEM_LIMIT_BYTES)


def _rms_norm(v, g):
    return v * lax.rsqrt(jnp.mean(v * v, axis=-1, keepdims=True) + EPS) * g


def _sigmoid(v):
    return 1.0 / (1.0 + jnp.exp(-v))


def _dot(a, b):
    return jnp.dot(a, b, preferred_element_type=jnp.float32)


def _dot_nt(a, b):
    return lax.dot_general(a, b, (((1,), (1,)), ((), ())), preferred_element_type=jnp.float32)


def _split_bf16(v):
    hi = v.astype(jnp.bfloat16)
    lo = (v - hi.astype(jnp.float32)).astype(jnp.bfloat16)
    return hi, lo


ROW_WORDS = D_MODEL // 2
ROW_SUB = ROW_WORDS // LANES


def _pack_rows(ref, v):
    m = v.shape[0]
    bits = pltpu.bitcast(v.astype(jnp.bfloat16).astype(jnp.float32), jnp.uint32)
    words = bits[:, :ROW_WORDS] | (bits[:, ROW_WORDS:] >> 16)
    for j in range(ROW_SUB):
        ref[pl.ds(j, m, stride=ROW_SUB), :] = words[:, j * LANES:(j + 1) * LANES]


def _unpack_rows(ref, m):
    words = jnp.concatenate([ref[pl.ds(j, m, stride=ROW_SUB), :] for j in range(ROW_SUB)], axis=1)
    first = pltpu.bitcast(words & jnp.uint32(0xFFFF0000), jnp.float32)
    second = pltpu.bitcast(words << 16, jnp.float32)
    return first, second


def _ada_kernel(c_ref, w_ref, b_ref, o_ref):
    c = c_ref[...]
    a_hi, a_lo = _split_bf16(c * _sigmoid(c))
    w_hi, w_lo = _split_bf16(w_ref[...])
    o_ref[...] = _dot(a_hi, w_hi) + _dot(a_lo, w_hi) + _dot(a_hi, w_lo) + b_ref[...]


def ada_modulation(c, w_ada, b_ada, *, tn=1536):
    b, d = c.shape
    n = w_ada.shape[1]
    return pl.pallas_call(
        _ada_kernel,
        grid=(n // tn,),
        in_specs=[
            pl.BlockSpec((b, d), lambda j: (0, 0)),
            pl.BlockSpec((d, tn), lambda j: (0, j)),
            pl.BlockSpec((1, tn), lambda j: (0, j)),
        ],
        out_specs=pl.BlockSpec((b, tn), lambda j: (0, j)),
        out_shape=jax.ShapeDtypeStruct((b, n), jnp.float32),
        compiler_params=_cparams("parallel"),
        name="ada_modulation",
    )(c, w_ada, b_ada.reshape(1, n))


def _in_proj_kernel(x_ref, mod_ref, g_ref, w_rows_ref, w_cols_ref, qT_ref, k_ref, vT_ref, *rest, tm):
    dil_refs, ga_ref, gb_ref, stage_ref = rest[:9], rest[9], rest[10], rest[11]
    h = _rms_norm(x_ref[0], g_ref[...]) * (1.0 + mod_ref[0, 1:2, :]) + mod_ref[0, 0:1, :]
    hb = h.astype(jnp.bfloat16)
    off = 0
    k_ref[0] = _dot(hb, w_rows_ref[:, off:off + SB_WIDTH]).astype(k_ref.dtype)
    off += SB_WIDTH
    for n, r in enumerate(dil_refs):
        y = _dot(hb, w_rows_ref[:, off:off + DIL_GROUP_WIDTH])
        off += DIL_GROUP_WIDTH
        dilation = DIL_CONFIGS[n % len(DIL_CONFIGS)][1]
        if dilation == 1:
            r[0] = y.astype(r.dtype)
            continue
        for j in range(DIL_GROUP_WIDTH // LANES):
            stage_ref[j] = y[:, j * LANES:(j + 1) * LANES]
        for c in range(dilation):
            for j in range(DIL_GROUP_WIDTH // LANES):
                col = c * DIL_GROUP_WIDTH + j * LANES
                r[0, :, col:col + LANES] = stage_ref[j, pl.ds(c, tm // dilation, stride=dilation), :].astype(r.dtype)
    for r in (ga_ref, gb_ref):
        r[0] = _sigmoid(_dot(hb, w_rows_ref[:, off:off + D_MODEL])).astype(r.dtype)
        off += D_MODEL
    qT_ref[0] = _dot_nt(w_cols_ref[0:SB_WIDTH, :], hb).astype(qT_ref.dtype)
    vT = _dot_nt(w_cols_ref[SB_WIDTH:2 * SB_WIDTH, :], hb).astype(vT_ref.dtype)
    for p in range(SB_WIDTH // LANES):
        for jb in range(tm // LANES):
            vT_ref[0, p, jb] = vT[p * LANES:(p + 1) * LANES, jb * LANES:(jb + 1) * LANES]


def _dilated_shape(b, s, dilation, dtype):
    return jax.ShapeDtypeStruct((b, s // dilation, dilation * DIL_GROUP_WIDTH), dtype)


def _dilated_spec(tm, dilation):
    return pl.BlockSpec((1, tm // dilation, dilation * DIL_GROUP_WIDTH), lambda bi, i: (bi, i, 0))


def in_projection(x, mod, g_pre, w_rows, w_cols, *, tm=256):
    b, s, d = x.shape
    nt = s // tm
    bf = jnp.bfloat16
    tok = lambda width: pl.BlockSpec((1, tm, width), lambda bi, i: (bi, i, 0))
    out_shapes = [
        jax.ShapeDtypeStruct((b, SB_WIDTH, s), bf),
        jax.ShapeDtypeStruct((b, s, SB_WIDTH), bf),
        jax.ShapeDtypeStruct((b, SB_WIDTH // LANES, s // LANES, LANES, LANES), bf),
    ] + [_dilated_shape(b, s, r, bf) for _, r in DIL_CONFIGS] * 3 + [jax.ShapeDtypeStruct((b, s, d), bf)] * 2
    out_specs = [
        pl.BlockSpec((1, SB_WIDTH, tm), lambda bi, i: (bi, 0, i)),
        tok(SB_WIDTH),
        pl.BlockSpec((1, SB_WIDTH // LANES, tm // LANES, LANES, LANES), lambda bi, i: (bi, 0, i, 0, 0)),
    ] + [_dilated_spec(tm, r) for _, r in DIL_CONFIGS] * 3 + [tok(d)] * 2
    return pl.pallas_call(
        functools.partial(_in_proj_kernel, tm=tm),
        grid=(b, nt),
        in_specs=[
            tok(d),
            pl.BlockSpec((1, N_MOD, d), lambda bi, i: (bi, 0, 0)),
            pl.BlockSpec((1, d), lambda bi, i: (0, 0)),
            pl.BlockSpec(w_rows.shape, lambda bi, i: (0, 0)),
            pl.BlockSpec(w_cols.shape, lambda bi, i: (0, 0)),
        ],
        out_specs=out_specs,
        out_shape=out_shapes,
        scratch_shapes=[pltpu.VMEM((DIL_GROUP_WIDTH // LANES, tm, LANES), jnp.float32)],
        compiler_params=_cparams("parallel", "parallel"),
        name="in_projection",
    )(x, mod, g_pre.reshape(1, d), w_rows, w_cols)


SB_TILE = 512
SB_SUB = 256


def _sb_chunk(kc, vc, q_h, tri, csum, past):
    z = _dot(kc, q_h)
    neg_abs = pltpu.bitcast(pltpu.bitcast(z, jnp.uint32) | jnp.uint32(0x80000000), jnp.float32)
    drop = jnp.maximum(z, 0.0) + jnp.log(1.0 + jnp.exp2(neg_abs)) * LOG2E
    if past is not None:
        drop = jnp.where(past, drop, 0.0)
    weights = [None] * (SB_TILE // SB_SUB)
    for sb in reversed(range(SB_TILE // SB_SUB)):
        rows = slice(sb * SB_SUB, (sb + 1) * SB_SUB)
        d = drop[rows]
        later = _dot(tri, d.astype(jnp.bfloat16))
        w = jnp.exp2(z[rows] - d - later - csum)
        if past is not None:
            w = jnp.where(past[rows], w, 0.0)
        weights[sb] = w.astype(jnp.bfloat16)
        csum = csum + jnp.sum(d, axis=0, keepdims=True)
    w = jnp.concatenate(weights, axis=0)
    t = SB_TILE
    part = jnp.concatenate([_dot(vc[h * HEAD_DIM:(h + 1) * HEAD_DIM], w[:, h * t:(h + 1) * t]) for h in range(2)],
                           axis=0)
    return part, csum


def _sb_attn_kernel(qT_ref, k_ref, vT_ref, tri_ref, o_ref, acc_ref, csum_ref):
    i = pl.program_id(2)
    t = SB_TILE
    q2 = qT_ref[0]
    row = lax.broadcasted_iota(jnp.int32, q2.shape, 0)
    zero = jnp.zeros_like(q2)
    q_heads = jnp.concatenate([jnp.where(row < HEAD_DIM, q2, zero), jnp.where(row >= HEAD_DIM, q2, zero)], axis=1)
    tri = tri_ref[...]
    blocks = t // LANES

    def chunk(c, past, first):
        kc = k_ref[0, pl.ds(pl.multiple_of(c * t, t), t), :]
        vc = jnp.concatenate([vT_ref[0, 0, c * blocks + jb] for jb in range(blocks)], axis=1)
        csum = jnp.zeros((1, 2 * t), jnp.float32) if first else csum_ref[...]
        part, csum = _sb_chunk(kc, vc, q_heads, tri, csum, past)
        acc_ref[...] = part if first else acc_ref[...] + part
        csum_ref[...] = csum

    key = lax.broadcasted_iota(jnp.int32, (t, 2 * t), 0)
    query = lax.broadcasted_iota(jnp.int32, (t, 2 * t), 1)
    chunk(i, key < jnp.where(query < t, query, query - t), True)

    def body(it, carry):
        chunk(i - 1 - it, None, False)
        return carry

    lax.fori_loop(0, i, body, 0)
    o_ref[0] = acc_ref[...].T.astype(o_ref.dtype)


def sb_attention(qT, k, vT3):
    b, width, s = qT.shape
    n_pairs = width // LANES
    t = SB_TILE
    tri = jnp.asarray(np.triu(np.ones((SB_SUB, SB_SUB), np.float32), k=1), jnp.bfloat16)
    return pl.pallas_call(
        _sb_attn_kernel,
        grid=(b, n_pairs, s // t),
        in_specs=[
            pl.BlockSpec((1, LANES, t), lambda bi, p, i: (bi, p, i)),
            pl.BlockSpec((1, s, LANES), lambda bi, p, i: (bi, 0, p)),
            pl.BlockSpec((1, 1, s // LANES, LANES, LANES), lambda bi, p, i: (bi, p, 0, 0, 0)),
            pl.BlockSpec((SB_SUB, SB_SUB), lambda bi, p, i: (0, 0)),
        ],
        out_specs=pl.BlockSpec((1, t, LANES), lambda bi, p, i: (bi, i, p)),
        out_shape=jax.ShapeDtypeStruct((b, s, width), jnp.bfloat16),
        scratch_shapes=[pltpu.VMEM((LANES, t), jnp.float32), pltpu.VMEM((1, 2 * t), jnp.float32)],
        compiler_params=_cparams("parallel", "parallel", "arbitrary"),
        name="sb_attention",
    )(qT, k, vT3, tri)


DIL_BLOCKS_PER_STEP = 4


def _dil_attn_kernel(q_ref, kp_ref, kc_ref, vp_ref, vc_ref, bias_ref, o_ref, lse_ref, *, n_blocks):
    m = pl.program_id(2)
    kk_all = jnp.concatenate([kp_ref[0], kc_ref[0]], axis=0)
    vv_all = jnp.concatenate([vp_ref[0], vc_ref[0]], axis=0)
    lane = lax.broadcasted_iota(jnp.int32, (Q_BLOCK, DIL_GROUP_WIDTH), 1)
    in_head = [jnp.logical_and(lane >= h * HEAD_DIM, lane < (h + 1) * HEAD_DIM) for h in range(DIL_HEADS_PER_GROUP)]
    bias = bias_ref[...]
    key = lax.broadcasted_iota(jnp.int32, bias.shape, 1)
    for j in range(n_blocks):
        q_rows = slice(j * Q_BLOCK, (j + 1) * Q_BLOCK)
        q4 = q_ref[0, q_rows, :]
        kk = kk_all[j * Q_BLOCK:(j + 2) * Q_BLOCK]
        vv = vv_all[j * Q_BLOCK:(j + 2) * Q_BLOCK]
        zero = jnp.zeros_like(q4)
        q_stack = jnp.concatenate([jnp.where(mask, q4, zero) for mask in in_head], axis=0)
        logits = _dot_nt(q_stack, kk) + bias
        if j == 0:
            logits = jnp.where(jnp.logical_or(key >= Q_BLOCK, m > 0), logits, NEG_INF)
        mx = jnp.max(logits, axis=1, keepdims=True)
        p = jnp.exp(logits - mx)
        den = jnp.sum(p, axis=1, keepdims=True)
        pv = _dot(p.astype(jnp.bfloat16), vv) / den
        lse_rows = mx + jnp.log(den)
        out = jnp.zeros(q4.shape, jnp.float32)
        lse = jnp.zeros(q4.shape, jnp.float32)
        for h, mask in enumerate(in_head):
            rows = slice(h * Q_BLOCK, (h + 1) * Q_BLOCK)
            out = jnp.where(mask, pv[rows], out)
            lse = jnp.where(mask, lse_rows[rows], lse)
        o_ref[0, q_rows, :] = out.astype(o_ref.dtype)
        lse_ref[0, q_rows, :] = lse


def dilated_attention(q, k, v, bias, dilation):
    b, length, width = q.shape
    gw = DIL_GROUP_WIDTH
    n_blocks = min(DIL_BLOCKS_PER_STEP, length // Q_BLOCK)
    cur = pl.BlockSpec((1, n_blocks * Q_BLOCK, gw), lambda bi, c, m: (bi, m, c))
    prev = pl.BlockSpec((1, Q_BLOCK, gw), lambda bi, c, m: (bi, jnp.maximum(m * n_blocks - 1, 0), c))
    return pl.pallas_call(
        functools.partial(_dil_attn_kernel, n_blocks=n_blocks),
        grid=(b, width // gw, length // (n_blocks * Q_BLOCK)),
        in_specs=[cur, prev, cur, prev, cur,
                  pl.BlockSpec((DIL_HEADS_PER_GROUP * Q_BLOCK, 2 * Q_BLOCK), lambda bi, c, m: (0, 0))],
        out_specs=[cur, cur],
        out_shape=[jax.ShapeDtypeStruct((b, length, width), jnp.bfloat16),
                   jax.ShapeDtypeStruct((b, length, width), jnp.float32)],
        compiler_params=_cparams("parallel", "parallel", "arbitrary"),
        name=f"dilated_attention_r{dilation}",
    )(q, k, k, v, v, bias.reshape(DIL_HEADS_PER_GROUP * Q_BLOCK, 2 * Q_BLOCK))


def _t5_causal_bucket(dist):
    max_exact = N_BUCKETS // 2
    d = jnp.maximum(dist, 1).astype(jnp.float32)
    large = max_exact + (jnp.log(d / max_exact) / math.log(MAX_DISTANCE / max_exact)
                         * (N_BUCKETS - max_exact)).astype(jnp.int32)
    large = jnp.minimum(large, N_BUCKETS - 1)
    return jnp.where(dist < max_exact, dist, large)


def dilated_bias(rel_bias_group, window, dilation):
    qi = jnp.arange(Q_BLOCK)[:, None]
    kj = jnp.arange(2 * Q_BLOCK)[None, :]
    dist_sub = qi + Q_BLOCK - kj
    bucket = _t5_causal_bucket(jnp.maximum(dist_sub, 0) * dilation)
    onehot = (bucket[..., None] == jnp.arange(N_BUCKETS)).astype(jnp.float32)
    bias = jnp.einsum("qkn,nh->hqk", onehot, rel_bias_group.astype(jnp.float32), precision=lax.Precision.HIGHEST)
    valid = (dist_sub >= 0) & (dist_sub <= window // dilation)
    return jnp.where(valid[None], bias, NEG_INF)


def _mix_out_kernel(oa_ref, o0_ref, o1_ref, o2_ref, l0_ref, l1_ref, l2_ref, ga_ref, gb_ref, x_ref, mod_ref,
                    wa_ref, wb_ref, wo_ref, gpost_ref, gpre_ref, wr_ref, wsgu_ref, wsd_ref,
                    x1_ref, hp_ref, sh_ref, lt_ref, stage_ref):
    f32 = jnp.float32
    tm = x_ref.shape[1]

    def token_major(ref, dilation):
        if dilation == 1:
            return ref[0].astype(f32)
        slabs = DIL_GROUP_WIDTH // LANES
        for c in range(dilation):
            for j in range(slabs):
                col = c * DIL_GROUP_WIDTH + j * LANES
                stage_ref[j, pl.ds(c, tm // dilation, stride=dilation), :] = ref[0, :, col:col + LANES].astype(f32)
        return jnp.concatenate([stage_ref[j] for j in range(slabs)], axis=1)

    dils = [r for _, r in DIL_CONFIGS]
    l0, l1, l2 = (token_major(ref, r) for ref, r in zip((l0_ref, l1_ref, l2_ref), dils))
    mx = jnp.maximum(jnp.maximum(l0, l1), l2)
    e0, e1, e2 = jnp.exp(l0 - mx), jnp.exp(l1 - mx), jnp.exp(l2 - mx)
    den = e0 + e1 + e2
    o0, o1, o2 = (token_major(ref, r) for ref, r in zip((o0_ref, o1_ref, o2_ref), dils))
    out_b = (e0 / den) * o0 + (e1 / den) * o1 + (e2 / den) * o2
    ya = _dot(oa_ref[0], wa_ref[...])
    yb = _dot(out_b.astype(jnp.bfloat16), wb_ref[...])
    merged = ga_ref[0].astype(f32) * ya + gb_ref[0].astype(f32) * yb
    y = _dot(merged.astype(jnp.bfloat16), wo_ref[...])
    x1 = x_ref[0] + mod_ref[0, 2:3, :] * _rms_norm(y, gpost_ref[...])
    x1_ref[0] = x1
    h = _rms_norm(x1, gpre_ref[...]) * (1.0 + mod_ref[0, 4:5, :]) + mod_ref[0, 3:4, :]
    hb = h.astype(jnp.bfloat16)
    _pack_rows(hp_ref, h)
    lt_ref[...] = _dot_nt(wr_ref[...], hb)
    gu = _dot(hb, wsgu_ref[...])
    g, u = gu[:, :EXPERT_DIM], gu[:, EXPERT_DIM:]
    sh_ref[0] = _dot((g * _sigmoid(g) * u).astype(jnp.bfloat16), wsd_ref[...])


def mix_out(out_a, o_g, lse_g, gate_a, gate_b, x, mod, w_a, w_b, w_o, g_post, g_pre_ffn, w_rT, w_sgu, w_sd, *, tm=256):
    b, s, d = x.shape
    tok = lambda width: pl.BlockSpec((1, tm, width), lambda bi, i: (bi, i, 0))
    full = lambda a: pl.BlockSpec(a.shape, lambda bi, i: (0,) * a.ndim)
    nt = s // tm
    g_post, g_pre_ffn = g_post.reshape(1, d), g_pre_ffn.reshape(1, d)
    weights = (w_a, w_b, w_o, g_post, g_pre_ffn, w_rT, w_sgu, w_sd)
    return pl.pallas_call(
        _mix_out_kernel,
        grid=(b, nt),
        in_specs=[tok(SB_WIDTH)] + [_dilated_spec(tm, r) for _, r in DIL_CONFIGS] * 2 + [tok(d)] * 3
        + [pl.BlockSpec((1, N_MOD, d), lambda bi, i: (bi, 0, 0))] + [full(w) for w in weights],
        out_specs=[tok(d), pl.BlockSpec((tm * ROW_SUB, LANES), lambda bi, i: (bi * nt + i, 0)), tok(d),
                   pl.BlockSpec((N_EXPERTS, tm), lambda bi, i: (0, bi * nt + i))],
        out_shape=[jax.ShapeDtypeStruct((b, s, d), jnp.float32),
                   jax.ShapeDtypeStruct((b * s * ROW_SUB, LANES), jnp.uint32),
                   jax.ShapeDtypeStruct((b, s, d), jnp.float32),
                   jax.ShapeDtypeStruct((N_EXPERTS, b * s), jnp.float32)],
        scratch_shapes=[pltpu.VMEM((DIL_GROUP_WIDTH // LANES, tm, LANES), jnp.float32)],
        compiler_params=_cparams("parallel", "parallel"),
        name="mix_out",
    )(out_a, *o_g, *lse_g, gate_a, gate_b, x, mod, *weights)


def _route_kernel(lt_ref, bias_ref, tri_ref, idx_ref, gate_ref, rank_ref, cnt_ref, base_ref, *, tn):
    f32 = jnp.float32

    @pl.when(pl.program_id(0) == 0)
    def _():
        base_ref[...] = jnp.zeros_like(base_ref)

    per_group = N_EXPERTS // N_GROUPS
    scores = _sigmoid(lt_ref[...])
    sel = scores + bias_ref[...]
    sel_groups = [sel[g * per_group:(g + 1) * per_group] for g in range(N_GROUPS)]
    ri = lax.broadcasted_iota(jnp.int32, (per_group, tn), 0)
    group_scores = []
    for sg in sel_groups:
        m1 = jnp.max(sg, axis=0, keepdims=True)
        first = jnp.min(jnp.where(sg == m1, ri, per_group), axis=0, keepdims=True)
        group_scores.append(m1 + jnp.max(jnp.where(ri == first, NEG_INF, sg), axis=0, keepdims=True))
    cur = jnp.concatenate(group_scores, axis=0)
    gi = lax.broadcasted_iota(jnp.int32, cur.shape, 0)
    keep = jnp.zeros(cur.shape, jnp.float32)
    for _ in range(TOP_K_GROUPS):
        mg = jnp.max(cur, axis=0, keepdims=True)
        pick = gi == jnp.min(jnp.where(cur == mg, gi, N_GROUPS), axis=0, keepdims=True)
        keep = jnp.where(pick, 1.0, keep)
        cur = jnp.where(pick, NEG_INF, cur)
    cur = jnp.concatenate([jnp.where(keep[g:g + 1, :] > 0.0, sg, NEG_INF) for g, sg in enumerate(sel_groups)], axis=0)
    ei = lax.broadcasted_iota(jnp.int32, cur.shape, 0)
    picks, idxs, gates = [], [], []
    for _ in range(TOP_K):
        mk = jnp.max(cur, axis=0, keepdims=True)
        fk = jnp.min(jnp.where(cur == mk, ei, N_EXPERTS), axis=0, keepdims=True)
        pick = ei == fk
        picks.append(pick)
        idxs.append(fk)
        gates.append(jnp.sum(jnp.where(pick, scores, 0.0), axis=0, keepdims=True))
        cur = jnp.where(pick, NEG_INF, cur)
    gate = jnp.concatenate(gates, axis=0)
    gate_ref[...] = gate / jnp.sum(gate, axis=0, keepdims=True) * ROUTED_SCALE
    idx_ref[...] = jnp.concatenate(idxs, axis=0)
    chosen = functools.reduce(jnp.logical_or, picks)
    onehot = jnp.where(chosen, 1.0, 0.0)
    earlier = _dot(onehot.astype(jnp.bfloat16), tri_ref[...]) + base_ref[...]
    ranks = [jnp.sum(jnp.where(p, earlier, 0.0), axis=0, keepdims=True) for p in picks]
    rank_ref[...] = jnp.concatenate(ranks, axis=0).astype(jnp.int32)
    base = base_ref[...] + jnp.sum(onehot, axis=1, keepdims=True)
    base_ref[...] = base
    cnt_ref[...] = jnp.broadcast_to(base, cnt_ref.shape).astype(jnp.int32)


def route(logits_t, router_bias, *, tn=512):
    ne, n = logits_t.shape
    tri = jnp.asarray(np.triu(np.ones((tn, tn), np.float32), k=1), jnp.bfloat16)
    slot = pl.BlockSpec((TOP_K, tn), lambda i: (0, i))
    return pl.pallas_call(
        functools.partial(_route_kernel, tn=tn),
        grid=(n // tn,),
        in_specs=[pl.BlockSpec((ne, tn), lambda i: (0, i)),
                  pl.BlockSpec((ne, 1), lambda i: (0, 0)),
                  pl.BlockSpec((tn, tn), lambda i: (0, 0))],
        out_specs=[slot, slot, slot, pl.BlockSpec((ne, LANES), lambda i: (0, 0))],
        out_shape=[jax.ShapeDtypeStruct((TOP_K, n), jnp.int32),
                   jax.ShapeDtypeStruct((TOP_K, n), jnp.float32),
                   jax.ShapeDtypeStruct((TOP_K, n), jnp.int32),
                   jax.ShapeDtypeStruct((ne, LANES), jnp.int32)],
        scratch_shapes=[pltpu.VMEM((ne, 1), jnp.float32)],
        compiler_params=_cparams("arbitrary"),
        name="route",
    )(logits_t, router_bias.reshape(ne, 1), tri)


def _dest_kernel(idx_ref, rank_ref, start_ref, dest_ref):
    idx = idx_ref[...]
    ei = lax.broadcasted_iota(jnp.int32, (N_EXPERTS, idx.shape[1]), 0)
    start = start_ref[...]
    rows = [jnp.sum(jnp.where(ei == idx[k:k + 1, :], start, 0.0), axis=0, keepdims=True) for k in range(TOP_K)]
    dest_ref[...] = jnp.concatenate(rows, axis=0).astype(jnp.int32) + rank_ref[...]


def slot_destinations(idx, rank, starts, *, tn=512):
    kk, n = idx.shape
    slot = pl.BlockSpec((kk, tn), lambda i: (0, i))
    return pl.pallas_call(
        _dest_kernel,
        grid=(n // tn,),
        in_specs=[slot, slot, pl.BlockSpec((N_EXPERTS, 1), lambda i: (0, 0))],
        out_specs=slot,
        out_shape=jax.ShapeDtypeStruct((kk, n), jnp.int32),
        compiler_params=_cparams("parallel"),
        name="slot_destinations",
    )(idx, rank, starts.astype(jnp.float32).reshape(N_EXPERTS, 1))


def _packed_row(ref, row):
    start = row * ROW_SUB
    return ref.at[pl.ds(start if isinstance(row, int) else pl.multiple_of(start, ROW_SUB), ROW_SUB), :]


def _dispatch_kernel(dest_ref, h_ref, xs_ref, zero_ref, sem, *, tm, n_rows):
    pad = zero_ref.shape[0]

    @pl.when(pl.program_id(0) == 0)
    def _():
        zero_ref[...] = jnp.zeros_like(zero_ref)
        tail = pltpu.make_async_copy(zero_ref, xs_ref.at[pl.ds(n_rows * ROW_SUB, pad), :], sem)
        tail.start()
        tail.wait()

    def body(t, _):
        for k in range(TOP_K):
            pltpu.make_async_copy(_packed_row(h_ref, t), _packed_row(xs_ref, dest_ref[k, t]), sem).start(priority=k % 2)
        return 0

    lax.fori_loop(0, tm, body, 0)
    for _ in range(TOP_K):
        pltpu.make_async_copy(h_ref, xs_ref.at[pl.ds(0, tm * ROW_SUB), :], sem).wait()


def dispatch(dest, hp, *, pad_rows, tm=512):
    kk, n = dest.shape
    n_rows = n * kk
    return pl.pallas_call(
        functools.partial(_dispatch_kernel, tm=tm, n_rows=n_rows),
        grid=(n // tm,),
        in_specs=[pl.BlockSpec((kk, tm), lambda i: (0, i), memory_space=pltpu.SMEM),
                  pl.BlockSpec((tm * ROW_SUB, LANES), lambda i: (i, 0))],
        out_specs=pl.BlockSpec(memory_space=pl.ANY),
        out_shape=jax.ShapeDtypeStruct(((n_rows + pad_rows) * ROW_SUB, LANES), hp.dtype),
        scratch_shapes=[pltpu.VMEM((pad_rows * ROW_SUB, LANES), hp.dtype), pltpu.SemaphoreType.DMA],
        compiler_params=_cparams("arbitrary"),
        name="dispatch",
    )(dest, hp)


EXPERT_CHUNK = 256
EXPERT_LOOKAHEAD = 3
EXPERT_IN_BUFS = EXPERT_LOOKAHEAD + 1
EXPERT_OUT_BUFS = 3


def _expert_kernel(first_ref, nch_ref, crow_ref, total_ref, xs_ref, wg_ref, wu_ref, wd_ref, ys_ref,
                   xbuf, ybuf, wgu_bf, wd_bf, sem_in, sem_out, waited_ref):
    e = pl.program_id(0)
    chunk = EXPERT_CHUNK
    first, n_chunks, total = first_ref[e], nch_ref[e], total_ref[0]

    def rows(g):
        return pl.ds(pl.multiple_of(crow_ref[g] * ROW_SUB, ROW_SUB), chunk * ROW_SUB)

    def in_copy(g):
        slot = lax.rem(g, EXPERT_IN_BUFS)
        return pltpu.make_async_copy(xs_ref.at[rows(g), :], xbuf.at[slot], sem_in.at[slot])

    def out_copy(g):
        slot = lax.rem(g, EXPERT_OUT_BUFS)
        return pltpu.make_async_copy(ybuf.at[slot], ys_ref.at[rows(g), :], sem_out.at[slot])

    def wait_writes_through(g):
        def wait_one(j, carry):
            out_copy(j).wait()
            return carry

        lax.fori_loop(waited_ref[0] + 1, g + 1, wait_one, 0)
        waited_ref[0] = jnp.maximum(waited_ref[0], g)

    @pl.when(e == 0)
    def _():
        waited_ref[0] = -1
        for g in range(EXPERT_LOOKAHEAD):
            @pl.when(g < total)
            def _():
                in_copy(g).start()

    @pl.when(n_chunks > 0)
    def _():
        wgu_bf[:, :EXPERT_DIM] = wg_ref[0].astype(wgu_bf.dtype)
        wgu_bf[:, EXPERT_DIM:] = wu_ref[0].astype(wgu_bf.dtype)
        wd_bf[...] = wd_ref[0].astype(wd_bf.dtype)

    def body(c, carry):
        g = first + c
        in_copy(g).wait()

        @pl.when(g + EXPERT_LOOKAHEAD < total)
        def _():
            in_copy(g + EXPERT_LOOKAHEAD).start()

        bf = jnp.bfloat16
        xa, xb = _unpack_rows(xbuf.at[lax.rem(g, EXPERT_IN_BUFS)], chunk)
        gu = _dot(xa.astype(bf), wgu_bf[:ROW_WORDS, :]) + _dot(xb.astype(bf), wgu_bf[ROW_WORDS:, :])
        gate, up = gu[:, :EXPERT_DIM], gu[:, EXPERT_DIM:]
        y = _dot((gate * _sigmoid(gate) * up).astype(bf), wd_bf[...])
        wait_writes_through(jnp.where(c == 0, g - 1, g - EXPERT_OUT_BUFS))
        _pack_rows(ybuf.at[lax.rem(g, EXPERT_OUT_BUFS)], y)
        out_copy(g).start()
        return carry

    lax.fori_loop(0, n_chunks, body, 0)

    @pl.when(e == pl.num_programs(0) - 1)
    def _():
        wait_writes_through(total - 1)
        ybuf[0] = jnp.zeros(ybuf.shape[1:], ybuf.dtype)
        pad = pl.ds(ys_ref.shape[0] - chunk * ROW_SUB, chunk * ROW_SUB)
        tail = pltpu.make_async_copy(ybuf.at[0], ys_ref.at[pad, :], sem_out.at[0])
        tail.start()
        tail.wait()


def expert_chunks(starts, counts, n_rows):
    chunk = EXPERT_CHUNK
    max_chunks = n_rows // chunk + N_EXPERTS
    nch = (counts + (chunk - 1)) // chunk
    ends = jnp.cumsum(nch)
    first = ends - nch
    g = jnp.arange(max_chunks, dtype=jnp.int32)
    owner = jnp.minimum(jnp.sum(g[:, None] >= ends[None, :], axis=1), N_EXPERTS - 1)
    onehot = (owner[:, None] == jnp.arange(N_EXPERTS)[None, :]).astype(jnp.int32)
    crow = (onehot * (starts - first * chunk)[None, :]).sum(axis=1) + g * chunk
    crow = jnp.where(g < ends[-1], crow, 0).astype(jnp.int32)
    return first.astype(jnp.int32), nch.astype(jnp.int32), crow, ends[-1:].astype(jnp.int32)


def expert_ffn(starts, counts, xs, w_gate, w_up, w_down):
    d = D_MODEL
    n_rows = xs.shape[0] // ROW_SUB - EXPERT_CHUNK
    tables = expert_chunks(starts, counts, n_rows)
    chunk_shape = (EXPERT_CHUNK * ROW_SUB, LANES)
    weights = lambda shape: pl.BlockSpec(shape, lambda e, *_: (e, 0, 0))
    grid_spec = pltpu.PrefetchScalarGridSpec(
        num_scalar_prefetch=len(tables),
        grid=(N_EXPERTS,),
        in_specs=[pl.BlockSpec(memory_space=pl.ANY),
                  weights((1, d, EXPERT_DIM)), weights((1, d, EXPERT_DIM)), weights((1, EXPERT_DIM, d))],
        out_specs=pl.BlockSpec(memory_space=pl.ANY),
        scratch_shapes=[pltpu.VMEM((EXPERT_IN_BUFS,) + chunk_shape, xs.dtype),
                        pltpu.VMEM((EXPERT_OUT_BUFS,) + chunk_shape, xs.dtype),
                        pltpu.VMEM((d, 2 * EXPERT_DIM), jnp.bfloat16), pltpu.VMEM((EXPERT_DIM, d), jnp.bfloat16),
                        pltpu.SemaphoreType.DMA((EXPERT_IN_BUFS,)), pltpu.SemaphoreType.DMA((EXPERT_OUT_BUFS,)),
                        pltpu.SMEM((1,), jnp.int32)],
    )
    return pl.pallas_call(
        _expert_kernel,
        grid_spec=grid_spec,
        out_shape=jax.ShapeDtypeStruct(xs.shape, xs.dtype),
        compiler_params=_cparams("arbitrary"),
        name="expert_ffn",
    )(*tables, xs, w_gate, w_up, w_down)


COMBINE_AHEAD = 2
COMBINE_SLOTS = COMBINE_AHEAD + 1
COMBINE_GROUP = 32


def _combine_kernel(*refs, tm):
    head_refs, (dest_ref, ys_ref, gate_ref, sh_ref, x1_ref, mod_ref, g_ref, o_ref, buf, sem) = (
        refs[:COMBINE_AHEAD], refs[COMBINE_AHEAD:])
    i, n = pl.program_id(0), pl.num_programs(0)

    def gather(src_ref, slot, t):
        for k in range(TOP_K):
            pltpu.make_async_copy(_packed_row(ys_ref, src_ref[t * TOP_K + k]), _packed_row(buf.at[slot, k], t),
                                  sem.at[slot]).start(priority=k % 2)

    def wait_tile(slot):
        for k in range(TOP_K):
            pltpu.make_async_copy(ys_ref.at[pl.ds(0, tm * ROW_SUB), :], buf.at[slot, k], sem.at[slot]).wait()

    @pl.when(i == 0)
    def _():
        for a, head in enumerate(head_refs):
            def body(t, carry, head=head, a=a):
                gather(head, a, t)
                return carry

            lax.fori_loop(0, tm, body, 0)

    def step(cur):
        ahead = (cur + COMBINE_AHEAD) % COMBINE_SLOTS
        wait_tile(cur)
        g_post = g_ref[...]
        gt2 = mod_ref[0, 5:6, :]
        for t0 in range(0, tm, COMBINE_GROUP):
            for t in range(t0, t0 + COMBINE_GROUP):
                gather(dest_ref, ahead, t)
            rows = slice(t0, t0 + COMBINE_GROUP)
            gate = gate_ref[rows, :]
            first = jnp.zeros((COMBINE_GROUP, ROW_WORDS), jnp.float32)
            second = jnp.zeros((COMBINE_GROUP, ROW_WORDS), jnp.float32)
            for k in range(TOP_K):
                packed = buf.at[cur, k, pl.ds(t0 * ROW_SUB, COMBINE_GROUP * ROW_SUB), :]
                ya, yb = _unpack_rows(packed, COMBINE_GROUP)
                first = first + ya * gate[:, k:k + 1]
                second = second + yb * gate[:, k:k + 1]
            y = sh_ref[rows, :] + jnp.concatenate([first, second], axis=1)
            o_ref[rows, :] = x1_ref[rows, :] + gt2 * _rms_norm(y, g_post)

        @pl.when(i == n - 1)
        def _():
            for a in range(1, COMBINE_SLOTS):
                wait_tile((cur + a) % COMBINE_SLOTS)

    for slot in range(COMBINE_SLOTS):
        pl.when(lax.rem(i, COMBINE_SLOTS) == slot)(functools.partial(step, slot))


def combine(dest, ys, gate_nk, shared, x1, mod, g_post_ffn, *, seq, tm=128):
    n, d = x1.shape
    per_batch = seq // tm
    n_tiles = n // tm
    tok = pl.BlockSpec((tm, d), lambda i: (i, 0))
    dest_tile = lambda index: pl.BlockSpec((tm * TOP_K,), index, memory_space=pltpu.SMEM)
    head_specs = [dest_tile(lambda i, a=a: (a,)) for a in range(COMBINE_AHEAD)]
    return pl.pallas_call(
        functools.partial(_combine_kernel, tm=tm),
        grid=(n_tiles,),
        in_specs=head_specs + [dest_tile(lambda i: (jnp.minimum(i + COMBINE_AHEAD, n_tiles - 1),)),
                               pl.BlockSpec(memory_space=pl.ANY),
                               pl.BlockSpec((tm, TOP_K), lambda i: (i, 0)),
                               tok, tok,
                               pl.BlockSpec((1, N_MOD, d), lambda i: (i // per_batch, 0, 0)),
                               pl.BlockSpec((1, d), lambda i: (0, 0))],
        out_specs=tok,
        out_shape=jax.ShapeDtypeStruct((n, d), jnp.float32),
        scratch_shapes=[pltpu.VMEM((COMBINE_SLOTS, TOP_K, tm * ROW_SUB, LANES), ys.dtype),
                        pltpu.SemaphoreType.DMA((COMBINE_SLOTS,))],
        compiler_params=_cparams("arbitrary"),
        name="combine",
    )(*([dest] * COMBINE_AHEAD), dest, ys, gate_nk, shared, x1, mod, g_post_ffn.reshape(1, d))


def _layer(x, mod, rel_bias, g_pre_mix, w_in, w_branch_a, w_branch_b, w_out, g_post_mix, g_pre_ffn, w_router,
           router_bias, w_exp_gate, w_exp_up, w_exp_down, w_sh_gate, w_sh_up, w_sh_down, g_post_ffn):
    b, s, d = x.shape
    bf = jnp.bfloat16
    scale = HEAD_DIM ** -0.5
    sb_q, sb_k, sb_v, dq, dk, dv, w_ga, w_gb = jnp.split(
        w_in, np.cumsum((SB_WIDTH,) * 3 + (DIL_WIDTH,) * 3 + (d,)).tolist(), axis=1)
    groups = lambda w: [w[:, g * DIL_GROUP_WIDTH:(g + 1) * DIL_GROUP_WIDTH] for g in range(len(DIL_CONFIGS))]
    w_rows = jnp.concatenate([sb_k] + [w * scale for w in groups(dq)] + groups(dk) + groups(dv) + [w_ga, w_gb],
                             axis=1).astype(bf)
    w_cols = jnp.concatenate([sb_q * (scale * LOG2E), sb_v], axis=1).T.astype(bf)
    qT, k, vT3, *rest = in_projection(x, mod, g_pre_mix, w_rows, w_cols)
    dqs, dks, dvs, gate_a, gate_b = rest[0:3], rest[3:6], rest[6:9], rest[9], rest[10]
    out_a = sb_attention(qT, k, vT3)
    o_g, lse_g = [], []
    for g, (window, dilation) in enumerate(DIL_CONFIGS):
        bias = dilated_bias(rel_bias[:, g * DIL_HEADS_PER_GROUP:(g + 1) * DIL_HEADS_PER_GROUP], window, dilation)
        o, lse = dilated_attention(dqs[g], dks[g], dvs[g], bias, dilation)
        o_g.append(o)
        lse_g.append(lse)
    x1, hp, shared, logits_t = mix_out(
        out_a, o_g, lse_g, gate_a, gate_b, x, mod, w_branch_a.astype(bf), w_branch_b.astype(bf), w_out.astype(bf),
        g_post_mix, g_pre_ffn, w_router.T.astype(bf), jnp.concatenate([w_sh_gate, w_sh_up], axis=1).astype(bf),
        w_sh_down.astype(bf))
    n = b * s
    idx, gate, rank, counts = route(logits_t, router_bias)
    counts = counts[:, 0]
    starts = jnp.cumsum(counts) - counts
    dest = slot_destinations(idx, rank, starts)
    xs = dispatch(dest, hp, pad_rows=EXPERT_CHUNK)
    ys = expert_ffn(starts, counts, xs, w_exp_gate, w_exp_up, w_exp_down)
    out = combine(dest.T.reshape(-1), ys, gate.T, shared.reshape(n, d), x1.reshape(n, d), mod, g_post_ffn, seq=s)
    return out.reshape(b, s, d)


def kernel(x, c, rel_bias, w_ada, b_ada, g_pre_mix, w_in, w_branch_a, w_branch_b, w_out, g_post_mix, g_pre_ffn,
           w_router, router_bias, w_exp_gate, w_exp_up, w_exp_down, w_sh_gate, w_sh_up, w_sh_down, g_post_ffn):
    depth = w_in.shape[0]
    for layer in range(depth):
        mod = ada_modulation(c, w_ada[layer], b_ada[layer]).reshape(x.shape[0], N_MOD, x.shape[2])
        x = _layer(x, mod, rel_bias, g_pre_mix[layer], w_in[layer], w_branch_a[layer], w_branch_b[layer], w_out[layer],
                   g_post_mix[layer], g_pre_ffn[layer], w_router[layer], router_bias[layer], w_exp_gate[layer],
                   w_exp_up[layer], w_exp_down[layer], w_sh_gate[layer], w_sh_up[layer], w_sh_down[layer],
                   g_post_ffn[layer])
    return x
```

```python
import functools
import math

import jax
import jax.numpy as jnp
import numpy as np
from jax import lax
from jax.experimental import pallas as pl
from jax.experimental.pallas import tpu as pltpu

D_MODEL = 1024
HEAD_DIM = 64
SB_HEADS = 8
SB_WIDTH = SB_HEADS * HEAD_DIM
DIL_CONFIGS = ((128, 1), (512, 4), (2048, 16))
DIL_HEADS_PER_GROUP = 4
DIL_GROUP_WIDTH = DIL_HEADS_PER_GROUP * HEAD_DIM
DIL_WIDTH = DIL_GROUP_WIDTH * len(DIL_CONFIGS)
Q_BLOCK = 128
N_BUCKETS = 32
MAX_DISTANCE = 2048
N_EXPERTS = 256
TOP_K = 8
N_GROUPS = 8
TOP_K_GROUPS = 4
EXPERT_DIM = 256
ROUTED_SCALE = 2.5
N_MOD = 6
EPS = 1e-6

LANES = 128
VMEM_LIMIT_BYTES = 56 * 1024 * 1024
NEG_INF = float("-inf")
LOG2E = math.log2(math.e)


def _cparams(*semantics):
    return pltpu.CompilerParams(dimension_semantics=semantics, vmem_limit_bytes=VMEM_LIMIT_BYTES)


def _rms_norm(v, g):
    return v * lax.rsqrt(jnp.mean(v * v, axis=-1, keepdims=True) + EPS) * g


def _sigmoid(v):
    return 1.0 / (1.0 + jnp.exp(-v))


def _dot(a, b):
    return jnp.dot(a, b, preferred_element_type=jnp.float32)


def _dot_nt(a, b):
    return lax.dot_general(a, b, (((1,), (1,)), ((), ())), preferred_element_type=jnp.float32)


def _split_bf16(v):
    hi = v.astype(jnp.bfloat16)
    lo = (v - hi.astype(jnp.float32)).astype(jnp.bfloat16)
    return hi, lo


ROW_WORDS = D_MODEL // 2
ROW_SUB = ROW_WORDS // LANES


def _pack_rows(ref, v):
    m = v.shape[0]
    bits = pltpu.bitcast(v.astype(jnp.bfloat16).astype(jnp.float32), jnp.uint32)
    words = bits[:, :ROW_WORDS] | (bits[:, ROW_WORDS:] >> 16)
    for j in range(ROW_SUB):
        ref[pl.ds(j, m, stride=ROW_SUB), :] = words[:, j * LANES:(j + 1) * LANES]


def _unpack_rows(ref, m):
    words = jnp.concatenate([ref[pl.ds(j, m, stride=ROW_SUB), :] for j in range(ROW_SUB)], axis=1)
    first = pltpu.bitcast(words & jnp.uint32(0xFFFF0000), jnp.float32)
    second = pltpu.bitcast(words << 16, jnp.float32)
    return first, second


def _ada_kernel(c_ref, w_ref, b_ref, o_ref):
    c = c_ref[...]
    a_hi, a_lo = _split_bf16(c * _sigmoid(c))
    w_hi, w_lo = _split_bf16(w_ref[...])
    o_ref[...] = _dot(a_hi, w_hi) + _dot(a_lo, w_hi) + _dot(a_hi, w_lo) + b_ref[...]


def ada_modulation(c, w_ada, b_ada, *, tn=1536):
    b, d = c.shape
    n = w_ada.shape[1]
    return pl.pallas_call(
        _ada_kernel,
        grid=(n // tn,),
        in_specs=[
            pl.BlockSpec((b, d), lambda j: (0, 0)),
            pl.BlockSpec((d, tn), lambda j: (0, j)),
            pl.BlockSpec((1, tn), lambda j: (0, j)),
        ],
        out_specs=pl.BlockSpec((b, tn), lambda j: (0, j)),
        out_shape=jax.ShapeDtypeStruct((b, n), jnp.float32),
        compiler_params=_cparams("parallel"),
        name="ada_modulation",
    )(c, w_ada, b_ada.reshape(1, n))


def _in_proj_kernel(x_ref, mod_ref, g_ref, w_rows_ref, w_cols_ref, qT_ref, k_ref, vT_ref, *rest, tm):
    dil_refs, ga_ref, gb_ref, stage_ref = rest[:9], rest[9], rest[10], rest[11]
    h = _rms_norm(x_ref[0], g_ref[...]) * (1.0 + mod_ref[0, 1:2, :]) + mod_ref[0, 0:1, :]
    hb = h.astype(jnp.bfloat16)
    off = 0
    k_ref[0] = _dot(hb, w_rows_ref[:, off:off + SB_WIDTH]).astype(k_ref.dtype)
    off += SB_WIDTH
    for n, r in enumerate(dil_refs):
        y = _dot(hb, w_rows_ref[:, off:off + DIL_GROUP_WIDTH])
        off += DIL_GROUP_WIDTH
        dilation = DIL_CONFIGS[n % len(DIL_CONFIGS)][1]
        if dilation == 1:
            r[0] = y.astype(r.dtype)
            continue
        for j in range(DIL_GROUP_WIDTH // LANES):
            stage_ref[j] = y[:, j * LANES:(j + 1) * LANES]
        for c in range(dilation):
            for j in range(DIL_GROUP_WIDTH // LANES):
                col = c * DIL_GROUP_WIDTH + j * LANES
                r[0, :, col:col + LANES] = stage_ref[j, pl.ds(c, tm // dilation, stride=dilation), :].astype(r.dtype)
    for r in (ga_ref, gb_ref):
        r[0] = _sigmoid(_dot(hb, w_rows_ref[:, off:off + D_MODEL])).astype(r.dtype)
        off += D_MODEL
    qT_ref[0] = _dot_nt(w_cols_ref[0:SB_WIDTH, :], hb).astype(qT_ref.dtype)
    vT = _dot_nt(w_cols_ref[SB_WIDTH:2 * SB_WIDTH, :], hb).astype(vT_ref.dtype)
    for p in range(SB_WIDTH // LANES):
        for jb in range(tm // LANES):
            vT_ref[0, p, jb] = vT[p * LANES:(p + 1) * LANES, jb * LANES:(jb + 1) * LANES]


def _dilated_shape(b, s, dilation, dtype):
    return jax.ShapeDtypeStruct((b, s // dilation, dilation * DIL_GROUP_WIDTH), dtype)


def _dilated_spec(tm, dilation):
    return pl.BlockSpec((1, tm // dilation, dilation * DIL_GROUP_WIDTH), lambda bi, i: (bi, i, 0))


def in_projection(x, mod, g_pre, w_rows, w_cols, *, tm=256):
    b, s, d = x.shape
    nt = s // tm
    bf = jnp.bfloat16
    tok = lambda width: pl.BlockSpec((1, tm, width), lambda bi, i: (bi, i, 0))
    out_shapes = [
        jax.ShapeDtypeStruct((b, SB_WIDTH, s), bf),
        jax.ShapeDtypeStruct((b, s, SB_WIDTH), bf),
        jax.ShapeDtypeStruct((b, SB_WIDTH // LANES, s // LANES, LANES, LANES), bf),
    ] + [_dilated_shape(b, s, r, bf) for _, r in DIL_CONFIGS] * 3 + [jax.ShapeDtypeStruct((b, s, d), bf)] * 2
    out_specs = [
        pl.BlockSpec((1, SB_WIDTH, tm), lambda bi, i: (bi, 0, i)),
        tok(SB_WIDTH),
        pl.BlockSpec((1, SB_WIDTH // LANES, tm // LANES, LANES, LANES), lambda bi, i: (bi, 0, i, 0, 0)),
    ] + [_dilated_spec(tm, r) for _, r in DIL_CONFIGS] * 3 + [tok(d)] * 2
    return pl.pallas_call(
        functools.partial(_in_proj_kernel, tm=tm),
        grid=(b, nt),
        in_specs=[
            tok(d),
            pl.BlockSpec((1, N_MOD, d), lambda bi, i: (bi, 0, 0)),
            pl.BlockSpec((1, d), lambda bi, i: (0, 0)),
            pl.BlockSpec(w_rows.shape, lambda bi, i: (0, 0)),
            pl.BlockSpec(w_cols.shape, lambda bi, i: (0, 0)),
        ],
        out_specs=out_specs,
        out_shape=out_shapes,
        scratch_shapes=[pltpu.VMEM((DIL_GROUP_WIDTH // LANES, tm, LANES), jnp.float32)],
        compiler_params=_cparams("parallel", "parallel"),
        name="in_projection",
    )(x, mod, g_pre.reshape(1, d), w_rows, w_cols)


SB_TILE = 512
SB_SUB = 256


def _sb_chunk(kc, vc, q_h, tri, csum, past):
    z = _dot(kc, q_h)
    neg_abs = pltpu.bitcast(pltpu.bitcast(z, jnp.uint32) | jnp.uint32(0x80000000), jnp.float32)
    drop = jnp.maximum(z, 0.0) + jnp.log(1.0 + jnp.exp2(neg_abs)) * LOG2E
    if past is not None:
        drop = jnp.where(past, drop, 0.0)
    weights = [None] * (SB_TILE // SB_SUB)
    for sb in reversed(range(SB_TILE // SB_SUB)):
        rows = slice(sb * SB_SUB, (sb + 1) * SB_SUB)
        d = drop[rows]
        later = _dot(tri, d.astype(jnp.bfloat16))
        w = jnp.exp2(z[rows] - d - later - csum)
        if past is not None:
            w = jnp.where(past[rows], w, 0.0)
        weights[sb] = w.astype(jnp.bfloat16)
        csum = csum + jnp.sum(d, axis=0, keepdims=True)
    w = jnp.concatenate(weights, axis=0)
    t = SB_TILE
    part = jnp.concatenate([_dot(vc[h * HEAD_DIM:(h + 1) * HEAD_DIM], w[:, h * t:(h + 1) * t]) for h in range(2)],
                           axis=0)
    return part, csum


def _sb_attn_kernel(qT_ref, k_ref, vT_ref, tri_ref, o_ref, acc_ref, csum_ref):
    i = pl.program_id(2)
    t = SB_TILE
    q2 = qT_ref[0]
    row = lax.broadcasted_iota(jnp.int32, q2.shape, 0)
    zero = jnp.zeros_like(q2)
    q_heads = jnp.concatenate([jnp.where(row < HEAD_DIM, q2, zero), jnp.where(row >= HEAD_DIM, q2, zero)], axis=1)
    tri = tri_ref[...]
    blocks = t // LANES

    def chunk(c, past, first):
        kc = k_ref[0, pl.ds(pl.multiple_of(c * t, t), t), :]
        vc = jnp.concatenate([vT_ref[0, 0, c * blocks + jb] for jb in range(blocks)], axis=1)
        csum = jnp.zeros((1, 2 * t), jnp.float32) if first else csum_ref[...]
        part, csum = _sb_chunk(kc, vc, q_heads, tri, csum, past)
        acc_ref[...] = part if first else acc_ref[...] + part
        csum_ref[...] = csum

    key = lax.broadcasted_iota(jnp.int32, (t, 2 * t), 0)
    query = lax.broadcasted_iota(jnp.int32, (t, 2 * t), 1)
    chunk(i, key < jnp.where(query < t, query, query - t), True)

    def body(it, carry):
        chunk(i - 1 - it, None, False)
        return carry

    lax.fori_loop(0, i, body, 0)
    o_ref[0] = acc_ref[...].T.astype(o_ref.dtype)


def sb_attention(qT, k, vT3):
    b, width, s = qT.shape
    n_pairs = width // LANES
    t = SB_TILE
    tri = jnp.asarray(np.triu(np.ones((SB_SUB, SB_SUB), np.float32), k=1), jnp.bfloat16)
    return pl.pallas_call(
        _sb_attn_kernel,
        grid=(b, n_pairs, s // t),
        in_specs=[
            pl.BlockSpec((1, LANES, t), lambda bi, p, i: (bi, p, i)),
            pl.BlockSpec((1, s, LANES), lambda bi, p, i: (bi, 0, p)),
            pl.BlockSpec((1, 1, s // LANES, LANES, LANES), lambda bi, p, i: (bi, p, 0, 0, 0)),
            pl.BlockSpec((SB_SUB, SB_SUB), lambda bi, p, i: (0, 0)),
        ],
        out_specs=pl.BlockSpec((1, t, LANES), lambda bi, p, i: (bi, i, p)),
        out_shape=jax.ShapeDtypeStruct((b, s, width), jnp.bfloat16),
        scratch_shapes=[pltpu.VMEM((LANES, t), jnp.float32), pltpu.VMEM((1, 2 * t), jnp.float32)],
        compiler_params=_cparams("parallel", "parallel", "arbitrary"),
        name="sb_attention",
    )(qT, k, vT3, tri)


DIL_BLOCKS_PER_STEP = 4


def _dil_attn_kernel(q_ref, kp_ref, kc_ref, vp_ref, vc_ref, bias_ref, o_ref, lse_ref, *, n_blocks):
    m = pl.program_id(2)
    kk_all = jnp.concatenate([kp_ref[0], kc_ref[0]], axis=0)
    vv_all = jnp.concatenate([vp_ref[0], vc_ref[0]], axis=0)
    lane = lax.broadcasted_iota(jnp.int32, (Q_BLOCK, DIL_GROUP_WIDTH), 1)
    in_head = [jnp.logical_and(lane >= h * HEAD_DIM, lane < (h + 1) * HEAD_DIM) for h in range(DIL_HEADS_PER_GROUP)]
    bias = bias_ref[...]
    key = lax.broadcasted_iota(jnp.int32, bias.shape, 1)
    for j in range(n_blocks):
        q_rows = slice(j * Q_BLOCK, (j + 1) * Q_BLOCK)
        q4 = q_ref[0, q_rows, :]
        kk = kk_all[j * Q_BLOCK:(j + 2) * Q_BLOCK]
        vv = vv_all[j * Q_BLOCK:(j + 2) * Q_BLOCK]
        zero = jnp.zeros_like(q4)
        q_stack = jnp.concatenate([jnp.where(mask, q4, zero) for mask in in_head], axis=0)
        logits = _dot_nt(q_stack, kk) + bias
        if j == 0:
            logits = jnp.where(jnp.logical_or(key >= Q_BLOCK, m > 0), logits, NEG_INF)
        mx = jnp.max(logits, axis=1, keepdims=True)
        p = jnp.exp(logits - mx)
        den = jnp.sum(p, axis=1, keepdims=True)
        pv = _dot(p.astype(jnp.bfloat16), vv) / den
        lse_rows = mx + jnp.log(den)
        out = jnp.zeros(q4.shape, jnp.float32)
        lse = jnp.zeros(q4.shape, jnp.float32)
        for h, mask in enumerate(in_head):
            rows = slice(h * Q_BLOCK, (h + 1) * Q_BLOCK)
            out = jnp.where(mask, pv[rows], out)
            lse = jnp.where(mask, lse_rows[rows], lse)
        o_ref[0, q_rows, :] = out.astype(o_ref.dtype)
        lse_ref[0, q_rows, :] = lse


def dilated_attention(q, k, v, bias, dilation):
    b, length, width = q.shape
    gw = DIL_GROUP_WIDTH
    n_blocks = min(DIL_BLOCKS_PER_STEP, length // Q_BLOCK)
    cur = pl.BlockSpec((1, n_blocks * Q_BLOCK, gw), lambda bi, c, m: (bi, m, c))
    prev = pl.BlockSpec((1, Q_BLOCK, gw), lambda bi, c, m: (bi, jnp.maximum(m * n_blocks - 1, 0), c))
    return pl.pallas_call(
        functools.partial(_dil_attn_kernel, n_blocks=n_blocks),
        grid=(b, width // gw, length // (n_blocks * Q_BLOCK)),
        in_specs=[cur, prev, cur, prev, cur,
                  pl.BlockSpec((DIL_HEADS_PER_GROUP * Q_BLOCK, 2 * Q_BLOCK), lambda bi, c, m: (0, 0))],
        out_specs=[cur, cur],
        out_shape=[jax.ShapeDtypeStruct((b, length, width), jnp.bfloat16),
                   jax.ShapeDtypeStruct((b, length, width), jnp.float32)],
        compiler_params=_cparams("parallel", "parallel", "arbitrary"),
        name=f"dilated_attention_r{dilation}",
    )(q, k, k, v, v, bias.reshape(DIL_HEADS_PER_GROUP * Q_BLOCK, 2 * Q_BLOCK))


def _t5_causal_bucket(dist):
    max_exact = N_BUCKETS // 2
    d = jnp.maximum(dist, 1).astype(jnp.float32)
    large = max_exact + (jnp.log(d / max_exact) / math.log(MAX_DISTANCE / max_exact)
                         * (N_BUCKETS - max_exact)).astype(jnp.int32)
    large = jnp.minimum(large, N_BUCKETS - 1)
    return jnp.where(dist < max_exact, dist, large)


def dilated_bias(rel_bias_group, window, dilation):
    qi = jnp.arange(Q_BLOCK)[:, None]
    kj = jnp.arange(2 * Q_BLOCK)[None, :]
    dist_sub = qi + Q_BLOCK - kj
    bucket = _t5_causal_bucket(jnp.maximum(dist_sub, 0) * dilation)
    onehot = (bucket[..., None] == jnp.arange(N_BUCKETS)).astype(jnp.float32)
    bias = jnp.einsum("qkn,nh->hqk", onehot, rel_bias_group.astype(jnp.float32), precision=lax.Precision.HIGHEST)
    valid = (dist_sub >= 0) & (dist_sub <= window // dilation)
    return jnp.where(valid[None], bias, NEG_INF)


def _mix_out_kernel(oa_ref, o0_ref, o1_ref, o2_ref, l0_ref, l1_ref, l2_ref, ga_ref, gb_ref, x_ref, mod_ref,
                    wa_ref, wb_ref, wo_ref, gpost_ref, gpre_ref, wr_ref, wsgu_ref, wsd_ref,
                    x1_ref, hp_ref, sh_ref, lt_ref, stage_ref):
    f32 = jnp.float32
    tm = x_ref.shape[1]

    def token_major(ref, dilation):
        if dilation == 1:
            return ref[0].astype(f32)
        slabs = DIL_GROUP_WIDTH // LANES
        for c in range(dilation):
            for j in range(slabs):
                col = c * DIL_GROUP_WIDTH + j * LANES
                stage_ref[j, pl.ds(c, tm // dilation, stride=dilation), :] = ref[0, :, col:col + LANES].astype(f32)
        return jnp.concatenate([stage_ref[j] for j in range(slabs)], axis=1)

    dils = [r for _, r in DIL_CONFIGS]
    l0, l1, l2 = (token_major(ref, r) for ref, r in zip((l0_ref, l1_ref, l2_ref), dils))
    mx = jnp.maximum(jnp.maximum(l0, l1), l2)
    e0, e1, e2 = jnp.exp(l0 - mx), jnp.exp(l1 - mx), jnp.exp(l2 - mx)
    den = e0 + e1 + e2
    o0, o1, o2 = (token_major(ref, r) for ref, r in zip((o0_ref, o1_ref, o2_ref), dils))
    out_b = (e0 / den) * o0 + (e1 / den) * o1 + (e2 / den) * o2
    ya = _dot(oa_ref[0], wa_ref[...])
    yb = _dot(out_b.astype(jnp.bfloat16), wb_ref[...])
    merged = ga_ref[0].astype(f32) * ya + gb_ref[0].astype(f32) * yb
    y = _dot(merged.astype(jnp.bfloat16), wo_ref[...])
    x1 = x_ref[0] + mod_ref[0, 2:3, :] * _rms_norm(y, gpost_ref[...])
    x1_ref[0] = x1
    h = _rms_norm(x1, gpre_ref[...]) * (1.0 + mod_ref[0, 4:5, :]) + mod_ref[0, 3:4, :]
    hb = h.astype(jnp.bfloat16)
    _pack_rows(hp_ref, h)
    lt_ref[...] = _dot_nt(wr_ref[...], hb)
    gu = _dot(hb, wsgu_ref[...])
    g, u = gu[:, :EXPERT_DIM], gu[:, EXPERT_DIM:]
    sh_ref[0] = _dot((g * _sigmoid(g) * u).astype(jnp.bfloat16), wsd_ref[...])


def mix_out(out_a, o_g, lse_g, gate_a, gate_b, x, mod, w_a, w_b, w_o, g_post, g_pre_ffn, w_rT, w_sgu, w_sd, *, tm=512):
    b, s, d = x.shape
    tok = lambda width: pl.BlockSpec((1, tm, width), lambda bi, i: (bi, i, 0))
    full = lambda a: pl.BlockSpec(a.shape, lambda bi, i: (0,) * a.ndim)
    nt = s // tm
    g_post, g_pre_ffn = g_post.reshape(1, d), g_pre_ffn.reshape(1, d)
    weights = (w_a, w_b, w_o, g_post, g_pre_ffn, w_rT, w_sgu, w_sd)
    return pl.pallas_call(
        _mix_out_kernel,
        grid=(b, nt),
        in_specs=[tok(SB_WIDTH)] + [_dilated_spec(tm, r) for _, r in DIL_CONFIGS] * 2 + [tok(d)] * 3
        + [pl.BlockSpec((1, N_MOD, d), lambda bi, i: (bi, 0, 0))] + [full(w) for w in weights],
        out_specs=[tok(d), pl.BlockSpec((tm * ROW_SUB, LANES), lambda bi, i: (bi * nt + i, 0)), tok(d),
                   pl.BlockSpec((N_EXPERTS, tm), lambda bi, i: (0, bi * nt + i))],
        out_shape=[jax.ShapeDtypeStruct((b, s, d), jnp.float32),
                   jax.ShapeDtypeStruct((b * s * ROW_SUB, LANES), jnp.uint32),
                   jax.ShapeDtypeStruct((b, s, d), jnp.float32),
                   jax.ShapeDtypeStruct((N_EXPERTS, b * s), jnp.float32)],
        scratch_shapes=[pltpu.VMEM((DIL_GROUP_WIDTH // LANES, tm, LANES), jnp.float32)],
        compiler_params=_cparams("parallel", "parallel"),
        name="mix_out",
    )(out_a, *o_g, *lse_g, gate_a, gate_b, x, mod, *weights)


def _route_kernel(lt_ref, bias_ref, tri_ref, idx_ref, gate_ref, rank_ref, cnt_ref, base_ref, *, tn):
    f32 = jnp.float32

    @pl.when(pl.program_id(0) == 0)
    def _():
        base_ref[...] = jnp.zeros_like(base_ref)

    per_group = N_EXPERTS // N_GROUPS
    scores = _sigmoid(lt_ref[...])
    sel = scores + bias_ref[...]
    sel_groups = [sel[g * per_group:(g + 1) * per_group] for g in range(N_GROUPS)]
    ri = lax.broadcasted_iota(jnp.int32, (per_group, tn), 0)
    group_scores = []
    for sg in sel_groups:
        m1 = jnp.max(sg, axis=0, keepdims=True)
        first = jnp.min(jnp.where(sg == m1, ri, per_group), axis=0, keepdims=True)
        group_scores.append(m1 + jnp.max(jnp.where(ri == first, NEG_INF, sg), axis=0, keepdims=True))
    cur = jnp.concatenate(group_scores, axis=0)
    gi = lax.broadcasted_iota(jnp.int32, cur.shape, 0)
    keep = jnp.zeros(cur.shape, jnp.float32)
    for _ in range(TOP_K_GROUPS):
        mg = jnp.max(cur, axis=0, keepdims=True)
        pick = gi == jnp.min(jnp.where(cur == mg, gi, N_GROUPS), axis=0, keepdims=True)
        keep = jnp.where(pick, 1.0, keep)
        cur = jnp.where(pick, NEG_INF, cur)
    cur = jnp.concatenate([jnp.where(keep[g:g + 1, :] > 0.0, sg, NEG_INF) for g, sg in enumerate(sel_groups)], axis=0)
    ei = lax.broadcasted_iota(jnp.int32, cur.shape, 0)
    picks, idxs, gates = [], [], []
    for _ in range(TOP_K):
        mk = jnp.max(cur, axis=0, keepdims=True)
        fk = jnp.min(jnp.where(cur == mk, ei, N_EXPERTS), axis=0, keepdims=True)
        pick = ei == fk
        picks.append(pick)
        idxs.append(fk)
        gates.append(jnp.sum(jnp.where(pick, scores, 0.0), axis=0, keepdims=True))
        cur = jnp.where(pick, NEG_INF, cur)
    gate = jnp.concatenate(gates, axis=0)
    gate_ref[...] = gate / jnp.sum(gate, axis=0, keepdims=True) * ROUTED_SCALE
    idx_ref[...] = jnp.concatenate(idxs, axis=0)
    chosen = functools.reduce(jnp.logical_or, picks)
    onehot = jnp.where(chosen, 1.0, 0.0)
    earlier = _dot(onehot.astype(jnp.bfloat16), tri_ref[...]) + base_ref[...]
    ranks = [jnp.sum(jnp.where(p, earlier, 0.0), axis=0, keepdims=True) for p in picks]
    rank_ref[...] = jnp.concatenate(ranks, axis=0).astype(jnp.int32)
    base = base_ref[...] + jnp.sum(onehot, axis=1, keepdims=True)
    base_ref[...] = base
    cnt_ref[...] = jnp.broadcast_to(base, cnt_ref.shape).astype(jnp.int32)


def route(logits_t, router_bias, *, tn=512):
    ne, n = logits_t.shape
    tri = jnp.asarray(np.triu(np.ones((tn, tn), np.float32), k=1), jnp.bfloat16)
    slot = pl.BlockSpec((TOP_K, tn), lambda i: (0, i))
    return pl.pallas_call(
        functools.partial(_route_kernel, tn=tn),
        grid=(n // tn,),
        in_specs=[pl.BlockSpec((ne, tn), lambda i: (0, i)),
                  pl.BlockSpec((ne, 1), lambda i: (0, 0)),
                  pl.BlockSpec((tn, tn), lambda i: (0, 0))],
        out_specs=[slot, slot, slot, pl.BlockSpec((ne, LANES), lambda i: (0, 0))],
        out_shape=[jax.ShapeDtypeStruct((TOP_K, n), jnp.int32),
                   jax.ShapeDtypeStruct((TOP_K, n), jnp.float32),
                   jax.ShapeDtypeStruct((TOP_K, n), jnp.int32),
                   jax.ShapeDtypeStruct((ne, LANES), jnp.int32)],
        scratch_shapes=[pltpu.VMEM((ne, 1), jnp.float32)],
        compiler_params=_cparams("arbitrary"),
        name="route",
    )(logits_t, router_bias.reshape(ne, 1), tri)


def _dest_kernel(idx_ref, rank_ref, start_ref, dest_ref):
    idx = idx_ref[...]
    ei = lax.broadcasted_iota(jnp.int32, (N_EXPERTS, idx.shape[1]), 0)
    start = start_ref[...]
    rows = [jnp.sum(jnp.where(ei == idx[k:k + 1, :], start, 0.0), axis=0, keepdims=True) for k in range(TOP_K)]
    dest_ref[...] = jnp.concatenate(rows, axis=0).astype(jnp.int32) + rank_ref[...]


def slot_destinations(idx, rank, starts, *, tn=512):
    kk, n = idx.shape
    slot = pl.BlockSpec((kk, tn), lambda i: (0, i))
    return pl.pallas_call(
        _dest_kernel,
        grid=(n // tn,),
        in_specs=[slot, slot, pl.BlockSpec((N_EXPERTS, 1), lambda i: (0, 0))],
        out_specs=slot,
        out_shape=jax.ShapeDtypeStruct((kk, n), jnp.int32),
        compiler_params=_cparams("parallel"),
        name="slot_destinations",
    )(idx, rank, starts.astype(jnp.float32).reshape(N_EXPERTS, 1))


def _packed_row(ref, row):
    start = row * ROW_SUB
    return ref.at[pl.ds(start if isinstance(row, int) else pl.multiple_of(start, ROW_SUB), ROW_SUB), :]


def _dispatch_kernel(dest_ref, h_ref, xs_ref, zero_ref, sem, *, tm, n_rows):
    pad = zero_ref.shape[0]

    @pl.when(pl.program_id(0) == 0)
    def _():
        zero_ref[...] = jnp.zeros_like(zero_ref)
        tail = pltpu.make_async_copy(zero_ref, xs_ref.at[pl.ds(n_rows * ROW_SUB, pad), :], sem)
        tail.start()
        tail.wait()

    def body(t, _):
        for k in range(TOP_K):
            pltpu.make_async_copy(_packed_row(h_ref, t), _packed_row(xs_ref, dest_ref[k, t]), sem).start(priority=k % 2)
        return 0

    lax.fori_loop(0, tm, body, 0)
    for _ in range(TOP_K):
        pltpu.make_async_copy(h_ref, xs_ref.at[pl.ds(0, tm * ROW_SUB), :], sem).wait()


def dispatch(dest, hp, *, pad_rows, tm=512):
    kk, n = dest.shape
    n_rows = n * kk
    return pl.pallas_call(
        functools.partial(_dispatch_kernel, tm=tm, n_rows=n_rows),
        grid=(n // tm,),
        in_specs=[pl.BlockSpec((kk, tm), lambda i: (0, i), memory_space=pltpu.SMEM),
                  pl.BlockSpec((tm * ROW_SUB, LANES), lambda i: (i, 0))],
        out_specs=pl.BlockSpec(memory_space=pl.ANY),
        out_shape=jax.ShapeDtypeStruct(((n_rows + pad_rows) * ROW_SUB, LANES), hp.dtype),
        scratch_shapes=[pltpu.VMEM((pad_rows * ROW_SUB, LANES), hp.dtype), pltpu.SemaphoreType.DMA],
        compiler_params=_cparams("arbitrary"),
        name="dispatch",
    )(dest, hp)


EXPERT_CHUNK = 256
EXPERT_LOOKAHEAD = 3
EXPERT_IN_BUFS = EXPERT_LOOKAHEAD + 1
EXPERT_OUT_BUFS = 3


def _expert_kernel(first_ref, nch_ref, crow_ref, total_ref, xs_ref, wg_ref, wu_ref, wd_ref, ys_ref,
                   xbuf, ybuf, wgu_bf, wd_bf, h_ref, sem_in, sem_out, waited_ref):
    e = pl.program_id(0)
    chunk = EXPERT_CHUNK
    first, n_chunks, total = first_ref[e], nch_ref[e], total_ref[0]

    def rows(g):
        return pl.ds(pl.multiple_of(crow_ref[g] * ROW_SUB, ROW_SUB), chunk * ROW_SUB)

    def in_copy(g):
        slot = lax.rem(g, EXPERT_IN_BUFS)
        return pltpu.make_async_copy(xs_ref.at[rows(g), :], xbuf.at[slot], sem_in.at[slot])

    def out_copy(g):
        slot = lax.rem(g, EXPERT_OUT_BUFS)
        return pltpu.make_async_copy(ybuf.at[slot], ys_ref.at[rows(g), :], sem_out.at[slot])

    def wait_writes_through(g):
        def wait_one(j, carry):
            out_copy(j).wait()
            return carry

        lax.fori_loop(waited_ref[0] + 1, g + 1, wait_one, 0)
        waited_ref[0] = jnp.maximum(waited_ref[0], g)

    @pl.when(e == 0)
    def _():
        waited_ref[0] = -1
        for g in range(EXPERT_LOOKAHEAD):
            @pl.when(g < total)
            def _():
                in_copy(g).start()

    @pl.when(n_chunks > 0)
    def _():
        wgu_bf[:, :EXPERT_DIM] = wg_ref[0].astype(wgu_bf.dtype)
        wgu_bf[:, EXPERT_DIM:] = wu_ref[0].astype(wgu_bf.dtype)
        wd_bf[...] = wd_ref[0].astype(wd_bf.dtype)

    def acquire(g):
        in_copy(g).wait()

        @pl.when(g + EXPERT_LOOKAHEAD < total)
        def _():
            in_copy(g + EXPERT_LOOKAHEAD).start()

    def hidden(g):
        bf = jnp.bfloat16
        xa, xb = _unpack_rows(xbuf.at[lax.rem(g, EXPERT_IN_BUFS)], chunk)
        gu = _dot(xa.astype(bf), wgu_bf[:ROW_WORDS, :]) + _dot(xb.astype(bf), wgu_bf[ROW_WORDS:, :])
        gate, up = gu[:, :EXPERT_DIM], gu[:, EXPERT_DIM:]
        return (gate * _sigmoid(gate) * up).astype(bf)

    def pack(g, h):
        _pack_rows(ybuf.at[lax.rem(g, EXPERT_OUT_BUFS)], _dot(h, wd_bf[...]))

    def release(g):
        @pl.when(g == first)
        def _():
            wait_writes_through(g - 1)

        out_copy(g).start()

    @pl.when(n_chunks > 0)
    def _():
        acquire(first)
        h_ref[...] = hidden(first)

    def body(c, carry):
        g = first + c
        acquire(g)
        wait_writes_through(g - 1 - EXPERT_OUT_BUFS)
        h_prev = h_ref[...]
        pack(g - 1, h_prev)
        h_ref[...] = hidden(g)
        release(g - 1)
        return carry

    lax.fori_loop(1, n_chunks, body, 0)

    @pl.when(n_chunks > 0)
    def _():
        last = first + n_chunks - 1
        wait_writes_through(last - EXPERT_OUT_BUFS)
        pack(last, h_ref[...])
        release(last)

    @pl.when(e == pl.num_programs(0) - 1)
    def _():
        wait_writes_through(total - 1)
        ybuf[0] = jnp.zeros(ybuf.shape[1:], ybuf.dtype)
        pad = pl.ds(ys_ref.shape[0] - chunk * ROW_SUB, chunk * ROW_SUB)
        tail = pltpu.make_async_copy(ybuf.at[0], ys_ref.at[pad, :], sem_out.at[0])
        tail.start()
        tail.wait()


def expert_chunks(starts, counts, n_rows):
    chunk = EXPERT_CHUNK
    max_chunks = n_rows // chunk + N_EXPERTS
    nch = (counts + (chunk - 1)) // chunk
    ends = jnp.cumsum(nch)
    first = ends - nch
    g = jnp.arange(max_chunks, dtype=jnp.int32)
    owner = jnp.minimum(jnp.sum(g[:, None] >= ends[None, :], axis=1), N_EXPERTS - 1)
    onehot = (owner[:, None] == jnp.arange(N_EXPERTS)[None, :]).astype(jnp.int32)
    crow = (onehot * (starts - first * chunk)[None, :]).sum(axis=1) + g * chunk
    crow = jnp.where(g < ends[-1], crow, 0).astype(jnp.int32)
    return first.astype(jnp.int32), nch.astype(jnp.int32), crow, ends[-1:].astype(jnp.int32)


def expert_ffn(starts, counts, xs, w_gate, w_up, w_down):
    d = D_MODEL
    n_rows = xs.shape[0] // ROW_SUB - EXPERT_CHUNK
    tables = expert_chunks(starts, counts, n_rows)
    chunk_shape = (EXPERT_CHUNK * ROW_SUB, LANES)
    weights = lambda shape: pl.BlockSpec(shape, lambda e, *_: (e, 0, 0))
    grid_spec = pltpu.PrefetchScalarGridSpec(
        num_scalar_prefetch=len(tables),
        grid=(N_EXPERTS,),
        in_specs=[pl.BlockSpec(memory_space=pl.ANY),
                  weights((1, d, EXPERT_DIM)), weights((1, d, EXPERT_DIM)), weights((1, EXPERT_DIM, d))],
        out_specs=pl.BlockSpec(memory_space=pl.ANY),
        scratch_shapes=[pltpu.VMEM((EXPERT_IN_BUFS,) + chunk_shape, xs.dtype),
                        pltpu.VMEM((EXPERT_OUT_BUFS,) + chunk_shape, xs.dtype),
                        pltpu.VMEM((d, 2 * EXPERT_DIM), jnp.bfloat16), pltpu.VMEM((EXPERT_DIM, d), jnp.bfloat16),
                        pltpu.VMEM((EXPERT_CHUNK, EXPERT_DIM), jnp.bfloat16),
                        pltpu.SemaphoreType.DMA((EXPERT_IN_BUFS,)), pltpu.SemaphoreType.DMA((EXPERT_OUT_BUFS,)),
                        pltpu.SMEM((1,), jnp.int32)],
    )
    return pl.pallas_call(
        _expert_kernel,
        grid_spec=grid_spec,
        out_shape=jax.ShapeDtypeStruct(xs.shape, xs.dtype),
        compiler_params=_cparams("arbitrary"),
        name="expert_ffn",
    )(*tables, xs, w_gate, w_up, w_down)


COMBINE_AHEAD = 2
COMBINE_SLOTS = COMBINE_AHEAD + 1
COMBINE_GROUP = 32


def _combine_kernel(*refs, tm):
    head_refs, (dest_ref, ys_ref, gate_ref, sh_ref, x1_ref, mod_ref, g_ref, o_ref, buf, sem) = (
        refs[:COMBINE_AHEAD], refs[COMBINE_AHEAD:])
    i, n = pl.program_id(0), pl.num_programs(0)

    def gather(src_ref, slot, t):
        for k in range(TOP_K):
            pltpu.make_async_copy(_packed_row(ys_ref, src_ref[t * TOP_K + k]), _packed_row(buf.at[slot, k], t),
                                  sem.at[slot]).start(priority=k % 2)

    def wait_tile(slot):
        for k in range(TOP_K):
            pltpu.make_async_copy(ys_ref.at[pl.ds(0, tm * ROW_SUB), :], buf.at[slot, k], sem.at[slot]).wait()

    @pl.when(i == 0)
    def _():
        for a, head in enumerate(head_refs):
            def body(t, carry, head=head, a=a):
                gather(head, a, t)
                return carry

            lax.fori_loop(0, tm, body, 0)

    def step(cur):
        ahead = (cur + COMBINE_AHEAD) % COMBINE_SLOTS
        wait_tile(cur)
        g_post = g_ref[...]
        gt2 = mod_ref[0, 5:6, :]
        for t0 in range(0, tm, COMBINE_GROUP):
            for t in range(t0, t0 + COMBINE_GROUP):
                gather(dest_ref, ahead, t)
            rows = slice(t0, t0 + COMBINE_GROUP)
            gate = gate_ref[rows, :]
            first = jnp.zeros((COMBINE_GROUP, ROW_WORDS), jnp.float32)
            second = jnp.zeros((COMBINE_GROUP, ROW_WORDS), jnp.float32)
            for k in range(TOP_K):
                packed = buf.at[cur, k, pl.ds(t0 * ROW_SUB, COMBINE_GROUP * ROW_SUB), :]
                ya, yb = _unpack_rows(packed, COMBINE_GROUP)
                first = first + ya * gate[:, k:k + 1]
                second = second + yb * gate[:, k:k + 1]
            y = sh_ref[rows, :] + jnp.concatenate([first, second], axis=1)
            o_ref[rows, :] = x1_ref[rows, :] + gt2 * _rms_norm(y, g_post)

        @pl.when(i == n - 1)
        def _():
            for a in range(1, COMBINE_SLOTS):
                wait_tile((cur + a) % COMBINE_SLOTS)

    for slot in range(COMBINE_SLOTS):
        pl.when(lax.rem(i, COMBINE_SLOTS) == slot)(functools.partial(step, slot))


def combine(dest, ys, gate_nk, shared, x1, mod, g_post_ffn, *, seq, tm=128):
    n, d = x1.shape
    per_batch = seq // tm
    n_tiles = n // tm
    tok = pl.BlockSpec((tm, d), lambda i: (i, 0))
    dest_tile = lambda index: pl.BlockSpec((tm * TOP_K,), index, memory_space=pltpu.SMEM)
    head_specs = [dest_tile(lambda i, a=a: (a,)) for a in range(COMBINE_AHEAD)]
    return pl.pallas_call(
        functools.partial(_combine_kernel, tm=tm),
        grid=(n_tiles,),
        in_specs=head_specs + [dest_tile(lambda i: (jnp.minimum(i + COMBINE_AHEAD, n_tiles - 1),)),
                               pl.BlockSpec(memory_space=pl.ANY),
                               pl.BlockSpec((tm, TOP_K), lambda i: (i, 0)),
                               tok, tok,
                               pl.BlockSpec((1, N_MOD, d), lambda i: (i // per_batch, 0, 0)),
                               pl.BlockSpec((1, d), lambda i: (0, 0))],
        out_specs=tok,
        out_shape=jax.ShapeDtypeStruct((n, d), jnp.float32),
        scratch_shapes=[pltpu.VMEM((COMBINE_SLOTS, TOP_K, tm * ROW_SUB, LANES), ys.dtype),
                        pltpu.SemaphoreType.DMA((COMBINE_SLOTS,))],
        compiler_params=_cparams("arbitrary"),
        name="combine",
    )(*([dest] * COMBINE_AHEAD), dest, ys, gate_nk, shared, x1, mod, g_post_ffn.reshape(1, d))


def _layer(x, mod, rel_bias, g_pre_mix, w_in, w_branch_a, w_branch_b, w_out, g_post_mix, g_pre_ffn, w_router,
           router_bias, w_exp_gate, w_exp_up, w_exp_down, w_sh_gate, w_sh_up, w_sh_down, g_post_ffn):
    b, s, d = x.shape
    bf = jnp.bfloat16
    scale = HEAD_DIM ** -0.5
    sb_q, sb_k, sb_v, dq, dk, dv, w_ga, w_gb = jnp.split(
        w_in, np.cumsum((SB_WIDTH,) * 3 + (DIL_WIDTH,) * 3 + (d,)).tolist(), axis=1)
    groups = lambda w: [w[:, g * DIL_GROUP_WIDTH:(g + 1) * DIL_GROUP_WIDTH] for g in range(len(DIL_CONFIGS))]
    w_rows = jnp.concatenate([sb_k] + [w * scale for w in groups(dq)] + groups(dk) + groups(dv) + [w_ga, w_gb],
                             axis=1).astype(bf)
    w_cols = jnp.concatenate([sb_q * (scale * LOG2E), sb_v], axis=1).T.astype(bf)
    qT, k, vT3, *rest = in_projection(x, mod, g_pre_mix, w_rows, w_cols)
    dqs, dks, dvs, gate_a, gate_b = rest[0:3], rest[3:6], rest[6:9], rest[9], rest[10]
    out_a = sb_attention(qT, k, vT3)
    o_g, lse_g = [], []
    for g, (window, dilation) in enumerate(DIL_CONFIGS):
        bias = dilated_bias(rel_bias[:, g * DIL_HEADS_PER_GROUP:(g + 1) * DIL_HEADS_PER_GROUP], window, dilation)
        o, lse = dilated_attention(dqs[g], dks[g], dvs[g], bias, dilation)
        o_g.append(o)
        lse_g.append(lse)
    x1, hp, shared, logits_t = mix_out(
        out_a, o_g, lse_g, gate_a, gate_b, x, mod, w_branch_a.astype(bf), w_branch_b.astype(bf), w_out.astype(bf),
        g_post_mix, g_pre_ffn, w_router.T.astype(bf), jnp.concatenate([w_sh_gate, w_sh_up], axis=1).astype(bf),
        w_sh_down.astype(bf))
    n = b * s
    idx, gate, rank, counts = route(logits_t, router_bias)
    counts = counts[:, 0]
    starts = jnp.cumsum(counts) - counts
    dest = slot_destinations(idx, rank, starts)
    xs = dispatch(dest, hp, pad_rows=EXPERT_CHUNK)
    ys = expert_ffn(starts, counts, xs, w_exp_gate, w_exp_up, w_exp_down)
    out = combine(dest.T.reshape(-1), ys, gate.T, shared.reshape(n, d), x1.reshape(n, d), mod, g_post_ffn, seq=s)
    return out.reshape(b, s, d)


def kernel(x, c, rel_bias, w_ada, b_ada, g_pre_mix, w_in, w_branch_a, w_branch_b, w_out, g_post_mix, g_pre_ffn,
           w_router, router_bias, w_exp_gate, w_exp_up, w_exp_down, w_sh_gate, w_sh_up, w_sh_down, g_post_ffn):
    depth = w_in.shape[0]
    for layer in range(depth):
        mod = ada_modulation(c, w_ada[layer], b_ada[layer]).reshape(x.shape[0], N_MOD, x.shape[2])
        x = _layer(x, mod, rel_bias, g_pre_mix[layer], w_in[layer], w_branch_a[layer], w_branch_b[layer], w_out[layer],
                   g_post_mix[layer], g_pre_ffn[layer], w_router[layer], router_bias[layer], w_exp_gate[layer],
                   w_exp_up[layer], w_exp_down[layer], w_sh_gate[layer], w_sh_up[layer], w_sh_down[layer],
                   g_post_ffn[layer])
    return x
```

```python
import functools
import math

import jax
import jax.numpy as jnp
import numpy as np
from jax import lax
from jax.experimental import pallas as pl
from jax.experimental.pallas import tpu as pltpu

D_MODEL = 1024
HEAD_DIM = 64
SB_HEADS = 8
SB_WIDTH = SB_HEADS * HEAD_DIM
DIL_CONFIGS = ((128, 1), (512, 4), (2048, 16))
DIL_HEADS_PER_GROUP = 4
DIL_GROUP_WIDTH = DIL_HEADS_PER_GROUP * HEAD_DIM
DIL_WIDTH = DIL_GROUP_WIDTH * len(DIL_CONFIGS)
Q_BLOCK = 128
N_BUCKETS = 32
MAX_DISTANCE = 2048
N_EXPERTS = 256
TOP_K = 8
N_GROUPS = 8
TOP_K_GROUPS = 4
EXPERT_DIM = 256
ROUTED_SCALE = 2.5
N_MOD = 6
EPS = 1e-6

LANES = 128
VMEM_LIMIT_BYTES = 56 * 1024 * 1024
NEG_INF = float("-inf")
LOG2E = math.log2(math.e)


def _cparams(*semantics):
    return pltpu.CompilerParams(dimension_semantics=semantics, vmem_limit_bytes=VMEM_LIMIT_BYTES)


def _rms_norm(v, g):
    return v * lax.rsqrt(jnp.mean(v * v, axis=-1, keepdims=True) + EPS) * g


def _sigmoid(v):
    return 1.0 / (1.0 + jnp.exp(-v))


def _dot(a, b):
    return jnp.dot(a, b, preferred_element_type=jnp.float32)


def _dot_nt(a, b):
    return lax.dot_general(a, b, (((1,), (1,)), ((), ())), preferred_element_type=jnp.float32)


def _split_bf16(v):
    hi = v.astype(jnp.bfloat16)
    lo = (v - hi.astype(jnp.float32)).astype(jnp.bfloat16)
    return hi, lo


ROW_WORDS = D_MODEL // 2
ROW_SUB = ROW_WORDS // LANES


def _pack_rows(ref, v):
    m = v.shape[0]
    bits = pltpu.bitcast(v.astype(jnp.bfloat16).astype(jnp.float32), jnp.uint32)
    words = bits[:, :ROW_WORDS] | (bits[:, ROW_WORDS:] >> 16)
    for j in range(ROW_SUB):
        ref[pl.ds(j, m, stride=ROW_SUB), :] = words[:, j * LANES:(j + 1) * LANES]


def _unpack_rows(ref, m):
    words = jnp.concatenate([ref[pl.ds(j, m, stride=ROW_SUB), :] for j in range(ROW_SUB)], axis=1)
    first = pltpu.bitcast(words & jnp.uint32(0xFFFF0000), jnp.float32)
    second = pltpu.bitcast(words << 16, jnp.float32)
    return first, second


def _ada_kernel(c_ref, w_ref, b_ref, o_ref):
    c = c_ref[...]
    a_hi, a_lo = _split_bf16(c * _sigmoid(c))
    w_hi, w_lo = _split_bf16(w_ref[...])
    o_ref[...] = _dot(a_hi, w_hi) + _dot(a_lo, w_hi) + _dot(a_hi, w_lo) + b_ref[...]


def ada_modulation(c, w_ada, b_ada, *, tn=1536):
    b, d = c.shape
    n = w_ada.shape[1]
    return pl.pallas_call(
        _ada_kernel,
        grid=(n // tn,),
        in_specs=[
            pl.BlockSpec((b, d), lambda j: (0, 0)),
            pl.BlockSpec((d, tn), lambda j: (0, j)),
            pl.BlockSpec((1, tn), lambda j: (0, j)),
        ],
        out_specs=pl.BlockSpec((b, tn), lambda j: (0, j)),
        out_shape=jax.ShapeDtypeStruct((b, n), jnp.float32),
        compiler_params=_cparams("parallel"),
        name="ada_modulation",
    )(c, w_ada, b_ada.reshape(1, n))


def _in_proj_kernel(x_ref, mod_ref, g_ref, w_rows_ref, w_cols_ref, qT_ref, k_ref, vT_ref, *rest, tm):
    dil_refs, ga_ref, gb_ref, stage_ref = rest[:9], rest[9], rest[10], rest[11]
    h = _rms_norm(x_ref[0], g_ref[...]) * (1.0 + mod_ref[0, 1:2, :]) + mod_ref[0, 0:1, :]
    hb = h.astype(jnp.bfloat16)
    off = 0
    k_ref[0] = _dot(hb, w_rows_ref[:, off:off + SB_WIDTH]).astype(k_ref.dtype)
    off += SB_WIDTH
    for n, r in enumerate(dil_refs):
        y = _dot(hb, w_rows_ref[:, off:off + DIL_GROUP_WIDTH])
        off += DIL_GROUP_WIDTH
        dilation = DIL_CONFIGS[n % len(DIL_CONFIGS)][1]
        if dilation == 1:
            r[0] = y.astype(r.dtype)
            continue
        for j in range(DIL_GROUP_WIDTH // LANES):
            stage_ref[j] = y[:, j * LANES:(j + 1) * LANES]
        for c in range(dilation):
            for j in range(DIL_GROUP_WIDTH // LANES):
                col = c * DIL_GROUP_WIDTH + j * LANES
                r[0, :, col:col + LANES] = stage_ref[j, pl.ds(c, tm // dilation, stride=dilation), :].astype(r.dtype)
    for r in (ga_ref, gb_ref):
        r[0] = _sigmoid(_dot(hb, w_rows_ref[:, off:off + D_MODEL])).astype(r.dtype)
        off += D_MODEL
    qT_ref[0] = _dot_nt(w_cols_ref[0:SB_WIDTH, :], hb).astype(qT_ref.dtype)
    vT = _dot_nt(w_cols_ref[SB_WIDTH:2 * SB_WIDTH, :], hb).astype(vT_ref.dtype)
    for p in range(SB_WIDTH // LANES):
        for jb in range(tm // LANES):
            vT_ref[0, p, jb] = vT[p * LANES:(p + 1) * LANES, jb * LANES:(jb + 1) * LANES]


def _dilated_shape(b, s, dilation, dtype):
    return jax.ShapeDtypeStruct((b, s // dilation, dilation * DIL_GROUP_WIDTH), dtype)


def _dilated_spec(tm, dilation):
    return pl.BlockSpec((1, tm // dilation, dilation * DIL_GROUP_WIDTH), lambda bi, i: (bi, i, 0))


def in_projection(x, mod, g_pre, w_rows, w_cols, *, tm=256):
    b, s, d = x.shape
    nt = s // tm
    bf = jnp.bfloat16
    tok = lambda width: pl.BlockSpec((1, tm, width), lambda bi, i: (bi, i, 0))
    out_shapes = [
        jax.ShapeDtypeStruct((b, SB_WIDTH, s), bf),
        jax.ShapeDtypeStruct((b, s, SB_WIDTH), bf),
        jax.ShapeDtypeStruct((b, SB_WIDTH // LANES, s // LANES, LANES, LANES), bf),
    ] + [_dilated_shape(b, s, r, bf) for _, r in DIL_CONFIGS] * 3 + [jax.ShapeDtypeStruct((b, s, d), bf)] * 2
    out_specs = [
        pl.BlockSpec((1, SB_WIDTH, tm), lambda bi, i: (bi, 0, i)),
        tok(SB_WIDTH),
        pl.BlockSpec((1, SB_WIDTH // LANES, tm // LANES, LANES, LANES), lambda bi, i: (bi, 0, i, 0, 0)),
    ] + [_dilated_spec(tm, r) for _, r in DIL_CONFIGS] * 3 + [tok(d)] * 2
    return pl.pallas_call(
        functools.partial(_in_proj_kernel, tm=tm),
        grid=(b, nt),
        in_specs=[
            tok(d),
            pl.BlockSpec((1, N_MOD, d), lambda bi, i: (bi, 0, 0)),
            pl.BlockSpec((1, d), lambda bi, i: (0, 0)),
            pl.BlockSpec(w_rows.shape, lambda bi, i: (0, 0)),
            pl.BlockSpec(w_cols.shape, lambda bi, i: (0, 0)),
        ],
        out_specs=out_specs,
        out_shape=out_shapes,
        scratch_shapes=[pltpu.VMEM((DIL_GROUP_WIDTH // LANES, tm, LANES), jnp.float32)],
        compiler_params=_cparams("parallel", "parallel"),
        name="in_projection",
    )(x, mod, g_pre.reshape(1, d), w_rows, w_cols)


SB_TILE = 512
SB_SUB = 256
SB_STEP_HEADS = 4


def _sb_chunk(kc, vc, q_h, tri, csum, past):
    z = _dot(kc, q_h)
    neg_abs = pltpu.bitcast(pltpu.bitcast(z, jnp.uint32) | jnp.uint32(0x80000000), jnp.float32)
    drop = jnp.maximum(z, 0.0) + jnp.log(1.0 + jnp.exp2(neg_abs)) * LOG2E
    if past is not None:
        drop = jnp.where(past, drop, 0.0)
    weights = [None] * (SB_TILE // SB_SUB)
    for sb in reversed(range(SB_TILE // SB_SUB)):
        rows = slice(sb * SB_SUB, (sb + 1) * SB_SUB)
        d = drop[rows]
        later = _dot(tri, d.astype(jnp.bfloat16))
        w = jnp.exp2(z[rows] - d - later - csum)
        if past is not None:
            w = jnp.where(past[rows], w, 0.0)
        weights[sb] = w.astype(jnp.bfloat16)
        csum = csum + jnp.sum(d, axis=0, keepdims=True)
    w = jnp.concatenate(weights, axis=0)
    t = SB_TILE
    part = jnp.concatenate([_dot(vc[h * HEAD_DIM:(h + 1) * HEAD_DIM], w[:, h * t:(h + 1) * t])
                            for h in range(SB_STEP_HEADS)], axis=0)
    return part, csum


def _sb_attn_kernel(qT_ref, k_ref, vT_ref, tri_ref, o_ref, acc_ref, csum_ref):
    i = pl.program_id(2)
    t = SB_TILE
    q_all = qT_ref[0]
    row = lax.broadcasted_iota(jnp.int32, q_all.shape, 0)
    zero = jnp.zeros_like(q_all)
    q_heads = jnp.concatenate(
        [jnp.where(jnp.logical_and(row >= h * HEAD_DIM, row < (h + 1) * HEAD_DIM), q_all, zero)
         for h in range(SB_STEP_HEADS)], axis=1)
    tri = tri_ref[...]
    blocks = t // LANES

    def chunk(c, past, first):
        kc = k_ref[0, pl.ds(pl.multiple_of(c * t, t), t), :]
        vc = jnp.concatenate(
            [jnp.concatenate([vT_ref[0, p, c * blocks + jb] for jb in range(blocks)], axis=1)
             for p in range(SB_STEP_HEADS // 2)], axis=0)
        csum = jnp.zeros((1, SB_STEP_HEADS * t), jnp.float32) if first else csum_ref[...]
        part, csum = _sb_chunk(kc, vc, q_heads, tri, csum, past)
        acc_ref[...] = part if first else acc_ref[...] + part
        csum_ref[...] = csum

    key = lax.broadcasted_iota(jnp.int32, (t, SB_STEP_HEADS * t), 0)
    query = lax.broadcasted_iota(jnp.int32, (t, SB_STEP_HEADS * t), 1)
    chunk(i, key < jnp.bitwise_and(query, t - 1), True)

    def body(it, carry):
        chunk(i - 1 - it, None, False)
        return carry

    lax.fori_loop(0, i, body, 0)
    o_ref[0] = acc_ref[...].T.astype(o_ref.dtype)


def sb_attention(qT, k, vT3):
    b, width, s = qT.shape
    t = SB_TILE
    hw = SB_STEP_HEADS * HEAD_DIM
    tri = jnp.asarray(np.triu(np.ones((SB_SUB, SB_SUB), np.float32), k=1), jnp.bfloat16)
    return pl.pallas_call(
        _sb_attn_kernel,
        grid=(b, width // hw, s // t),
        in_specs=[
            pl.BlockSpec((1, hw, t), lambda bi, p, i: (bi, p, i)),
            pl.BlockSpec((1, s, hw), lambda bi, p, i: (bi, 0, p)),
            pl.BlockSpec((1, hw // LANES, s // LANES, LANES, LANES), lambda bi, p, i: (bi, p, 0, 0, 0)),
            pl.BlockSpec((SB_SUB, SB_SUB), lambda bi, p, i: (0, 0)),
        ],
        out_specs=pl.BlockSpec((1, t, hw), lambda bi, p, i: (bi, i, p)),
        out_shape=jax.ShapeDtypeStruct((b, s, width), jnp.bfloat16),
        scratch_shapes=[pltpu.VMEM((hw, t), jnp.float32), pltpu.VMEM((1, SB_STEP_HEADS * t), jnp.float32)],
        compiler_params=_cparams("parallel", "parallel", "arbitrary"),
        name="sb_attention",
    )(qT, k, vT3, tri)


DIL_BLOCKS_PER_STEP = 4


def _dil_attn_kernel(q_ref, kp_ref, kc_ref, vp_ref, vc_ref, bias_ref, o_ref, lse_ref, *, n_blocks):
    m = pl.program_id(2)
    kk_all = jnp.concatenate([kp_ref[0], kc_ref[0]], axis=0)
    vv_all = jnp.concatenate([vp_ref[0], vc_ref[0]], axis=0)
    lane = lax.broadcasted_iota(jnp.int32, (Q_BLOCK, DIL_GROUP_WIDTH), 1)
    in_head = [jnp.logical_and(lane >= h * HEAD_DIM, lane < (h + 1) * HEAD_DIM) for h in range(DIL_HEADS_PER_GROUP)]
    bias = bias_ref[...]
    key = lax.broadcasted_iota(jnp.int32, bias.shape, 1)
    for j in range(n_blocks):
        q_rows = slice(j * Q_BLOCK, (j + 1) * Q_BLOCK)
        q4 = q_ref[0, q_rows, :]
        kk = kk_all[j * Q_BLOCK:(j + 2) * Q_BLOCK]
        vv = vv_all[j * Q_BLOCK:(j + 2) * Q_BLOCK]
        zero = jnp.zeros_like(q4)
        q_stack = jnp.concatenate([jnp.where(mask, q4, zero) for mask in in_head], axis=0)
        logits = _dot_nt(q_stack, kk) + bias
        if j == 0:
            logits = jnp.where(jnp.logical_or(key >= Q_BLOCK, m > 0), logits, NEG_INF)
        mx = jnp.max(logits, axis=1, keepdims=True)
        p = jnp.exp(logits - mx)
        den = jnp.sum(p, axis=1, keepdims=True)
        pv = _dot(p.astype(jnp.bfloat16), vv) / den
        lse_rows = mx + jnp.log(den)
        out = jnp.zeros(q4.shape, jnp.float32)
        lse = jnp.zeros(q4.shape, jnp.float32)
        for h, mask in enumerate(in_head):
            rows = slice(h * Q_BLOCK, (h + 1) * Q_BLOCK)
            out = jnp.where(mask, pv[rows], out)
            lse = jnp.where(mask, lse_rows[rows], lse)
        o_ref[0, q_rows, :] = out.astype(o_ref.dtype)
        lse_ref[0, q_rows, :] = lse


def dilated_attention(q, k, v, bias, dilation):
    b, length, width = q.shape
    gw = DIL_GROUP_WIDTH
    n_blocks = min(DIL_BLOCKS_PER_STEP, length // Q_BLOCK)
    cur = pl.BlockSpec((1, n_blocks * Q_BLOCK, gw), lambda bi, c, m: (bi, m, c))
    prev = pl.BlockSpec((1, Q_BLOCK, gw), lambda bi, c, m: (bi, jnp.maximum(m * n_blocks - 1, 0), c))
    return pl.pallas_call(
        functools.partial(_dil_attn_kernel, n_blocks=n_blocks),
        grid=(b, width // gw, length // (n_blocks * Q_BLOCK)),
        in_specs=[cur, prev, cur, prev, cur,
                  pl.BlockSpec((DIL_HEADS_PER_GROUP * Q_BLOCK, 2 * Q_BLOCK), lambda bi, c, m: (0, 0))],
        out_specs=[cur, cur],
        out_shape=[jax.ShapeDtypeStruct((b, length, width), jnp.bfloat16),
                   jax.ShapeDtypeStruct((b, length, width), jnp.float32)],
        compiler_params=_cparams("parallel", "parallel", "arbitrary"),
        name=f"dilated_attention_r{dilation}",
    )(q, k, k, v, v, bias.reshape(DIL_HEADS_PER_GROUP * Q_BLOCK, 2 * Q_BLOCK))


def _t5_causal_bucket(dist):
    max_exact = N_BUCKETS // 2
    d = jnp.maximum(dist, 1).astype(jnp.float32)
    large = max_exact + (jnp.log(d / max_exact) / math.log(MAX_DISTANCE / max_exact)
                         * (N_BUCKETS - max_exact)).astype(jnp.int32)
    large = jnp.minimum(large, N_BUCKETS - 1)
    return jnp.where(dist < max_exact, dist, large)


def dilated_bias(rel_bias_group, window, dilation):
    qi = jnp.arange(Q_BLOCK)[:, None]
    kj = jnp.arange(2 * Q_BLOCK)[None, :]
    dist_sub = qi + Q_BLOCK - kj
    bucket = _t5_causal_bucket(jnp.maximum(dist_sub, 0) * dilation)
    onehot = (bucket[..., None] == jnp.arange(N_BUCKETS)).astype(jnp.float32)
    bias = jnp.einsum("qkn,nh->hqk", onehot, rel_bias_group.astype(jnp.float32), precision=lax.Precision.HIGHEST)
    valid = (dist_sub >= 0) & (dist_sub <= window // dilation)
    return jnp.where(valid[None], bias, NEG_INF)


def _mix_out_kernel(oa_ref, o0_ref, o1_ref, o2_ref, l0_ref, l1_ref, l2_ref, ga_ref, gb_ref, x_ref, mod_ref,
                    wa_ref, wb_ref, wo_ref, gpost_ref, gpre_ref, wr_ref, wsgu_ref, wsd_ref,
                    x1_ref, hp_ref, sh_ref, lt_ref, stage_ref):
    f32 = jnp.float32
    tm = x_ref.shape[1]

    def token_major(ref, dilation):
        if dilation == 1:
            return ref[0].astype(f32)
        slabs = DIL_GROUP_WIDTH // LANES
        for c in range(dilation):
            for j in range(slabs):
                col = c * DIL_GROUP_WIDTH + j * LANES
                stage_ref[j, pl.ds(c, tm // dilation, stride=dilation), :] = ref[0, :, col:col + LANES].astype(f32)
        return jnp.concatenate([stage_ref[j] for j in range(slabs)], axis=1)

    dils = [r for _, r in DIL_CONFIGS]
    l0, l1, l2 = (token_major(ref, r) for ref, r in zip((l0_ref, l1_ref, l2_ref), dils))
    mx = jnp.maximum(jnp.maximum(l0, l1), l2)
    e0, e1, e2 = jnp.exp(l0 - mx), jnp.exp(l1 - mx), jnp.exp(l2 - mx)
    den = e0 + e1 + e2
    o0, o1, o2 = (token_major(ref, r) for ref, r in zip((o0_ref, o1_ref, o2_ref), dils))
    out_b = (e0 / den) * o0 + (e1 / den) * o1 + (e2 / den) * o2
    ya = _dot(oa_ref[0], wa_ref[...])
    yb = _dot(out_b.astype(jnp.bfloat16), wb_ref[...])
    merged = ga_ref[0].astype(f32) * ya + gb_ref[0].astype(f32) * yb
    y = _dot(merged.astype(jnp.bfloat16), wo_ref[...])
    x1 = x_ref[0] + mod_ref[0, 2:3, :] * _rms_norm(y, gpost_ref[...])
    x1_ref[0] = x1
    h = _rms_norm(x1, gpre_ref[...]) * (1.0 + mod_ref[0, 4:5, :]) + mod_ref[0, 3:4, :]
    hb = h.astype(jnp.bfloat16)
    _pack_rows(hp_ref, h)
    lt_ref[...] = _dot_nt(wr_ref[...], hb)
    gu = _dot(hb, wsgu_ref[...])
    g, u = gu[:, :EXPERT_DIM], gu[:, EXPERT_DIM:]
    sh_ref[0] = _dot((g * _sigmoid(g) * u).astype(jnp.bfloat16), wsd_ref[...])


def mix_out(out_a, o_g, lse_g, gate_a, gate_b, x, mod, w_a, w_b, w_o, g_post, g_pre_ffn, w_rT, w_sgu, w_sd, *, tm=512):
    b, s, d = x.shape
    tok = lambda width: pl.BlockSpec((1, tm, width), lambda bi, i: (bi, i, 0))
    full = lambda a: pl.BlockSpec(a.shape, lambda bi, i: (0,) * a.ndim)
    nt = s // tm
    g_post, g_pre_ffn = g_post.reshape(1, d), g_pre_ffn.reshape(1, d)
    weights = (w_a, w_b, w_o, g_post, g_pre_ffn, w_rT, w_sgu, w_sd)
    return pl.pallas_call(
        _mix_out_kernel,
        grid=(b, nt),
        in_specs=[tok(SB_WIDTH)] + [_dilated_spec(tm, r) for _, r in DIL_CONFIGS] * 2 + [tok(d)] * 3
        + [pl.BlockSpec((1, N_MOD, d), lambda bi, i: (bi, 0, 0))] + [full(w) for w in weights],
        out_specs=[tok(d), pl.BlockSpec((tm * ROW_SUB, LANES), lambda bi, i: (bi * nt + i, 0)), tok(d),
                   pl.BlockSpec((N_EXPERTS, tm), lambda bi, i: (0, bi * nt + i))],
        out_shape=[jax.ShapeDtypeStruct((b, s, d), jnp.float32),
                   jax.ShapeDtypeStruct((b * s * ROW_SUB, LANES), jnp.uint32),
                   jax.ShapeDtypeStruct((b, s, d), jnp.float32),
                   jax.ShapeDtypeStruct((N_EXPERTS, b * s), jnp.float32)],
        scratch_shapes=[pltpu.VMEM((DIL_GROUP_WIDTH // LANES, tm, LANES), jnp.float32)],
        compiler_params=_cparams("parallel", "parallel"),
        name="mix_out",
    )(out_a, *o_g, *lse_g, gate_a, gate_b, x, mod, *weights)


def _route_kernel(lt_ref, bias_ref, tri_ref, idx_ref, gate_ref, rank_ref, cnt_ref, base_ref, *, tn):
    f32 = jnp.float32

    @pl.when(pl.program_id(0) == 0)
    def _():
        base_ref[...] = jnp.zeros_like(base_ref)

    per_group = N_EXPERTS // N_GROUPS
    scores = _sigmoid(lt_ref[...])
    sel = scores + bias_ref[...]
    sel_groups = [sel[g * per_group:(g + 1) * per_group] for g in range(N_GROUPS)]
    ri = lax.broadcasted_iota(jnp.int32, (per_group, tn), 0)
    group_scores = []
    for sg in sel_groups:
        m1 = jnp.max(sg, axis=0, keepdims=True)
        first = jnp.min(jnp.where(sg == m1, ri, per_group), axis=0, keepdims=True)
        group_scores.append(m1 + jnp.max(jnp.where(ri == first, NEG_INF, sg), axis=0, keepdims=True))
    cur = jnp.concatenate(group_scores, axis=0)
    gi = lax.broadcasted_iota(jnp.int32, cur.shape, 0)
    keep = jnp.zeros(cur.shape, jnp.float32)
    for _ in range(TOP_K_GROUPS):
        mg = jnp.max(cur, axis=0, keepdims=True)
        pick = gi == jnp.min(jnp.where(cur == mg, gi, N_GROUPS), axis=0, keepdims=True)
        keep = jnp.where(pick, 1.0, keep)
        cur = jnp.where(pick, NEG_INF, cur)
    cur = jnp.concatenate([jnp.where(keep[g:g + 1, :] > 0.0, sg, NEG_INF) for g, sg in enumerate(sel_groups)], axis=0)
    ei = lax.broadcasted_iota(jnp.int32, cur.shape, 0)
    idxs, gates = [], []
    candidates = cur
    for _ in range(TOP_K):
        mk = jnp.max(cur, axis=0, keepdims=True)
        fk = jnp.min(jnp.where(cur == mk, ei, N_EXPERTS), axis=0, keepdims=True)
        pick = ei == fk
        idxs.append(fk)
        gates.append(jnp.sum(jnp.where(pick, scores, 0.0), axis=0, keepdims=True))
        cur = jnp.where(pick, NEG_INF, cur)
    gate = jnp.concatenate(gates, axis=0)
    gate_ref[...] = gate / jnp.sum(gate, axis=0, keepdims=True) * ROUTED_SCALE
    idx_ref[...] = jnp.concatenate(idxs, axis=0)
    onehot = jnp.where(cur == NEG_INF, 1.0, 0.0) - jnp.where(candidates == NEG_INF, 1.0, 0.0)
    earlier = _dot(onehot.astype(jnp.bfloat16), tri_ref[...]) + base_ref[...]
    ranks = [jnp.sum(jnp.where(ei == fk, earlier, 0.0), axis=0, keepdims=True) for fk in idxs]
    rank_ref[...] = jnp.concatenate(ranks, axis=0).astype(jnp.int32)
    base = base_ref[...] + jnp.sum(onehot, axis=1, keepdims=True)
    base_ref[...] = base
    cnt_ref[...] = jnp.broadcast_to(base, cnt_ref.shape).astype(jnp.int32)


def route(logits_t, router_bias, *, tn=512):
    ne, n = logits_t.shape
    tri = jnp.asarray(np.triu(np.ones((tn, tn), np.float32), k=1), jnp.bfloat16)
    slot = pl.BlockSpec((TOP_K, tn), lambda i: (0, i))
    return pl.pallas_call(
        functools.partial(_route_kernel, tn=tn),
        grid=(n // tn,),
        in_specs=[pl.BlockSpec((ne, tn), lambda i: (0, i)),
                  pl.BlockSpec((ne, 1), lambda i: (0, 0)),
                  pl.BlockSpec((tn, tn), lambda i: (0, 0))],
        out_specs=[slot, slot, slot, pl.BlockSpec((ne, LANES), lambda i: (0, 0))],
        out_shape=[jax.ShapeDtypeStruct((TOP_K, n), jnp.int32),
                   jax.ShapeDtypeStruct((TOP_K, n), jnp.float32),
                   jax.ShapeDtypeStruct((TOP_K, n), jnp.int32),
                   jax.ShapeDtypeStruct((ne, LANES), jnp.int32)],
        scratch_shapes=[pltpu.VMEM((ne, 1), jnp.float32)],
        compiler_params=_cparams("arbitrary"),
        name="route",
    )(logits_t, router_bias.reshape(ne, 1), tri)


def _dest_kernel(idx_ref, rank_ref, start_ref, dest_ref):
    idx = idx_ref[...]
    ei = lax.broadcasted_iota(jnp.int32, (N_EXPERTS, idx.shape[1]), 0)
    start = start_ref[...]
    rows = [jnp.sum(jnp.where(ei == idx[k:k + 1, :], start, 0.0), axis=0, keepdims=True) for k in range(TOP_K)]
    dest_ref[...] = jnp.concatenate(rows, axis=0).astype(jnp.int32) + rank_ref[...]


def slot_destinations(idx, rank, starts, *, tn=512):
    kk, n = idx.shape
    slot = pl.BlockSpec((kk, tn), lambda i: (0, i))
    return pl.pallas_call(
        _dest_kernel,
        grid=(n // tn,),
        in_specs=[slot, slot, pl.BlockSpec((N_EXPERTS, 1), lambda i: (0, 0))],
        out_specs=slot,
        out_shape=jax.ShapeDtypeStruct((kk, n), jnp.int32),
        compiler_params=_cparams("parallel"),
        name="slot_destinations",
    )(idx, rank, starts.astype(jnp.float32).reshape(N_EXPERTS, 1))


def _packed_row(ref, row):
    start = row * ROW_SUB
    return ref.at[pl.ds(start if isinstance(row, int) else pl.multiple_of(start, ROW_SUB), ROW_SUB), :]


def _dispatch_kernel(dest_ref, h_ref, xs_ref, zero_ref, sem, *, tm, n_rows):
    pad = zero_ref.shape[0]

    @pl.when(pl.program_id(0) == 0)
    def _():
        zero_ref[...] = jnp.zeros_like(zero_ref)
        tail = pltpu.make_async_copy(zero_ref, xs_ref.at[pl.ds(n_rows * ROW_SUB, pad), :], sem)
        tail.start()
        tail.wait()

    def body(t, _):
        for k in range(TOP_K):
            pltpu.make_async_copy(_packed_row(h_ref, t), _packed_row(xs_ref, dest_ref[k, t]), sem).start(priority=k % 2)
        return 0

    lax.fori_loop(0, tm, body, 0)
    for _ in range(TOP_K):
        pltpu.make_async_copy(h_ref, xs_ref.at[pl.ds(0, tm * ROW_SUB), :], sem).wait()


def dispatch(dest, hp, *, pad_rows, tm=512):
    kk, n = dest.shape
    n_rows = n * kk
    return pl.pallas_call(
        functools.partial(_dispatch_kernel, tm=tm, n_rows=n_rows),
        grid=(n // tm,),
        in_specs=[pl.BlockSpec((kk, tm), lambda i: (0, i), memory_space=pltpu.SMEM),
                  pl.BlockSpec((tm * ROW_SUB, LANES), lambda i: (i, 0))],
        out_specs=pl.BlockSpec(memory_space=pl.ANY),
        out_shape=jax.ShapeDtypeStruct(((n_rows + pad_rows) * ROW_SUB, LANES), hp.dtype),
        scratch_shapes=[pltpu.VMEM((pad_rows * ROW_SUB, LANES), hp.dtype), pltpu.SemaphoreType.DMA],
        compiler_params=_cparams("arbitrary"),
        name="dispatch",
    )(dest, hp)


EXPERT_CHUNK = 256
EXPERT_LOOKAHEAD = 3
EXPERT_IN_BUFS = EXPERT_LOOKAHEAD + 1
EXPERT_OUT_BUFS = 3


def _expert_kernel(first_ref, nch_ref, crow_ref, total_ref, xs_ref, wg_ref, wu_ref, wd_ref, ys_ref,
                   xbuf, ybuf, wgu_bf, wd_bf, h_ref, sem_in, sem_out, waited_ref):
    e = pl.program_id(0)
    chunk = EXPERT_CHUNK
    first, n_chunks, total = first_ref[e], nch_ref[e], total_ref[0]

    def rows(g):
        return pl.ds(pl.multiple_of(crow_ref[g] * ROW_SUB, ROW_SUB), chunk * ROW_SUB)

    def in_copy(g):
        slot = lax.rem(g, EXPERT_IN_BUFS)
        return pltpu.make_async_copy(xs_ref.at[rows(g), :], xbuf.at[slot], sem_in.at[slot])

    def out_copy(g):
        slot = lax.rem(g, EXPERT_OUT_BUFS)
        return pltpu.make_async_copy(ybuf.at[slot], ys_ref.at[rows(g), :], sem_out.at[slot])

    def wait_writes_through(g):
        def wait_one(j, carry):
            out_copy(j).wait()
            return carry

        lax.fori_loop(waited_ref[0] + 1, g + 1, wait_one, 0)
        waited_ref[0] = jnp.maximum(waited_ref[0], g)

    @pl.when(e == 0)
    def _():
        waited_ref[0] = -1
        for g in range(EXPERT_LOOKAHEAD):
            @pl.when(g < total)
            def _():
                in_copy(g).start()

    @pl.when(n_chunks > 0)
    def _():
        wgu_bf[:, :EXPERT_DIM] = wg_ref[0].astype(wgu_bf.dtype)
        wgu_bf[:, EXPERT_DIM:] = wu_ref[0].astype(wgu_bf.dtype)
        wd_bf[...] = wd_ref[0].astype(wd_bf.dtype)

    def acquire(g):
        in_copy(g).wait()

        @pl.when(g + EXPERT_LOOKAHEAD < total)
        def _():
            in_copy(g + EXPERT_LOOKAHEAD).start()

    def hidden(g):
        bf = jnp.bfloat16
        xa, xb = _unpack_rows(xbuf.at[lax.rem(g, EXPERT_IN_BUFS)], chunk)
        gu = _dot(xa.astype(bf), wgu_bf[:ROW_WORDS, :]) + _dot(xb.astype(bf), wgu_bf[ROW_WORDS:, :])
        gate, up = gu[:, :EXPERT_DIM], gu[:, EXPERT_DIM:]
        return (gate * _sigmoid(gate) * up).astype(bf)

    def pack(g, h):
        _pack_rows(ybuf.at[lax.rem(g, EXPERT_OUT_BUFS)], _dot(h, wd_bf[...]))

    def release(g):
        @pl.when(g == first)
        def _():
            wait_writes_through(g - 1)

        out_copy(g).start()

    @pl.when(n_chunks > 0)
    def _():
        acquire(first)
        h_ref[...] = hidden(first)

    def body(c, carry):
        g = first + c
        acquire(g)
        wait_writes_through(g - 1 - EXPERT_OUT_BUFS)
        h_prev = h_ref[...]
        pack(g - 1, h_prev)
        h_ref[...] = hidden(g)
        release(g - 1)
        return carry

    lax.fori_loop(1, n_chunks, body, 0)

    @pl.when(n_chunks > 0)
    def _():
        last = first + n_chunks - 1
        wait_writes_through(last - EXPERT_OUT_BUFS)
        pack(last, h_ref[...])
        release(last)

    @pl.when(e == pl.num_programs(0) - 1)
    def _():
        wait_writes_through(total - 1)
        ybuf[0] = jnp.zeros(ybuf.shape[1:], ybuf.dtype)
        pad = pl.ds(ys_ref.shape[0] - chunk * ROW_SUB, chunk * ROW_SUB)
        tail = pltpu.make_async_copy(ybuf.at[0], ys_ref.at[pad, :], sem_out.at[0])
        tail.start()
        tail.wait()


def expert_chunks(starts, counts, n_rows):
    chunk = EXPERT_CHUNK
    max_chunks = n_rows // chunk + N_EXPERTS
    nch = (counts + (chunk - 1)) // chunk
    ends = jnp.cumsum(nch)
    first = ends - nch
    g = jnp.arange(max_chunks, dtype=jnp.int32)
    owner = jnp.minimum(jnp.sum(g[:, None] >= ends[None, :], axis=1), N_EXPERTS - 1)
    onehot = (owner[:, None] == jnp.arange(N_EXPERTS)[None, :]).astype(jnp.int32)
    crow = (onehot * (starts - first * chunk)[None, :]).sum(axis=1) + g * chunk
    crow = jnp.where(g < ends[-1], crow, 0).astype(jnp.int32)
    return first.astype(jnp.int32), nch.astype(jnp.int32), crow, ends[-1:].astype(jnp.int32)


def expert_ffn(starts, counts, xs, w_gate, w_up, w_down):
    d = D_MODEL
    n_rows = xs.shape[0] // ROW_SUB - EXPERT_CHUNK
    tables = expert_chunks(starts, counts, n_rows)
    chunk_shape = (EXPERT_CHUNK * ROW_SUB, LANES)
    weights = lambda shape: pl.BlockSpec(shape, lambda e, *_: (e, 0, 0))
    grid_spec = pltpu.PrefetchScalarGridSpec(
        num_scalar_prefetch=len(tables),
        grid=(N_EXPERTS,),
        in_specs=[pl.BlockSpec(memory_space=pl.ANY),
                  weights((1, d, EXPERT_DIM)), weights((1, d, EXPERT_DIM)), weights((1, EXPERT_DIM, d))],
        out_specs=pl.BlockSpec(memory_space=pl.ANY),
        scratch_shapes=[pltpu.VMEM((EXPERT_IN_BUFS,) + chunk_shape, xs.dtype),
                        pltpu.VMEM((EXPERT_OUT_BUFS,) + chunk_shape, xs.dtype),
                        pltpu.VMEM((d, 2 * EXPERT_DIM), jnp.bfloat16), pltpu.VMEM((EXPERT_DIM, d), jnp.bfloat16),
                        pltpu.VMEM((EXPERT_CHUNK, EXPERT_DIM), jnp.bfloat16),
                        pltpu.SemaphoreType.DMA((EXPERT_IN_BUFS,)), pltpu.SemaphoreType.DMA((EXPERT_OUT_BUFS,)),
                        pltpu.SMEM((1,), jnp.int32)],
    )
    return pl.pallas_call(
        _expert_kernel,
        grid_spec=grid_spec,
        out_shape=jax.ShapeDtypeStruct(xs.shape, xs.dtype),
        compiler_params=_cparams("arbitrary"),
        name="expert_ffn",
    )(*tables, xs, w_gate, w_up, w_down)


COMBINE_AHEAD = 2
COMBINE_SLOTS = COMBINE_AHEAD + 1
COMBINE_GROUP = 32


def _combine_kernel(*refs, tm):
    head_refs, (dest_ref, ys_ref, gate_ref, sh_ref, x1_ref, mod_ref, g_ref, o_ref, buf, sem) = (
        refs[:COMBINE_AHEAD], refs[COMBINE_AHEAD:])
    i, n = pl.program_id(0), pl.num_programs(0)

    def gather(src_ref, slot, t):
        for k in range(TOP_K):
            pltpu.make_async_copy(_packed_row(ys_ref, src_ref[t * TOP_K + k]), _packed_row(buf.at[slot, k], t),
                                  sem.at[slot]).start(priority=k % 2)

    def wait_tile(slot):
        for k in range(TOP_K):
            pltpu.make_async_copy(ys_ref.at[pl.ds(0, tm * ROW_SUB), :], buf.at[slot, k], sem.at[slot]).wait()

    @pl.when(i == 0)
    def _():
        for a, head in enumerate(head_refs):
            def body(t, carry, head=head, a=a):
                gather(head, a, t)
                return carry

            lax.fori_loop(0, tm, body, 0)

    def step(cur):
        ahead = (cur + COMBINE_AHEAD) % COMBINE_SLOTS
        wait_tile(cur)
        g_post = g_ref[...]
        gt2 = mod_ref[0, 5:6, :]
        for t0 in range(0, tm, COMBINE_GROUP):
            for t in range(t0, t0 + COMBINE_GROUP):
                gather(dest_ref, ahead, t)
            rows = slice(t0, t0 + COMBINE_GROUP)
            gate = gate_ref[rows, :]
            first = jnp.zeros((COMBINE_GROUP, ROW_WORDS), jnp.float32)
            second = jnp.zeros((COMBINE_GROUP, ROW_WORDS), jnp.float32)
            for k in range(TOP_K):
                packed = buf.at[cur, k, pl.ds(t0 * ROW_SUB, COMBINE_GROUP * ROW_SUB), :]
                ya, yb = _unpack_rows(packed, COMBINE_GROUP)
                first = first + ya * gate[:, k:k + 1]
                second = second + yb * gate[:, k:k + 1]
            y = sh_ref[rows, :] + jnp.concatenate([first, second], axis=1)
            o_ref[rows, :] = x1_ref[rows, :] + gt2 * _rms_norm(y, g_post)

        @pl.when(i == n - 1)
        def _():
            for a in range(1, COMBINE_SLOTS):
                wait_tile((cur + a) % COMBINE_SLOTS)

    for slot in range(COMBINE_SLOTS):
        pl.when(lax.rem(i, COMBINE_SLOTS) == slot)(functools.partial(step, slot))


def combine(dest, ys, gate_nk, shared, x1, mod, g_post_ffn, *, seq, tm=128):
    n, d = x1.shape
    per_batch = seq // tm
    n_tiles = n // tm
    tok = pl.BlockSpec((tm, d), lambda i: (i, 0))
    dest_tile = lambda index: pl.BlockSpec((tm * TOP_K,), index, memory_space=pltpu.SMEM)
    head_specs = [dest_tile(lambda i, a=a: (a,)) for a in range(COMBINE_AHEAD)]
    return pl.pallas_call(
        functools.partial(_combine_kernel, tm=tm),
        grid=(n_tiles,),
        in_specs=head_specs + [dest_tile(lambda i: (jnp.minimum(i + COMBINE_AHEAD, n_tiles - 1),)),
                               pl.BlockSpec(memory_space=pl.ANY),
                               pl.BlockSpec((tm, TOP_K), lambda i: (i, 0)),
                               tok, tok,
                               pl.BlockSpec((1, N_MOD, d), lambda i: (i // per_batch, 0, 0)),
                               pl.BlockSpec((1, d), lambda i: (0, 0))],
        out_specs=tok,
        out_shape=jax.ShapeDtypeStruct((n, d), jnp.float32),
        scratch_shapes=[pltpu.VMEM((COMBINE_SLOTS, TOP_K, tm * ROW_SUB, LANES), ys.dtype),
                        pltpu.SemaphoreType.DMA((COMBINE_SLOTS,))],
        compiler_params=_cparams("arbitrary"),
        name="combine",
    )(*([dest] * COMBINE_AHEAD), dest, ys, gate_nk, shared, x1, mod, g_post_ffn.reshape(1, d))


def _layer(x, mod, rel_bias, g_pre_mix, w_in, w_branch_a, w_branch_b, w_out, g_post_mix, g_pre_ffn, w_router,
           router_bias, w_exp_gate, w_exp_up, w_exp_down, w_sh_gate, w_sh_up, w_sh_down, g_post_ffn):
    b, s, d = x.shape
    bf = jnp.bfloat16
    scale = HEAD_DIM ** -0.5
    sb_q, sb_k, sb_v, dq, dk, dv, w_ga, w_gb = jnp.split(
        w_in, np.cumsum((SB_WIDTH,) * 3 + (DIL_WIDTH,) * 3 + (d,)).tolist(), axis=1)
    groups = lambda w: [w[:, g * DIL_GROUP_WIDTH:(g + 1) * DIL_GROUP_WIDTH] for g in range(len(DIL_CONFIGS))]
    w_rows = jnp.concatenate([sb_k] + [w * scale for w in groups(dq)] + groups(dk) + groups(dv) + [w_ga, w_gb],
                             axis=1).astype(bf)
    w_cols = jnp.concatenate([sb_q * (scale * LOG2E), sb_v], axis=1).T.astype(bf)
    qT, k, vT3, *rest = in_projection(x, mod, g_pre_mix, w_rows, w_cols)
    dqs, dks, dvs, gate_a, gate_b = rest[0:3], rest[3:6], rest[6:9], rest[9], rest[10]
    out_a = sb_attention(qT, k, vT3)
    o_g, lse_g = [], []
    for g, (window, dilation) in enumerate(DIL_CONFIGS):
        bias = dilated_bias(rel_bias[:, g * DIL_HEADS_PER_GROUP:(g + 1) * DIL_HEADS_PER_GROUP], window, dilation)
        o, lse = dilated_attention(dqs[g], dks[g], dvs[g], bias, dilation)
        o_g.append(o)
        lse_g.append(lse)
    x1, hp, shared, logits_t = mix_out(
        out_a, o_g, lse_g, gate_a, gate_b, x, mod, w_branch_a.astype(bf), w_branch_b.astype(bf), w_out.astype(bf),
        g_post_mix, g_pre_ffn, w_router.T.astype(bf), jnp.concatenate([w_sh_gate, w_sh_up], axis=1).astype(bf),
        w_sh_down.astype(bf))
    n = b * s
    idx, gate, rank, counts = route(logits_t, router_bias)
    counts = counts[:, 0]
    starts = jnp.cumsum(counts) - counts
    dest = slot_destinations(idx, rank, starts)
    xs = dispatch(dest, hp, pad_rows=EXPERT_CHUNK)
    ys = expert_ffn(starts, counts, xs, w_exp_gate, w_exp_up, w_exp_down)
    out = combine(dest.T.reshape(-1), ys, gate.T, shared.reshape(n, d), x1.reshape(n, d), mod, g_post_ffn, seq=s)
    return out.reshape(b, s, d)


def kernel(x, c, rel_bias, w_ada, b_ada, g_pre_mix, w_in, w_branch_a, w_branch_b, w_out, g_post_mix, g_pre_ffn,
           w_router, router_bias, w_exp_gate, w_exp_up, w_exp_down, w_sh_gate, w_sh_up, w_sh_down, g_post_ffn):
    depth = w_in.shape[0]
    for layer in range(depth):
        mod = ada_modulation(c, w_ada[layer], b_ada[layer]).reshape(x.shape[0], N_MOD, x.shape[2])
        x = _layer(x, mod, rel_bias, g_pre_mix[layer], w_in[layer], w_branch_a[layer], w_branch_b[layer], w_out[layer],
                   g_post_mix[layer], g_pre_ffn[layer], w_router[layer], router_bias[layer], w_exp_gate[layer],
                   w_exp_up[layer], w_exp_down[layer], w_sh_gate[layer], w_sh_up[layer], w_sh_down[layer],
                   g_post_ffn[layer])
    return x
```

```python
import functools
import math

import jax
import jax.numpy as jnp
import numpy as np
from jax import lax
from jax.experimental import pallas as pl
from jax.experimental.pallas import tpu as pltpu

D_MODEL = 1024
HEAD_DIM = 64
SB_HEADS = 8
SB_WIDTH = SB_HEADS * HEAD_DIM
DIL_CONFIGS = ((128, 1), (512, 4), (2048, 16))
DIL_HEADS_PER_GROUP = 4
DIL_GROUP_WIDTH = DIL_HEADS_PER_GROUP * HEAD_DIM
DIL_WIDTH = DIL_GROUP_WIDTH * len(DIL_CONFIGS)
Q_BLOCK = 128
N_BUCKETS = 32
MAX_DISTANCE = 2048
N_EXPERTS = 256
TOP_K = 8
N_GROUPS = 8
TOP_K_GROUPS = 4
EXPERT_DIM = 256
ROUTED_SCALE = 2.5
N_MOD = 6
EPS = 1e-6

LANES = 128
VMEM_LIMIT_BYTES = 56 * 1024 * 1024
NEG_INF = float("-inf")
LOG2E = math.log2(math.e)


def _cparams(*semantics):
    return pltpu.CompilerParams(dimension_semantics=semantics, vmem_limit_bytes=VMEM_LIMIT_BYTES)


def _rms_norm(v, g):
    return v * lax.rsqrt(jnp.mean(v * v, axis=-1, keepdims=True) + EPS) * g


def _sigmoid(v):
    return 1.0 / (1.0 + jnp.exp(-v))


def _dot(a, b):
    return jnp.dot(a, b, preferred_element_type=jnp.float32)


def _dot_nt(a, b):
    return lax.dot_general(a, b, (((1,), (1,)), ((), ())), preferred_element_type=jnp.float32)


def _split_bf16(v):
    hi = v.astype(jnp.bfloat16)
    lo = (v - hi.astype(jnp.float32)).astype(jnp.bfloat16)
    return hi, lo


ROW_WORDS = D_MODEL // 2
ROW_SUB = ROW_WORDS // LANES


def _pack_rows(ref, v):
    m = v.shape[0]
    bits = pltpu.bitcast(v.astype(jnp.bfloat16).astype(jnp.float32), jnp.uint32)
    words = bits[:, :ROW_WORDS] | (bits[:, ROW_WORDS:] >> 16)
    for j in range(ROW_SUB):
        ref[pl.ds(j, m, stride=ROW_SUB), :] = words[:, j * LANES:(j + 1) * LANES]


def _unpack_rows(ref, m):
    words = jnp.concatenate([ref[pl.ds(j, m, stride=ROW_SUB), :] for j in range(ROW_SUB)], axis=1)
    first = pltpu.bitcast(words & jnp.uint32(0xFFFF0000), jnp.float32)
    second = pltpu.bitcast(words << 16, jnp.float32)
    return first, second


def _ada_kernel(c_ref, w_ref, b_ref, o_ref):
    c = c_ref[...]
    a_hi, a_lo = _split_bf16(c * _sigmoid(c))
    w_hi, w_lo = _split_bf16(w_ref[...])
    o_ref[...] = _dot(a_hi, w_hi) + _dot(a_lo, w_hi) + _dot(a_hi, w_lo) + b_ref[...]


def ada_modulation(c, w_ada, b_ada, *, tn=1536):
    b, d = c.shape
    n = w_ada.shape[1]
    return pl.pallas_call(
        _ada_kernel,
        grid=(n // tn,),
        in_specs=[
            pl.BlockSpec((b, d), lambda j: (0, 0)),
            pl.BlockSpec((d, tn), lambda j: (0, j)),
            pl.BlockSpec((1, tn), lambda j: (0, j)),
        ],
        out_specs=pl.BlockSpec((b, tn), lambda j: (0, j)),
        out_shape=jax.ShapeDtypeStruct((b, n), jnp.float32),
        compiler_params=_cparams("parallel"),
        name="ada_modulation",
    )(c, w_ada, b_ada.reshape(1, n))


def _in_proj_kernel(x_ref, mod_ref, g_ref, w_rows_ref, w_cols_ref, qT_ref, k_ref, vT_ref, *rest, tm):
    dil_refs, ga_ref, gb_ref, stage_ref = rest[:9], rest[9], rest[10], rest[11]
    h = _rms_norm(x_ref[0], g_ref[...]) * (1.0 + mod_ref[0, 1:2, :]) + mod_ref[0, 0:1, :]
    hb = h.astype(jnp.bfloat16)
    off = 0
    k_ref[0] = _dot(hb, w_rows_ref[:, off:off + SB_WIDTH]).astype(k_ref.dtype)
    off += SB_WIDTH
    for n, r in enumerate(dil_refs):
        y = _dot(hb, w_rows_ref[:, off:off + DIL_GROUP_WIDTH])
        off += DIL_GROUP_WIDTH
        dilation = DIL_CONFIGS[n % len(DIL_CONFIGS)][1]
        if dilation == 1:
            r[0] = y.astype(r.dtype)
            continue
        for j in range(DIL_GROUP_WIDTH // LANES):
            stage_ref[j] = y[:, j * LANES:(j + 1) * LANES]
        for c in range(dilation):
            for j in range(DIL_GROUP_WIDTH // LANES):
                col = c * DIL_GROUP_WIDTH + j * LANES
                r[0, :, col:col + LANES] = stage_ref[j, pl.ds(c, tm // dilation, stride=dilation), :].astype(r.dtype)
    for r in (ga_ref, gb_ref):
        r[0] = _sigmoid(_dot(hb, w_rows_ref[:, off:off + D_MODEL])).astype(r.dtype)
        off += D_MODEL
    qT_ref[0] = _dot_nt(w_cols_ref[0:SB_WIDTH, :], hb).astype(qT_ref.dtype)
    vT = _dot_nt(w_cols_ref[SB_WIDTH:2 * SB_WIDTH, :], hb).astype(vT_ref.dtype)
    for p in range(SB_WIDTH // LANES):
        for jb in range(tm // LANES):
            vT_ref[0, p, jb] = vT[p * LANES:(p + 1) * LANES, jb * LANES:(jb + 1) * LANES]


def _dilated_shape(b, s, dilation, dtype):
    return jax.ShapeDtypeStruct((b, s // dilation, dilation * DIL_GROUP_WIDTH), dtype)


def _dilated_spec(tm, dilation):
    return pl.BlockSpec((1, tm // dilation, dilation * DIL_GROUP_WIDTH), lambda bi, i: (bi, i, 0))


def in_projection(x, mod, g_pre, w_rows, w_cols, *, tm=256):
    b, s, d = x.shape
    nt = s // tm
    bf = jnp.bfloat16
    tok = lambda width: pl.BlockSpec((1, tm, width), lambda bi, i: (bi, i, 0))
    out_shapes = [
        jax.ShapeDtypeStruct((b, SB_WIDTH, s), bf),
        jax.ShapeDtypeStruct((b, s, SB_WIDTH), bf),
        jax.ShapeDtypeStruct((b, SB_WIDTH // LANES, s // LANES, LANES, LANES), bf),
    ] + [_dilated_shape(b, s, r, bf) for _, r in DIL_CONFIGS] * 3 + [jax.ShapeDtypeStruct((b, s, d), bf)] * 2
    out_specs = [
        pl.BlockSpec((1, SB_WIDTH, tm), lambda bi, i: (bi, 0, i)),
        tok(SB_WIDTH),
        pl.BlockSpec((1, SB_WIDTH // LANES, tm // LANES, LANES, LANES), lambda bi, i: (bi, 0, i, 0, 0)),
    ] + [_dilated_spec(tm, r) for _, r in DIL_CONFIGS] * 3 + [tok(d)] * 2
    return pl.pallas_call(
        functools.partial(_in_proj_kernel, tm=tm),
        grid=(b, nt),
        in_specs=[
            tok(d),
            pl.BlockSpec((1, N_MOD, d), lambda bi, i: (bi, 0, 0)),
            pl.BlockSpec((1, d), lambda bi, i: (0, 0)),
            pl.BlockSpec(w_rows.shape, lambda bi, i: (0, 0)),
            pl.BlockSpec(w_cols.shape, lambda bi, i: (0, 0)),
        ],
        out_specs=out_specs,
        out_shape=out_shapes,
        scratch_shapes=[pltpu.VMEM((DIL_GROUP_WIDTH // LANES, tm, LANES), jnp.float32)],
        compiler_params=_cparams("parallel", "parallel"),
        name="in_projection",
    )(x, mod, g_pre.reshape(1, d), w_rows, w_cols)


SB_TILE = 512
SB_SUB = 256
SB_STEP_HEADS = 4


def _sb_chunk(kc, vc, q_h, tri, csum, past):
    z = _dot(kc, q_h)
    neg_abs = pltpu.bitcast(pltpu.bitcast(z, jnp.uint32) | jnp.uint32(0x80000000), jnp.float32)
    drop = jnp.maximum(z, 0.0) + jnp.log(1.0 + jnp.exp2(neg_abs)) * LOG2E
    if past is not None:
        drop = jnp.where(past, drop, 0.0)
    weights = [None] * (SB_TILE // SB_SUB)
    for sb in reversed(range(SB_TILE // SB_SUB)):
        rows = slice(sb * SB_SUB, (sb + 1) * SB_SUB)
        d = drop[rows]
        later = _dot(tri, d.astype(jnp.bfloat16))
        w = jnp.exp2(z[rows] - d - later - csum)
        if past is not None:
            w = jnp.where(past[rows], w, 0.0)
        weights[sb] = w.astype(jnp.bfloat16)
        csum = csum + jnp.sum(d, axis=0, keepdims=True)
    w = jnp.concatenate(weights, axis=0)
    t = SB_TILE
    part = jnp.concatenate([_dot(vc[h * HEAD_DIM:(h + 1) * HEAD_DIM], w[:, h * t:(h + 1) * t])
                            for h in range(SB_STEP_HEADS)], axis=0)
    return part, csum


def _sb_attn_kernel(qT_ref, k_ref, vT_ref, tri_ref, o_ref, acc_ref, csum_ref):
    i = pl.program_id(2)
    t = SB_TILE
    q_all = qT_ref[0]
    row = lax.broadcasted_iota(jnp.int32, q_all.shape, 0)
    zero = jnp.zeros_like(q_all)
    q_heads = jnp.concatenate(
        [jnp.where(jnp.logical_and(row >= h * HEAD_DIM, row < (h + 1) * HEAD_DIM), q_all, zero)
         for h in range(SB_STEP_HEADS)], axis=1)
    tri = tri_ref[...]
    blocks = t // LANES

    def chunk(c, past, first):
        kc = k_ref[0, pl.ds(pl.multiple_of(c * t, t), t), :]
        vc = jnp.concatenate(
            [jnp.concatenate([vT_ref[0, p, c * blocks + jb] for jb in range(blocks)], axis=1)
             for p in range(SB_STEP_HEADS // 2)], axis=0)
        csum = jnp.zeros((1, SB_STEP_HEADS * t), jnp.float32) if first else csum_ref[...]
        part, csum = _sb_chunk(kc, vc, q_heads, tri, csum, past)
        acc_ref[...] = part if first else acc_ref[...] + part
        csum_ref[...] = csum

    key = lax.broadcasted_iota(jnp.int32, (t, SB_STEP_HEADS * t), 0)
    query = lax.broadcasted_iota(jnp.int32, (t, SB_STEP_HEADS * t), 1)
    chunk(i, key < jnp.bitwise_and(query, t - 1), True)

    def body(it, carry):
        chunk(i - 1 - it, None, False)
        return carry

    lax.fori_loop(0, i, body, 0)
    o_ref[0] = acc_ref[...].T.astype(o_ref.dtype)


def sb_attention(qT, k, vT3):
    b, width, s = qT.shape
    t = SB_TILE
    hw = SB_STEP_HEADS * HEAD_DIM
    tri = jnp.asarray(np.triu(np.ones((SB_SUB, SB_SUB), np.float32), k=1), jnp.bfloat16)
    return pl.pallas_call(
        _sb_attn_kernel,
        grid=(b, width // hw, s // t),
        in_specs=[
            pl.BlockSpec((1, hw, t), lambda bi, p, i: (bi, p, i)),
            pl.BlockSpec((1, s, hw), lambda bi, p, i: (bi, 0, p)),
            pl.BlockSpec((1, hw // LANES, s // LANES, LANES, LANES), lambda bi, p, i: (bi, p, 0, 0, 0)),
            pl.BlockSpec((SB_SUB, SB_SUB), lambda bi, p, i: (0, 0)),
        ],
        out_specs=pl.BlockSpec((1, t, hw), lambda bi, p, i: (bi, i, p)),
        out_shape=jax.ShapeDtypeStruct((b, s, width), jnp.bfloat16),
        scratch_shapes=[pltpu.VMEM((hw, t), jnp.float32), pltpu.VMEM((1, SB_STEP_HEADS * t), jnp.float32)],
        compiler_params=_cparams("parallel", "parallel", "arbitrary"),
        name="sb_attention",
    )(qT, k, vT3, tri)


DIL_BLOCKS_PER_STEP = 4


def _dil_attn_kernel(q_ref, kp_ref, kc_ref, vp_ref, vc_ref, bias_ref, o_ref, lse_ref, *, n_blocks, n_cols):
    m = pl.program_id(2)
    kk_all = jnp.concatenate([kp_ref[0], kc_ref[0]], axis=0)
    vv_all = jnp.concatenate([vp_ref[0], vc_ref[0]], axis=0)
    lane = lax.broadcasted_iota(jnp.int32, (Q_BLOCK, DIL_GROUP_WIDTH), 1)
    in_head = [jnp.logical_and(lane >= h * HEAD_DIM, lane < (h + 1) * HEAD_DIM) for h in range(DIL_HEADS_PER_GROUP)]
    bias = bias_ref[...]
    key = lax.broadcasted_iota(jnp.int32, bias.shape, 1)
    for j, cc in [(j, cc) for cc in range(n_cols) for j in range(n_blocks)]:
        q_rows = slice(j * Q_BLOCK, (j + 1) * Q_BLOCK)
        cols = slice(cc * DIL_GROUP_WIDTH, (cc + 1) * DIL_GROUP_WIDTH)
        q4 = q_ref[0, q_rows, cols]
        kk = kk_all[j * Q_BLOCK:(j + 2) * Q_BLOCK, cols]
        vv = vv_all[j * Q_BLOCK:(j + 2) * Q_BLOCK, cols]
        zero = jnp.zeros_like(q4)
        q_stack = jnp.concatenate([jnp.where(mask, q4, zero) for mask in in_head], axis=0)
        logits = _dot_nt(q_stack, kk) + bias
        if j == 0:
            logits = jnp.where(jnp.logical_or(key >= Q_BLOCK, m > 0), logits, NEG_INF)
        mx = jnp.max(logits, axis=1, keepdims=True)
        p = jnp.exp(logits - mx)
        den = jnp.sum(p, axis=1, keepdims=True)
        pv = _dot(p.astype(jnp.bfloat16), vv) / den
        lse_rows = mx + jnp.log(den)
        out = jnp.zeros(q4.shape, jnp.float32)
        lse = jnp.zeros(q4.shape, jnp.float32)
        for h, mask in enumerate(in_head):
            rows = slice(h * Q_BLOCK, (h + 1) * Q_BLOCK)
            out = jnp.where(mask, pv[rows], out)
            lse = jnp.where(mask, lse_rows[rows], lse)
        o_ref[0, q_rows, cols] = out.astype(o_ref.dtype)
        lse_ref[0, q_rows, cols] = lse


def dilated_attention(q, k, v, bias, dilation):
    b, length, width = q.shape
    gw = DIL_GROUP_WIDTH
    n_blocks = min(DIL_BLOCKS_PER_STEP, length // Q_BLOCK)
    n_cols = min(DIL_BLOCKS_PER_STEP // n_blocks, width // gw)
    cur = pl.BlockSpec((1, n_blocks * Q_BLOCK, n_cols * gw), lambda bi, c, m: (bi, m, c))
    prev = pl.BlockSpec((1, Q_BLOCK, n_cols * gw), lambda bi, c, m: (bi, jnp.maximum(m * n_blocks - 1, 0), c))
    return pl.pallas_call(
        functools.partial(_dil_attn_kernel, n_blocks=n_blocks, n_cols=n_cols),
        grid=(b, width // (n_cols * gw), length // (n_blocks * Q_BLOCK)),
        in_specs=[cur, prev, cur, prev, cur,
                  pl.BlockSpec((DIL_HEADS_PER_GROUP * Q_BLOCK, 2 * Q_BLOCK), lambda bi, c, m: (0, 0))],
        out_specs=[cur, cur],
        out_shape=[jax.ShapeDtypeStruct((b, length, width), jnp.bfloat16),
                   jax.ShapeDtypeStruct((b, length, width), jnp.float32)],
        compiler_params=_cparams("parallel", "parallel", "arbitrary"),
        name=f"dilated_attention_r{dilation}",
    )(q, k, k, v, v, bias.reshape(DIL_HEADS_PER_GROUP * Q_BLOCK, 2 * Q_BLOCK))


def _t5_causal_bucket(dist):
    max_exact = N_BUCKETS // 2
    d = jnp.maximum(dist, 1).astype(jnp.float32)
    large = max_exact + (jnp.log(d / max_exact) / math.log(MAX_DISTANCE / max_exact)
                         * (N_BUCKETS - max_exact)).astype(jnp.int32)
    large = jnp.minimum(large, N_BUCKETS - 1)
    return jnp.where(dist < max_exact, dist, large)


def dilated_bias(rel_bias_group, window, dilation):
    qi = jnp.arange(Q_BLOCK)[:, None]
    kj = jnp.arange(2 * Q_BLOCK)[None, :]
    dist_sub = qi + Q_BLOCK - kj
    bucket = _t5_causal_bucket(jnp.maximum(dist_sub, 0) * dilation)
    onehot = (bucket[..., None] == jnp.arange(N_BUCKETS)).astype(jnp.float32)
    bias = jnp.einsum("qkn,nh->hqk", onehot, rel_bias_group.astype(jnp.float32), precision=lax.Precision.HIGHEST)
    valid = (dist_sub >= 0) & (dist_sub <= window // dilation)
    return jnp.where(valid[None], bias, NEG_INF)


def _mix_out_kernel(oa_ref, o0_ref, o1_ref, o2_ref, l0_ref, l1_ref, l2_ref, ga_ref, gb_ref, x_ref, mod_ref,
                    wa_ref, wb_ref, wo_ref, gpost_ref, gpre_ref, wr_ref, wsgu_ref, wsd_ref,
                    x1_ref, hp_ref, sh_ref, lt_ref, stage_ref):
    f32 = jnp.float32
    tm = x_ref.shape[1]

    def token_major(ref, dilation):
        if dilation == 1:
            return ref[0].astype(f32)
        slabs = DIL_GROUP_WIDTH // LANES
        for c in range(dilation):
            for j in range(slabs):
                col = c * DIL_GROUP_WIDTH + j * LANES
                stage_ref[j, pl.ds(c, tm // dilation, stride=dilation), :] = ref[0, :, col:col + LANES].astype(f32)
        return jnp.concatenate([stage_ref[j] for j in range(slabs)], axis=1)

    dils = [r for _, r in DIL_CONFIGS]
    l0, l1, l2 = (token_major(ref, r) for ref, r in zip((l0_ref, l1_ref, l2_ref), dils))
    mx = jnp.maximum(jnp.maximum(l0, l1), l2)
    e0, e1, e2 = jnp.exp(l0 - mx), jnp.exp(l1 - mx), jnp.exp(l2 - mx)
    den = e0 + e1 + e2
    o0, o1, o2 = (token_major(ref, r) for ref, r in zip((o0_ref, o1_ref, o2_ref), dils))
    out_b = (e0 / den) * o0 + (e1 / den) * o1 + (e2 / den) * o2
    ya = _dot(oa_ref[0], wa_ref[...])
    yb = _dot(out_b.astype(jnp.bfloat16), wb_ref[...])
    merged = ga_ref[0].astype(f32) * ya + gb_ref[0].astype(f32) * yb
    y = _dot(merged.astype(jnp.bfloat16), wo_ref[...])
    x1 = x_ref[0] + mod_ref[0, 2:3, :] * _rms_norm(y, gpost_ref[...])
    x1_ref[0] = x1
    h = _rms_norm(x1, gpre_ref[...]) * (1.0 + mod_ref[0, 4:5, :]) + mod_ref[0, 3:4, :]
    hb = h.astype(jnp.bfloat16)
    _pack_rows(hp_ref, h)
    lt_ref[...] = _dot_nt(wr_ref[...], hb)
    gu = _dot(hb, wsgu_ref[...])
    g, u = gu[:, :EXPERT_DIM], gu[:, EXPERT_DIM:]
    sh_ref[0] = _dot((g * _sigmoid(g) * u).astype(jnp.bfloat16), wsd_ref[...])


def mix_out(out_a, o_g, lse_g, gate_a, gate_b, x, mod, w_a, w_b, w_o, g_post, g_pre_ffn, w_rT, w_sgu, w_sd, *, tm=512):
    b, s, d = x.shape
    tok = lambda width: pl.BlockSpec((1, tm, width), lambda bi, i: (bi, i, 0))
    full = lambda a: pl.BlockSpec(a.shape, lambda bi, i: (0,) * a.ndim)
    nt = s // tm
    g_post, g_pre_ffn = g_post.reshape(1, d), g_pre_ffn.reshape(1, d)
    weights = (w_a, w_b, w_o, g_post, g_pre_ffn, w_rT, w_sgu, w_sd)
    return pl.pallas_call(
        _mix_out_kernel,
        grid=(b, nt),
        in_specs=[tok(SB_WIDTH)] + [_dilated_spec(tm, r) for _, r in DIL_CONFIGS] * 2 + [tok(d)] * 3
        + [pl.BlockSpec((1, N_MOD, d), lambda bi, i: (bi, 0, 0))] + [full(w) for w in weights],
        out_specs=[tok(d), pl.BlockSpec((tm * ROW_SUB, LANES), lambda bi, i: (bi * nt + i, 0)), tok(d),
                   pl.BlockSpec((N_EXPERTS, tm), lambda bi, i: (0, bi * nt + i))],
        out_shape=[jax.ShapeDtypeStruct((b, s, d), jnp.float32),
                   jax.ShapeDtypeStruct((b * s * ROW_SUB, LANES), jnp.uint32),
                   jax.ShapeDtypeStruct((b, s, d), jnp.float32),
                   jax.ShapeDtypeStruct((N_EXPERTS, b * s), jnp.float32)],
        scratch_shapes=[pltpu.VMEM((DIL_GROUP_WIDTH // LANES, tm, LANES), jnp.float32)],
        compiler_params=_cparams("parallel", "parallel"),
        name="mix_out",
    )(out_a, *o_g, *lse_g, gate_a, gate_b, x, mod, *weights)


def _route_kernel(lt_ref, bias_ref, tri_ref, idx_ref, gate_ref, rank_ref, cnt_ref, base_ref, *, tn):
    f32 = jnp.float32

    @pl.when(pl.program_id(0) == 0)
    def _():
        base_ref[...] = jnp.zeros_like(base_ref)

    per_group = N_EXPERTS // N_GROUPS
    scores = _sigmoid(lt_ref[...])
    sel = scores + bias_ref[...]
    sel_groups = [sel[g * per_group:(g + 1) * per_group] for g in range(N_GROUPS)]
    ri = lax.broadcasted_iota(jnp.int32, (per_group, tn), 0)
    group_scores = []
    for sg in sel_groups:
        m1 = jnp.max(sg, axis=0, keepdims=True)
        first = jnp.min(jnp.where(sg == m1, ri, per_group), axis=0, keepdims=True)
        group_scores.append(m1 + jnp.max(jnp.where(ri == first, NEG_INF, sg), axis=0, keepdims=True))
    cur = jnp.concatenate(group_scores, axis=0)
    gi = lax.broadcasted_iota(jnp.int32, cur.shape, 0)
    keep = jnp.zeros(cur.shape, jnp.float32)
    for _ in range(TOP_K_GROUPS):
        mg = jnp.max(cur, axis=0, keepdims=True)
        pick = gi == jnp.min(jnp.where(cur == mg, gi, N_GROUPS), axis=0, keepdims=True)
        keep = jnp.where(pick, 1.0, keep)
        cur = jnp.where(pick, NEG_INF, cur)
    cur = jnp.concatenate([jnp.where(keep[g:g + 1, :] > 0.0, sg, NEG_INF) for g, sg in enumerate(sel_groups)], axis=0)
    ei = lax.broadcasted_iota(jnp.int32, cur.shape, 0)
    idxs, gates = [], []
    candidates = cur
    for _ in range(TOP_K):
        mk = jnp.max(cur, axis=0, keepdims=True)
        fk = jnp.min(jnp.where(cur == mk, ei, N_EXPERTS), axis=0, keepdims=True)
        pick = ei == fk
        idxs.append(fk)
        gates.append(jnp.sum(jnp.where(pick, scores, 0.0), axis=0, keepdims=True))
        cur = jnp.where(pick, NEG_INF, cur)
    gate = jnp.concatenate(gates, axis=0)
    gate_ref[...] = gate / jnp.sum(gate, axis=0, keepdims=True) * ROUTED_SCALE
    idx_ref[...] = jnp.concatenate(idxs, axis=0)
    onehot = jnp.where(cur == NEG_INF, 1.0, 0.0) - jnp.where(candidates == NEG_INF, 1.0, 0.0)
    earlier = _dot(onehot.astype(jnp.bfloat16), tri_ref[...]) + base_ref[...]
    ranks = [jnp.sum(jnp.where(ei == fk, earlier, 0.0), axis=0, keepdims=True) for fk in idxs]
    rank_ref[...] = jnp.concatenate(ranks, axis=0).astype(jnp.int32)
    base = base_ref[...] + jnp.sum(onehot, axis=1, keepdims=True)
    base_ref[...] = base
    cnt_ref[...] = jnp.broadcast_to(base, cnt_ref.shape).astype(jnp.int32)


def route(logits_t, router_bias, *, tn=512):
    ne, n = logits_t.shape
    tri = jnp.asarray(np.triu(np.ones((tn, tn), np.float32), k=1), jnp.bfloat16)
    slot = pl.BlockSpec((TOP_K, tn), lambda i: (0, i))
    return pl.pallas_call(
        functools.partial(_route_kernel, tn=tn),
        grid=(n // tn,),
        in_specs=[pl.BlockSpec((ne, tn), lambda i: (0, i)),
                  pl.BlockSpec((ne, 1), lambda i: (0, 0)),
                  pl.BlockSpec((tn, tn), lambda i: (0, 0))],
        out_specs=[slot, slot, slot, pl.BlockSpec((ne, LANES), lambda i: (0, 0))],
        out_shape=[jax.ShapeDtypeStruct((TOP_K, n), jnp.int32),
                   jax.ShapeDtypeStruct((TOP_K, n), jnp.float32),
                   jax.ShapeDtypeStruct((TOP_K, n), jnp.int32),
                   jax.ShapeDtypeStruct((ne, LANES), jnp.int32)],
        scratch_shapes=[pltpu.VMEM((ne, 1), jnp.float32)],
        compiler_params=_cparams("arbitrary"),
        name="route",
    )(logits_t, router_bias.reshape(ne, 1), tri)


def _dest_kernel(idx_ref, rank_ref, start_ref, dest_ref):
    idx = idx_ref[...]
    ei = lax.broadcasted_iota(jnp.int32, (N_EXPERTS, idx.shape[1]), 0)
    start = start_ref[...]
    rows = [jnp.sum(jnp.where(ei == idx[k:k + 1, :], start, 0.0), axis=0, keepdims=True) for k in range(TOP_K)]
    dest_ref[...] = jnp.concatenate(rows, axis=0).astype(jnp.int32) + rank_ref[...]


def slot_destinations(idx, rank, starts, *, tn=512):
    kk, n = idx.shape
    slot = pl.BlockSpec((kk, tn), lambda i: (0, i))
    return pl.pallas_call(
        _dest_kernel,
        grid=(n // tn,),
        in_specs=[slot, slot, pl.BlockSpec((N_EXPERTS, 1), lambda i: (0, 0))],
        out_specs=slot,
        out_shape=jax.ShapeDtypeStruct((kk, n), jnp.int32),
        compiler_params=_cparams("parallel"),
        name="slot_destinations",
    )(idx, rank, starts.astype(jnp.float32).reshape(N_EXPERTS, 1))


def _packed_row(ref, row):
    start = row * ROW_SUB
    return ref.at[pl.ds(start if isinstance(row, int) else pl.multiple_of(start, ROW_SUB), ROW_SUB), :]


def _dispatch_kernel(dest_ref, h_ref, xs_ref, zero_ref, sem, *, tm, n_rows):
    pad = zero_ref.shape[0]

    @pl.when(pl.program_id(0) == 0)
    def _():
        zero_ref[...] = jnp.zeros_like(zero_ref)
        tail = pltpu.make_async_copy(zero_ref, xs_ref.at[pl.ds(n_rows * ROW_SUB, pad), :], sem)
        tail.start()
        tail.wait()

    def body(t, _):
        for k in range(TOP_K):
            pltpu.make_async_copy(_packed_row(h_ref, t), _packed_row(xs_ref, dest_ref[k, t]), sem).start(priority=k % 2)
        return 0

    lax.fori_loop(0, tm, body, 0)
    for _ in range(TOP_K):
        pltpu.make_async_copy(h_ref, xs_ref.at[pl.ds(0, tm * ROW_SUB), :], sem).wait()


def dispatch(dest, hp, *, pad_rows, tm=512):
    kk, n = dest.shape
    n_rows = n * kk
    return pl.pallas_call(
        functools.partial(_dispatch_kernel, tm=tm, n_rows=n_rows),
        grid=(n // tm,),
        in_specs=[pl.BlockSpec((kk, tm), lambda i: (0, i), memory_space=pltpu.SMEM),
                  pl.BlockSpec((tm * ROW_SUB, LANES), lambda i: (i, 0))],
        out_specs=pl.BlockSpec(memory_space=pl.ANY),
        out_shape=jax.ShapeDtypeStruct(((n_rows + pad_rows) * ROW_SUB, LANES), hp.dtype),
        scratch_shapes=[pltpu.VMEM((pad_rows * ROW_SUB, LANES), hp.dtype), pltpu.SemaphoreType.DMA],
        compiler_params=_cparams("arbitrary"),
        name="dispatch",
    )(dest, hp)


EXPERT_CHUNK = 256
EXPERT_LOOKAHEAD = 3
EXPERT_IN_BUFS = EXPERT_LOOKAHEAD + 1
EXPERT_OUT_BUFS = 3


def _expert_kernel(first_ref, nch_ref, crow_ref, total_ref, xs_ref, wg_ref, wu_ref, wd_ref, ys_ref,
                   xbuf, ybuf, wgu_bf, wd_bf, h_ref, sem_in, sem_out, waited_ref):
    e = pl.program_id(0)
    chunk = EXPERT_CHUNK
    first, n_chunks, total = first_ref[e], nch_ref[e], total_ref[0]

    def rows(g):
        return pl.ds(pl.multiple_of(crow_ref[g] * ROW_SUB, ROW_SUB), chunk * ROW_SUB)

    def in_copy(g):
        slot = lax.rem(g, EXPERT_IN_BUFS)
        return pltpu.make_async_copy(xs_ref.at[rows(g), :], xbuf.at[slot], sem_in.at[slot])

    def out_copy(g):
        slot = lax.rem(g, EXPERT_OUT_BUFS)
        return pltpu.make_async_copy(ybuf.at[slot], ys_ref.at[rows(g), :], sem_out.at[slot])

    def wait_writes_through(g):
        def wait_one(j, carry):
            out_copy(j).wait()
            return carry

        lax.fori_loop(waited_ref[0] + 1, g + 1, wait_one, 0)
        waited_ref[0] = jnp.maximum(waited_ref[0], g)

    @pl.when(e == 0)
    def _():
        waited_ref[0] = -1
        for g in range(EXPERT_LOOKAHEAD):
            @pl.when(g < total)
            def _():
                in_copy(g).start()

    @pl.when(n_chunks > 0)
    def _():
        wgu_bf[:, :EXPERT_DIM] = wg_ref[0].astype(wgu_bf.dtype)
        wgu_bf[:, EXPERT_DIM:] = wu_ref[0].astype(wgu_bf.dtype)
        wd_bf[...] = wd_ref[0].astype(wd_bf.dtype)

    def acquire(g):
        in_copy(g).wait()

        @pl.when(g + EXPERT_LOOKAHEAD < total)
        def _():
            in_copy(g + EXPERT_LOOKAHEAD).start()

    def hidden(g):
        bf = jnp.bfloat16
        xa, xb = _unpack_rows(xbuf.at[lax.rem(g, EXPERT_IN_BUFS)], chunk)
        gu = _dot(xa.astype(bf), wgu_bf[:ROW_WORDS, :]) + _dot(xb.astype(bf), wgu_bf[ROW_WORDS:, :])
        gate, up = gu[:, :EXPERT_DIM], gu[:, EXPERT_DIM:]
        return (gate * _sigmoid(gate) * up).astype(bf)

    def pack(g, h):
        _pack_rows(ybuf.at[lax.rem(g, EXPERT_OUT_BUFS)], _dot(h, wd_bf[...]))

    def release(g):
        @pl.when(g == first)
        def _():
            wait_writes_through(g - 1)

        out_copy(g).start()

    @pl.when(n_chunks > 0)
    def _():
        acquire(first)
        h_ref[...] = hidden(first)

    def body(c, carry):
        g = first + c
        acquire(g)
        wait_writes_through(g - 1 - EXPERT_OUT_BUFS)
        h_prev = h_ref[...]
        pack(g - 1, h_prev)
        h_ref[...] = hidden(g)
        release(g - 1)
        return carry

    lax.fori_loop(1, n_chunks, body, 0)

    @pl.when(n_chunks > 0)
    def _():
        last = first + n_chunks - 1
        wait_writes_through(last - EXPERT_OUT_BUFS)
        pack(last, h_ref[...])
        release(last)

    @pl.when(e == pl.num_programs(0) - 1)
    def _():
        wait_writes_through(total - 1)
        ybuf[0] = jnp.zeros(ybuf.shape[1:], ybuf.dtype)
        pad = pl.ds(ys_ref.shape[0] - chunk * ROW_SUB, chunk * ROW_SUB)
        tail = pltpu.make_async_copy(ybuf.at[0], ys_ref.at[pad, :], sem_out.at[0])
        tail.start()
        tail.wait()


def expert_chunks(starts, counts, n_rows):
    chunk = EXPERT_CHUNK
    max_chunks = n_rows // chunk + N_EXPERTS
    nch = (counts + (chunk - 1)) // chunk
    ends = jnp.cumsum(nch)
    first = ends - nch
    g = jnp.arange(max_chunks, dtype=jnp.int32)
    owner = jnp.minimum(jnp.sum(g[:, None] >= ends[None, :], axis=1), N_EXPERTS - 1)
    onehot = (owner[:, None] == jnp.arange(N_EXPERTS)[None, :]).astype(jnp.int32)
    crow = (onehot * (starts - first * chunk)[None, :]).sum(axis=1) + g * chunk
    crow = jnp.where(g < ends[-1], crow, 0).astype(jnp.int32)
    return first.astype(jnp.int32), nch.astype(jnp.int32), crow, ends[-1:].astype(jnp.int32)


def expert_ffn(starts, counts, xs, w_gate, w_up, w_down):
    d = D_MODEL
    n_rows = xs.shape[0] // ROW_SUB - EXPERT_CHUNK
    tables = expert_chunks(starts, counts, n_rows)
    chunk_shape = (EXPERT_CHUNK * ROW_SUB, LANES)
    weights = lambda shape: pl.BlockSpec(shape, lambda e, *_: (e, 0, 0))
    grid_spec = pltpu.PrefetchScalarGridSpec(
        num_scalar_prefetch=len(tables),
        grid=(N_EXPERTS,),
        in_specs=[pl.BlockSpec(memory_space=pl.ANY),
                  weights((1, d, EXPERT_DIM)), weights((1, d, EXPERT_DIM)), weights((1, EXPERT_DIM, d))],
        out_specs=pl.BlockSpec(memory_space=pl.ANY),
        scratch_shapes=[pltpu.VMEM((EXPERT_IN_BUFS,) + chunk_shape, xs.dtype),
                        pltpu.VMEM((EXPERT_OUT_BUFS,) + chunk_shape, xs.dtype),
                        pltpu.VMEM((d, 2 * EXPERT_DIM), jnp.bfloat16), pltpu.VMEM((EXPERT_DIM, d), jnp.bfloat16),
                        pltpu.VMEM((EXPERT_CHUNK, EXPERT_DIM), jnp.bfloat16),
                        pltpu.SemaphoreType.DMA((EXPERT_IN_BUFS,)), pltpu.SemaphoreType.DMA((EXPERT_OUT_BUFS,)),
                        pltpu.SMEM((1,), jnp.int32)],
    )
    return pl.pallas_call(
        _expert_kernel,
        grid_spec=grid_spec,
        out_shape=jax.ShapeDtypeStruct(xs.shape, xs.dtype),
        compiler_params=_cparams("arbitrary"),
        name="expert_ffn",
    )(*tables, xs, w_gate, w_up, w_down)


COMBINE_AHEAD = 2
COMBINE_SLOTS = COMBINE_AHEAD + 1
COMBINE_GROUP = 32


def _combine_kernel(*refs, tm):
    head_refs, (dest_ref, ys_ref, gate_ref, sh_ref, x1_ref, mod_ref, g_ref, o_ref, buf, sem) = (
        refs[:COMBINE_AHEAD], refs[COMBINE_AHEAD:])
    i, n = pl.program_id(0), pl.num_programs(0)

    def gather(src_ref, slot, t):
        for k in range(TOP_K):
            pltpu.make_async_copy(_packed_row(ys_ref, src_ref[k, t]), _packed_row(buf.at[slot, k], t),
                                  sem.at[slot]).start(priority=k % 2)

    def wait_tile(slot):
        for k in range(TOP_K):
            pltpu.make_async_copy(ys_ref.at[pl.ds(0, tm * ROW_SUB), :], buf.at[slot, k], sem.at[slot]).wait()

    @pl.when(i == 0)
    def _():
        for a, head in enumerate(head_refs):
            def body(t, carry, head=head, a=a):
                gather(head, a, t)
                return carry

            lax.fori_loop(0, tm, body, 0)

    def step(cur):
        ahead = (cur + COMBINE_AHEAD) % COMBINE_SLOTS
        wait_tile(cur)
        g_post = g_ref[...]
        gt2 = mod_ref[0, 5:6, :]
        for t0 in range(0, tm, COMBINE_GROUP):
            for t in range(t0, t0 + COMBINE_GROUP):
                gather(dest_ref, ahead, t)
            rows = slice(t0, t0 + COMBINE_GROUP)
            gate = gate_ref[rows, :]
            first = jnp.zeros((COMBINE_GROUP, ROW_WORDS), jnp.float32)
            second = jnp.zeros((COMBINE_GROUP, ROW_WORDS), jnp.float32)
            for k in range(TOP_K):
                packed = buf.at[cur, k, pl.ds(t0 * ROW_SUB, COMBINE_GROUP * ROW_SUB), :]
                ya, yb = _unpack_rows(packed, COMBINE_GROUP)
                first = first + ya * gate[:, k:k + 1]
                second = second + yb * gate[:, k:k + 1]
            y = sh_ref[rows, :] + jnp.concatenate([first, second], axis=1)
            o_ref[rows, :] = x1_ref[rows, :] + gt2 * _rms_norm(y, g_post)

        @pl.when(i == n - 1)
        def _():
            for a in range(1, COMBINE_SLOTS):
                wait_tile((cur + a) % COMBINE_SLOTS)

    for slot in range(COMBINE_SLOTS):
        pl.when(lax.rem(i, COMBINE_SLOTS) == slot)(functools.partial(step, slot))


def combine(dest, ys, gate_nk, shared, x1, mod, g_post_ffn, *, seq, tm=128):
    n, d = x1.shape
    per_batch = seq // tm
    n_tiles = n // tm
    tok = pl.BlockSpec((tm, d), lambda i: (i, 0))
    dest_tile = lambda index: pl.BlockSpec((TOP_K, tm), index, memory_space=pltpu.SMEM)
    head_specs = [dest_tile(lambda i, a=a: (0, a)) for a in range(COMBINE_AHEAD)]
    return pl.pallas_call(
        functools.partial(_combine_kernel, tm=tm),
        grid=(n_tiles,),
        in_specs=head_specs + [dest_tile(lambda i: (0, jnp.minimum(i + COMBINE_AHEAD, n_tiles - 1))),
                               pl.BlockSpec(memory_space=pl.ANY),
                               pl.BlockSpec((tm, TOP_K), lambda i: (i, 0)),
                               tok, tok,
                               pl.BlockSpec((1, N_MOD, d), lambda i: (i // per_batch, 0, 0)),
                               pl.BlockSpec((1, d), lambda i: (0, 0))],
        out_specs=tok,
        out_shape=jax.ShapeDtypeStruct((n, d), jnp.float32),
        scratch_shapes=[pltpu.VMEM((COMBINE_SLOTS, TOP_K, tm * ROW_SUB, LANES), ys.dtype),
                        pltpu.SemaphoreType.DMA((COMBINE_SLOTS,))],
        compiler_params=_cparams("arbitrary"),
        name="combine",
    )(*([dest] * COMBINE_AHEAD), dest, ys, gate_nk, shared, x1, mod, g_post_ffn.reshape(1, d))


def _layer(x, mod, rel_bias, g_pre_mix, w_in, w_branch_a, w_branch_b, w_out, g_post_mix, g_pre_ffn, w_router,
           router_bias, w_exp_gate, w_exp_up, w_exp_down, w_sh_gate, w_sh_up, w_sh_down, g_post_ffn):
    b, s, d = x.shape
    bf = jnp.bfloat16
    scale = HEAD_DIM ** -0.5
    sb_q, sb_k, sb_v, dq, dk, dv, w_ga, w_gb = jnp.split(
        w_in, np.cumsum((SB_WIDTH,) * 3 + (DIL_WIDTH,) * 3 + (d,)).tolist(), axis=1)
    groups = lambda w: [w[:, g * DIL_GROUP_WIDTH:(g + 1) * DIL_GROUP_WIDTH] for g in range(len(DIL_CONFIGS))]
    w_rows = jnp.concatenate([sb_k] + [w * scale for w in groups(dq)] + groups(dk) + groups(dv) + [w_ga, w_gb],
                             axis=1).astype(bf)
    w_cols = jnp.concatenate([sb_q * (scale * LOG2E), sb_v], axis=1).T.astype(bf)
    qT, k, vT3, *rest = in_projection(x, mod, g_pre_mix, w_rows, w_cols)
    dqs, dks, dvs, gate_a, gate_b = rest[0:3], rest[3:6], rest[6:9], rest[9], rest[10]
    out_a = sb_attention(qT, k, vT3)
    o_g, lse_g = [], []
    for g, (window, dilation) in enumerate(DIL_CONFIGS):
        bias = dilated_bias(rel_bias[:, g * DIL_HEADS_PER_GROUP:(g + 1) * DIL_HEADS_PER_GROUP], window, dilation)
        o, lse = dilated_attention(dqs[g], dks[g], dvs[g], bias, dilation)
        o_g.append(o)
        lse_g.append(lse)
    x1, hp, shared, logits_t = mix_out(
        out_a, o_g, lse_g, gate_a, gate_b, x, mod, w_branch_a.astype(bf), w_branch_b.astype(bf), w_out.astype(bf),
        g_post_mix, g_pre_ffn, w_router.T.astype(bf), jnp.concatenate([w_sh_gate, w_sh_up], axis=1).astype(bf),
        w_sh_down.astype(bf))
    n = b * s
    idx, gate, rank, counts = route(logits_t, router_bias)
    counts = counts[:, 0]
    starts = jnp.cumsum(counts) - counts
    dest = slot_destinations(idx, rank, starts)
    xs = dispatch(dest, hp, pad_rows=EXPERT_CHUNK)
    ys = expert_ffn(starts, counts, xs, w_exp_gate, w_exp_up, w_exp_down)
    out = combine(dest, ys, gate.T, shared.reshape(n, d), x1.reshape(n, d), mod, g_post_ffn, seq=s)
    return out.reshape(b, s, d)


def kernel(x, c, rel_bias, w_ada, b_ada, g_pre_mix, w_in, w_branch_a, w_branch_b, w_out, g_post_mix, g_pre_ffn,
           w_router, router_bias, w_exp_gate, w_exp_up, w_exp_down, w_sh_gate, w_sh_up, w_sh_down, g_post_ffn):
    depth = w_in.shape[0]
    for layer in range(depth):
        mod = ada_modulation(c, w_ada[layer], b_ada[layer]).reshape(x.shape[0], N_MOD, x.shape[2])
        x = _layer(x, mod, rel_bias, g_pre_mix[layer], w_in[layer], w_branch_a[layer], w_branch_b[layer], w_out[layer],
                   g_post_mix[layer], g_pre_ffn[layer], w_router[layer], router_bias[layer], w_exp_gate[layer],
                   w_exp_up[layer], w_exp_down[layer], w_sh_gate[layer], w_sh_up[layer], w_sh_down[layer],
                   g_post_ffn[layer])
    return x
```

```python
import functools
import math

import jax
import jax.numpy as jnp
import numpy as np
from jax import lax
from jax.experimental import pallas as pl
from jax.experimental.pallas import tpu as pltpu

D_MODEL = 1024
HEAD_DIM = 64
SB_HEADS = 8
SB_WIDTH = SB_HEADS * HEAD_DIM
DIL_CONFIGS = ((128, 1), (512, 4), (2048, 16))
DIL_HEADS_PER_GROUP = 4
DIL_GROUP_WIDTH = DIL_HEADS_PER_GROUP * HEAD_DIM
DIL_WIDTH = DIL_GROUP_WIDTH * len(DIL_CONFIGS)
Q_BLOCK = 128
N_BUCKETS = 32
MAX_DISTANCE = 2048
N_EXPERTS = 256
TOP_K = 8
N_GROUPS = 8
TOP_K_GROUPS = 4
EXPERT_DIM = 256
ROUTED_SCALE = 2.5
N_MOD = 6
EPS = 1e-6

LANES = 128
VMEM_LIMIT_BYTES = 56 * 1024 * 1024
NEG_INF = float("-inf")
LOG2E = math.log2(math.e)


def _cparams(*semantics):
    return pltpu.CompilerParams(dimension_semantics=semantics, vmem_limit_bytes=VMEM_LIMIT_BYTES)


def _rms_norm(v, g):
    return v * lax.rsqrt(jnp.mean(v * v, axis=-1, keepdims=True) + EPS) * g


def _sigmoid(v):
    return 1.0 / (1.0 + jnp.exp(-v))


def _dot(a, b):
    return jnp.dot(a, b, preferred_element_type=jnp.float32)


def _dot_nt(a, b):
    return lax.dot_general(a, b, (((1,), (1,)), ((), ())), preferred_element_type=jnp.float32)


def _split_bf16(v):
    hi = v.astype(jnp.bfloat16)
    lo = (v - hi.astype(jnp.float32)).astype(jnp.bfloat16)
    return hi, lo


ROW_WORDS = D_MODEL // 2
ROW_SUB = ROW_WORDS // LANES


def _pack_rows(ref, v):
    m = v.shape[0]
    bits = pltpu.bitcast(v.astype(jnp.bfloat16).astype(jnp.float32), jnp.uint32)
    words = bits[:, :ROW_WORDS] | (bits[:, ROW_WORDS:] >> 16)
    for j in range(ROW_SUB):
        ref[pl.ds(j, m, stride=ROW_SUB), :] = words[:, j * LANES:(j + 1) * LANES]


def _unpack_rows(ref, m):
    words = jnp.concatenate([ref[pl.ds(j, m, stride=ROW_SUB), :] for j in range(ROW_SUB)], axis=1)
    first = pltpu.bitcast(words & jnp.uint32(0xFFFF0000), jnp.float32)
    second = pltpu.bitcast(words << 16, jnp.float32)
    return first, second


def _ada_kernel(c_ref, w_ref, b_ref, o_ref):
    c = c_ref[...]
    a_hi, a_lo = _split_bf16(c * _sigmoid(c))
    w_hi, w_lo = _split_bf16(w_ref[...])
    o_ref[...] = _dot(a_hi, w_hi) + _dot(a_lo, w_hi) + _dot(a_hi, w_lo) + b_ref[...]


def ada_modulation(c, w_ada, b_ada, *, tn=1536):
    b, d = c.shape
    n = w_ada.shape[1]
    return pl.pallas_call(
        _ada_kernel,
        grid=(n // tn,),
        in_specs=[
            pl.BlockSpec((b, d), lambda j: (0, 0)),
            pl.BlockSpec((d, tn), lambda j: (0, j)),
            pl.BlockSpec((1, tn), lambda j: (0, j)),
        ],
        out_specs=pl.BlockSpec((b, tn), lambda j: (0, j)),
        out_shape=jax.ShapeDtypeStruct((b, n), jnp.float32),
        compiler_params=_cparams("parallel"),
        name="ada_modulation",
    )(c, w_ada, b_ada.reshape(1, n))


def _in_proj_kernel(x_ref, mod_ref, g_ref, w_rows_ref, w_cols_ref, qT_ref, k_ref, vT_ref, *rest, tm):
    dil_refs, ga_ref, gb_ref, stage_ref = rest[:9], rest[9], rest[10], rest[11]
    h = _rms_norm(x_ref[0], g_ref[...]) * (1.0 + mod_ref[0, 1:2, :]) + mod_ref[0, 0:1, :]
    hb = h.astype(jnp.bfloat16)
    off = 0
    k_ref[0] = _dot(hb, w_rows_ref[:, off:off + SB_WIDTH]).astype(k_ref.dtype)
    off += SB_WIDTH
    for n, r in enumerate(dil_refs):
        y = _dot(hb, w_rows_ref[:, off:off + DIL_GROUP_WIDTH])
        off += DIL_GROUP_WIDTH
        dilation = DIL_CONFIGS[n % len(DIL_CONFIGS)][1]
        if dilation == 1:
            r[0] = y.astype(r.dtype)
            continue
        for j in range(DIL_GROUP_WIDTH // LANES):
            stage_ref[j] = y[:, j * LANES:(j + 1) * LANES]
        for c in range(dilation):
            for j in range(DIL_GROUP_WIDTH // LANES):
                col = c * DIL_GROUP_WIDTH + j * LANES
                r[0, :, col:col + LANES] = stage_ref[j, pl.ds(c, tm // dilation, stride=dilation), :].astype(r.dtype)
    for r in (ga_ref, gb_ref):
        r[0] = _sigmoid(_dot(hb, w_rows_ref[:, off:off + D_MODEL])).astype(r.dtype)
        off += D_MODEL
    qT_ref[0] = _dot_nt(w_cols_ref[0:SB_WIDTH, :], hb).astype(qT_ref.dtype)
    vT = _dot_nt(w_cols_ref[SB_WIDTH:2 * SB_WIDTH, :], hb).astype(vT_ref.dtype)
    for p in range(SB_WIDTH // LANES):
        for jb in range(tm // LANES):
            vT_ref[0, p, jb] = vT[p * LANES:(p + 1) * LANES, jb * LANES:(jb + 1) * LANES]


def _dilated_shape(b, s, dilation, dtype):
    return jax.ShapeDtypeStruct((b, s // dilation, dilation * DIL_GROUP_WIDTH), dtype)


def _dilated_spec(tm, dilation):
    return pl.BlockSpec((1, tm // dilation, dilation * DIL_GROUP_WIDTH), lambda bi, i: (bi, i, 0))


def in_projection(x, mod, g_pre, w_rows, w_cols, *, tm=256):
    b, s, d = x.shape
    nt = s // tm
    bf = jnp.bfloat16
    tok = lambda width: pl.BlockSpec((1, tm, width), lambda bi, i: (bi, i, 0))
    out_shapes = [
        jax.ShapeDtypeStruct((b, SB_WIDTH, s), bf),
        jax.ShapeDtypeStruct((b, s, SB_WIDTH), bf),
        jax.ShapeDtypeStruct((b, SB_WIDTH // LANES, s // LANES, LANES, LANES), bf),
    ] + [_dilated_shape(b, s, r, bf) for _, r in DIL_CONFIGS] * 3 + [jax.ShapeDtypeStruct((b, s, d), bf)] * 2
    out_specs = [
        pl.BlockSpec((1, SB_WIDTH, tm), lambda bi, i: (bi, 0, i)),
        tok(SB_WIDTH),
        pl.BlockSpec((1, SB_WIDTH // LANES, tm // LANES, LANES, LANES), lambda bi, i: (bi, 0, i, 0, 0)),
    ] + [_dilated_spec(tm, r) for _, r in DIL_CONFIGS] * 3 + [tok(d)] * 2
    return pl.pallas_call(
        functools.partial(_in_proj_kernel, tm=tm),
        grid=(b, nt),
        in_specs=[
            tok(d),
            pl.BlockSpec((1, N_MOD, d), lambda bi, i: (bi, 0, 0)),
            pl.BlockSpec((1, d), lambda bi, i: (0, 0)),
            pl.BlockSpec(w_rows.shape, lambda bi, i: (0, 0)),
            pl.BlockSpec(w_cols.shape, lambda bi, i: (0, 0)),
        ],
        out_specs=out_specs,
        out_shape=out_shapes,
        scratch_shapes=[pltpu.VMEM((DIL_GROUP_WIDTH // LANES, tm, LANES), jnp.float32)],
        compiler_params=_cparams("parallel", "parallel"),
        name="in_projection",
    )(x, mod, g_pre.reshape(1, d), w_rows, w_cols)


SB_TILE = 512
SB_SUB = 256
SB_STEP_HEADS = 4


def _sb_chunk(kc, vc, q_h, tri, csum, past):
    z = _dot(kc, q_h)
    neg_abs = pltpu.bitcast(pltpu.bitcast(z, jnp.uint32) | jnp.uint32(0x80000000), jnp.float32)
    drop = jnp.maximum(z, 0.0) + jnp.log(1.0 + jnp.exp2(neg_abs)) * LOG2E
    if past is not None:
        drop = jnp.where(past, drop, 0.0)
    weights = [None] * (SB_TILE // SB_SUB)
    for sb in reversed(range(SB_TILE // SB_SUB)):
        rows = slice(sb * SB_SUB, (sb + 1) * SB_SUB)
        d = drop[rows]
        later = _dot(tri, d.astype(jnp.bfloat16))
        w = jnp.exp2(z[rows] - d - later - csum)
        if past is not None:
            w = jnp.where(past[rows], w, 0.0)
        weights[sb] = w.astype(jnp.bfloat16)
        csum = csum + jnp.sum(d, axis=0, keepdims=True)
    w = jnp.concatenate(weights, axis=0)
    t = SB_TILE
    part = jnp.concatenate([_dot(vc[h * HEAD_DIM:(h + 1) * HEAD_DIM], w[:, h * t:(h + 1) * t])
                            for h in range(SB_STEP_HEADS)], axis=0)
    return part, csum


def _sb_attn_kernel(qT_ref, k_ref, vT_ref, tri_ref, o_ref, acc_ref, csum_ref):
    i = pl.program_id(2)
    t = SB_TILE
    q_all = qT_ref[0]
    row = lax.broadcasted_iota(jnp.int32, q_all.shape, 0)
    zero = jnp.zeros_like(q_all)
    q_heads = jnp.concatenate(
        [jnp.where(jnp.logical_and(row >= h * HEAD_DIM, row < (h + 1) * HEAD_DIM), q_all, zero)
         for h in range(SB_STEP_HEADS)], axis=1)
    tri = tri_ref[...]
    blocks = t // LANES

    def chunk(c, past, first):
        kc = k_ref[0, pl.ds(pl.multiple_of(c * t, t), t), :]
        vc = jnp.concatenate(
            [jnp.concatenate([vT_ref[0, p, c * blocks + jb] for jb in range(blocks)], axis=1)
             for p in range(SB_STEP_HEADS // 2)], axis=0)
        csum = jnp.zeros((1, SB_STEP_HEADS * t), jnp.float32) if first else csum_ref[...]
        part, csum = _sb_chunk(kc, vc, q_heads, tri, csum, past)
        acc_ref[...] = part if first else acc_ref[...] + part
        csum_ref[...] = csum

    key = lax.broadcasted_iota(jnp.int32, (t, SB_STEP_HEADS * t), 0)
    query = lax.broadcasted_iota(jnp.int32, (t, SB_STEP_HEADS * t), 1)
    chunk(i, key < jnp.bitwise_and(query, t - 1), True)

    def body(it, carry):
        chunk(i - 1 - it, None, False)
        return carry

    lax.fori_loop(0, i, body, 0)
    o_ref[0] = acc_ref[...].T.astype(o_ref.dtype)


def sb_attention(qT, k, vT3):
    b, width, s = qT.shape
    t = SB_TILE
    hw = SB_STEP_HEADS * HEAD_DIM
    tri = jnp.asarray(np.triu(np.ones((SB_SUB, SB_SUB), np.float32), k=1), jnp.bfloat16)
    return pl.pallas_call(
        _sb_attn_kernel,
        grid=(b, width // hw, s // t),
        in_specs=[
            pl.BlockSpec((1, hw, t), lambda bi, p, i: (bi, p, i)),
            pl.BlockSpec((1, s, hw), lambda bi, p, i: (bi, 0, p)),
            pl.BlockSpec((1, hw // LANES, s // LANES, LANES, LANES), lambda bi, p, i: (bi, p, 0, 0, 0)),
            pl.BlockSpec((SB_SUB, SB_SUB), lambda bi, p, i: (0, 0)),
        ],
        out_specs=pl.BlockSpec((1, t, hw), lambda bi, p, i: (bi, i, p)),
        out_shape=jax.ShapeDtypeStruct((b, s, width), jnp.bfloat16),
        scratch_shapes=[pltpu.VMEM((hw, t), jnp.float32), pltpu.VMEM((1, SB_STEP_HEADS * t), jnp.float32)],
        compiler_params=_cparams("parallel", "parallel", "arbitrary"),
        name="sb_attention",
    )(qT, k, vT3, tri)


DIL_BLOCKS_PER_STEP = 4


def _dil_attn_kernel(q_ref, kp_ref, kc_ref, vp_ref, vc_ref, bias_ref, o_ref, lse_ref, *, n_blocks, n_cols):
    m = pl.program_id(2)
    kk_all = jnp.concatenate([kp_ref[0], kc_ref[0]], axis=0)
    vv_all = jnp.concatenate([vp_ref[0], vc_ref[0]], axis=0)
    lane = lax.broadcasted_iota(jnp.int32, (Q_BLOCK, DIL_GROUP_WIDTH), 1)
    in_head = [jnp.logical_and(lane >= h * HEAD_DIM, lane < (h + 1) * HEAD_DIM) for h in range(DIL_HEADS_PER_GROUP)]
    bias = bias_ref[...]
    key = lax.broadcasted_iota(jnp.int32, bias.shape, 1)
    for j, cc in [(j, cc) for cc in range(n_cols) for j in range(n_blocks)]:
        q_rows = slice(j * Q_BLOCK, (j + 1) * Q_BLOCK)
        cols = slice(cc * DIL_GROUP_WIDTH, (cc + 1) * DIL_GROUP_WIDTH)
        q4 = q_ref[0, q_rows, cols]
        kk = kk_all[j * Q_BLOCK:(j + 2) * Q_BLOCK, cols]
        vv = vv_all[j * Q_BLOCK:(j + 2) * Q_BLOCK, cols]
        zero = jnp.zeros_like(q4)
        q_stack = jnp.concatenate([jnp.where(mask, q4, zero) for mask in in_head], axis=0)
        logits = _dot_nt(q_stack, kk) + bias
        if j == 0:
            logits = jnp.where(jnp.logical_or(key >= Q_BLOCK, m > 0), logits, NEG_INF)
        mx = jnp.max(logits, axis=1, keepdims=True)
        p = jnp.exp(logits - mx)
        den = jnp.sum(p, axis=1, keepdims=True)
        pv = _dot(p.astype(jnp.bfloat16), vv) / den
        lse_rows = mx + jnp.log(den)
        out = jnp.zeros(q4.shape, jnp.float32)
        lse = jnp.zeros(q4.shape, jnp.float32)
        for h, mask in enumerate(in_head):
            rows = slice(h * Q_BLOCK, (h + 1) * Q_BLOCK)
            out = jnp.where(mask, pv[rows], out)
            lse = jnp.where(mask, lse_rows[rows], lse)
        o_ref[0, q_rows, cols] = out.astype(o_ref.dtype)
        lse_ref[0, q_rows, cols] = lse


def dilated_attention(q, k, v, bias, dilation):
    b, length, width = q.shape
    gw = DIL_GROUP_WIDTH
    n_blocks = min(DIL_BLOCKS_PER_STEP, length // Q_BLOCK)
    n_cols = min(DIL_BLOCKS_PER_STEP // n_blocks, width // gw)
    cur = pl.BlockSpec((1, n_blocks * Q_BLOCK, n_cols * gw), lambda bi, c, m: (bi, m, c))
    prev = pl.BlockSpec((1, Q_BLOCK, n_cols * gw), lambda bi, c, m: (bi, jnp.maximum(m * n_blocks - 1, 0), c))
    return pl.pallas_call(
        functools.partial(_dil_attn_kernel, n_blocks=n_blocks, n_cols=n_cols),
        grid=(b, width // (n_cols * gw), length // (n_blocks * Q_BLOCK)),
        in_specs=[cur, prev, cur, prev, cur,
                  pl.BlockSpec((DIL_HEADS_PER_GROUP * Q_BLOCK, 2 * Q_BLOCK), lambda bi, c, m: (0, 0))],
        out_specs=[cur, cur],
        out_shape=[jax.ShapeDtypeStruct((b, length, width), jnp.bfloat16),
                   jax.ShapeDtypeStruct((b, length, width), jnp.float32)],
        compiler_params=_cparams("parallel", "parallel", "arbitrary"),
        name=f"dilated_attention_r{dilation}",
    )(q, k, k, v, v, bias.reshape(DIL_HEADS_PER_GROUP * Q_BLOCK, 2 * Q_BLOCK))


def _t5_causal_bucket(dist):
    max_exact = N_BUCKETS // 2
    d = jnp.maximum(dist, 1).astype(jnp.float32)
    large = max_exact + (jnp.log(d / max_exact) / math.log(MAX_DISTANCE / max_exact)
                         * (N_BUCKETS - max_exact)).astype(jnp.int32)
    large = jnp.minimum(large, N_BUCKETS - 1)
    return jnp.where(dist < max_exact, dist, large)


def dilated_bias(rel_bias_group, window, dilation):
    qi = jnp.arange(Q_BLOCK)[:, None]
    kj = jnp.arange(2 * Q_BLOCK)[None, :]
    dist_sub = qi + Q_BLOCK - kj
    bucket = _t5_causal_bucket(jnp.maximum(dist_sub, 0) * dilation)
    onehot = (bucket[..., None] == jnp.arange(N_BUCKETS)).astype(jnp.float32)
    bias = jnp.einsum("qkn,nh->hqk", onehot, rel_bias_group.astype(jnp.float32), precision=lax.Precision.HIGHEST)
    valid = (dist_sub >= 0) & (dist_sub <= window // dilation)
    return jnp.where(valid[None], bias, NEG_INF)


def _mix_out_kernel(oa_ref, o0_ref, o1_ref, o2_ref, l0_ref, l1_ref, l2_ref, ga_ref, gb_ref, x_ref, mod_ref,
                    wa_ref, wb_ref, wo_ref, gpost_ref, gpre_ref, wr_ref, wsgu_ref, wsd_ref,
                    x1_ref, hp_ref, sh_ref, lt_ref, stage_ref):
    f32 = jnp.float32
    tm = x_ref.shape[1]

    def token_major(ref, dilation):
        if dilation == 1:
            return ref[0].astype(f32)
        slabs = DIL_GROUP_WIDTH // LANES
        for c in range(dilation):
            for j in range(slabs):
                col = c * DIL_GROUP_WIDTH + j * LANES
                stage_ref[j, pl.ds(c, tm // dilation, stride=dilation), :] = ref[0, :, col:col + LANES].astype(f32)
        return jnp.concatenate([stage_ref[j] for j in range(slabs)], axis=1)

    dils = [r for _, r in DIL_CONFIGS]
    l0, l1, l2 = (token_major(ref, r) for ref, r in zip((l0_ref, l1_ref, l2_ref), dils))
    mx = jnp.maximum(jnp.maximum(l0, l1), l2)
    e0, e1, e2 = jnp.exp(l0 - mx), jnp.exp(l1 - mx), jnp.exp(l2 - mx)
    den = e0 + e1 + e2
    o0, o1, o2 = (token_major(ref, r) for ref, r in zip((o0_ref, o1_ref, o2_ref), dils))
    out_b = (e0 / den) * o0 + (e1 / den) * o1 + (e2 / den) * o2
    ya = _dot(oa_ref[0], wa_ref[...])
    yb = _dot(out_b.astype(jnp.bfloat16), wb_ref[...])
    merged = ga_ref[0].astype(f32) * ya + gb_ref[0].astype(f32) * yb
    y = _dot(merged.astype(jnp.bfloat16), wo_ref[...])
    x1 = x_ref[0] + mod_ref[0, 2:3, :] * _rms_norm(y, gpost_ref[...])
    x1_ref[0] = x1
    h = _rms_norm(x1, gpre_ref[...]) * (1.0 + mod_ref[0, 4:5, :]) + mod_ref[0, 3:4, :]
    hb = h.astype(jnp.bfloat16)
    _pack_rows(hp_ref, h)
    lt_ref[...] = _dot_nt(wr_ref[...], hb)
    gu = _dot(hb, wsgu_ref[...])
    g, u = gu[:, :EXPERT_DIM], gu[:, EXPERT_DIM:]
    sh_ref[0] = _dot((g * _sigmoid(g) * u).astype(jnp.bfloat16), wsd_ref[...])


def mix_out(out_a, o_g, lse_g, gate_a, gate_b, x, mod, w_a, w_b, w_o, g_post, g_pre_ffn, w_rT, w_sgu, w_sd, *, tm=512):
    b, s, d = x.shape
    tok = lambda width: pl.BlockSpec((1, tm, width), lambda bi, i: (bi, i, 0))
    full = lambda a: pl.BlockSpec(a.shape, lambda bi, i: (0,) * a.ndim)
    nt = s // tm
    g_post, g_pre_ffn = g_post.reshape(1, d), g_pre_ffn.reshape(1, d)
    weights = (w_a, w_b, w_o, g_post, g_pre_ffn, w_rT, w_sgu, w_sd)
    return pl.pallas_call(
        _mix_out_kernel,
        grid=(b, nt),
        in_specs=[tok(SB_WIDTH)] + [_dilated_spec(tm, r) for _, r in DIL_CONFIGS] * 2 + [tok(d)] * 3
        + [pl.BlockSpec((1, N_MOD, d), lambda bi, i: (bi, 0, 0))] + [full(w) for w in weights],
        out_specs=[tok(d), pl.BlockSpec((tm * ROW_SUB, LANES), lambda bi, i: (bi * nt + i, 0)), tok(d),
                   pl.BlockSpec((N_EXPERTS, tm), lambda bi, i: (0, bi * nt + i))],
        out_shape=[jax.ShapeDtypeStruct((b, s, d), jnp.float32),
                   jax.ShapeDtypeStruct((b * s * ROW_SUB, LANES), jnp.uint32),
                   jax.ShapeDtypeStruct((b, s, d), jnp.float32),
                   jax.ShapeDtypeStruct((N_EXPERTS, b * s), jnp.float32)],
        scratch_shapes=[pltpu.VMEM((DIL_GROUP_WIDTH // LANES, tm, LANES), jnp.float32)],
        compiler_params=_cparams("parallel", "parallel"),
        name="mix_out",
    )(out_a, *o_g, *lse_g, gate_a, gate_b, x, mod, *weights)


def _route_kernel(lt_ref, bias_ref, tri_ref, idx_ref, gate_ref, rank_ref, cnt_ref, base_ref, *, tn):
    f32 = jnp.float32

    @pl.when(pl.program_id(0) == 0)
    def _():
        base_ref[...] = jnp.zeros_like(base_ref)

    per_group = N_EXPERTS // N_GROUPS
    scores = _sigmoid(lt_ref[...])
    sel = scores + bias_ref[...]
    sel_groups = [sel[g * per_group:(g + 1) * per_group] for g in range(N_GROUPS)]
    ri = lax.broadcasted_iota(jnp.int32, (per_group, tn), 0)
    group_scores = []
    for sg in sel_groups:
        m1 = jnp.max(sg, axis=0, keepdims=True)
        first = jnp.min(jnp.where(sg == m1, ri, per_group), axis=0, keepdims=True)
        group_scores.append(m1 + jnp.max(jnp.where(ri == first, NEG_INF, sg), axis=0, keepdims=True))
    cur = jnp.concatenate(group_scores, axis=0)
    gi = lax.broadcasted_iota(jnp.int32, cur.shape, 0)
    keep = jnp.zeros(cur.shape, jnp.float32)
    for _ in range(TOP_K_GROUPS):
        mg = jnp.max(cur, axis=0, keepdims=True)
        pick = gi == jnp.min(jnp.where(cur == mg, gi, N_GROUPS), axis=0, keepdims=True)
        keep = jnp.where(pick, 1.0, keep)
        cur = jnp.where(pick, NEG_INF, cur)
    cur = jnp.concatenate([jnp.where(keep[g:g + 1, :] > 0.0, sg, NEG_INF) for g, sg in enumerate(sel_groups)], axis=0)
    ei = lax.broadcasted_iota(jnp.int32, cur.shape, 0)
    idxs, gates = [], []
    candidates = cur
    for _ in range(TOP_K):
        mk = jnp.max(cur, axis=0, keepdims=True)
        fk = jnp.min(jnp.where(cur == mk, ei, N_EXPERTS), axis=0, keepdims=True)
        pick = ei == fk
        idxs.append(fk)
        gates.append(jnp.sum(jnp.where(pick, scores, 0.0), axis=0, keepdims=True))
        cur = jnp.where(pick, NEG_INF, cur)
    gate = jnp.concatenate(gates, axis=0)
    gate_ref[...] = gate / jnp.sum(gate, axis=0, keepdims=True) * ROUTED_SCALE
    idx_ref[...] = jnp.concatenate(idxs, axis=0)
    onehot = jnp.where(cur == NEG_INF, 1.0, 0.0) - jnp.where(candidates == NEG_INF, 1.0, 0.0)
    earlier = _dot(onehot.astype(jnp.bfloat16), tri_ref[...]) + base_ref[...]
    ranks = [jnp.sum(jnp.where(ei == fk, earlier, 0.0), axis=0, keepdims=True) for fk in idxs]
    rank_ref[...] = jnp.concatenate(ranks, axis=0).astype(jnp.int32)
    base = base_ref[...] + jnp.sum(onehot, axis=1, keepdims=True)
    base_ref[...] = base
    cnt_ref[...] = jnp.broadcast_to(base, cnt_ref.shape).astype(jnp.int32)


def route(logits_t, router_bias, *, tn=512):
    ne, n = logits_t.shape
    tri = jnp.asarray(np.triu(np.ones((tn, tn), np.float32), k=1), jnp.bfloat16)
    slot = pl.BlockSpec((TOP_K, tn), lambda i: (0, i))
    return pl.pallas_call(
        functools.partial(_route_kernel, tn=tn),
        grid=(n // tn,),
        in_specs=[pl.BlockSpec((ne, tn), lambda i: (0, i)),
                  pl.BlockSpec((ne, 1), lambda i: (0, 0)),
                  pl.BlockSpec((tn, tn), lambda i: (0, 0))],
        out_specs=[slot, slot, slot, pl.BlockSpec((ne, LANES), lambda i: (0, 0))],
        out_shape=[jax.ShapeDtypeStruct((TOP_K, n), jnp.int32),
                   jax.ShapeDtypeStruct((TOP_K, n), jnp.float32),
                   jax.ShapeDtypeStruct((TOP_K, n), jnp.int32),
                   jax.ShapeDtypeStruct((ne, LANES), jnp.int32)],
        scratch_shapes=[pltpu.VMEM((ne, 1), jnp.float32)],
        compiler_params=_cparams("arbitrary"),
        name="route",
    )(logits_t, router_bias.reshape(ne, 1), tri)


def _dest_kernel(idx_ref, rank_ref, start_ref, dest_ref):
    idx = idx_ref[...]
    ei = lax.broadcasted_iota(jnp.int32, (N_EXPERTS, idx.shape[1]), 0)
    start = start_ref[...]
    rows = [jnp.sum(jnp.where(ei == idx[k:k + 1, :], start, 0.0), axis=0, keepdims=True) for k in range(TOP_K)]
    dest_ref[...] = jnp.concatenate(rows, axis=0).astype(jnp.int32) + rank_ref[...]


def slot_destinations(idx, rank, starts, *, tn=512):
    kk, n = idx.shape
    slot = pl.BlockSpec((kk, tn), lambda i: (0, i))
    return pl.pallas_call(
        _dest_kernel,
        grid=(n // tn,),
        in_specs=[slot, slot, pl.BlockSpec((N_EXPERTS, 1), lambda i: (0, 0))],
        out_specs=slot,
        out_shape=jax.ShapeDtypeStruct((kk, n), jnp.int32),
        compiler_params=_cparams("parallel"),
        name="slot_destinations",
    )(idx, rank, starts.astype(jnp.float32).reshape(N_EXPERTS, 1))


def _packed_row(ref, row):
    start = row * ROW_SUB
    return ref.at[pl.ds(start if isinstance(row, int) else pl.multiple_of(start, ROW_SUB), ROW_SUB), :]


def _dispatch_kernel(dest_ref, h_ref, xs_ref, zero_ref, sem, *, tm, n_rows):
    pad = zero_ref.shape[0]

    @pl.when(pl.program_id(0) == 0)
    def _():
        zero_ref[...] = jnp.zeros_like(zero_ref)
        tail = pltpu.make_async_copy(zero_ref, xs_ref.at[pl.ds(n_rows * ROW_SUB, pad), :], sem)
        tail.start()
        tail.wait()

    def body(t, _):
        for k in range(TOP_K):
            pltpu.make_async_copy(_packed_row(h_ref, t), _packed_row(xs_ref, dest_ref[k, t]), sem).start(priority=k % 2)
        return 0

    lax.fori_loop(0, tm, body, 0)
    for _ in range(TOP_K):
        pltpu.make_async_copy(h_ref, xs_ref.at[pl.ds(0, tm * ROW_SUB), :], sem).wait()


def dispatch(dest, hp, *, pad_rows, tm=512):
    kk, n = dest.shape
    n_rows = n * kk
    return pl.pallas_call(
        functools.partial(_dispatch_kernel, tm=tm, n_rows=n_rows),
        grid=(n // tm,),
        in_specs=[pl.BlockSpec((kk, tm), lambda i: (0, i), memory_space=pltpu.SMEM),
                  pl.BlockSpec((tm * ROW_SUB, LANES), lambda i: (i, 0))],
        out_specs=pl.BlockSpec(memory_space=pl.ANY),
        out_shape=jax.ShapeDtypeStruct(((n_rows + pad_rows) * ROW_SUB, LANES), hp.dtype),
        scratch_shapes=[pltpu.VMEM((pad_rows * ROW_SUB, LANES), hp.dtype), pltpu.SemaphoreType.DMA],
        compiler_params=_cparams("arbitrary"),
        name="dispatch",
    )(dest, hp)


EXPERT_CHUNK = 256
EXPERT_LOOKAHEAD = 3
EXPERT_IN_BUFS = EXPERT_LOOKAHEAD + 1
EXPERT_OUT_BUFS = 3


def _expert_kernel(first_ref, nch_ref, crow_ref, total_ref, xs_ref, wg_ref, wu_ref, wd_ref, ys_ref,
                   xbuf, ybuf, wgu_bf, wd_bf, h_ref, sem_in, sem_out, waited_ref):
    e = pl.program_id(0)
    chunk = EXPERT_CHUNK
    first, n_chunks, total = first_ref[e], nch_ref[e], total_ref[0]

    def rows(g):
        return pl.ds(pl.multiple_of(crow_ref[g] * ROW_SUB, ROW_SUB), chunk * ROW_SUB)

    def in_copy(g):
        slot = lax.rem(g, EXPERT_IN_BUFS)
        return pltpu.make_async_copy(xs_ref.at[rows(g), :], xbuf.at[slot], sem_in.at[slot])

    def out_copy(g):
        slot = lax.rem(g, EXPERT_OUT_BUFS)
        return pltpu.make_async_copy(ybuf.at[slot], ys_ref.at[rows(g), :], sem_out.at[slot])

    def wait_writes_through(g):
        def wait_one(j, carry):
            out_copy(j).wait()
            return carry

        lax.fori_loop(waited_ref[0] + 1, g + 1, wait_one, 0)
        waited_ref[0] = jnp.maximum(waited_ref[0], g)

    @pl.when(e == 0)
    def _():
        waited_ref[0] = -1
        for g in range(EXPERT_LOOKAHEAD):
            @pl.when(g < total)
            def _():
                in_copy(g).start()

    @pl.when(n_chunks > 0)
    def _():
        wgu_bf[:, :EXPERT_DIM] = wg_ref[0].astype(wgu_bf.dtype)
        wgu_bf[:, EXPERT_DIM:] = wu_ref[0].astype(wgu_bf.dtype)
        wd_bf[...] = wd_ref[0].astype(wd_bf.dtype)

    def acquire(g):
        in_copy(g).wait()

        @pl.when(g + EXPERT_LOOKAHEAD < total)
        def _():
            in_copy(g + EXPERT_LOOKAHEAD).start()

    def hidden(g):
        bf = jnp.bfloat16
        xa, xb = _unpack_rows(xbuf.at[lax.rem(g, EXPERT_IN_BUFS)], chunk)
        gu = _dot(xa.astype(bf), wgu_bf[:ROW_WORDS, :]) + _dot(xb.astype(bf), wgu_bf[ROW_WORDS:, :])
        gate, up = gu[:, :EXPERT_DIM], gu[:, EXPERT_DIM:]
        return (gate * _sigmoid(gate) * up).astype(bf)

    def pack(g, h):
        _pack_rows(ybuf.at[lax.rem(g, EXPERT_OUT_BUFS)], _dot(h, wd_bf[...]))

    def release(g):
        @pl.when(g == first)
        def _():
            wait_writes_through(g - 1)

        out_copy(g).start()

    @pl.when(n_chunks > 0)
    def _():
        acquire(first)
        h_ref[...] = hidden(first)

    def body(c, carry):
        g = first + c
        acquire(g)
        wait_writes_through(g - 1 - EXPERT_OUT_BUFS)
        h_prev = h_ref[...]
        pack(g - 1, h_prev)
        h_ref[...] = hidden(g)
        release(g - 1)
        return carry

    lax.fori_loop(1, n_chunks, body, 0)

    @pl.when(n_chunks > 0)
    def _():
        last = first + n_chunks - 1
        wait_writes_through(last - EXPERT_OUT_BUFS)
        pack(last, h_ref[...])
        release(last)

    @pl.when(e == pl.num_programs(0) - 1)
    def _():
        wait_writes_through(total - 1)
        ybuf[0] = jnp.zeros(ybuf.shape[1:], ybuf.dtype)
        pad = pl.ds(ys_ref.shape[0] - chunk * ROW_SUB, chunk * ROW_SUB)
        tail = pltpu.make_async_copy(ybuf.at[0], ys_ref.at[pad, :], sem_out.at[0])
        tail.start()
        tail.wait()


def expert_chunks(starts, counts, n_rows):
    chunk = EXPERT_CHUNK
    max_chunks = n_rows // chunk + N_EXPERTS
    nch = (counts + (chunk - 1)) // chunk
    ends = jnp.cumsum(nch)
    first = ends - nch
    g = jnp.arange(max_chunks, dtype=jnp.int32)
    owner = jnp.minimum(jnp.sum(g[:, None] >= ends[None, :], axis=1), N_EXPERTS - 1)
    onehot = (owner[:, None] == jnp.arange(N_EXPERTS)[None, :]).astype(jnp.int32)
    crow = (onehot * (starts - first * chunk)[None, :]).sum(axis=1) + g * chunk
    crow = jnp.where(g < ends[-1], crow, 0).astype(jnp.int32)
    return first.astype(jnp.int32), nch.astype(jnp.int32), crow, ends[-1:].astype(jnp.int32)


def expert_ffn(starts, counts, xs, w_gate, w_up, w_down):
    d = D_MODEL
    n_rows = xs.shape[0] // ROW_SUB - EXPERT_CHUNK
    tables = expert_chunks(starts, counts, n_rows)
    chunk_shape = (EXPERT_CHUNK * ROW_SUB, LANES)
    weights = lambda shape: pl.BlockSpec(shape, lambda e, *_: (e, 0, 0))
    grid_spec = pltpu.PrefetchScalarGridSpec(
        num_scalar_prefetch=len(tables),
        grid=(N_EXPERTS,),
        in_specs=[pl.BlockSpec(memory_space=pl.ANY),
                  weights((1, d, EXPERT_DIM)), weights((1, d, EXPERT_DIM)), weights((1, EXPERT_DIM, d))],
        out_specs=pl.BlockSpec(memory_space=pl.ANY),
        scratch_shapes=[pltpu.VMEM((EXPERT_IN_BUFS,) + chunk_shape, xs.dtype),
                        pltpu.VMEM((EXPERT_OUT_BUFS,) + chunk_shape, xs.dtype),
                        pltpu.VMEM((d, 2 * EXPERT_DIM), jnp.bfloat16), pltpu.VMEM((EXPERT_DIM, d), jnp.bfloat16),
                        pltpu.VMEM((EXPERT_CHUNK, EXPERT_DIM), jnp.bfloat16),
                        pltpu.SemaphoreType.DMA((EXPERT_IN_BUFS,)), pltpu.SemaphoreType.DMA((EXPERT_OUT_BUFS,)),
                        pltpu.SMEM((1,), jnp.int32)],
    )
    return pl.pallas_call(
        _expert_kernel,
        grid_spec=grid_spec,
        out_shape=jax.ShapeDtypeStruct(xs.shape, xs.dtype),
        compiler_params=_cparams("arbitrary"),
        name="expert_ffn",
    )(*tables, xs, w_gate, w_up, w_down)


COMBINE_AHEAD = 2
COMBINE_SLOTS = COMBINE_AHEAD + 1
COMBINE_GROUP = 32


def _combine_kernel(*refs, tm):
    head_refs, (dest_ref, ys_ref, gate_ref, sh_ref, x1_ref, mod_ref, g_ref, o_ref, buf, sem) = (
        refs[:COMBINE_AHEAD], refs[COMBINE_AHEAD:])
    i, n = pl.program_id(0), pl.num_programs(0)

    def gather(src_ref, slot, t):
        for k in range(TOP_K):
            pltpu.make_async_copy(_packed_row(ys_ref, src_ref[k, t]), _packed_row(buf.at[slot, k], t),
                                  sem.at[slot]).start(priority=k % 2)

    def wait_tile(slot):
        for k in range(TOP_K):
            pltpu.make_async_copy(ys_ref.at[pl.ds(0, tm * ROW_SUB), :], buf.at[slot, k], sem.at[slot]).wait()

    @pl.when(i == 0)
    def _():
        for a, head in enumerate(head_refs):
            def body(t, carry, head=head, a=a):
                gather(head, a, t)
                return carry

            lax.fori_loop(0, tm, body, 0)

    def step(cur):
        ahead = (cur + COMBINE_AHEAD) % COMBINE_SLOTS
        wait_tile(cur)
        g_post = g_ref[...]
        gt2 = mod_ref[0, 5:6, :]
        for t0 in range(0, tm, COMBINE_GROUP):
            for t in range(t0, t0 + COMBINE_GROUP):
                gather(dest_ref, ahead, t)
            rows = slice(t0, t0 + COMBINE_GROUP)
            gate = gate_ref[rows, :]
            first = jnp.zeros((COMBINE_GROUP, ROW_WORDS), jnp.float32)
            second = jnp.zeros((COMBINE_GROUP, ROW_WORDS), jnp.float32)
            for k in range(TOP_K):
                packed = buf.at[cur, k, pl.ds(t0 * ROW_SUB, COMBINE_GROUP * ROW_SUB), :]
                ya, yb = _unpack_rows(packed, COMBINE_GROUP)
                first = first + ya * gate[:, k:k + 1]
                second = second + yb * gate[:, k:k + 1]
            y = sh_ref[rows, :] + jnp.concatenate([first, second], axis=1)
            o_ref[rows, :] = x1_ref[rows, :] + gt2 * _rms_norm(y, g_post)

        @pl.when(i == n - 1)
        def _():
            for a in range(1, COMBINE_SLOTS):
                wait_tile((cur + a) % COMBINE_SLOTS)

    for slot in range(COMBINE_SLOTS):
        pl.when(lax.rem(i, COMBINE_SLOTS) == slot)(functools.partial(step, slot))


def combine(dest, ys, gate_nk, shared, x1, mod, g_post_ffn, *, seq, tm=128):
    n, d = x1.shape
    per_batch = seq // tm
    n_tiles = n // tm
    tok = pl.BlockSpec((tm, d), lambda i: (i, 0))
    dest_tile = lambda index: pl.BlockSpec((TOP_K, tm), index, memory_space=pltpu.SMEM)
    head_specs = [dest_tile(lambda i, a=a: (0, a)) for a in range(COMBINE_AHEAD)]
    return pl.pallas_call(
        functools.partial(_combine_kernel, tm=tm),
        grid=(n_tiles,),
        in_specs=head_specs + [dest_tile(lambda i: (0, jnp.minimum(i + COMBINE_AHEAD, n_tiles - 1))),
                               pl.BlockSpec(memory_space=pl.ANY),
                               pl.BlockSpec((tm, TOP_K), lambda i: (i, 0)),
                               tok, tok,
                               pl.BlockSpec((1, N_MOD, d), lambda i: (i // per_batch, 0, 0)),
                               pl.BlockSpec((1, d), lambda i: (0, 0))],
        out_specs=tok,
        out_shape=jax.ShapeDtypeStruct((n, d), jnp.float32),
        scratch_shapes=[pltpu.VMEM((COMBINE_SLOTS, TOP_K, tm * ROW_SUB, LANES), ys.dtype),
                        pltpu.SemaphoreType.DMA((COMBINE_SLOTS,))],
        compiler_params=_cparams("arbitrary"),
        name="combine",
    )(*([dest] * COMBINE_AHEAD), dest, ys, gate_nk, shared, x1, mod, g_post_ffn.reshape(1, d))


def _layer(x, mod, rel_bias, g_pre_mix, w_in, w_branch_a, w_branch_b, w_out, g_post_mix, g_pre_ffn, w_router,
           router_bias, w_exp_gate, w_exp_up, w_exp_down, w_sh_gate, w_sh_up, w_sh_down, g_post_ffn):
    b, s, d = x.shape
    assert d == D_MODEL and w_exp_gate.shape == (N_EXPERTS, D_MODEL, EXPERT_DIM), (x.shape, w_exp_gate.shape)
    assert s % SB_TILE == 0 and all(s % (r * Q_BLOCK) == 0 for _, r in DIL_CONFIGS), s
    bf = jnp.bfloat16
    scale = HEAD_DIM ** -0.5
    sb_q, sb_k, sb_v, dq, dk, dv, w_ga, w_gb = jnp.split(
        w_in, np.cumsum((SB_WIDTH,) * 3 + (DIL_WIDTH,) * 3 + (d,)).tolist(), axis=1)
    groups = lambda w: [w[:, g * DIL_GROUP_WIDTH:(g + 1) * DIL_GROUP_WIDTH] for g in range(len(DIL_CONFIGS))]
    w_rows = jnp.concatenate([sb_k] + [w * scale for w in groups(dq)] + groups(dk) + groups(dv) + [w_ga, w_gb],
                             axis=1).astype(bf)
    w_cols = jnp.concatenate([sb_q * (scale * LOG2E), sb_v], axis=1).T.astype(bf)
    qT, k, vT3, *rest = in_projection(x, mod, g_pre_mix, w_rows, w_cols)
    dqs, dks, dvs, gate_a, gate_b = rest[0:3], rest[3:6], rest[6:9], rest[9], rest[10]
    out_a = sb_attention(qT, k, vT3)
    o_g, lse_g = [], []
    for g, (window, dilation) in enumerate(DIL_CONFIGS):
        bias = dilated_bias(rel_bias[:, g * DIL_HEADS_PER_GROUP:(g + 1) * DIL_HEADS_PER_GROUP], window, dilation)
        o, lse = dilated_attention(dqs[g], dks[g], dvs[g], bias, dilation)
        o_g.append(o)
        lse_g.append(lse)
    x1, hp, shared, logits_t = mix_out(
        out_a, o_g, lse_g, gate_a, gate_b, x, mod, w_branch_a.astype(bf), w_branch_b.astype(bf), w_out.astype(bf),
        g_post_mix, g_pre_ffn, w_router.T.astype(bf), jnp.concatenate([w_sh_gate, w_sh_up], axis=1).astype(bf),
        w_sh_down.astype(bf))
    n = b * s
    idx, gate, rank, counts = route(logits_t, router_bias)
    counts = counts[:, 0]
    starts = jnp.cumsum(counts) - counts
    dest = slot_destinations(idx, rank, starts)
    xs = dispatch(dest, hp, pad_rows=EXPERT_CHUNK)
    ys = expert_ffn(starts, counts, xs, w_exp_gate, w_exp_up, w_exp_down)
    out = combine(dest, ys, gate.T, shared.reshape(n, d), x1.reshape(n, d), mod, g_post_ffn, seq=s)
    return out.reshape(b, s, d)


def kernel(x, c, rel_bias, w_ada, b_ada, g_pre_mix, w_in, w_branch_a, w_branch_b, w_out, g_post_mix, g_pre_ffn,
           w_router, router_bias, w_exp_gate, w_exp_up, w_exp_down, w_sh_gate, w_sh_up, w_sh_down, g_post_ffn):
    depth = w_in.shape[0]
    for layer in range(depth):
        mod = ada_modulation(c, w_ada[layer], b_ada[layer]).reshape(x.shape[0], N_MOD, x.shape[2])
        x = _layer(x, mod, rel_bias, g_pre_mix[layer], w_in[layer], w_branch_a[layer], w_branch_b[layer], w_out[layer],
                   g_post_mix[layer], g_pre_ffn[layer], w_router[layer], router_bias[layer], w_exp_gate[layer],
                   w_exp_up[layer], w_exp_down[layer], w_sh_gate[layer], w_sh_up[layer], w_sh_down[layer],
                   g_post_ffn[layer])
    return x
```

```python
import functools
import math

import jax
import jax.numpy as jnp
import numpy as np
from jax import lax
from jax.experimental import pallas as pl
from jax.experimental.pallas import tpu as pltpu

D_MODEL = 1024
HEAD_DIM = 64
SB_HEADS = 8
SB_WIDTH = SB_HEADS * HEAD_DIM
DIL_CONFIGS = ((128, 1), (512, 4), (2048, 16))
DIL_HEADS_PER_GROUP = 4
DIL_GROUP_WIDTH = DIL_HEADS_PER_GROUP * HEAD_DIM
DIL_WIDTH = DIL_GROUP_WIDTH * len(DIL_CONFIGS)
Q_BLOCK = 128
N_BUCKETS = 32
MAX_DISTANCE = 2048
N_EXPERTS = 256
TOP_K = 8
N_GROUPS = 8
TOP_K_GROUPS = 4
EXPERT_DIM = 256
ROUTED_SCALE = 2.5
N_MOD = 6
EPS = 1e-6

LANES = 128
VMEM_LIMIT_BYTES = 56 * 1024 * 1024
NEG_INF = float("-inf")
LOG2E = math.log2(math.e)


def _cparams(*semantics):
    return pltpu.CompilerParams(dimension_semantics=semantics, vmem_limit_bytes=VMEM_LIMIT_BYTES)


def _rms_norm(v, g):
    return v * lax.rsqrt(jnp.mean(v * v, axis=-1, keepdims=True) + EPS) * g


def _sigmoid(v):
    return 1.0 / (1.0 + jnp.exp(-v))


def _dot(a, b):
    return jnp.dot(a, b, preferred_element_type=jnp.float32)


def _dot_nt(a, b):
    return lax.dot_general(a, b, (((1,), (1,)), ((), ())), preferred_element_type=jnp.float32)


def _split_bf16(v):
    hi = v.astype(jnp.bfloat16)
    lo = (v - hi.astype(jnp.float32)).astype(jnp.bfloat16)
    return hi, lo


ROW_WORDS = D_MODEL // 2
ROW_SUB = ROW_WORDS // LANES


def _pack_rows(ref, v):
    m = v.shape[0]
    bits = pltpu.bitcast(v.astype(jnp.bfloat16).astype(jnp.float32), jnp.uint32)
    words = bits[:, :ROW_WORDS] | (bits[:, ROW_WORDS:] >> 16)
    for j in range(ROW_SUB):
        ref[pl.ds(j, m, stride=ROW_SUB), :] = words[:, j * LANES:(j + 1) * LANES]


def _unpack_rows(ref, m):
    words = jnp.concatenate([ref[pl.ds(j, m, stride=ROW_SUB), :] for j in range(ROW_SUB)], axis=1)
    first = pltpu.bitcast(words & jnp.uint32(0xFFFF0000), jnp.float32)
    second = pltpu.bitcast(words << 16, jnp.float32)
    return first, second


def _ada_kernel(c_ref, w_ref, b_ref, o_ref):
    c = c_ref[...]
    a_hi, a_lo = _split_bf16(c * _sigmoid(c))
    w_hi, w_lo = _split_bf16(w_ref[...])
    o_ref[...] = _dot(a_hi, w_hi) + _dot(a_lo, w_hi) + _dot(a_hi, w_lo) + b_ref[...]


def ada_modulation(c, w_ada, b_ada, *, tn=1536):
    b, d = c.shape
    n = w_ada.shape[1]
    return pl.pallas_call(
        _ada_kernel,
        grid=(n // tn,),
        in_specs=[
            pl.BlockSpec((b, d), lambda j: (0, 0)),
            pl.BlockSpec((d, tn), lambda j: (0, j)),
            pl.BlockSpec((1, tn), lambda j: (0, j)),
        ],
        out_specs=pl.BlockSpec((b, tn), lambda j: (0, j)),
        out_shape=jax.ShapeDtypeStruct((b, n), jnp.float32),
        compiler_params=_cparams("parallel"),
        name="ada_modulation",
    )(c, w_ada, b_ada.reshape(1, n))


def _in_proj_kernel(x_ref, mod_ref, g_ref, w_rows_ref, w_cols_ref, qT_ref, k_ref, vT_ref, *rest, tm):
    dil_refs, ga_ref, gb_ref, stage_ref = rest[:9], rest[9], rest[10], rest[11]
    h = _rms_norm(x_ref[0], g_ref[...]) * (1.0 + mod_ref[0, 1:2, :]) + mod_ref[0, 0:1, :]
    hb = h.astype(jnp.bfloat16)
    off = 0
    k_ref[0] = _dot(hb, w_rows_ref[:, off:off + SB_WIDTH]).astype(k_ref.dtype)
    off += SB_WIDTH
    for n, r in enumerate(dil_refs):
        y = _dot(hb, w_rows_ref[:, off:off + DIL_GROUP_WIDTH])
        off += DIL_GROUP_WIDTH
        dilation = DIL_CONFIGS[n % len(DIL_CONFIGS)][1]
        if dilation == 1:
            r[0] = y.astype(r.dtype)
            continue
        for j in range(DIL_GROUP_WIDTH // LANES):
            stage_ref[j] = y[:, j * LANES:(j + 1) * LANES]
        for c in range(dilation):
            for j in range(DIL_GROUP_WIDTH // LANES):
                col = c * DIL_GROUP_WIDTH + j * LANES
                r[0, :, col:col + LANES] = stage_ref[j, pl.ds(c, tm // dilation, stride=dilation), :].astype(r.dtype)
    for r in (ga_ref, gb_ref):
        r[0] = _sigmoid(_dot(hb, w_rows_ref[:, off:off + D_MODEL])).astype(r.dtype)
        off += D_MODEL
    qT_ref[0] = _dot_nt(w_cols_ref[0:SB_WIDTH, :], hb).astype(qT_ref.dtype)
    vT = _dot_nt(w_cols_ref[SB_WIDTH:2 * SB_WIDTH, :], hb).astype(vT_ref.dtype)
    for p in range(SB_WIDTH // LANES):
        for jb in range(tm // LANES):
            vT_ref[0, p, jb] = vT[p * LANES:(p + 1) * LANES, jb * LANES:(jb + 1) * LANES]


def _dilated_shape(b, s, dilation, dtype):
    return jax.ShapeDtypeStruct((b, s // dilation, dilation * DIL_GROUP_WIDTH), dtype)


def _dilated_spec(tm, dilation):
    return pl.BlockSpec((1, tm // dilation, dilation * DIL_GROUP_WIDTH), lambda bi, i: (bi, i, 0))


def in_projection(x, mod, g_pre, w_rows, w_cols, *, tm=256):
    b, s, d = x.shape
    nt = s // tm
    bf = jnp.bfloat16
    tok = lambda width: pl.BlockSpec((1, tm, width), lambda bi, i: (bi, i, 0))
    out_shapes = [
        jax.ShapeDtypeStruct((b, SB_WIDTH, s), bf),
        jax.ShapeDtypeStruct((b, s, SB_WIDTH), bf),
        jax.ShapeDtypeStruct((b, SB_WIDTH // LANES, s // LANES, LANES, LANES), bf),
    ] + [_dilated_shape(b, s, r, bf) for _, r in DIL_CONFIGS] * 3 + [jax.ShapeDtypeStruct((b, s, d), bf)] * 2
    out_specs = [
        pl.BlockSpec((1, SB_WIDTH, tm), lambda bi, i: (bi, 0, i)),
        tok(SB_WIDTH),
        pl.BlockSpec((1, SB_WIDTH // LANES, tm // LANES, LANES, LANES), lambda bi, i: (bi, 0, i, 0, 0)),
    ] + [_dilated_spec(tm, r) for _, r in DIL_CONFIGS] * 3 + [tok(d)] * 2
    return pl.pallas_call(
        functools.partial(_in_proj_kernel, tm=tm),
        grid=(b, nt),
        in_specs=[
            tok(d),
            pl.BlockSpec((1, N_MOD, d), lambda bi, i: (bi, 0, 0)),
            pl.BlockSpec((1, d), lambda bi, i: (0, 0)),
            pl.BlockSpec(w_rows.shape, lambda bi, i: (0, 0)),
            pl.BlockSpec(w_cols.shape, lambda bi, i: (0, 0)),
        ],
        out_specs=out_specs,
        out_shape=out_shapes,
        scratch_shapes=[pltpu.VMEM((DIL_GROUP_WIDTH // LANES, tm, LANES), jnp.float32)],
        compiler_params=_cparams("parallel", "parallel"),
        name="in_projection",
    )(x, mod, g_pre.reshape(1, d), w_rows, w_cols)


SB_TILE = 512
SB_SUB = 256
SB_STEP_HEADS = 4


def _sb_chunk(kc, vc, q_h, tri, csum, past):
    z = _dot(kc, q_h)
    neg_abs = pltpu.bitcast(pltpu.bitcast(z, jnp.uint32) | jnp.uint32(0x80000000), jnp.float32)
    drop = jnp.maximum(z, 0.0) + jnp.log(1.0 + jnp.exp2(neg_abs)) * LOG2E
    if past is not None:
        drop = jnp.where(past, drop, 0.0)
    weights = [None] * (SB_TILE // SB_SUB)
    for sb in reversed(range(SB_TILE // SB_SUB)):
        rows = slice(sb * SB_SUB, (sb + 1) * SB_SUB)
        d = drop[rows]
        later = _dot(tri, d.astype(jnp.bfloat16))
        w = jnp.exp2(z[rows] - d - later - csum)
        if past is not None:
            w = jnp.where(past[rows], w, 0.0)
        weights[sb] = w.astype(jnp.bfloat16)
        csum = csum + jnp.sum(d, axis=0, keepdims=True)
    w = jnp.concatenate(weights, axis=0)
    t = SB_TILE
    part = jnp.concatenate([_dot(vc[h * HEAD_DIM:(h + 1) * HEAD_DIM], w[:, h * t:(h + 1) * t])
                            for h in range(SB_STEP_HEADS)], axis=0)
    return part, csum


def _sb_attn_kernel(qT_ref, k_ref, vT_ref, tri_ref, o_ref, acc_ref, csum_ref):
    i = pl.program_id(2)
    t = SB_TILE
    q_all = qT_ref[0]
    row = lax.broadcasted_iota(jnp.int32, q_all.shape, 0)
    zero = jnp.zeros_like(q_all)
    q_heads = jnp.concatenate(
        [jnp.where(jnp.logical_and(row >= h * HEAD_DIM, row < (h + 1) * HEAD_DIM), q_all, zero)
         for h in range(SB_STEP_HEADS)], axis=1)
    tri = tri_ref[...]
    blocks = t // LANES

    def chunk(c, past, first):
        kc = k_ref[0, pl.ds(pl.multiple_of(c * t, t), t), :]
        vc = jnp.concatenate(
            [jnp.concatenate([vT_ref[0, p, c * blocks + jb] for jb in range(blocks)], axis=1)
             for p in range(SB_STEP_HEADS // 2)], axis=0)
        csum = jnp.zeros((1, SB_STEP_HEADS * t), jnp.float32) if first else csum_ref[...]
        part, csum = _sb_chunk(kc, vc, q_heads, tri, csum, past)
        acc_ref[...] = part if first else acc_ref[...] + part
        csum_ref[...] = csum

    key = lax.broadcasted_iota(jnp.int32, (t, SB_STEP_HEADS * t), 0)
    query = lax.broadcasted_iota(jnp.int32, (t, SB_STEP_HEADS * t), 1)
    chunk(i, key < jnp.bitwise_and(query, t - 1), True)

    def body(it, carry):
        chunk(i - 1 - it, None, False)
        return carry

    lax.fori_loop(0, i, body, 0)
    o_ref[0] = acc_ref[...].T.astype(o_ref.dtype)


def sb_attention(qT, k, vT3):
    b, width, s = qT.shape
    t = SB_TILE
    hw = SB_STEP_HEADS * HEAD_DIM
    tri = jnp.asarray(np.triu(np.ones((SB_SUB, SB_SUB), np.float32), k=1), jnp.bfloat16)
    return pl.pallas_call(
        _sb_attn_kernel,
        grid=(b, width // hw, s // t),
        in_specs=[
            pl.BlockSpec((1, hw, t), lambda bi, p, i: (bi, p, i)),
            pl.BlockSpec((1, s, hw), lambda bi, p, i: (bi, 0, p)),
            pl.BlockSpec((1, hw // LANES, s // LANES, LANES, LANES), lambda bi, p, i: (bi, p, 0, 0, 0)),
            pl.BlockSpec((SB_SUB, SB_SUB), lambda bi, p, i: (0, 0)),
        ],
        out_specs=pl.BlockSpec((1, t, hw), lambda bi, p, i: (bi, i, p)),
        out_shape=jax.ShapeDtypeStruct((b, s, width), jnp.bfloat16),
        scratch_shapes=[pltpu.VMEM((hw, t), jnp.float32), pltpu.VMEM((1, SB_STEP_HEADS * t), jnp.float32)],
        compiler_params=_cparams("parallel", "parallel", "arbitrary"),
        name="sb_attention",
    )(qT, k, vT3, tri)


DIL_BLOCKS_PER_STEP = 4


def _dil_attn_kernel(q_ref, kp_ref, kc_ref, vp_ref, vc_ref, bias_ref, o_ref, lse_ref, *, n_blocks, n_cols):
    m = pl.program_id(2)
    kk_all = jnp.concatenate([kp_ref[0], kc_ref[0]], axis=0)
    vv_all = jnp.concatenate([vp_ref[0], vc_ref[0]], axis=0)
    lane = lax.broadcasted_iota(jnp.int32, (Q_BLOCK, DIL_GROUP_WIDTH), 1)
    in_head = [jnp.logical_and(lane >= h * HEAD_DIM, lane < (h + 1) * HEAD_DIM) for h in range(DIL_HEADS_PER_GROUP)]
    bias = bias_ref[...]
    key = lax.broadcasted_iota(jnp.int32, bias.shape, 1)
    for j, cc in [(j, cc) for cc in range(n_cols) for j in range(n_blocks)]:
        q_rows = slice(j * Q_BLOCK, (j + 1) * Q_BLOCK)
        cols = slice(cc * DIL_GROUP_WIDTH, (cc + 1) * DIL_GROUP_WIDTH)
        q4 = q_ref[0, q_rows, cols]
        kk = kk_all[j * Q_BLOCK:(j + 2) * Q_BLOCK, cols]
        vv = vv_all[j * Q_BLOCK:(j + 2) * Q_BLOCK, cols]
        zero = jnp.zeros_like(q4)
        q_stack = jnp.concatenate([jnp.where(mask, q4, zero) for mask in in_head], axis=0)
        logits = _dot_nt(q_stack, kk) + bias
        if j == 0:
            logits = jnp.where(jnp.logical_or(key >= Q_BLOCK, m > 0), logits, NEG_INF)
        mx = jnp.max(logits, axis=1, keepdims=True)
        p = jnp.exp(logits - mx)
        den = jnp.sum(p, axis=1, keepdims=True)
        pv = _dot(p.astype(jnp.bfloat16), vv) / den
        lse_rows = mx + jnp.log(den)
        out = jnp.zeros(q4.shape, jnp.float32)
        lse = jnp.zeros(q4.shape, jnp.float32)
        for h, mask in enumerate(in_head):
            rows = slice(h * Q_BLOCK, (h + 1) * Q_BLOCK)
            out = jnp.where(mask, pv[rows], out)
            lse = jnp.where(mask, lse_rows[rows], lse)
        o_ref[0, q_rows, cols] = out.astype(o_ref.dtype)
        lse_ref[0, q_rows, cols] = lse


def dilated_attention(q, k, v, bias, dilation):
    b, length, width = q.shape
    gw = DIL_GROUP_WIDTH
    n_blocks = min(DIL_BLOCKS_PER_STEP, length // Q_BLOCK)
    n_cols = min(DIL_BLOCKS_PER_STEP // n_blocks, width // gw)
    cur = pl.BlockSpec((1, n_blocks * Q_BLOCK, n_cols * gw), lambda bi, c, m: (bi, m, c))
    prev = pl.BlockSpec((1, Q_BLOCK, n_cols * gw), lambda bi, c, m: (bi, jnp.maximum(m * n_blocks - 1, 0), c))
    return pl.pallas_call(
        functools.partial(_dil_attn_kernel, n_blocks=n_blocks, n_cols=n_cols),
        grid=(b, width // (n_cols * gw), length // (n_blocks * Q_BLOCK)),
        in_specs=[cur, prev, cur, prev, cur,
                  pl.BlockSpec((DIL_HEADS_PER_GROUP * Q_BLOCK, 2 * Q_BLOCK), lambda bi, c, m: (0, 0))],
        out_specs=[cur, cur],
        out_shape=[jax.ShapeDtypeStruct((b, length, width), jnp.bfloat16),
                   jax.ShapeDtypeStruct((b, length, width), jnp.float32)],
        compiler_params=_cparams("parallel", "parallel", "arbitrary"),
        name=f"dilated_attention_r{dilation}",
    )(q, k, k, v, v, bias.reshape(DIL_HEADS_PER_GROUP * Q_BLOCK, 2 * Q_BLOCK))


def _t5_causal_bucket(dist):
    max_exact = N_BUCKETS // 2
    d = jnp.maximum(dist, 1).astype(jnp.float32)
    large = max_exact + (jnp.log(d / max_exact) / math.log(MAX_DISTANCE / max_exact)
                         * (N_BUCKETS - max_exact)).astype(jnp.int32)
    large = jnp.minimum(large, N_BUCKETS - 1)
    return jnp.where(dist < max_exact, dist, large)


def dilated_bias(rel_bias_group, window, dilation):
    qi = jnp.arange(Q_BLOCK)[:, None]
    kj = jnp.arange(2 * Q_BLOCK)[None, :]
    dist_sub = qi + Q_BLOCK - kj
    bucket = _t5_causal_bucket(jnp.maximum(dist_sub, 0) * dilation)
    onehot = (bucket[..., None] == jnp.arange(N_BUCKETS)).astype(jnp.float32)
    bias = jnp.einsum("qkn,nh->hqk", onehot, rel_bias_group.astype(jnp.float32), precision=lax.Precision.HIGHEST)
    valid = (dist_sub >= 0) & (dist_sub <= window // dilation)
    return jnp.where(valid[None], bias, NEG_INF)


def _mix_out_kernel(oa_ref, o0_ref, o1_ref, o2_ref, l0_ref, l1_ref, l2_ref, ga_ref, gb_ref, x_ref, mod_ref,
                    wa_ref, wb_ref, wo_ref, gpost_ref, gpre_ref, wr_ref, wsgu_ref, wsd_ref,
                    x1_ref, hp_ref, sh_ref, lt_ref, stage_ref):
    f32 = jnp.float32
    tm = x_ref.shape[1]

    def token_major(ref, dilation):
        if dilation == 1:
            return ref[0].astype(f32)
        slabs = DIL_GROUP_WIDTH // LANES
        for c in range(dilation):
            for j in range(slabs):
                col = c * DIL_GROUP_WIDTH + j * LANES
                stage_ref[j, pl.ds(c, tm // dilation, stride=dilation), :] = ref[0, :, col:col + LANES].astype(f32)
        return jnp.concatenate([stage_ref[j] for j in range(slabs)], axis=1)

    dils = [r for _, r in DIL_CONFIGS]
    l0, l1, l2 = (token_major(ref, r) for ref, r in zip((l0_ref, l1_ref, l2_ref), dils))
    mx = jnp.maximum(jnp.maximum(l0, l1), l2)
    e0, e1, e2 = jnp.exp(l0 - mx), jnp.exp(l1 - mx), jnp.exp(l2 - mx)
    den = e0 + e1 + e2
    o0, o1, o2 = (token_major(ref, r) for ref, r in zip((o0_ref, o1_ref, o2_ref), dils))
    out_b = (e0 / den) * o0 + (e1 / den) * o1 + (e2 / den) * o2
    ya = _dot(oa_ref[0], wa_ref[...])
    yb = _dot(out_b.astype(jnp.bfloat16), wb_ref[...])
    merged = ga_ref[0].astype(f32) * ya + gb_ref[0].astype(f32) * yb
    y = _dot(merged.astype(jnp.bfloat16), wo_ref[...])
    x1 = x_ref[0] + mod_ref[0, 2:3, :] * _rms_norm(y, gpost_ref[...])
    x1_ref[0] = x1
    h = _rms_norm(x1, gpre_ref[...]) * (1.0 + mod_ref[0, 4:5, :]) + mod_ref[0, 3:4, :]
    hb = h.astype(jnp.bfloat16)
    _pack_rows(hp_ref, h)
    lt_ref[...] = _dot_nt(wr_ref[...], hb)
    gu = _dot(hb, wsgu_ref[...])
    g, u = gu[:, :EXPERT_DIM], gu[:, EXPERT_DIM:]
    sh_ref[0] = _dot((g * _sigmoid(g) * u).astype(jnp.bfloat16), wsd_ref[...])


def mix_out(out_a, o_g, lse_g, gate_a, gate_b, x, mod, w_a, w_b, w_o, g_post, g_pre_ffn, w_rT, w_sgu, w_sd, *, tm=512):
    b, s, d = x.shape
    tok = lambda width: pl.BlockSpec((1, tm, width), lambda bi, i: (bi, i, 0))
    full = lambda a: pl.BlockSpec(a.shape, lambda bi, i: (0,) * a.ndim)
    nt = s // tm
    g_post, g_pre_ffn = g_post.reshape(1, d), g_pre_ffn.reshape(1, d)
    weights = (w_a, w_b, w_o, g_post, g_pre_ffn, w_rT, w_sgu, w_sd)
    return pl.pallas_call(
        _mix_out_kernel,
        grid=(b, nt),
        in_specs=[tok(SB_WIDTH)] + [_dilated_spec(tm, r) for _, r in DIL_CONFIGS] * 2 + [tok(d)] * 3
        + [pl.BlockSpec((1, N_MOD, d), lambda bi, i: (bi, 0, 0))] + [full(w) for w in weights],
        out_specs=[tok(d), pl.BlockSpec((tm * ROW_SUB, LANES), lambda bi, i: (bi * nt + i, 0)), tok(d),
                   pl.BlockSpec((N_EXPERTS, tm), lambda bi, i: (0, bi * nt + i))],
        out_shape=[jax.ShapeDtypeStruct((b, s, d), jnp.float32),
                   jax.ShapeDtypeStruct((b * s * ROW_SUB, LANES), jnp.uint32),
                   jax.ShapeDtypeStruct((b, s, d), jnp.float32),
                   jax.ShapeDtypeStruct((N_EXPERTS, b * s), jnp.float32)],
        scratch_shapes=[pltpu.VMEM((DIL_GROUP_WIDTH // LANES, tm, LANES), jnp.float32)],
        compiler_params=_cparams("parallel", "parallel"),
        name="mix_out",
    )(out_a, *o_g, *lse_g, gate_a, gate_b, x, mod, *weights)


def _route_kernel(lt_ref, bias_ref, tri_ref, idx_ref, gate_ref, rank_ref, cnt_ref, base_ref, *, tn):
    f32 = jnp.float32

    @pl.when(pl.program_id(0) == 0)
    def _():
        base_ref[...] = jnp.zeros_like(base_ref)

    per_group = N_EXPERTS // N_GROUPS
    scores = _sigmoid(lt_ref[...])
    sel = scores + bias_ref[...]
    sel_groups = [sel[g * per_group:(g + 1) * per_group] for g in range(N_GROUPS)]
    ri = lax.broadcasted_iota(jnp.int32, (per_group, tn), 0)
    group_scores = []
    for sg in sel_groups:
        m1 = jnp.max(sg, axis=0, keepdims=True)
        first = jnp.min(jnp.where(sg == m1, ri, per_group), axis=0, keepdims=True)
        group_scores.append(m1 + jnp.max(jnp.where(ri == first, NEG_INF, sg), axis=0, keepdims=True))
    cur = jnp.concatenate(group_scores, axis=0)
    gi = lax.broadcasted_iota(jnp.int32, cur.shape, 0)
    keep = jnp.zeros(cur.shape, jnp.float32)
    for _ in range(TOP_K_GROUPS):
        mg = jnp.max(cur, axis=0, keepdims=True)
        pick = gi == jnp.min(jnp.where(cur == mg, gi, N_GROUPS), axis=0, keepdims=True)
        keep = jnp.where(pick, 1.0, keep)
        cur = jnp.where(pick, NEG_INF, cur)
    cur = jnp.concatenate([jnp.where(keep[g:g + 1, :] > 0.0, sg, NEG_INF) for g, sg in enumerate(sel_groups)], axis=0)
    ei = lax.broadcasted_iota(jnp.int32, cur.shape, 0)
    idxs, gates = [], []
    candidates = cur
    for _ in range(TOP_K):
        mk = jnp.max(cur, axis=0, keepdims=True)
        fk = jnp.min(jnp.where(cur == mk, ei, N_EXPERTS), axis=0, keepdims=True)
        pick = ei == fk
        idxs.append(fk)
        gates.append(jnp.sum(jnp.where(pick, scores, 0.0), axis=0, keepdims=True))
        cur = jnp.where(pick, NEG_INF, cur)
    gate = jnp.concatenate(gates, axis=0)
    gate_ref[...] = gate / jnp.sum(gate, axis=0, keepdims=True) * ROUTED_SCALE
    idx_ref[...] = jnp.concatenate(idxs, axis=0)
    onehot = jnp.where(cur == NEG_INF, 1.0, 0.0) - jnp.where(candidates == NEG_INF, 1.0, 0.0)
    earlier = _dot(onehot.astype(jnp.bfloat16), tri_ref[...]) + base_ref[...]
    ranks = [jnp.sum(jnp.where(ei == fk, earlier, 0.0), axis=0, keepdims=True) for fk in idxs]
    rank_ref[...] = jnp.concatenate(ranks, axis=0).astype(jnp.int32)
    base = base_ref[...] + jnp.sum(onehot, axis=1, keepdims=True)
    base_ref[...] = base
    cnt_ref[...] = jnp.broadcast_to(base, cnt_ref.shape).astype(jnp.int32)


def route(logits_t, router_bias, *, tn=512):
    ne, n = logits_t.shape
    tri = jnp.asarray(np.triu(np.ones((tn, tn), np.float32), k=1), jnp.bfloat16)
    slot = pl.BlockSpec((TOP_K, tn), lambda i: (0, i))
    return pl.pallas_call(
        functools.partial(_route_kernel, tn=tn),
        grid=(n // tn,),
        in_specs=[pl.BlockSpec((ne, tn), lambda i: (0, i)),
                  pl.BlockSpec((ne, 1), lambda i: (0, 0)),
                  pl.BlockSpec((tn, tn), lambda i: (0, 0))],
        out_specs=[slot, slot, slot, pl.BlockSpec((ne, LANES), lambda i: (0, 0))],
        out_shape=[jax.ShapeDtypeStruct((TOP_K, n), jnp.int32),
                   jax.ShapeDtypeStruct((TOP_K, n), jnp.float32),
                   jax.ShapeDtypeStruct((TOP_K, n), jnp.int32),
                   jax.ShapeDtypeStruct((ne, LANES), jnp.int32)],
        scratch_shapes=[pltpu.VMEM((ne, 1), jnp.float32)],
        compiler_params=_cparams("arbitrary"),
        name="route",
    )(logits_t, router_bias.reshape(ne, 1), tri)


def _dest_kernel(idx_ref, rank_ref, start_ref, dest_ref):
    idx = idx_ref[...]
    ei = lax.broadcasted_iota(jnp.int32, (N_EXPERTS, idx.shape[1]), 0)
    start = start_ref[...]
    rows = [jnp.sum(jnp.where(ei == idx[k:k + 1, :], start, 0.0), axis=0, keepdims=True) for k in range(TOP_K)]
    dest_ref[...] = jnp.concatenate(rows, axis=0).astype(jnp.int32) + rank_ref[...]


def slot_destinations(idx, rank, starts, *, tn=512):
    kk, n = idx.shape
    slot = pl.BlockSpec((kk, tn), lambda i: (0, i))
    return pl.pallas_call(
        _dest_kernel,
        grid=(n // tn,),
        in_specs=[slot, slot, pl.BlockSpec((N_EXPERTS, 1), lambda i: (0, 0))],
        out_specs=slot,
        out_shape=jax.ShapeDtypeStruct((kk, n), jnp.int32),
        compiler_params=_cparams("parallel"),
        name="slot_destinations",
    )(idx, rank, starts.astype(jnp.float32).reshape(N_EXPERTS, 1))


def _packed_row(ref, row):
    start = row * ROW_SUB
    return ref.at[pl.ds(start if isinstance(row, int) else pl.multiple_of(start, ROW_SUB), ROW_SUB), :]


def _dispatch_kernel(dest_ref, h_ref, xs_ref, zero_ref, sem, *, tm, n_rows):
    pad = zero_ref.shape[0]

    @pl.when(pl.program_id(0) == 0)
    def _():
        zero_ref[...] = jnp.zeros_like(zero_ref)
        tail = pltpu.make_async_copy(zero_ref, xs_ref.at[pl.ds(n_rows * ROW_SUB, pad), :], sem)
        tail.start()
        tail.wait()

    def body(t, _):
        for k in range(TOP_K):
            pltpu.make_async_copy(_packed_row(h_ref, t), _packed_row(xs_ref, dest_ref[k, t]), sem).start(priority=k % 2)
        return 0

    lax.fori_loop(0, tm, body, 0)
    for _ in range(TOP_K):
        pltpu.make_async_copy(h_ref, xs_ref.at[pl.ds(0, tm * ROW_SUB), :], sem).wait()


def dispatch(dest, hp, *, pad_rows, tm=512):
    kk, n = dest.shape
    n_rows = n * kk
    return pl.pallas_call(
        functools.partial(_dispatch_kernel, tm=tm, n_rows=n_rows),
        grid=(n // tm,),
        in_specs=[pl.BlockSpec((kk, tm), lambda i: (0, i), memory_space=pltpu.SMEM),
                  pl.BlockSpec((tm * ROW_SUB, LANES), lambda i: (i, 0))],
        out_specs=pl.BlockSpec(memory_space=pl.ANY),
        out_shape=jax.ShapeDtypeStruct(((n_rows + pad_rows) * ROW_SUB, LANES), hp.dtype),
        scratch_shapes=[pltpu.VMEM((pad_rows * ROW_SUB, LANES), hp.dtype), pltpu.SemaphoreType.DMA],
        compiler_params=_cparams("arbitrary"),
        name="dispatch",
    )(dest, hp)


EXPERT_CHUNK = 512
EXPERT_LOOKAHEAD = 3
EXPERT_IN_BUFS = EXPERT_LOOKAHEAD + 1
EXPERT_OUT_BUFS = 3


def _expert_kernel(first_ref, nch_ref, crow_ref, total_ref, xs_ref, wg_ref, wu_ref, wd_ref, ys_ref,
                   xbuf, ybuf, wgu_bf, wd_bf, h_ref, sem_in, sem_out, waited_ref):
    e = pl.program_id(0)
    chunk = EXPERT_CHUNK
    first, n_chunks, total = first_ref[e], nch_ref[e], total_ref[0]

    def rows(g):
        return pl.ds(pl.multiple_of(crow_ref[g] * ROW_SUB, ROW_SUB), chunk * ROW_SUB)

    def in_copy(g):
        slot = lax.rem(g, EXPERT_IN_BUFS)
        return pltpu.make_async_copy(xs_ref.at[rows(g), :], xbuf.at[slot], sem_in.at[slot])

    def out_copy(g):
        slot = lax.rem(g, EXPERT_OUT_BUFS)
        return pltpu.make_async_copy(ybuf.at[slot], ys_ref.at[rows(g), :], sem_out.at[slot])

    def wait_writes_through(g):
        def wait_one(j, carry):
            out_copy(j).wait()
            return carry

        lax.fori_loop(waited_ref[0] + 1, g + 1, wait_one, 0)
        waited_ref[0] = jnp.maximum(waited_ref[0], g)

    @pl.when(e == 0)
    def _():
        waited_ref[0] = -1
        for g in range(EXPERT_LOOKAHEAD):
            @pl.when(g < total)
            def _():
                in_copy(g).start()

    @pl.when(n_chunks > 0)
    def _():
        wgu_bf[:, :EXPERT_DIM] = wg_ref[0].astype(wgu_bf.dtype)
        wgu_bf[:, EXPERT_DIM:] = wu_ref[0].astype(wgu_bf.dtype)
        wd_bf[...] = wd_ref[0].astype(wd_bf.dtype)

    def acquire(g):
        in_copy(g).wait()

        @pl.when(g + EXPERT_LOOKAHEAD < total)
        def _():
            in_copy(g + EXPERT_LOOKAHEAD).start()

    def hidden(g):
        bf = jnp.bfloat16
        xa, xb = _unpack_rows(xbuf.at[lax.rem(g, EXPERT_IN_BUFS)], chunk)
        gu = _dot(xa.astype(bf), wgu_bf[:ROW_WORDS, :]) + _dot(xb.astype(bf), wgu_bf[ROW_WORDS:, :])
        gate, up = gu[:, :EXPERT_DIM], gu[:, EXPERT_DIM:]
        return (gate * _sigmoid(gate) * up).astype(bf)

    def pack(g, h):
        _pack_rows(ybuf.at[lax.rem(g, EXPERT_OUT_BUFS)], _dot(h, wd_bf[...]))

    def release(g):
        @pl.when(g == first)
        def _():
            wait_writes_through(g - 1)

        out_copy(g).start()

    @pl.when(n_chunks > 0)
    def _():
        acquire(first)
        h_ref[...] = hidden(first)

    def body(c, carry):
        g = first + c
        acquire(g)
        wait_writes_through(g - 1 - EXPERT_OUT_BUFS)
        h_prev = h_ref[...]
        pack(g - 1, h_prev)
        h_ref[...] = hidden(g)
        release(g - 1)
        return carry

    lax.fori_loop(1, n_chunks, body, 0)

    @pl.when(n_chunks > 0)
    def _():
        last = first + n_chunks - 1
        wait_writes_through(last - EXPERT_OUT_BUFS)
        pack(last, h_ref[...])
        release(last)

    @pl.when(e == pl.num_programs(0) - 1)
    def _():
        wait_writes_through(total - 1)
        ybuf[0] = jnp.zeros(ybuf.shape[1:], ybuf.dtype)
        pad = pl.ds(ys_ref.shape[0] - chunk * ROW_SUB, chunk * ROW_SUB)
        tail = pltpu.make_async_copy(ybuf.at[0], ys_ref.at[pad, :], sem_out.at[0])
        tail.start()
        tail.wait()


def expert_chunks(starts, counts, n_rows):
    chunk = EXPERT_CHUNK
    max_chunks = n_rows // chunk + N_EXPERTS
    nch = (counts + (chunk - 1)) // chunk
    ends = jnp.cumsum(nch)
    first = ends - nch
    g = jnp.arange(max_chunks, dtype=jnp.int32)
    owner = jnp.minimum(jnp.sum(g[:, None] >= ends[None, :], axis=1), N_EXPERTS - 1)
    onehot = (owner[:, None] == jnp.arange(N_EXPERTS)[None, :]).astype(jnp.int32)
    crow = (onehot * (starts - first * chunk)[None, :]).sum(axis=1) + g * chunk
    crow = jnp.where(g < ends[-1], crow, 0).astype(jnp.int32)
    return first.astype(jnp.int32), nch.astype(jnp.int32), crow, ends[-1:].astype(jnp.int32)


def expert_ffn(starts, counts, xs, w_gate, w_up, w_down):
    d = D_MODEL
    n_rows = xs.shape[0] // ROW_SUB - EXPERT_CHUNK
    tables = expert_chunks(starts, counts, n_rows)
    chunk_shape = (EXPERT_CHUNK * ROW_SUB, LANES)
    weights = lambda shape: pl.BlockSpec(shape, lambda e, *_: (e, 0, 0))
    grid_spec = pltpu.PrefetchScalarGridSpec(
        num_scalar_prefetch=len(tables),
        grid=(N_EXPERTS,),
        in_specs=[pl.BlockSpec(memory_space=pl.ANY),
                  weights((1, d, EXPERT_DIM)), weights((1, d, EXPERT_DIM)), weights((1, EXPERT_DIM, d))],
        out_specs=pl.BlockSpec(memory_space=pl.ANY),
        scratch_shapes=[pltpu.VMEM((EXPERT_IN_BUFS,) + chunk_shape, xs.dtype),
                        pltpu.VMEM((EXPERT_OUT_BUFS,) + chunk_shape, xs.dtype),
                        pltpu.VMEM((d, 2 * EXPERT_DIM), jnp.bfloat16), pltpu.VMEM((EXPERT_DIM, d), jnp.bfloat16),
                        pltpu.VMEM((EXPERT_CHUNK, EXPERT_DIM), jnp.bfloat16),
                        pltpu.SemaphoreType.DMA((EXPERT_IN_BUFS,)), pltpu.SemaphoreType.DMA((EXPERT_OUT_BUFS,)),
                        pltpu.SMEM((1,), jnp.int32)],
    )
    return pl.pallas_call(
        _expert_kernel,
        grid_spec=grid_spec,
        out_shape=jax.ShapeDtypeStruct(xs.shape, xs.dtype),
        compiler_params=_cparams("arbitrary"),
        name="expert_ffn",
    )(*tables, xs, w_gate, w_up, w_down)


COMBINE_AHEAD = 2
COMBINE_SLOTS = COMBINE_AHEAD + 1
COMBINE_GROUP = 32


def _combine_kernel(*refs, tm):
    head_refs, (dest_ref, ys_ref, gate_ref, sh_ref, x1_ref, mod_ref, g_ref, o_ref, buf, sem) = (
        refs[:COMBINE_AHEAD], refs[COMBINE_AHEAD:])
    i, n = pl.program_id(0), pl.num_programs(0)

    def gather(src_ref, slot, t):
        for k in range(TOP_K):
            pltpu.make_async_copy(_packed_row(ys_ref, src_ref[k, t]), _packed_row(buf.at[slot, k], t),
                                  sem.at[slot]).start(priority=k % 2)

    def wait_tile(slot):
        for k in range(TOP_K):
            pltpu.make_async_copy(ys_ref.at[pl.ds(0, tm * ROW_SUB), :], buf.at[slot, k], sem.at[slot]).wait()

    @pl.when(i == 0)
    def _():
        for a, head in enumerate(head_refs):
            def body(t, carry, head=head, a=a):
                gather(head, a, t)
                return carry

            lax.fori_loop(0, tm, body, 0)

    def step(cur):
        ahead = (cur + COMBINE_AHEAD) % COMBINE_SLOTS
        wait_tile(cur)
        g_post = g_ref[...]
        gt2 = mod_ref[0, 5:6, :]
        for t0 in range(0, tm, COMBINE_GROUP):
            for t in range(t0, t0 + COMBINE_GROUP):
                gather(dest_ref, ahead, t)
            rows = slice(t0, t0 + COMBINE_GROUP)
            gate = gate_ref[rows, :]
            first = jnp.zeros((COMBINE_GROUP, ROW_WORDS), jnp.float32)
            second = jnp.zeros((COMBINE_GROUP, ROW_WORDS), jnp.float32)
            for k in range(TOP_K):
                packed = buf.at[cur, k, pl.ds(t0 * ROW_SUB, COMBINE_GROUP * ROW_SUB), :]
                ya, yb = _unpack_rows(packed, COMBINE_GROUP)
                first = first + ya * gate[:, k:k + 1]
                second = second + yb * gate[:, k:k + 1]
            y = sh_ref[rows, :] + jnp.concatenate([first, second], axis=1)
            o_ref[rows, :] = x1_ref[rows, :] + gt2 * _rms_norm(y, g_post)

        @pl.when(i == n - 1)
        def _():
            for a in range(1, COMBINE_SLOTS):
                wait_tile((cur + a) % COMBINE_SLOTS)

    for slot in range(COMBINE_SLOTS):
        pl.when(lax.rem(i, COMBINE_SLOTS) == slot)(functools.partial(step, slot))


def combine(dest, ys, gate_nk, shared, x1, mod, g_post_ffn, *, seq, tm=128):
    n, d = x1.shape
    per_batch = seq // tm
    n_tiles = n // tm
    tok = pl.BlockSpec((tm, d), lambda i: (i, 0))
    dest_tile = lambda index: pl.BlockSpec((TOP_K, tm), index, memory_space=pltpu.SMEM)
    head_specs = [dest_tile(lambda i, a=a: (0, a)) for a in range(COMBINE_AHEAD)]
    return pl.pallas_call(
        functools.partial(_combine_kernel, tm=tm),
        grid=(n_tiles,),
        in_specs=head_specs + [dest_tile(lambda i: (0, jnp.minimum(i + COMBINE_AHEAD, n_tiles - 1))),
                               pl.BlockSpec(memory_space=pl.ANY),
                               pl.BlockSpec((tm, TOP_K), lambda i: (i, 0)),
                               tok, tok,
                               pl.BlockSpec((1, N_MOD, d), lambda i: (i // per_batch, 0, 0)),
                               pl.BlockSpec((1, d), lambda i: (0, 0))],
        out_specs=tok,
        out_shape=jax.ShapeDtypeStruct((n, d), jnp.float32),
        scratch_shapes=[pltpu.VMEM((COMBINE_SLOTS, TOP_K, tm * ROW_SUB, LANES), ys.dtype),
                        pltpu.SemaphoreType.DMA((COMBINE_SLOTS,))],
        compiler_params=_cparams("arbitrary"),
        name="combine",
    )(*([dest] * COMBINE_AHEAD), dest, ys, gate_nk, shared, x1, mod, g_post_ffn.reshape(1, d))


def _layer(x, mod, rel_bias, g_pre_mix, w_in, w_branch_a, w_branch_b, w_out, g_post_mix, g_pre_ffn, w_router,
           router_bias, w_exp_gate, w_exp_up, w_exp_down, w_sh_gate, w_sh_up, w_sh_down, g_post_ffn):
    b, s, d = x.shape
    assert d == D_MODEL and w_exp_gate.shape == (N_EXPERTS, D_MODEL, EXPERT_DIM), (x.shape, w_exp_gate.shape)
    assert s % SB_TILE == 0 and all(s % (r * Q_BLOCK) == 0 for _, r in DIL_CONFIGS), s
    bf = jnp.bfloat16
    scale = HEAD_DIM ** -0.5
    sb_q, sb_k, sb_v, dq, dk, dv, w_ga, w_gb = jnp.split(
        w_in, np.cumsum((SB_WIDTH,) * 3 + (DIL_WIDTH,) * 3 + (d,)).tolist(), axis=1)
    groups = lambda w: [w[:, g * DIL_GROUP_WIDTH:(g + 1) * DIL_GROUP_WIDTH] for g in range(len(DIL_CONFIGS))]
    w_rows = jnp.concatenate([sb_k] + [w * scale for w in groups(dq)] + groups(dk) + groups(dv) + [w_ga, w_gb],
                             axis=1).astype(bf)
    w_cols = jnp.concatenate([sb_q * (scale * LOG2E), sb_v], axis=1).T.astype(bf)
    qT, k, vT3, *rest = in_projection(x, mod, g_pre_mix, w_rows, w_cols)
    dqs, dks, dvs, gate_a, gate_b = rest[0:3], rest[3:6], rest[6:9], rest[9], rest[10]
    out_a = sb_attention(qT, k, vT3)
    o_g, lse_g = [], []
    for g, (window, dilation) in enumerate(DIL_CONFIGS):
        bias = dilated_bias(rel_bias[:, g * DIL_HEADS_PER_GROUP:(g + 1) * DIL_HEADS_PER_GROUP], window, dilation)
        o, lse = dilated_attention(dqs[g], dks[g], dvs[g], bias, dilation)
        o_g.append(o)
        lse_g.append(lse)
    x1, hp, shared, logits_t = mix_out(
        out_a, o_g, lse_g, gate_a, gate_b, x, mod, w_branch_a.astype(bf), w_branch_b.astype(bf), w_out.astype(bf),
        g_post_mix, g_pre_ffn, w_router.T.astype(bf), jnp.concatenate([w_sh_gate, w_sh_up], axis=1).astype(bf),
        w_sh_down.astype(bf))
    n = b * s
    idx, gate, rank, counts = route(logits_t, router_bias)
    counts = counts[:, 0]
    starts = jnp.cumsum(counts) - counts
    dest = slot_destinations(idx, rank, starts)
    xs = dispatch(dest, hp, pad_rows=EXPERT_CHUNK)
    ys = expert_ffn(starts, counts, xs, w_exp_gate, w_exp_up, w_exp_down)
    out = combine(dest, ys, gate.T, shared.reshape(n, d), x1.reshape(n, d), mod, g_post_ffn, seq=s)
    return out.reshape(b, s, d)


def kernel(x, c, rel_bias, w_ada, b_ada, g_pre_mix, w_in, w_branch_a, w_branch_b, w_out, g_post_mix, g_pre_ffn,
           w_router, router_bias, w_exp_gate, w_exp_up, w_exp_down, w_sh_gate, w_sh_up, w_sh_down, g_post_ffn):
    depth = w_in.shape[0]
    for layer in range(depth):
        mod = ada_modulation(c, w_ada[layer], b_ada[layer]).reshape(x.shape[0], N_MOD, x.shape[2])
        x = _layer(x, mod, rel_bias, g_pre_mix[layer], w_in[layer], w_branch_a[layer], w_branch_b[layer], w_out[layer],
                   g_post_mix[layer], g_pre_ffn[layer], w_router[layer], router_bias[layer], w_exp_gate[layer],
                   w_exp_up[layer], w_exp_down[layer], w_sh_gate[layer], w_sh_up[layer], w_sh_down[layer],
                   g_post_ffn[layer])
    return x
```

```python
import functools
import math

import jax
import jax.numpy as jnp
import numpy as np
from jax import lax
from jax.experimental import pallas as pl
from jax.experimental.pallas import tpu as pltpu

D_MODEL = 1024
HEAD_DIM = 64
SB_HEADS = 8
SB_WIDTH = SB_HEADS * HEAD_DIM
DIL_CONFIGS = ((128, 1), (512, 4), (2048, 16))
DIL_HEADS_PER_GROUP = 4
DIL_GROUP_WIDTH = DIL_HEADS_PER_GROUP * HEAD_DIM
DIL_WIDTH = DIL_GROUP_WIDTH * len(DIL_CONFIGS)
Q_BLOCK = 128
N_BUCKETS = 32
MAX_DISTANCE = 2048
N_EXPERTS = 256
TOP_K = 8
N_GROUPS = 8
TOP_K_GROUPS = 4
EXPERT_DIM = 256
ROUTED_SCALE = 2.5
N_MOD = 6
EPS = 1e-6

LANES = 128
VMEM_LIMIT_BYTES = 56 * 1024 * 1024
NEG_INF = float("-inf")
LOG2E = math.log2(math.e)


def _cparams(*semantics):
    return pltpu.CompilerParams(dimension_semantics=semantics, vmem_limit_bytes=VMEM_LIMIT_BYTES)


def _rms_norm(v, g):
    return v * lax.rsqrt(jnp.mean(v * v, axis=-1, keepdims=True) + EPS) * g


def _sigmoid(v):
    return 1.0 / (1.0 + jnp.exp(-v))


def _dot(a, b):
    return jnp.dot(a, b, preferred_element_type=jnp.float32)


def _dot_nt(a, b):
    return lax.dot_general(a, b, (((1,), (1,)), ((), ())), preferred_element_type=jnp.float32)


def _split_bf16(v):
    hi = v.astype(jnp.bfloat16)
    lo = (v - hi.astype(jnp.float32)).astype(jnp.bfloat16)
    return hi, lo


ROW_WORDS = D_MODEL // 2
ROW_SUB = ROW_WORDS // LANES


def _pack_rows(ref, v):
    m = v.shape[0]
    bits = pltpu.bitcast(v.astype(jnp.bfloat16).astype(jnp.float32), jnp.uint32)
    words = bits[:, :ROW_WORDS] | (bits[:, ROW_WORDS:] >> 16)
    for j in range(ROW_SUB):
        ref[pl.ds(j, m, stride=ROW_SUB), :] = words[:, j * LANES:(j + 1) * LANES]


def _unpack_rows(ref, m):
    words = jnp.concatenate([ref[pl.ds(j, m, stride=ROW_SUB), :] for j in range(ROW_SUB)], axis=1)
    first = pltpu.bitcast(words & jnp.uint32(0xFFFF0000), jnp.float32)
    second = pltpu.bitcast(words << 16, jnp.float32)
    return first, second


def _ada_kernel(c_ref, w_ref, b_ref, o_ref):
    c = c_ref[...]
    a_hi, a_lo = _split_bf16(c * _sigmoid(c))
    w_hi, w_lo = _split_bf16(w_ref[...])
    o_ref[...] = _dot(a_hi, w_hi) + _dot(a_lo, w_hi) + _dot(a_hi, w_lo) + b_ref[...]


def ada_modulation(c, w_ada, b_ada, *, tn=1536):
    b, d = c.shape
    n = w_ada.shape[1]
    return pl.pallas_call(
        _ada_kernel,
        grid=(n // tn,),
        in_specs=[
            pl.BlockSpec((b, d), lambda j: (0, 0)),
            pl.BlockSpec((d, tn), lambda j: (0, j)),
            pl.BlockSpec((1, tn), lambda j: (0, j)),
        ],
        out_specs=pl.BlockSpec((b, tn), lambda j: (0, j)),
        out_shape=jax.ShapeDtypeStruct((b, n), jnp.float32),
        compiler_params=_cparams("parallel"),
        name="ada_modulation",
    )(c, w_ada, b_ada.reshape(1, n))


def _in_proj_kernel(x_ref, mod_ref, g_ref, w_rows_ref, w_cols_ref, qT_ref, k_ref, vT_ref, *rest, tm):
    dil_refs, ga_ref, gb_ref, stage_ref = rest[:9], rest[9], rest[10], rest[11]
    h = _rms_norm(x_ref[0], g_ref[...]) * (1.0 + mod_ref[0, 1:2, :]) + mod_ref[0, 0:1, :]
    hb = h.astype(jnp.bfloat16)
    off = 0
    k_ref[0] = _dot(hb, w_rows_ref[:, off:off + SB_WIDTH]).astype(k_ref.dtype)
    off += SB_WIDTH
    for n, r in enumerate(dil_refs):
        y = _dot(hb, w_rows_ref[:, off:off + DIL_GROUP_WIDTH])
        off += DIL_GROUP_WIDTH
        dilation = DIL_CONFIGS[n % len(DIL_CONFIGS)][1]
        if dilation == 1:
            r[0] = y.astype(r.dtype)
            continue
        for j in range(DIL_GROUP_WIDTH // LANES):
            stage_ref[j] = y[:, j * LANES:(j + 1) * LANES]
        for c in range(dilation):
            for j in range(DIL_GROUP_WIDTH // LANES):
                col = c * DIL_GROUP_WIDTH + j * LANES
                r[0, :, col:col + LANES] = stage_ref[j, pl.ds(c, tm // dilation, stride=dilation), :].astype(r.dtype)
    for r in (ga_ref, gb_ref):
        r[0] = _sigmoid(_dot(hb, w_rows_ref[:, off:off + D_MODEL])).astype(r.dtype)
        off += D_MODEL
    qT_ref[0] = _dot_nt(w_cols_ref[0:SB_WIDTH, :], hb).astype(qT_ref.dtype)
    vT = _dot_nt(w_cols_ref[SB_WIDTH:2 * SB_WIDTH, :], hb).astype(vT_ref.dtype)
    for p in range(SB_WIDTH // LANES):
        for jb in range(tm // LANES):
            vT_ref[0, p, jb] = vT[p * LANES:(p + 1) * LANES, jb * LANES:(jb + 1) * LANES]


def _dilated_shape(b, s, dilation, dtype):
    return jax.ShapeDtypeStruct((b, s // dilation, dilation * DIL_GROUP_WIDTH), dtype)


def _dilated_spec(tm, dilation):
    return pl.BlockSpec((1, tm // dilation, dilation * DIL_GROUP_WIDTH), lambda bi, i: (bi, i, 0))


def in_projection(x, mod, g_pre, w_rows, w_cols, *, tm=256):
    b, s, d = x.shape
    nt = s // tm
    bf = jnp.bfloat16
    tok = lambda width: pl.BlockSpec((1, tm, width), lambda bi, i: (bi, i, 0))
    out_shapes = [
        jax.ShapeDtypeStruct((b, SB_WIDTH, s), bf),
        jax.ShapeDtypeStruct((b, s, SB_WIDTH), bf),
        jax.ShapeDtypeStruct((b, SB_WIDTH // LANES, s // LANES, LANES, LANES), bf),
    ] + [_dilated_shape(b, s, r, bf) for _, r in DIL_CONFIGS] * 3 + [jax.ShapeDtypeStruct((b, s, d), bf)] * 2
    out_specs = [
        pl.BlockSpec((1, SB_WIDTH, tm), lambda bi, i: (bi, 0, i)),
        tok(SB_WIDTH),
        pl.BlockSpec((1, SB_WIDTH // LANES, tm // LANES, LANES, LANES), lambda bi, i: (bi, 0, i, 0, 0)),
    ] + [_dilated_spec(tm, r) for _, r in DIL_CONFIGS] * 3 + [tok(d)] * 2
    return pl.pallas_call(
        functools.partial(_in_proj_kernel, tm=tm),
        grid=(b, nt),
        in_specs=[
            tok(d),
            pl.BlockSpec((1, N_MOD, d), lambda bi, i: (bi, 0, 0)),
            pl.BlockSpec((1, d), lambda bi, i: (0, 0)),
            pl.BlockSpec(w_rows.shape, lambda bi, i: (0, 0)),
            pl.BlockSpec(w_cols.shape, lambda bi, i: (0, 0)),
        ],
        out_specs=out_specs,
        out_shape=out_shapes,
        scratch_shapes=[pltpu.VMEM((DIL_GROUP_WIDTH // LANES, tm, LANES), jnp.float32)],
        compiler_params=_cparams("parallel", "parallel"),
        name="in_projection",
    )(x, mod, g_pre.reshape(1, d), w_rows, w_cols)


SB_TILE = 512
SB_SUB = 256
SB_STEP_HEADS = 4


def _sb_chunk(kc, vc, q_h, tri, csum, past):
    z = _dot(kc, q_h)
    neg_abs = pltpu.bitcast(pltpu.bitcast(z, jnp.uint32) | jnp.uint32(0x80000000), jnp.float32)
    drop = jnp.maximum(z, 0.0) + jnp.log(1.0 + jnp.exp2(neg_abs)) * LOG2E
    if past is not None:
        drop = jnp.where(past, drop, 0.0)
    weights = [None] * (SB_TILE // SB_SUB)
    for sb in reversed(range(SB_TILE // SB_SUB)):
        rows = slice(sb * SB_SUB, (sb + 1) * SB_SUB)
        d = drop[rows]
        later = _dot(tri, d.astype(jnp.bfloat16))
        w = jnp.exp2(z[rows] - d - later - csum)
        if past is not None:
            w = jnp.where(past[rows], w, 0.0)
        weights[sb] = w.astype(jnp.bfloat16)
        csum = csum + jnp.sum(d, axis=0, keepdims=True)
    w = jnp.concatenate(weights, axis=0)
    t = SB_TILE
    part = jnp.concatenate([_dot(vc[h * HEAD_DIM:(h + 1) * HEAD_DIM], w[:, h * t:(h + 1) * t])
                            for h in range(SB_STEP_HEADS)], axis=0)
    return part, csum


def _sb_attn_kernel(qT_ref, k_ref, vT_ref, tri_ref, o_ref, acc_ref, csum_ref):
    i = pl.program_id(2)
    t = SB_TILE
    q_all = qT_ref[0]
    row = lax.broadcasted_iota(jnp.int32, q_all.shape, 0)
    zero = jnp.zeros_like(q_all)
    q_heads = jnp.concatenate(
        [jnp.where(jnp.logical_and(row >= h * HEAD_DIM, row < (h + 1) * HEAD_DIM), q_all, zero)
         for h in range(SB_STEP_HEADS)], axis=1)
    tri = tri_ref[...]
    blocks = t // LANES

    def chunk(c, past, first):
        kc = k_ref[0, pl.ds(pl.multiple_of(c * t, t), t), :]
        vc = jnp.concatenate(
            [jnp.concatenate([vT_ref[0, p, c * blocks + jb] for jb in range(blocks)], axis=1)
             for p in range(SB_STEP_HEADS // 2)], axis=0)
        csum = jnp.zeros((1, SB_STEP_HEADS * t), jnp.float32) if first else csum_ref[...]
        part, csum = _sb_chunk(kc, vc, q_heads, tri, csum, past)
        acc_ref[...] = part if first else acc_ref[...] + part
        csum_ref[...] = csum

    key = lax.broadcasted_iota(jnp.int32, (t, SB_STEP_HEADS * t), 0)
    query = lax.broadcasted_iota(jnp.int32, (t, SB_STEP_HEADS * t), 1)
    chunk(i, key < jnp.bitwise_and(query, t - 1), True)

    def body(it, carry):
        chunk(i - 1 - it, None, False)
        return carry

    lax.fori_loop(0, i, body, 0)
    o_ref[0] = acc_ref[...].T.astype(o_ref.dtype)


def sb_attention(qT, k, vT3):
    b, width, s = qT.shape
    t = SB_TILE
    hw = SB_STEP_HEADS * HEAD_DIM
    tri = jnp.asarray(np.triu(np.ones((SB_SUB, SB_SUB), np.float32), k=1), jnp.bfloat16)
    return pl.pallas_call(
        _sb_attn_kernel,
        grid=(b, width // hw, s // t),
        in_specs=[
            pl.BlockSpec((1, hw, t), lambda bi, p, i: (bi, p, i)),
            pl.BlockSpec((1, s, hw), lambda bi, p, i: (bi, 0, p)),
            pl.BlockSpec((1, hw // LANES, s // LANES, LANES, LANES), lambda bi, p, i: (bi, p, 0, 0, 0)),
            pl.BlockSpec((SB_SUB, SB_SUB), lambda bi, p, i: (0, 0)),
        ],
        out_specs=pl.BlockSpec((1, t, hw), lambda bi, p, i: (bi, i, p)),
        out_shape=jax.ShapeDtypeStruct((b, s, width), jnp.bfloat16),
        scratch_shapes=[pltpu.VMEM((hw, t), jnp.float32), pltpu.VMEM((1, SB_STEP_HEADS * t), jnp.float32)],
        compiler_params=_cparams("parallel", "parallel", "arbitrary"),
        name="sb_attention",
    )(qT, k, vT3, tri)


DIL_BLOCKS_PER_STEP = 8


def _dil_attn_kernel(q_ref, kp_ref, kc_ref, vp_ref, vc_ref, bias_ref, o_ref, lse_ref, *, n_blocks, n_cols):
    m = pl.program_id(2)
    kk_all = jnp.concatenate([kp_ref[0], kc_ref[0]], axis=0)
    vv_all = jnp.concatenate([vp_ref[0], vc_ref[0]], axis=0)
    lane = lax.broadcasted_iota(jnp.int32, (Q_BLOCK, DIL_GROUP_WIDTH), 1)
    in_head = [jnp.logical_and(lane >= h * HEAD_DIM, lane < (h + 1) * HEAD_DIM) for h in range(DIL_HEADS_PER_GROUP)]
    bias = bias_ref[...]
    key = lax.broadcasted_iota(jnp.int32, bias.shape, 1)
    for j, cc in [(j, cc) for cc in range(n_cols) for j in range(n_blocks)]:
        q_rows = slice(j * Q_BLOCK, (j + 1) * Q_BLOCK)
        cols = slice(cc * DIL_GROUP_WIDTH, (cc + 1) * DIL_GROUP_WIDTH)
        q4 = q_ref[0, q_rows, cols]
        kk = kk_all[j * Q_BLOCK:(j + 2) * Q_BLOCK, cols]
        vv = vv_all[j * Q_BLOCK:(j + 2) * Q_BLOCK, cols]
        zero = jnp.zeros_like(q4)
        q_stack = jnp.concatenate([jnp.where(mask, q4, zero) for mask in in_head], axis=0)
        logits = _dot_nt(q_stack, kk) + bias
        if j == 0:
            logits = jnp.where(jnp.logical_or(key >= Q_BLOCK, m > 0), logits, NEG_INF)
        mx = jnp.max(logits, axis=1, keepdims=True)
        p = jnp.exp(logits - mx)
        den = jnp.sum(p, axis=1, keepdims=True)
        pv = _dot(p.astype(jnp.bfloat16), vv) / den
        lse_rows = mx + jnp.log(den)
        out = jnp.zeros(q4.shape, jnp.float32)
        lse = jnp.zeros(q4.shape, jnp.float32)
        for h, mask in enumerate(in_head):
            rows = slice(h * Q_BLOCK, (h + 1) * Q_BLOCK)
            out = jnp.where(mask, pv[rows], out)
            lse = jnp.where(mask, lse_rows[rows], lse)
        o_ref[0, q_rows, cols] = out.astype(o_ref.dtype)
        lse_ref[0, q_rows, cols] = lse


def dilated_attention(q, k, v, bias, dilation):
    b, length, width = q.shape
    gw = DIL_GROUP_WIDTH
    n_blocks = min(DIL_BLOCKS_PER_STEP, length // Q_BLOCK)
    n_cols = min(DIL_BLOCKS_PER_STEP // n_blocks, width // gw)
    cur = pl.BlockSpec((1, n_blocks * Q_BLOCK, n_cols * gw), lambda bi, c, m: (bi, m, c))
    prev = pl.BlockSpec((1, Q_BLOCK, n_cols * gw), lambda bi, c, m: (bi, jnp.maximum(m * n_blocks - 1, 0), c))
    return pl.pallas_call(
        functools.partial(_dil_attn_kernel, n_blocks=n_blocks, n_cols=n_cols),
        grid=(b, width // (n_cols * gw), length // (n_blocks * Q_BLOCK)),
        in_specs=[cur, prev, cur, prev, cur,
                  pl.BlockSpec((DIL_HEADS_PER_GROUP * Q_BLOCK, 2 * Q_BLOCK), lambda bi, c, m: (0, 0))],
        out_specs=[cur, cur],
        out_shape=[jax.ShapeDtypeStruct((b, length, width), jnp.bfloat16),
                   jax.ShapeDtypeStruct((b, length, width), jnp.float32)],
        compiler_params=_cparams("parallel", "parallel", "arbitrary"),
        name=f"dilated_attention_r{dilation}",
    )(q, k, k, v, v, bias.reshape(DIL_HEADS_PER_GROUP * Q_BLOCK, 2 * Q_BLOCK))


def _t5_causal_bucket(dist):
    max_exact = N_BUCKETS // 2
    d = jnp.maximum(dist, 1).astype(jnp.float32)
    large = max_exact + (jnp.log(d / max_exact) / math.log(MAX_DISTANCE / max_exact)
                         * (N_BUCKETS - max_exact)).astype(jnp.int32)
    large = jnp.minimum(large, N_BUCKETS - 1)
    return jnp.where(dist < max_exact, dist, large)


def dilated_bias(rel_bias_group, window, dilation):
    qi = jnp.arange(Q_BLOCK)[:, None]
    kj = jnp.arange(2 * Q_BLOCK)[None, :]
    dist_sub = qi + Q_BLOCK - kj
    bucket = _t5_causal_bucket(jnp.maximum(dist_sub, 0) * dilation)
    onehot = (bucket[..., None] == jnp.arange(N_BUCKETS)).astype(jnp.float32)
    bias = jnp.einsum("qkn,nh->hqk", onehot, rel_bias_group.astype(jnp.float32), precision=lax.Precision.HIGHEST)
    valid = (dist_sub >= 0) & (dist_sub <= window // dilation)
    return jnp.where(valid[None], bias, NEG_INF)


def _mix_out_kernel(oa_ref, o0_ref, o1_ref, o2_ref, l0_ref, l1_ref, l2_ref, ga_ref, gb_ref, x_ref, mod_ref,
                    wa_ref, wb_ref, wo_ref, gpost_ref, gpre_ref, wr_ref, wsgu_ref, wsd_ref,
                    x1_ref, hp_ref, sh_ref, lt_ref, stage_ref):
    f32 = jnp.float32
    tm = x_ref.shape[1]

    def token_major(ref, dilation):
        if dilation == 1:
            return ref[0].astype(f32)
        slabs = DIL_GROUP_WIDTH // LANES
        for c in range(dilation):
            for j in range(slabs):
                col = c * DIL_GROUP_WIDTH + j * LANES
                stage_ref[j, pl.ds(c, tm // dilation, stride=dilation), :] = ref[0, :, col:col + LANES].astype(f32)
        return jnp.concatenate([stage_ref[j] for j in range(slabs)], axis=1)

    dils = [r for _, r in DIL_CONFIGS]
    l0, l1, l2 = (token_major(ref, r) for ref, r in zip((l0_ref, l1_ref, l2_ref), dils))
    mx = jnp.maximum(jnp.maximum(l0, l1), l2)
    e0, e1, e2 = jnp.exp(l0 - mx), jnp.exp(l1 - mx), jnp.exp(l2 - mx)
    den = e0 + e1 + e2
    o0, o1, o2 = (token_major(ref, r) for ref, r in zip((o0_ref, o1_ref, o2_ref), dils))
    out_b = (e0 / den) * o0 + (e1 / den) * o1 + (e2 / den) * o2
    ya = _dot(oa_ref[0], wa_ref[...])
    yb = _dot(out_b.astype(jnp.bfloat16), wb_ref[...])
    merged = ga_ref[0].astype(f32) * ya + gb_ref[0].astype(f32) * yb
    y = _dot(merged.astype(jnp.bfloat16), wo_ref[...])
    x1 = x_ref[0] + mod_ref[0, 2:3, :] * _rms_norm(y, gpost_ref[...])
    x1_ref[0] = x1
    h = _rms_norm(x1, gpre_ref[...]) * (1.0 + mod_ref[0, 4:5, :]) + mod_ref[0, 3:4, :]
    hb = h.astype(jnp.bfloat16)
    _pack_rows(hp_ref, h)
    lt_ref[...] = _dot_nt(wr_ref[...], hb)
    gu = _dot(hb, wsgu_ref[...])
    g, u = gu[:, :EXPERT_DIM], gu[:, EXPERT_DIM:]
    sh_ref[0] = _dot((g * _sigmoid(g) * u).astype(jnp.bfloat16), wsd_ref[...])


def mix_out(out_a, o_g, lse_g, gate_a, gate_b, x, mod, w_a, w_b, w_o, g_post, g_pre_ffn, w_rT, w_sgu, w_sd, *, tm=512):
    b, s, d = x.shape
    tok = lambda width: pl.BlockSpec((1, tm, width), lambda bi, i: (bi, i, 0))
    full = lambda a: pl.BlockSpec(a.shape, lambda bi, i: (0,) * a.ndim)
    nt = s // tm
    g_post, g_pre_ffn = g_post.reshape(1, d), g_pre_ffn.reshape(1, d)
    weights = (w_a, w_b, w_o, g_post, g_pre_ffn, w_rT, w_sgu, w_sd)
    return pl.pallas_call(
        _mix_out_kernel,
        grid=(b, nt),
        in_specs=[tok(SB_WIDTH)] + [_dilated_spec(tm, r) for _, r in DIL_CONFIGS] * 2 + [tok(d)] * 3
        + [pl.BlockSpec((1, N_MOD, d), lambda bi, i: (bi, 0, 0))] + [full(w) for w in weights],
        out_specs=[tok(d), pl.BlockSpec((tm * ROW_SUB, LANES), lambda bi, i: (bi * nt + i, 0)), tok(d),
                   pl.BlockSpec((N_EXPERTS, tm), lambda bi, i: (0, bi * nt + i))],
        out_shape=[jax.ShapeDtypeStruct((b, s, d), jnp.float32),
                   jax.ShapeDtypeStruct((b * s * ROW_SUB, LANES), jnp.uint32),
                   jax.ShapeDtypeStruct((b, s, d), jnp.float32),
                   jax.ShapeDtypeStruct((N_EXPERTS, b * s), jnp.float32)],
        scratch_shapes=[pltpu.VMEM((DIL_GROUP_WIDTH // LANES, tm, LANES), jnp.float32)],
        compiler_params=_cparams("parallel", "parallel"),
        name="mix_out",
    )(out_a, *o_g, *lse_g, gate_a, gate_b, x, mod, *weights)


def _route_kernel(lt_ref, bias_ref, tri_ref, idx_ref, gate_ref, rank_ref, cnt_ref, base_ref, *, tn):
    f32 = jnp.float32

    @pl.when(pl.program_id(0) == 0)
    def _():
        base_ref[...] = jnp.zeros_like(base_ref)

    per_group = N_EXPERTS // N_GROUPS
    scores = _sigmoid(lt_ref[...])
    sel = scores + bias_ref[...]
    sel_groups = [sel[g * per_group:(g + 1) * per_group] for g in range(N_GROUPS)]
    ri = lax.broadcasted_iota(jnp.int32, (per_group, tn), 0)
    group_scores = []
    for sg in sel_groups:
        m1 = jnp.max(sg, axis=0, keepdims=True)
        first = jnp.min(jnp.where(sg == m1, ri, per_group), axis=0, keepdims=True)
        group_scores.append(m1 + jnp.max(jnp.where(ri == first, NEG_INF, sg), axis=0, keepdims=True))
    cur = jnp.concatenate(group_scores, axis=0)
    gi = lax.broadcasted_iota(jnp.int32, cur.shape, 0)
    keep = jnp.zeros(cur.shape, jnp.float32)
    for _ in range(TOP_K_GROUPS):
        mg = jnp.max(cur, axis=0, keepdims=True)
        pick = gi == jnp.min(jnp.where(cur == mg, gi, N_GROUPS), axis=0, keepdims=True)
        keep = jnp.where(pick, 1.0, keep)
        cur = jnp.where(pick, NEG_INF, cur)
    cur = jnp.concatenate([jnp.where(keep[g:g + 1, :] > 0.0, sg, NEG_INF) for g, sg in enumerate(sel_groups)], axis=0)
    ei = lax.broadcasted_iota(jnp.int32, cur.shape, 0)
    idxs, gates = [], []
    candidates = cur
    for _ in range(TOP_K):
        mk = jnp.max(cur, axis=0, keepdims=True)
        fk = jnp.min(jnp.where(cur == mk, ei, N_EXPERTS), axis=0, keepdims=True)
        pick = ei == fk
        idxs.append(fk)
        gates.append(jnp.sum(jnp.where(pick, scores, 0.0), axis=0, keepdims=True))
        cur = jnp.where(pick, NEG_INF, cur)
    gate = jnp.concatenate(gates, axis=0)
    gate_ref[...] = gate / jnp.sum(gate, axis=0, keepdims=True) * ROUTED_SCALE
    idx_ref[...] = jnp.concatenate(idxs, axis=0)
    onehot = jnp.where(cur == NEG_INF, 1.0, 0.0) - jnp.where(candidates == NEG_INF, 1.0, 0.0)
    earlier = _dot(onehot.astype(jnp.bfloat16), tri_ref[...]) + base_ref[...]
    ranks = [jnp.sum(jnp.where(ei == fk, earlier, 0.0), axis=0, keepdims=True) for fk in idxs]
    rank_ref[...] = jnp.concatenate(ranks, axis=0).astype(jnp.int32)
    base = base_ref[...] + jnp.sum(onehot, axis=1, keepdims=True)
    base_ref[...] = base
    cnt_ref[...] = jnp.broadcast_to(base, cnt_ref.shape).astype(jnp.int32)


def route(logits_t, router_bias, *, tn=512):
    ne, n = logits_t.shape
    tri = jnp.asarray(np.triu(np.ones((tn, tn), np.float32), k=1), jnp.bfloat16)
    slot = pl.BlockSpec((TOP_K, tn), lambda i: (0, i))
    return pl.pallas_call(
        functools.partial(_route_kernel, tn=tn),
        grid=(n // tn,),
        in_specs=[pl.BlockSpec((ne, tn), lambda i: (0, i)),
                  pl.BlockSpec((ne, 1), lambda i: (0, 0)),
                  pl.BlockSpec((tn, tn), lambda i: (0, 0))],
        out_specs=[slot, slot, slot, pl.BlockSpec((ne, LANES), lambda i: (0, 0))],
        out_shape=[jax.ShapeDtypeStruct((TOP_K, n), jnp.int32),
                   jax.ShapeDtypeStruct((TOP_K, n), jnp.float32),
                   jax.ShapeDtypeStruct((TOP_K, n), jnp.int32),
                   jax.ShapeDtypeStruct((ne, LANES), jnp.int32)],
        scratch_shapes=[pltpu.VMEM((ne, 1), jnp.float32)],
        compiler_params=_cparams("arbitrary"),
        name="route",
    )(logits_t, router_bias.reshape(ne, 1), tri)


def _dest_kernel(idx_ref, rank_ref, start_ref, dest_ref):
    idx = idx_ref[...]
    ei = lax.broadcasted_iota(jnp.int32, (N_EXPERTS, idx.shape[1]), 0)
    start = start_ref[...]
    rows = [jnp.sum(jnp.where(ei == idx[k:k + 1, :], start, 0.0), axis=0, keepdims=True) for k in range(TOP_K)]
    dest_ref[...] = jnp.concatenate(rows, axis=0).astype(jnp.int32) + rank_ref[...]


def slot_destinations(idx, rank, starts, *, tn=512):
    kk, n = idx.shape
    slot = pl.BlockSpec((kk, tn), lambda i: (0, i))
    return pl.pallas_call(
        _dest_kernel,
        grid=(n // tn,),
        in_specs=[slot, slot, pl.BlockSpec((N_EXPERTS, 1), lambda i: (0, 0))],
        out_specs=slot,
        out_shape=jax.ShapeDtypeStruct((kk, n), jnp.int32),
        compiler_params=_cparams("parallel"),
        name="slot_destinations",
    )(idx, rank, starts.astype(jnp.float32).reshape(N_EXPERTS, 1))


def _packed_row(ref, row):
    start = row * ROW_SUB
    return ref.at[pl.ds(start if isinstance(row, int) else pl.multiple_of(start, ROW_SUB), ROW_SUB), :]


def _dispatch_kernel(dest_ref, h_ref, xs_ref, zero_ref, sem, *, tm, n_rows):
    pad = zero_ref.shape[0]

    @pl.when(pl.program_id(0) == 0)
    def _():
        zero_ref[...] = jnp.zeros_like(zero_ref)
        tail = pltpu.make_async_copy(zero_ref, xs_ref.at[pl.ds(n_rows * ROW_SUB, pad), :], sem)
        tail.start()
        tail.wait()

    def body(t, _):
        for k in range(TOP_K):
            pltpu.make_async_copy(_packed_row(h_ref, t), _packed_row(xs_ref, dest_ref[k, t]), sem).start(priority=k % 2)
        return 0

    lax.fori_loop(0, tm, body, 0)
    for _ in range(TOP_K):
        pltpu.make_async_copy(h_ref, xs_ref.at[pl.ds(0, tm * ROW_SUB), :], sem).wait()


def dispatch(dest, hp, *, pad_rows, tm=512):
    kk, n = dest.shape
    n_rows = n * kk
    return pl.pallas_call(
        functools.partial(_dispatch_kernel, tm=tm, n_rows=n_rows),
        grid=(n // tm,),
        in_specs=[pl.BlockSpec((kk, tm), lambda i: (0, i), memory_space=pltpu.SMEM),
                  pl.BlockSpec((tm * ROW_SUB, LANES), lambda i: (i, 0))],
        out_specs=pl.BlockSpec(memory_space=pl.ANY),
        out_shape=jax.ShapeDtypeStruct(((n_rows + pad_rows) * ROW_SUB, LANES), hp.dtype),
        scratch_shapes=[pltpu.VMEM((pad_rows * ROW_SUB, LANES), hp.dtype), pltpu.SemaphoreType.DMA],
        compiler_params=_cparams("arbitrary"),
        name="dispatch",
    )(dest, hp)


EXPERT_CHUNK = 512
EXPERT_LOOKAHEAD = 3
EXPERT_IN_BUFS = EXPERT_LOOKAHEAD + 1
EXPERT_OUT_BUFS = 3


def _expert_kernel(first_ref, nch_ref, crow_ref, total_ref, xs_ref, wg_ref, wu_ref, wd_ref, ys_ref,
                   xbuf, ybuf, wgu_bf, wd_bf, h_ref, sem_in, sem_out, waited_ref):
    e = pl.program_id(0)
    chunk = EXPERT_CHUNK
    first, n_chunks, total = first_ref[e], nch_ref[e], total_ref[0]

    def rows(g):
        return pl.ds(pl.multiple_of(crow_ref[g] * ROW_SUB, ROW_SUB), chunk * ROW_SUB)

    def in_copy(g):
        slot = lax.rem(g, EXPERT_IN_BUFS)
        return pltpu.make_async_copy(xs_ref.at[rows(g), :], xbuf.at[slot], sem_in.at[slot])

    def out_copy(g):
        slot = lax.rem(g, EXPERT_OUT_BUFS)
        return pltpu.make_async_copy(ybuf.at[slot], ys_ref.at[rows(g), :], sem_out.at[slot])

    def wait_writes_through(g):
        def wait_one(j, carry):
            out_copy(j).wait()
            return carry

        lax.fori_loop(waited_ref[0] + 1, g + 1, wait_one, 0)
        waited_ref[0] = jnp.maximum(waited_ref[0], g)

    @pl.when(e == 0)
    def _():
        waited_ref[0] = -1
        for g in range(EXPERT_LOOKAHEAD):
            @pl.when(g < total)
            def _():
                in_copy(g).start()

    @pl.when(n_chunks > 0)
    def _():
        wgu_bf[:, :EXPERT_DIM] = wg_ref[0].astype(wgu_bf.dtype)
        wgu_bf[:, EXPERT_DIM:] = wu_ref[0].astype(wgu_bf.dtype)
        wd_bf[...] = wd_ref[0].astype(wd_bf.dtype)

    def acquire(g):
        in_copy(g).wait()

        @pl.when(g + EXPERT_LOOKAHEAD < total)
        def _():
            in_copy(g + EXPERT_LOOKAHEAD).start()

    def hidden(g):
        bf = jnp.bfloat16
        xa, xb = _unpack_rows(xbuf.at[lax.rem(g, EXPERT_IN_BUFS)], chunk)
        gu = _dot(xa.astype(bf), wgu_bf[:ROW_WORDS, :]) + _dot(xb.astype(bf), wgu_bf[ROW_WORDS:, :])
        gate, up = gu[:, :EXPERT_DIM], gu[:, EXPERT_DIM:]
        return (gate * _sigmoid(gate) * up).astype(bf)

    def pack(g, h):
        _pack_rows(ybuf.at[lax.rem(g, EXPERT_OUT_BUFS)], _dot(h, wd_bf[...]))

    def release(g):
        @pl.when(g == first)
        def _():
            wait_writes_through(g - 1)

        out_copy(g).start()

    @pl.when(n_chunks > 0)
    def _():
        acquire(first)
        h_ref[...] = hidden(first)

    def body(c, carry):
        g = first + c
        acquire(g)
        wait_writes_through(g - 1 - EXPERT_OUT_BUFS)
        h_prev = h_ref[...]
        pack(g - 1, h_prev)
        h_ref[...] = hidden(g)
        release(g - 1)
        return carry

    lax.fori_loop(1, n_chunks, body, 0)

    @pl.when(n_chunks > 0)
    def _():
        last = first + n_chunks - 1
        wait_writes_through(last - EXPERT_OUT_BUFS)
        pack(last, h_ref[...])
        release(last)

    @pl.when(e == pl.num_programs(0) - 1)
    def _():
        wait_writes_through(total - 1)
        ybuf[0] = jnp.zeros(ybuf.shape[1:], ybuf.dtype)
        pad = pl.ds(ys_ref.shape[0] - chunk * ROW_SUB, chunk * ROW_SUB)
        tail = pltpu.make_async_copy(ybuf.at[0], ys_ref.at[pad, :], sem_out.at[0])
        tail.start()
        tail.wait()


def expert_chunks(starts, counts, n_rows):
    chunk = EXPERT_CHUNK
    max_chunks = n_rows // chunk + N_EXPERTS
    nch = (counts + (chunk - 1)) // chunk
    ends = jnp.cumsum(nch)
    first = ends - nch
    g = jnp.arange(max_chunks, dtype=jnp.int32)
    owner = jnp.minimum(jnp.sum(g[:, None] >= ends[None, :], axis=1), N_EXPERTS - 1)
    onehot = (owner[:, None] == jnp.arange(N_EXPERTS)[None, :]).astype(jnp.int32)
    crow = (onehot * (starts - first * chunk)[None, :]).sum(axis=1) + g * chunk
    crow = jnp.where(g < ends[-1], crow, 0).astype(jnp.int32)
    return first.astype(jnp.int32), nch.astype(jnp.int32), crow, ends[-1:].astype(jnp.int32)


def expert_ffn(starts, counts, xs, w_gate, w_up, w_down):
    d = D_MODEL
    n_rows = xs.shape[0] // ROW_SUB - EXPERT_CHUNK
    tables = expert_chunks(starts, counts, n_rows)
    chunk_shape = (EXPERT_CHUNK * ROW_SUB, LANES)
    weights = lambda shape: pl.BlockSpec(shape, lambda e, *_: (e, 0, 0))
    grid_spec = pltpu.PrefetchScalarGridSpec(
        num_scalar_prefetch=len(tables),
        grid=(N_EXPERTS,),
        in_specs=[pl.BlockSpec(memory_space=pl.ANY),
                  weights((1, d, EXPERT_DIM)), weights((1, d, EXPERT_DIM)), weights((1, EXPERT_DIM, d))],
        out_specs=pl.BlockSpec(memory_space=pl.ANY),
        scratch_shapes=[pltpu.VMEM((EXPERT_IN_BUFS,) + chunk_shape, xs.dtype),
                        pltpu.VMEM((EXPERT_OUT_BUFS,) + chunk_shape, xs.dtype),
                        pltpu.VMEM((d, 2 * EXPERT_DIM), jnp.bfloat16), pltpu.VMEM((EXPERT_DIM, d), jnp.bfloat16),
                        pltpu.VMEM((EXPERT_CHUNK, EXPERT_DIM), jnp.bfloat16),
                        pltpu.SemaphoreType.DMA((EXPERT_IN_BUFS,)), pltpu.SemaphoreType.DMA((EXPERT_OUT_BUFS,)),
                        pltpu.SMEM((1,), jnp.int32)],
    )
    return pl.pallas_call(
        _expert_kernel,
        grid_spec=grid_spec,
        out_shape=jax.ShapeDtypeStruct(xs.shape, xs.dtype),
        compiler_params=_cparams("arbitrary"),
        name="expert_ffn",
    )(*tables, xs, w_gate, w_up, w_down)


COMBINE_AHEAD = 2
COMBINE_SLOTS = COMBINE_AHEAD + 1
COMBINE_GROUP = 32


def _combine_kernel(*refs, tm):
    head_refs, (dest_ref, ys_ref, gate_ref, sh_ref, x1_ref, mod_ref, g_ref, o_ref, buf, sem) = (
        refs[:COMBINE_AHEAD], refs[COMBINE_AHEAD:])
    i, n = pl.program_id(0), pl.num_programs(0)

    def gather(src_ref, slot, t):
        for k in range(TOP_K):
            pltpu.make_async_copy(_packed_row(ys_ref, src_ref[k, t]), _packed_row(buf.at[slot, k], t),
                                  sem.at[slot]).start(priority=k % 2)

    def wait_tile(slot):
        for k in range(TOP_K):
            pltpu.make_async_copy(ys_ref.at[pl.ds(0, tm * ROW_SUB), :], buf.at[slot, k], sem.at[slot]).wait()

    @pl.when(i == 0)
    def _():
        for a, head in enumerate(head_refs):
            def body(t, carry, head=head, a=a):
                gather(head, a, t)
                return carry

            lax.fori_loop(0, tm, body, 0)

    def step(cur):
        ahead = (cur + COMBINE_AHEAD) % COMBINE_SLOTS
        wait_tile(cur)
        g_post = g_ref[...]
        gt2 = mod_ref[0, 5:6, :]
        for t0 in range(0, tm, COMBINE_GROUP):
            for t in range(t0, t0 + COMBINE_GROUP):
                gather(dest_ref, ahead, t)
            rows = slice(t0, t0 + COMBINE_GROUP)
            gate = gate_ref[rows, :]
            first = jnp.zeros((COMBINE_GROUP, ROW_WORDS), jnp.float32)
            second = jnp.zeros((COMBINE_GROUP, ROW_WORDS), jnp.float32)
            for k in range(TOP_K):
                packed = buf.at[cur, k, pl.ds(t0 * ROW_SUB, COMBINE_GROUP * ROW_SUB), :]
                ya, yb = _unpack_rows(packed, COMBINE_GROUP)
                first = first + ya * gate[:, k:k + 1]
                second = second + yb * gate[:, k:k + 1]
            y = sh_ref[rows, :] + jnp.concatenate([first, second], axis=1)
            o_ref[rows, :] = x1_ref[rows, :] + gt2 * _rms_norm(y, g_post)

        @pl.when(i == n - 1)
        def _():
            for a in range(1, COMBINE_SLOTS):
                wait_tile((cur + a) % COMBINE_SLOTS)

    for slot in range(COMBINE_SLOTS):
        pl.when(lax.rem(i, COMBINE_SLOTS) == slot)(functools.partial(step, slot))


def combine(dest, ys, gate_nk, shared, x1, mod, g_post_ffn, *, seq, tm=128):
    n, d = x1.shape
    per_batch = seq // tm
    n_tiles = n // tm
    tok = pl.BlockSpec((tm, d), lambda i: (i, 0))
    dest_tile = lambda index: pl.BlockSpec((TOP_K, tm), index, memory_space=pltpu.SMEM)
    head_specs = [dest_tile(lambda i, a=a: (0, a)) for a in range(COMBINE_AHEAD)]
    return pl.pallas_call(
        functools.partial(_combine_kernel, tm=tm),
        grid=(n_tiles,),
        in_specs=head_specs + [dest_tile(lambda i: (0, jnp.minimum(i + COMBINE_AHEAD, n_tiles - 1))),
                               pl.BlockSpec(memory_space=pl.ANY),
                               pl.BlockSpec((tm, TOP_K), lambda i: (i, 0)),
                               tok, tok,
                               pl.BlockSpec((1, N_MOD, d), lambda i: (i // per_batch, 0, 0)),
                               pl.BlockSpec((1, d), lambda i: (0, 0))],
        out_specs=tok,
        out_shape=jax.ShapeDtypeStruct((n, d), jnp.float32),
        scratch_shapes=[pltpu.VMEM((COMBINE_SLOTS, TOP_K, tm * ROW_SUB, LANES), ys.dtype),
                        pltpu.SemaphoreType.DMA((COMBINE_SLOTS,))],
        compiler_params=_cparams("arbitrary"),
        name="combine",
    )(*([dest] * COMBINE_AHEAD), dest, ys, gate_nk, shared, x1, mod, g_post_ffn.reshape(1, d))


def _layer(x, mod, rel_bias, g_pre_mix, w_in, w_branch_a, w_branch_b, w_out, g_post_mix, g_pre_ffn, w_router,
           router_bias, w_exp_gate, w_exp_up, w_exp_down, w_sh_gate, w_sh_up, w_sh_down, g_post_ffn):
    b, s, d = x.shape
    assert d == D_MODEL and w_exp_gate.shape == (N_EXPERTS, D_MODEL, EXPERT_DIM), (x.shape, w_exp_gate.shape)
    assert s % SB_TILE == 0 and all(s % (r * Q_BLOCK) == 0 for _, r in DIL_CONFIGS), s
    bf = jnp.bfloat16
    scale = HEAD_DIM ** -0.5
    sb_q, sb_k, sb_v, dq, dk, dv, w_ga, w_gb = jnp.split(
        w_in, np.cumsum((SB_WIDTH,) * 3 + (DIL_WIDTH,) * 3 + (d,)).tolist(), axis=1)
    groups = lambda w: [w[:, g * DIL_GROUP_WIDTH:(g + 1) * DIL_GROUP_WIDTH] for g in range(len(DIL_CONFIGS))]
    w_rows = jnp.concatenate([sb_k] + [w * scale for w in groups(dq)] + groups(dk) + groups(dv) + [w_ga, w_gb],
                             axis=1).astype(bf)
    w_cols = jnp.concatenate([sb_q * (scale * LOG2E), sb_v], axis=1).T.astype(bf)
    qT, k, vT3, *rest = in_projection(x, mod, g_pre_mix, w_rows, w_cols)
    dqs, dks, dvs, gate_a, gate_b = rest[0:3], rest[3:6], rest[6:9], rest[9], rest[10]
    out_a = sb_attention(qT, k, vT3)
    o_g, lse_g = [], []
    for g, (window, dilation) in enumerate(DIL_CONFIGS):
        bias = dilated_bias(rel_bias[:, g * DIL_HEADS_PER_GROUP:(g + 1) * DIL_HEADS_PER_GROUP], window, dilation)
        o, lse = dilated_attention(dqs[g], dks[g], dvs[g], bias, dilation)
        o_g.append(o)
        lse_g.append(lse)
    x1, hp, shared, logits_t = mix_out(
        out_a, o_g, lse_g, gate_a, gate_b, x, mod, w_branch_a.astype(bf), w_branch_b.astype(bf), w_out.astype(bf),
        g_post_mix, g_pre_ffn, w_router.T.astype(bf), jnp.concatenate([w_sh_gate, w_sh_up], axis=1).astype(bf),
        w_sh_down.astype(bf))
    n = b * s
    idx, gate, rank, counts = route(logits_t, router_bias)
    counts = counts[:, 0]
    starts = jnp.cumsum(counts) - counts
    dest = slot_destinations(idx, rank, starts)
    xs = dispatch(dest, hp, pad_rows=EXPERT_CHUNK)
    ys = expert_ffn(starts, counts, xs, w_exp_gate, w_exp_up, w_exp_down)
    out = combine(dest, ys, gate.T, shared.reshape(n, d), x1.reshape(n, d), mod, g_post_ffn, seq=s)
    return out.reshape(b, s, d)


def kernel(x, c, rel_bias, w_ada, b_ada, g_pre_mix, w_in, w_branch_a, w_branch_b, w_out, g_post_mix, g_pre_ffn,
           w_router, router_bias, w_exp_gate, w_exp_up, w_exp_down, w_sh_gate, w_sh_up, w_sh_down, g_post_ffn):
    depth = w_in.shape[0]
    for layer in range(depth):
        mod = ada_modulation(c, w_ada[layer], b_ada[layer]).reshape(x.shape[0], N_MOD, x.shape[2])
        x = _layer(x, mod, rel_bias, g_pre_mix[layer], w_in[layer], w_branch_a[layer], w_branch_b[layer], w_out[layer],
                   g_post_mix[layer], g_pre_ffn[layer], w_router[layer], router_bias[layer], w_exp_gate[layer],
                   w_exp_up[layer], w_exp_down[layer], w_sh_gate[layer], w_sh_up[layer], w_sh_down[layer],
                   g_post_ffn[layer])
    return x
```

```python
import functools
import math

import jax
import jax.numpy as jnp
import numpy as np
from jax import lax
from jax.experimental import pallas as pl
from jax.experimental.pallas import tpu as pltpu

D_MODEL = 1024
HEAD_DIM = 64
SB_HEADS = 8
SB_WIDTH = SB_HEADS * HEAD_DIM
DIL_CONFIGS = ((128, 1), (512, 4), (2048, 16))
DIL_HEADS_PER_GROUP = 4
DIL_GROUP_WIDTH = DIL_HEADS_PER_GROUP * HEAD_DIM
DIL_WIDTH = DIL_GROUP_WIDTH * len(DIL_CONFIGS)
Q_BLOCK = 128
N_BUCKETS = 32
MAX_DISTANCE = 2048
N_EXPERTS = 256
TOP_K = 8
N_GROUPS = 8
TOP_K_GROUPS = 4
EXPERT_DIM = 256
ROUTED_SCALE = 2.5
N_MOD = 6
EPS = 1e-6

LANES = 128
VMEM_LIMIT_BYTES = 56 * 1024 * 1024
NEG_INF = float("-inf")
LOG2E = math.log2(math.e)


def _cparams(*semantics):
    return pltpu.CompilerParams(dimension_semantics=semantics, vmem_limit_bytes=VMEM_LIMIT_BYTES)


def _rms_norm(v, g):
    return v * lax.rsqrt(jnp.mean(v * v, axis=-1, keepdims=True) + EPS) * g


def _sigmoid(v):
    return 1.0 / (1.0 + jnp.exp(-v))


def _dot(a, b):
    return jnp.dot(a, b, preferred_element_type=jnp.float32)


def _dot_nt(a, b):
    return lax.dot_general(a, b, (((1,), (1,)), ((), ())), preferred_element_type=jnp.float32)


def _split_bf16(v):
    hi = v.astype(jnp.bfloat16)
    lo = (v - hi.astype(jnp.float32)).astype(jnp.bfloat16)
    return hi, lo


ROW_WORDS = D_MODEL // 2
ROW_SUB = ROW_WORDS // LANES


def _pack_rows(ref, v):
    m = v.shape[0]
    bits = pltpu.bitcast(v.astype(jnp.bfloat16).astype(jnp.float32), jnp.uint32)
    words = bits[:, :ROW_WORDS] | (bits[:, ROW_WORDS:] >> 16)
    for j in range(ROW_SUB):
        ref[pl.ds(j, m, stride=ROW_SUB), :] = words[:, j * LANES:(j + 1) * LANES]


def _unpack_rows(ref, m):
    words = jnp.concatenate([ref[pl.ds(j, m, stride=ROW_SUB), :] for j in range(ROW_SUB)], axis=1)
    first = pltpu.bitcast(words & jnp.uint32(0xFFFF0000), jnp.float32)
    second = pltpu.bitcast(words << 16, jnp.float32)
    return first, second


def _ada_kernel(c_ref, w_ref, b_ref, o_ref):
    c = c_ref[...]
    a_hi, a_lo = _split_bf16(c * _sigmoid(c))
    w_hi, w_lo = _split_bf16(w_ref[...])
    o_ref[...] = _dot(a_hi, w_hi) + _dot(a_lo, w_hi) + _dot(a_hi, w_lo) + b_ref[...]


def ada_modulation(c, w_ada, b_ada, *, tn=1536):
    b, d = c.shape
    n = w_ada.shape[1]
    return pl.pallas_call(
        _ada_kernel,
        grid=(n // tn,),
        in_specs=[
            pl.BlockSpec((b, d), lambda j: (0, 0)),
            pl.BlockSpec((d, tn), lambda j: (0, j)),
            pl.BlockSpec((1, tn), lambda j: (0, j)),
        ],
        out_specs=pl.BlockSpec((b, tn), lambda j: (0, j)),
        out_shape=jax.ShapeDtypeStruct((b, n), jnp.float32),
        compiler_params=_cparams("parallel"),
        name="ada_modulation",
    )(c, w_ada, b_ada.reshape(1, n))


def _in_proj_kernel(x_ref, mod_ref, g_ref, w_rows_ref, w_cols_ref, qT_ref, k_ref, vT_ref, *rest, tm):
    dil_refs, ga_ref, gb_ref, stage_ref = rest[:9], rest[9], rest[10], rest[11]
    h = _rms_norm(x_ref[0], g_ref[...]) * (1.0 + mod_ref[0, 1:2, :]) + mod_ref[0, 0:1, :]
    hb = h.astype(jnp.bfloat16)
    off = 0
    k_ref[0] = _dot(hb, w_rows_ref[:, off:off + SB_WIDTH]).astype(k_ref.dtype)
    off += SB_WIDTH
    for n, r in enumerate(dil_refs):
        y = _dot(hb, w_rows_ref[:, off:off + DIL_GROUP_WIDTH])
        off += DIL_GROUP_WIDTH
        dilation = DIL_CONFIGS[n % len(DIL_CONFIGS)][1]
        if dilation == 1:
            r[0] = y.astype(r.dtype)
            continue
        for j in range(DIL_GROUP_WIDTH // LANES):
            stage_ref[j] = y[:, j * LANES:(j + 1) * LANES]
        for c in range(dilation):
            for j in range(DIL_GROUP_WIDTH // LANES):
                col = c * DIL_GROUP_WIDTH + j * LANES
                r[0, :, col:col + LANES] = stage_ref[j, pl.ds(c, tm // dilation, stride=dilation), :].astype(r.dtype)
    for r in (ga_ref, gb_ref):
        r[0] = _sigmoid(_dot(hb, w_rows_ref[:, off:off + D_MODEL])).astype(r.dtype)
        off += D_MODEL
    qT_ref[0] = _dot_nt(w_cols_ref[0:SB_WIDTH, :], hb).astype(qT_ref.dtype)
    vT = _dot_nt(w_cols_ref[SB_WIDTH:2 * SB_WIDTH, :], hb).astype(vT_ref.dtype)
    for p in range(SB_WIDTH // LANES):
        for jb in range(tm // LANES):
            vT_ref[0, p, jb] = vT[p * LANES:(p + 1) * LANES, jb * LANES:(jb + 1) * LANES]


def _dilated_shape(b, s, dilation, dtype):
    return jax.ShapeDtypeStruct((b, s // dilation, dilation * DIL_GROUP_WIDTH), dtype)


def _dilated_spec(tm, dilation):
    return pl.BlockSpec((1, tm // dilation, dilation * DIL_GROUP_WIDTH), lambda bi, i: (bi, i, 0))


def in_projection(x, mod, g_pre, w_rows, w_cols, *, tm=256):
    b, s, d = x.shape
    nt = s // tm
    bf = jnp.bfloat16
    tok = lambda width: pl.BlockSpec((1, tm, width), lambda bi, i: (bi, i, 0))
    out_shapes = [
        jax.ShapeDtypeStruct((b, SB_WIDTH, s), bf),
        jax.ShapeDtypeStruct((b, s, SB_WIDTH), bf),
        jax.ShapeDtypeStruct((b, SB_WIDTH // LANES, s // LANES, LANES, LANES), bf),
    ] + [_dilated_shape(b, s, r, bf) for _, r in DIL_CONFIGS] * 3 + [jax.ShapeDtypeStruct((b, s, d), bf)] * 2
    out_specs = [
        pl.BlockSpec((1, SB_WIDTH, tm), lambda bi, i: (bi, 0, i)),
        tok(SB_WIDTH),
        pl.BlockSpec((1, SB_WIDTH // LANES, tm // LANES, LANES, LANES), lambda bi, i: (bi, 0, i, 0, 0)),
    ] + [_dilated_spec(tm, r) for _, r in DIL_CONFIGS] * 3 + [tok(d)] * 2
    return pl.pallas_call(
        functools.partial(_in_proj_kernel, tm=tm),
        grid=(b, nt),
        in_specs=[
            tok(d),
            pl.BlockSpec((1, N_MOD, d), lambda bi, i: (bi, 0, 0)),
            pl.BlockSpec((1, d), lambda bi, i: (0, 0)),
            pl.BlockSpec(w_rows.shape, lambda bi, i: (0, 0)),
            pl.BlockSpec(w_cols.shape, lambda bi, i: (0, 0)),
        ],
        out_specs=out_specs,
        out_shape=out_shapes,
        scratch_shapes=[pltpu.VMEM((DIL_GROUP_WIDTH // LANES, tm, LANES), jnp.float32)],
        compiler_params=_cparams("parallel", "parallel"),
        name="in_projection",
    )(x, mod, g_pre.reshape(1, d), w_rows, w_cols)


SB_TILE = 512
SB_SUB = 256
SB_STEP_HEADS = 4


def _sb_chunk(kc, vc, q_h, tri, csum, past):
    z = _dot(kc, q_h)
    neg_abs = pltpu.bitcast(pltpu.bitcast(z, jnp.uint32) | jnp.uint32(0x80000000), jnp.float32)
    drop = jnp.maximum(z, 0.0) + jnp.log(1.0 + jnp.exp2(neg_abs)) * LOG2E
    if past is not None:
        drop = jnp.where(past, drop, 0.0)
    weights = [None] * (SB_TILE // SB_SUB)
    for sb in reversed(range(SB_TILE // SB_SUB)):
        rows = slice(sb * SB_SUB, (sb + 1) * SB_SUB)
        d = drop[rows]
        later = _dot(tri, d.astype(jnp.bfloat16))
        w = jnp.exp2(z[rows] - d - later - csum)
        if past is not None:
            w = jnp.where(past[rows], w, 0.0)
        weights[sb] = w.astype(jnp.bfloat16)
        csum = csum + jnp.sum(d, axis=0, keepdims=True)
    w = jnp.concatenate(weights, axis=0)
    t = SB_TILE
    part = jnp.concatenate([_dot(vc[h * HEAD_DIM:(h + 1) * HEAD_DIM], w[:, h * t:(h + 1) * t])
                            for h in range(SB_STEP_HEADS)], axis=0)
    return part, csum


def _sb_attn_kernel(qT_ref, k_ref, vT_ref, tri_ref, o_ref, acc_ref, csum_ref):
    i = pl.program_id(2)
    t = SB_TILE
    q_all = qT_ref[0]
    row = lax.broadcasted_iota(jnp.int32, q_all.shape, 0)
    zero = jnp.zeros_like(q_all)
    q_heads = jnp.concatenate(
        [jnp.where(jnp.logical_and(row >= h * HEAD_DIM, row < (h + 1) * HEAD_DIM), q_all, zero)
         for h in range(SB_STEP_HEADS)], axis=1)
    tri = tri_ref[...]
    blocks = t // LANES

    def chunk(c, past, first):
        kc = k_ref[0, pl.ds(pl.multiple_of(c * t, t), t), :]
        vc = jnp.concatenate(
            [jnp.concatenate([vT_ref[0, p, c * blocks + jb] for jb in range(blocks)], axis=1)
             for p in range(SB_STEP_HEADS // 2)], axis=0)
        csum = jnp.zeros((1, SB_STEP_HEADS * t), jnp.float32) if first else csum_ref[...]
        part, csum = _sb_chunk(kc, vc, q_heads, tri, csum, past)
        acc_ref[...] = part if first else acc_ref[...] + part
        csum_ref[...] = csum

    key = lax.broadcasted_iota(jnp.int32, (t, SB_STEP_HEADS * t), 0)
    query = lax.broadcasted_iota(jnp.int32, (t, SB_STEP_HEADS * t), 1)
    chunk(i, key < jnp.bitwise_and(query, t - 1), True)

    def body(it, carry):
        chunk(i - 1 - it, None, False)
        return carry

    lax.fori_loop(0, i, body, 0)
    o_ref[0] = acc_ref[...].T.astype(o_ref.dtype)


def sb_attention(qT, k, vT3):
    b, width, s = qT.shape
    t = SB_TILE
    hw = SB_STEP_HEADS * HEAD_DIM
    tri = jnp.asarray(np.triu(np.ones((SB_SUB, SB_SUB), np.float32), k=1), jnp.bfloat16)
    return pl.pallas_call(
        _sb_attn_kernel,
        grid=(b, width // hw, s // t),
        in_specs=[
            pl.BlockSpec((1, hw, t), lambda bi, p, i: (bi, p, i)),
            pl.BlockSpec((1, s, hw), lambda bi, p, i: (bi, 0, p)),
            pl.BlockSpec((1, hw // LANES, s // LANES, LANES, LANES), lambda bi, p, i: (bi, p, 0, 0, 0)),
            pl.BlockSpec((SB_SUB, SB_SUB), lambda bi, p, i: (0, 0)),
        ],
        out_specs=pl.BlockSpec((1, t, hw), lambda bi, p, i: (bi, i, p)),
        out_shape=jax.ShapeDtypeStruct((b, s, width), jnp.bfloat16),
        scratch_shapes=[pltpu.VMEM((hw, t), jnp.float32), pltpu.VMEM((1, SB_STEP_HEADS * t), jnp.float32)],
        compiler_params=_cparams("parallel", "parallel", "arbitrary"),
        name="sb_attention",
    )(qT, k, vT3, tri)


DIL_BLOCKS_PER_STEP = 8


def _dil_attn_kernel(q_ref, kp_ref, kc_ref, vp_ref, vc_ref, bias_ref, o_ref, lse_ref, *, n_blocks, n_cols):
    m = pl.program_id(2)
    kk_all = jnp.concatenate([kp_ref[0], kc_ref[0]], axis=0)
    vv_all = jnp.concatenate([vp_ref[0], vc_ref[0]], axis=0)
    lane = lax.broadcasted_iota(jnp.int32, (Q_BLOCK, DIL_GROUP_WIDTH), 1)
    in_head = [jnp.logical_and(lane >= h * HEAD_DIM, lane < (h + 1) * HEAD_DIM) for h in range(DIL_HEADS_PER_GROUP)]
    bias = bias_ref[...]
    key = lax.broadcasted_iota(jnp.int32, bias.shape, 1)
    for j, cc in [(j, cc) for cc in range(n_cols) for j in range(n_blocks)]:
        q_rows = slice(j * Q_BLOCK, (j + 1) * Q_BLOCK)
        cols = slice(cc * DIL_GROUP_WIDTH, (cc + 1) * DIL_GROUP_WIDTH)
        q4 = q_ref[0, q_rows, cols]
        kk = kk_all[j * Q_BLOCK:(j + 2) * Q_BLOCK, cols]
        vv = vv_all[j * Q_BLOCK:(j + 2) * Q_BLOCK, cols]
        zero = jnp.zeros_like(q4)
        q_stack = jnp.concatenate([jnp.where(mask, q4, zero) for mask in in_head], axis=0)
        logits = _dot_nt(q_stack, kk) + bias
        if j == 0:
            logits = jnp.where(jnp.logical_or(key >= Q_BLOCK, m > 0), logits, NEG_INF)
        mx = jnp.max(logits, axis=1, keepdims=True)
        p = jnp.exp(logits - mx)
        den = jnp.sum(p, axis=1, keepdims=True)
        pv = _dot(p.astype(jnp.bfloat16), vv) / den
        lse_rows = mx + jnp.log(den)
        out = jnp.zeros(q4.shape, jnp.float32)
        lse = jnp.zeros(q4.shape, jnp.float32)
        for h, mask in enumerate(in_head):
            rows = slice(h * Q_BLOCK, (h + 1) * Q_BLOCK)
            out = jnp.where(mask, pv[rows], out)
            lse = jnp.where(mask, lse_rows[rows], lse)
        o_ref[0, q_rows, cols] = out.astype(o_ref.dtype)
        lse_ref[0, q_rows, cols] = lse


def dilated_attention(q, k, v, bias, dilation):
    b, length, width = q.shape
    gw = DIL_GROUP_WIDTH
    n_blocks = min(DIL_BLOCKS_PER_STEP, length // Q_BLOCK)
    n_cols = min(DIL_BLOCKS_PER_STEP // n_blocks, width // gw)
    cur = pl.BlockSpec((1, n_blocks * Q_BLOCK, n_cols * gw), lambda bi, c, m: (bi, m, c))
    prev = pl.BlockSpec((1, Q_BLOCK, n_cols * gw), lambda bi, c, m: (bi, jnp.maximum(m * n_blocks - 1, 0), c))
    return pl.pallas_call(
        functools.partial(_dil_attn_kernel, n_blocks=n_blocks, n_cols=n_cols),
        grid=(b, width // (n_cols * gw), length // (n_blocks * Q_BLOCK)),
        in_specs=[cur, prev, cur, prev, cur,
                  pl.BlockSpec((DIL_HEADS_PER_GROUP * Q_BLOCK, 2 * Q_BLOCK), lambda bi, c, m: (0, 0))],
        out_specs=[cur, cur],
        out_shape=[jax.ShapeDtypeStruct((b, length, width), jnp.bfloat16),
                   jax.ShapeDtypeStruct((b, length, width), jnp.float32)],
        compiler_params=_cparams("parallel", "parallel", "arbitrary"),
        name=f"dilated_attention_r{dilation}",
    )(q, k, k, v, v, bias.reshape(DIL_HEADS_PER_GROUP * Q_BLOCK, 2 * Q_BLOCK))


def _t5_causal_bucket(dist):
    max_exact = N_BUCKETS // 2
    d = jnp.maximum(dist, 1).astype(jnp.float32)
    large = max_exact + (jnp.log(d / max_exact) / math.log(MAX_DISTANCE / max_exact)
                         * (N_BUCKETS - max_exact)).astype(jnp.int32)
    large = jnp.minimum(large, N_BUCKETS - 1)
    return jnp.where(dist < max_exact, dist, large)


def dilated_bias(rel_bias_group, window, dilation):
    qi = jnp.arange(Q_BLOCK)[:, None]
    kj = jnp.arange(2 * Q_BLOCK)[None, :]
    dist_sub = qi + Q_BLOCK - kj
    bucket = _t5_causal_bucket(jnp.maximum(dist_sub, 0) * dilation)
    onehot = (bucket[..., None] == jnp.arange(N_BUCKETS)).astype(jnp.float32)
    bias = jnp.einsum("qkn,nh->hqk", onehot, rel_bias_group.astype(jnp.float32), precision=lax.Precision.HIGHEST)
    valid = (dist_sub >= 0) & (dist_sub <= window // dilation)
    return jnp.where(valid[None], bias, NEG_INF)


def _mix_out_kernel(oa_ref, o0_ref, o1_ref, o2_ref, l0_ref, l1_ref, l2_ref, ga_ref, gb_ref, x_ref, mod_ref,
                    wa_ref, wb_ref, wo_ref, gpost_ref, gpre_ref, wr_ref,
                    x1_ref, hp_ref, lt_ref, stage_ref):
    f32 = jnp.float32
    tm = x_ref.shape[1]

    def token_major(ref, dilation):
        if dilation == 1:
            return ref[0].astype(f32)
        slabs = DIL_GROUP_WIDTH // LANES
        for c in range(dilation):
            for j in range(slabs):
                col = c * DIL_GROUP_WIDTH + j * LANES
                stage_ref[j, pl.ds(c, tm // dilation, stride=dilation), :] = ref[0, :, col:col + LANES].astype(f32)
        return jnp.concatenate([stage_ref[j] for j in range(slabs)], axis=1)

    dils = [r for _, r in DIL_CONFIGS]
    l0, l1, l2 = (token_major(ref, r) for ref, r in zip((l0_ref, l1_ref, l2_ref), dils))
    mx = jnp.maximum(jnp.maximum(l0, l1), l2)
    e0, e1, e2 = jnp.exp(l0 - mx), jnp.exp(l1 - mx), jnp.exp(l2 - mx)
    den = e0 + e1 + e2
    o0, o1, o2 = (token_major(ref, r) for ref, r in zip((o0_ref, o1_ref, o2_ref), dils))
    out_b = (e0 / den) * o0 + (e1 / den) * o1 + (e2 / den) * o2
    ya = _dot(oa_ref[0], wa_ref[...])
    yb = _dot(out_b.astype(jnp.bfloat16), wb_ref[...])
    merged = ga_ref[0].astype(f32) * ya + gb_ref[0].astype(f32) * yb
    y = _dot(merged.astype(jnp.bfloat16), wo_ref[...])
    x1 = x_ref[0] + mod_ref[0, 2:3, :] * _rms_norm(y, gpost_ref[...])
    x1_ref[0] = x1
    h = _rms_norm(x1, gpre_ref[...]) * (1.0 + mod_ref[0, 4:5, :]) + mod_ref[0, 3:4, :]
    hb = h.astype(jnp.bfloat16)
    _pack_rows(hp_ref, h)
    lt_ref[...] = _dot_nt(wr_ref[...], hb)


def mix_out(out_a, o_g, lse_g, gate_a, gate_b, x, mod, w_a, w_b, w_o, g_post, g_pre_ffn, w_rT, *, tm=512):
    b, s, d = x.shape
    tok = lambda width: pl.BlockSpec((1, tm, width), lambda bi, i: (bi, i, 0))
    full = lambda a: pl.BlockSpec(a.shape, lambda bi, i: (0,) * a.ndim)
    nt = s // tm
    g_post, g_pre_ffn = g_post.reshape(1, d), g_pre_ffn.reshape(1, d)
    weights = (w_a, w_b, w_o, g_post, g_pre_ffn, w_rT)
    return pl.pallas_call(
        _mix_out_kernel,
        grid=(b, nt),
        in_specs=[tok(SB_WIDTH)] + [_dilated_spec(tm, r) for _, r in DIL_CONFIGS] * 2 + [tok(d)] * 3
        + [pl.BlockSpec((1, N_MOD, d), lambda bi, i: (bi, 0, 0))] + [full(w) for w in weights],
        out_specs=[tok(d), pl.BlockSpec((tm * ROW_SUB, LANES), lambda bi, i: (bi * nt + i, 0)),
                   pl.BlockSpec((N_EXPERTS, tm), lambda bi, i: (0, bi * nt + i))],
        out_shape=[jax.ShapeDtypeStruct((b, s, d), jnp.float32),
                   jax.ShapeDtypeStruct((b * s * ROW_SUB, LANES), jnp.uint32),
                   jax.ShapeDtypeStruct((N_EXPERTS, b * s), jnp.float32)],
        scratch_shapes=[pltpu.VMEM((DIL_GROUP_WIDTH // LANES, tm, LANES), jnp.float32)],
        compiler_params=_cparams("parallel", "parallel"),
        name="mix_out",
    )(out_a, *o_g, *lse_g, gate_a, gate_b, x, mod, *weights)


def _route_kernel(lt_ref, bias_ref, tri_ref, idx_ref, gate_ref, rank_ref, cnt_ref, base_ref, *, tn):
    f32 = jnp.float32

    @pl.when(pl.program_id(0) == 0)
    def _():
        base_ref[...] = jnp.zeros_like(base_ref)

    per_group = N_EXPERTS // N_GROUPS
    scores = _sigmoid(lt_ref[...])
    sel = scores + bias_ref[...]
    sel_groups = [sel[g * per_group:(g + 1) * per_group] for g in range(N_GROUPS)]
    ri = lax.broadcasted_iota(jnp.int32, (per_group, tn), 0)
    group_scores = []
    for sg in sel_groups:
        m1 = jnp.max(sg, axis=0, keepdims=True)
        first = jnp.min(jnp.where(sg == m1, ri, per_group), axis=0, keepdims=True)
        group_scores.append(m1 + jnp.max(jnp.where(ri == first, NEG_INF, sg), axis=0, keepdims=True))
    cur = jnp.concatenate(group_scores, axis=0)
    gi = lax.broadcasted_iota(jnp.int32, cur.shape, 0)
    keep = jnp.zeros(cur.shape, jnp.float32)
    for _ in range(TOP_K_GROUPS):
        mg = jnp.max(cur, axis=0, keepdims=True)
        pick = gi == jnp.min(jnp.where(cur == mg, gi, N_GROUPS), axis=0, keepdims=True)
        keep = jnp.where(pick, 1.0, keep)
        cur = jnp.where(pick, NEG_INF, cur)
    cur = jnp.concatenate([jnp.where(keep[g:g + 1, :] > 0.0, sg, NEG_INF) for g, sg in enumerate(sel_groups)], axis=0)
    ei = lax.broadcasted_iota(jnp.int32, cur.shape, 0)
    idxs, gates = [], []
    candidates = cur
    for _ in range(TOP_K):
        mk = jnp.max(cur, axis=0, keepdims=True)
        fk = jnp.min(jnp.where(cur == mk, ei, N_EXPERTS), axis=0, keepdims=True)
        pick = ei == fk
        idxs.append(fk)
        gates.append(jnp.sum(jnp.where(pick, scores, 0.0), axis=0, keepdims=True))
        cur = jnp.where(pick, NEG_INF, cur)
    gate = jnp.concatenate(gates, axis=0)
    gate_ref[...] = gate / jnp.sum(gate, axis=0, keepdims=True) * ROUTED_SCALE
    idx_ref[...] = jnp.concatenate(idxs, axis=0)
    onehot = jnp.where(cur == NEG_INF, 1.0, 0.0) - jnp.where(candidates == NEG_INF, 1.0, 0.0)
    earlier = _dot(onehot.astype(jnp.bfloat16), tri_ref[...]) + base_ref[...]
    ranks = [jnp.sum(jnp.where(ei == fk, earlier, 0.0), axis=0, keepdims=True) for fk in idxs]
    rank_ref[...] = jnp.concatenate(ranks, axis=0).astype(jnp.int32)
    base = base_ref[...] + jnp.sum(onehot, axis=1, keepdims=True)
    base_ref[...] = base
    cnt_ref[...] = jnp.broadcast_to(base, cnt_ref.shape).astype(jnp.int32)


def route(logits_t, router_bias, *, tn=512):
    ne, n = logits_t.shape
    tri = jnp.asarray(np.triu(np.ones((tn, tn), np.float32), k=1), jnp.bfloat16)
    slot = pl.BlockSpec((TOP_K, tn), lambda i: (0, i))
    return pl.pallas_call(
        functools.partial(_route_kernel, tn=tn),
        grid=(n // tn,),
        in_specs=[pl.BlockSpec((ne, tn), lambda i: (0, i)),
                  pl.BlockSpec((ne, 1), lambda i: (0, 0)),
                  pl.BlockSpec((tn, tn), lambda i: (0, 0))],
        out_specs=[slot, slot, slot, pl.BlockSpec((ne, LANES), lambda i: (0, 0))],
        out_shape=[jax.ShapeDtypeStruct((TOP_K, n), jnp.int32),
                   jax.ShapeDtypeStruct((TOP_K, n), jnp.float32),
                   jax.ShapeDtypeStruct((TOP_K, n), jnp.int32),
                   jax.ShapeDtypeStruct((ne, LANES), jnp.int32)],
        scratch_shapes=[pltpu.VMEM((ne, 1), jnp.float32)],
        compiler_params=_cparams("arbitrary"),
        name="route",
    )(logits_t, router_bias.reshape(ne, 1), tri)


def _dest_kernel(idx_ref, rank_ref, start_ref, dest_ref):
    idx = idx_ref[...]
    ei = lax.broadcasted_iota(jnp.int32, (N_EXPERTS, idx.shape[1]), 0)
    start = start_ref[...]
    rows = [jnp.sum(jnp.where(ei == idx[k:k + 1, :], start, 0.0), axis=0, keepdims=True) for k in range(TOP_K)]
    dest_ref[...] = jnp.concatenate(rows, axis=0).astype(jnp.int32) + rank_ref[...]


def slot_destinations(idx, rank, starts, *, tn=512):
    kk, n = idx.shape
    slot = pl.BlockSpec((kk, tn), lambda i: (0, i))
    return pl.pallas_call(
        _dest_kernel,
        grid=(n // tn,),
        in_specs=[slot, slot, pl.BlockSpec((N_EXPERTS, 1), lambda i: (0, 0))],
        out_specs=slot,
        out_shape=jax.ShapeDtypeStruct((kk, n), jnp.int32),
        compiler_params=_cparams("parallel"),
        name="slot_destinations",
    )(idx, rank, starts.astype(jnp.float32).reshape(N_EXPERTS, 1))


def _packed_row(ref, row):
    start = row * ROW_SUB
    return ref.at[pl.ds(start if isinstance(row, int) else pl.multiple_of(start, ROW_SUB), ROW_SUB), :]


def _dispatch_kernel(dest_ref, h_ref, xs_ref, zero_ref, sem, *, tm, n_rows):
    pad = zero_ref.shape[0]

    @pl.when(pl.program_id(0) == 0)
    def _():
        zero_ref[...] = jnp.zeros_like(zero_ref)
        tail = pltpu.make_async_copy(zero_ref, xs_ref.at[pl.ds(n_rows * ROW_SUB, pad), :], sem)
        tail.start()
        tail.wait()

    def body(t, _):
        for k in range(TOP_K):
            pltpu.make_async_copy(_packed_row(h_ref, t), _packed_row(xs_ref, dest_ref[k, t]), sem).start(priority=k % 2)
        return 0

    lax.fori_loop(0, tm, body, 0)
    for _ in range(TOP_K):
        pltpu.make_async_copy(h_ref, xs_ref.at[pl.ds(0, tm * ROW_SUB), :], sem).wait()


def dispatch(dest, hp, *, pad_rows, tm=512):
    kk, n = dest.shape
    n_rows = n * kk
    return pl.pallas_call(
        functools.partial(_dispatch_kernel, tm=tm, n_rows=n_rows),
        grid=(n // tm,),
        in_specs=[pl.BlockSpec((kk, tm), lambda i: (0, i), memory_space=pltpu.SMEM),
                  pl.BlockSpec((tm * ROW_SUB, LANES), lambda i: (i, 0))],
        out_specs=pl.BlockSpec(memory_space=pl.ANY),
        out_shape=jax.ShapeDtypeStruct(((n_rows + pad_rows) * ROW_SUB, LANES), hp.dtype),
        scratch_shapes=[pltpu.VMEM((pad_rows * ROW_SUB, LANES), hp.dtype), pltpu.SemaphoreType.DMA],
        compiler_params=_cparams("arbitrary"),
        name="dispatch",
    )(dest, hp)


EXPERT_CHUNK = 512
EXPERT_LOOKAHEAD = 3
EXPERT_IN_BUFS = EXPERT_LOOKAHEAD + 1
EXPERT_OUT_BUFS = 3


def _expert_kernel(first_ref, nch_ref, crow_ref, total_ref, xs_ref, wg_ref, wu_ref, wd_ref, ys_ref,
                   xbuf, ybuf, wgu_bf, wd_bf, h_ref, sem_in, sem_out, waited_ref):
    e = pl.program_id(0)
    chunk = EXPERT_CHUNK
    first, n_chunks, total = first_ref[e], nch_ref[e], total_ref[0]

    def rows(g):
        return pl.ds(pl.multiple_of(crow_ref[g] * ROW_SUB, ROW_SUB), chunk * ROW_SUB)

    def in_copy(g):
        slot = lax.rem(g, EXPERT_IN_BUFS)
        return pltpu.make_async_copy(xs_ref.at[rows(g), :], xbuf.at[slot], sem_in.at[slot])

    def out_copy(g):
        slot = lax.rem(g, EXPERT_OUT_BUFS)
        return pltpu.make_async_copy(ybuf.at[slot], ys_ref.at[rows(g), :], sem_out.at[slot])

    def wait_writes_through(g):
        def wait_one(j, carry):
            out_copy(j).wait()
            return carry

        lax.fori_loop(waited_ref[0] + 1, g + 1, wait_one, 0)
        waited_ref[0] = jnp.maximum(waited_ref[0], g)

    @pl.when(e == 0)
    def _():
        waited_ref[0] = -1
        for g in range(EXPERT_LOOKAHEAD):
            @pl.when(g < total)
            def _():
                in_copy(g).start()

    @pl.when(n_chunks > 0)
    def _():
        wgu_bf[:, :EXPERT_DIM] = wg_ref[0].astype(wgu_bf.dtype)
        wgu_bf[:, EXPERT_DIM:] = wu_ref[0].astype(wgu_bf.dtype)
        wd_bf[...] = wd_ref[0].astype(wd_bf.dtype)

    def acquire(g):
        in_copy(g).wait()

        @pl.when(g + EXPERT_LOOKAHEAD < total)
        def _():
            in_copy(g + EXPERT_LOOKAHEAD).start()

    def hidden(g):
        bf = jnp.bfloat16
        xa, xb = _unpack_rows(xbuf.at[lax.rem(g, EXPERT_IN_BUFS)], chunk)
        gu = _dot(xa.astype(bf), wgu_bf[:ROW_WORDS, :]) + _dot(xb.astype(bf), wgu_bf[ROW_WORDS:, :])
        gate, up = gu[:, :EXPERT_DIM], gu[:, EXPERT_DIM:]
        return (gate * _sigmoid(gate) * up).astype(bf)

    def pack(g, h):
        _pack_rows(ybuf.at[lax.rem(g, EXPERT_OUT_BUFS)], _dot(h, wd_bf[...]))

    def release(g):
        @pl.when(g == first)
        def _():
            wait_writes_through(g - 1)

        out_copy(g).start()

    @pl.when(n_chunks > 0)
    def _():
        acquire(first)
        h_ref[...] = hidden(first)

    def body(c, carry):
        g = first + c
        acquire(g)
        wait_writes_through(g - 1 - EXPERT_OUT_BUFS)
        h_prev = h_ref[...]
        pack(g - 1, h_prev)
        h_ref[...] = hidden(g)
        release(g - 1)
        return carry

    lax.fori_loop(1, n_chunks, body, 0)

    @pl.when(n_chunks > 0)
    def _():
        last = first + n_chunks - 1
        wait_writes_through(last - EXPERT_OUT_BUFS)
        pack(last, h_ref[...])
        release(last)

    @pl.when(e == pl.num_programs(0) - 1)
    def _():
        wait_writes_through(total - 1)
        ybuf[0] = jnp.zeros(ybuf.shape[1:], ybuf.dtype)
        pad = pl.ds(ys_ref.shape[0] - chunk * ROW_SUB, chunk * ROW_SUB)
        tail = pltpu.make_async_copy(ybuf.at[0], ys_ref.at[pad, :], sem_out.at[0])
        tail.start()
        tail.wait()


def expert_chunks(starts, counts, n_rows):
    chunk = EXPERT_CHUNK
    max_chunks = n_rows // chunk + N_EXPERTS
    nch = (counts + (chunk - 1)) // chunk
    ends = jnp.cumsum(nch)
    first = ends - nch
    g = jnp.arange(max_chunks, dtype=jnp.int32)
    owner = jnp.minimum(jnp.sum(g[:, None] >= ends[None, :], axis=1), N_EXPERTS - 1)
    onehot = (owner[:, None] == jnp.arange(N_EXPERTS)[None, :]).astype(jnp.int32)
    crow = (onehot * (starts - first * chunk)[None, :]).sum(axis=1) + g * chunk
    crow = jnp.where(g < ends[-1], crow, 0).astype(jnp.int32)
    return first.astype(jnp.int32), nch.astype(jnp.int32), crow, ends[-1:].astype(jnp.int32)


def expert_ffn(starts, counts, xs, w_gate, w_up, w_down):
    d = D_MODEL
    n_rows = xs.shape[0] // ROW_SUB - EXPERT_CHUNK
    tables = expert_chunks(starts, counts, n_rows)
    chunk_shape = (EXPERT_CHUNK * ROW_SUB, LANES)
    weights = lambda shape: pl.BlockSpec(shape, lambda e, *_: (e, 0, 0))
    grid_spec = pltpu.PrefetchScalarGridSpec(
        num_scalar_prefetch=len(tables),
        grid=(N_EXPERTS,),
        in_specs=[pl.BlockSpec(memory_space=pl.ANY),
                  weights((1, d, EXPERT_DIM)), weights((1, d, EXPERT_DIM)), weights((1, EXPERT_DIM, d))],
        out_specs=pl.BlockSpec(memory_space=pl.ANY),
        scratch_shapes=[pltpu.VMEM((EXPERT_IN_BUFS,) + chunk_shape, xs.dtype),
                        pltpu.VMEM((EXPERT_OUT_BUFS,) + chunk_shape, xs.dtype),
                        pltpu.VMEM((d, 2 * EXPERT_DIM), jnp.bfloat16), pltpu.VMEM((EXPERT_DIM, d), jnp.bfloat16),
                        pltpu.VMEM((EXPERT_CHUNK, EXPERT_DIM), jnp.bfloat16),
                        pltpu.SemaphoreType.DMA((EXPERT_IN_BUFS,)), pltpu.SemaphoreType.DMA((EXPERT_OUT_BUFS,)),
                        pltpu.SMEM((1,), jnp.int32)],
    )
    return pl.pallas_call(
        _expert_kernel,
        grid_spec=grid_spec,
        out_shape=jax.ShapeDtypeStruct(xs.shape, xs.dtype),
        compiler_params=_cparams("arbitrary"),
        name="expert_ffn",
    )(*tables, xs, w_gate, w_up, w_down)


COMBINE_AHEAD = 2
COMBINE_SLOTS = COMBINE_AHEAD + 1
COMBINE_GROUP = 32


def _combine_kernel(*refs, tm):
    head_refs, (dest_ref, ys_ref, gate_ref, hp_ref, wsgu_ref, wsd_ref, x1_ref, mod_ref, g_ref, o_ref, buf, sem) = (
        refs[:COMBINE_AHEAD], refs[COMBINE_AHEAD:])
    i, n = pl.program_id(0), pl.num_programs(0)

    def gather(src_ref, slot, t):
        for k in range(TOP_K):
            pltpu.make_async_copy(_packed_row(ys_ref, src_ref[k, t]), _packed_row(buf.at[slot, k], t),
                                  sem.at[slot]).start(priority=k % 2)

    def wait_tile(slot):
        for k in range(TOP_K):
            pltpu.make_async_copy(ys_ref.at[pl.ds(0, tm * ROW_SUB), :], buf.at[slot, k], sem.at[slot]).wait()

    @pl.when(i == 0)
    def _():
        for a, head in enumerate(head_refs):
            def body(t, carry, head=head, a=a):
                gather(head, a, t)
                return carry

            lax.fori_loop(0, tm, body, 0)

    def step(cur):
        ahead = (cur + COMBINE_AHEAD) % COMBINE_SLOTS
        wait_tile(cur)
        g_post = g_ref[...]
        gt2 = mod_ref[0, 5:6, :]
        bf = jnp.bfloat16
        xa, xb = _unpack_rows(hp_ref, tm)
        gu = _dot(xa.astype(bf), wsgu_ref[:ROW_WORDS, :]) + _dot(xb.astype(bf), wsgu_ref[ROW_WORDS:, :])
        sg, su = gu[:, :EXPERT_DIM], gu[:, EXPERT_DIM:]
        shared = _dot((sg * _sigmoid(sg) * su).astype(bf), wsd_ref[...])
        for t0 in range(0, tm, COMBINE_GROUP):
            for t in range(t0, t0 + COMBINE_GROUP):
                gather(dest_ref, ahead, t)
            rows = slice(t0, t0 + COMBINE_GROUP)
            gate = gate_ref[rows, :]
            first = jnp.zeros((COMBINE_GROUP, ROW_WORDS), jnp.float32)
            second = jnp.zeros((COMBINE_GROUP, ROW_WORDS), jnp.float32)
            for k in range(TOP_K):
                packed = buf.at[cur, k, pl.ds(t0 * ROW_SUB, COMBINE_GROUP * ROW_SUB), :]
                ya, yb = _unpack_rows(packed, COMBINE_GROUP)
                first = first + ya * gate[:, k:k + 1]
                second = second + yb * gate[:, k:k + 1]
            y = shared[rows] + jnp.concatenate([first, second], axis=1)
            o_ref[rows, :] = x1_ref[rows, :] + gt2 * _rms_norm(y, g_post)

        @pl.when(i == n - 1)
        def _():
            for a in range(1, COMBINE_SLOTS):
                wait_tile((cur + a) % COMBINE_SLOTS)

    for slot in range(COMBINE_SLOTS):
        pl.when(lax.rem(i, COMBINE_SLOTS) == slot)(functools.partial(step, slot))


def combine(dest, ys, gate_nk, hp, w_sgu, w_sd, x1, mod, g_post_ffn, *, seq, tm=128):
    n, d = x1.shape
    per_batch = seq // tm
    n_tiles = n // tm
    tok = pl.BlockSpec((tm, d), lambda i: (i, 0))
    dest_tile = lambda index: pl.BlockSpec((TOP_K, tm), index, memory_space=pltpu.SMEM)
    head_specs = [dest_tile(lambda i, a=a: (0, a)) for a in range(COMBINE_AHEAD)]
    return pl.pallas_call(
        functools.partial(_combine_kernel, tm=tm),
        grid=(n_tiles,),
        in_specs=head_specs + [dest_tile(lambda i: (0, jnp.minimum(i + COMBINE_AHEAD, n_tiles - 1))),
                               pl.BlockSpec(memory_space=pl.ANY),
                               pl.BlockSpec((tm, TOP_K), lambda i: (i, 0)),
                               pl.BlockSpec((tm * ROW_SUB, LANES), lambda i: (i, 0)),
                               pl.BlockSpec(w_sgu.shape, lambda i: (0, 0)),
                               pl.BlockSpec(w_sd.shape, lambda i: (0, 0)),
                               tok,
                               pl.BlockSpec((1, N_MOD, d), lambda i: (i // per_batch, 0, 0)),
                               pl.BlockSpec((1, d), lambda i: (0, 0))],
        out_specs=tok,
        out_shape=jax.ShapeDtypeStruct((n, d), jnp.float32),
        scratch_shapes=[pltpu.VMEM((COMBINE_SLOTS, TOP_K, tm * ROW_SUB, LANES), ys.dtype),
                        pltpu.SemaphoreType.DMA((COMBINE_SLOTS,))],
        compiler_params=_cparams("arbitrary"),
        name="combine",
    )(*([dest] * COMBINE_AHEAD), dest, ys, gate_nk, hp, w_sgu, w_sd, x1, mod, g_post_ffn.reshape(1, d))


def _layer(x, mod, rel_bias, g_pre_mix, w_in, w_branch_a, w_branch_b, w_out, g_post_mix, g_pre_ffn, w_router,
           router_bias, w_exp_gate, w_exp_up, w_exp_down, w_sh_gate, w_sh_up, w_sh_down, g_post_ffn):
    b, s, d = x.shape
    assert d == D_MODEL and w_exp_gate.shape == (N_EXPERTS, D_MODEL, EXPERT_DIM), (x.shape, w_exp_gate.shape)
    assert s % SB_TILE == 0 and all(s % (r * Q_BLOCK) == 0 for _, r in DIL_CONFIGS), s
    bf = jnp.bfloat16
    scale = HEAD_DIM ** -0.5
    sb_q, sb_k, sb_v, dq, dk, dv, w_ga, w_gb = jnp.split(
        w_in, np.cumsum((SB_WIDTH,) * 3 + (DIL_WIDTH,) * 3 + (d,)).tolist(), axis=1)
    groups = lambda w: [w[:, g * DIL_GROUP_WIDTH:(g + 1) * DIL_GROUP_WIDTH] for g in range(len(DIL_CONFIGS))]
    w_rows = jnp.concatenate([sb_k] + [w * scale for w in groups(dq)] + groups(dk) + groups(dv) + [w_ga, w_gb],
                             axis=1).astype(bf)
    w_cols = jnp.concatenate([sb_q * (scale * LOG2E), sb_v], axis=1).T.astype(bf)
    qT, k, vT3, *rest = in_projection(x, mod, g_pre_mix, w_rows, w_cols)
    dqs, dks, dvs, gate_a, gate_b = rest[0:3], rest[3:6], rest[6:9], rest[9], rest[10]
    out_a = sb_attention(qT, k, vT3)
    o_g, lse_g = [], []
    for g, (window, dilation) in enumerate(DIL_CONFIGS):
        bias = dilated_bias(rel_bias[:, g * DIL_HEADS_PER_GROUP:(g + 1) * DIL_HEADS_PER_GROUP], window, dilation)
        o, lse = dilated_attention(dqs[g], dks[g], dvs[g], bias, dilation)
        o_g.append(o)
        lse_g.append(lse)
    x1, hp, logits_t = mix_out(
        out_a, o_g, lse_g, gate_a, gate_b, x, mod, w_branch_a.astype(bf), w_branch_b.astype(bf), w_out.astype(bf),
        g_post_mix, g_pre_ffn, w_router.T.astype(bf))
    w_sgu = jnp.concatenate([w_sh_gate, w_sh_up], axis=1).astype(bf)
    n = b * s
    idx, gate, rank, counts = route(logits_t, router_bias)
    counts = counts[:, 0]
    starts = jnp.cumsum(counts) - counts
    dest = slot_destinations(idx, rank, starts)
    xs = dispatch(dest, hp, pad_rows=EXPERT_CHUNK)
    ys = expert_ffn(starts, counts, xs, w_exp_gate, w_exp_up, w_exp_down)
    out = combine(dest, ys, gate.T, hp, w_sgu, w_sh_down.astype(bf), x1.reshape(n, d), mod, g_post_ffn, seq=s)
    return out.reshape(b, s, d)


def kernel(x, c, rel_bias, w_ada, b_ada, g_pre_mix, w_in, w_branch_a, w_branch_b, w_out, g_post_mix, g_pre_ffn,
           w_router, router_bias, w_exp_gate, w_exp_up, w_exp_down, w_sh_gate, w_sh_up, w_sh_down, g_post_ffn):
    depth = w_in.shape[0]
    for layer in range(depth):
        mod = ada_modulation(c, w_ada[layer], b_ada[layer]).reshape(x.shape[0], N_MOD, x.shape[2])
        x = _layer(x, mod, rel_bias, g_pre_mix[layer], w_in[layer], w_branch_a[layer], w_branch_b[layer], w_out[layer],
                   g_post_mix[layer], g_pre_ffn[layer], w_router[layer], router_bias[layer], w_exp_gate[layer],
                   w_exp_up[layer], w_exp_down[layer], w_sh_gate[layer], w_sh_up[layer], w_sh_down[layer],
                   g_post_ffn[layer])
    return x
```
